```python
import math
import jax, jax.numpy as jnp
from jax import lax
import numpy as np

D_MODEL = 1024
BATCH = 2
SEQ = 16384
DEPTH = 2

GRID_W = 64
CTX_LEN = 256
NORM_EPS = 1e-6
N_MOD = 6

MLA_HEADS = 8
MLA_NOPE = 64
MLA_ROPE = 32
MLA_V = 64
MLA_Q_LORA = 256
MLA_KV_LORA = 128
ROPE_BASE = 10000.0
Q_BLOCK = 128

S5_WIDTH = 512
S5_GROUP = 16
S5_GROUPS = S5_WIDTH // S5_GROUP
S5_STATE = 64
S5_DT_MIN = 0.001
S5_DT_MAX = 0.1

HY_WIDTH = 512
HY_ORDER = 2
HY_BANDS = 16
HY_POS_DIM = 1 + 2 * HY_BANDS
HY_FILTER_HIDDEN = 64
HY_FILTER_OUT = HY_ORDER * 2 * HY_WIDTH
HY_DECAY_TARGET = 1e-2
HY_DECAY_SHORT = 0.3
HY_DECAY_LONG = 1.5
HY_DECAY_SHIFT = 0.05

N_BRANCH = 3
D_FF = -(-8 * D_MODEL // (3 * 256)) * 256

OFF_CQ = 0
OFF_CKV = OFF_CQ + MLA_Q_LORA
OFF_KR = OFF_CKV + MLA_KV_LORA
OFF_S5 = OFF_KR + MLA_ROPE
OFF_HY = OFF_S5 + S5_WIDTH
OFF_GATE = OFF_HY + 3 * HY_WIDTH
IN_WIDTH = OFF_GATE + N_BRANCH * D_MODEL

kernel_name = "hybrid_mla_s5_hyena_prefix_dit"


def rmsnorm(x, g):
    xf = x.astype(jnp.float32)
    y = xf * lax.rsqrt(jnp.mean(xf * xf, axis=-1, keepdims=True) + NORM_EPS)
    return (y * g.astype(jnp.float32)).astype(x.dtype)


def modulate(xn, shift, scale):
    return xn * (1.0 + scale) + shift


def axial_rope(n_tok):
    rows = n_tok // GRID_W
    row = jnp.broadcast_to(jnp.arange(rows, dtype=jnp.float32)[:, None], (rows, GRID_W)).reshape(-1)
    col = jnp.broadcast_to(jnp.arange(GRID_W, dtype=jnp.float32)[None, :], (rows, GRID_W)).reshape(-1)
    n_freq = MLA_ROPE // 4
    inv = ROPE_BASE ** (-jnp.arange(n_freq, dtype=jnp.float32) / n_freq)
    ang = jnp.concatenate([row[:, None] * inv, col[:, None] * inv], axis=-1)
    return jnp.cos(ang), jnp.sin(ang)


def apply_rope(x, cos, sin):
    shp = (cos.shape[0],) + (1,) * (x.ndim - 3) + (cos.shape[1],)
    cos = cos.reshape(shp)
    sin = sin.reshape(shp)
    xf = x.astype(jnp.float32)
    x1 = xf[..., 0::2]
    x2 = xf[..., 1::2]
    out = jnp.stack([x1 * cos - x2 * sin, x1 * sin + x2 * cos], axis=-1).reshape(x.shape)
    return out.astype(x.dtype)


def mla_queries(z, q_norm, w_uq, rope):
    b, n, _ = z.shape
    q = (rmsnorm(z[..., OFF_CQ:OFF_CKV], q_norm) @ w_uq).reshape(b, n, MLA_HEADS, MLA_NOPE + MLA_ROPE)
    if rope is not None:
        q = jnp.concatenate([q[..., :MLA_NOPE], apply_rope(q[..., MLA_NOPE:], *rope)], axis=-1)
    return q


def mla_keys_values(z, kv_norm, w_ukv, rope):
    b, n, _ = z.shape
    kv = (rmsnorm(z[..., OFF_CKV:OFF_KR], kv_norm) @ w_ukv).reshape(b, n, MLA_HEADS, MLA_NOPE + MLA_V)
    k_rope = z[..., OFF_KR:OFF_S5]
    if rope is not None:
        k_rope = apply_rope(k_rope, *rope)
    k = jnp.concatenate(
        [kv[..., :MLA_NOPE], jnp.broadcast_to(k_rope[:, :, None, :], (b, n, MLA_HEADS, MLA_ROPE))], axis=-1)
    return k, kv[..., MLA_NOPE:]


def attend(q, k, v):
    s = jnp.einsum('bqhd,bkhd->bhqk', q, k).astype(jnp.float32) * ((MLA_NOPE + MLA_ROPE) ** -0.5)
    p = jax.nn.softmax(s, axis=-1).astype(v.dtype)
    return jnp.einsum('bhqk,bkhd->bqhd', p, v)


def blocked_attend(q, k, v):
    b, n, h, dq = q.shape
    nb = n // Q_BLOCK
    qb = jnp.moveaxis(q.reshape(b, nb, Q_BLOCK, h, dq), 1, 0)
    out = lax.map(lambda qq: attend(qq, k, v), qb)
    return jnp.moveaxis(out, 0, 1).reshape(b, n, h * v.shape[-1])


def s5_discretize(lam_re, lam_im, log_dt, b_re, b_im):
    lam = lax.complex(lam_re.astype(jnp.float32), lam_im.astype(jnp.float32))
    dt = jnp.exp(log_dt.astype(jnp.float32))[:, None]
    lam_bar = jnp.exp(lam * dt)
    b_mat = lax.complex(b_re.astype(jnp.float32), b_im.astype(jnp.float32))
    b_bar = ((lam_bar - 1.0) / lam)[:, :, None] * b_mat
    return lam_bar, b_bar


def _linear_combine(left, right):
    a_l, b_l = left
    a_r, b_r = right
    return a_r * a_l, a_r * b_l + b_r


def s5_states(u, lam_bar, b_bar, h0):
    bu = jnp.einsum('gpn,blgn->blgp', b_bar, u.astype(jnp.complex64))
    bu = bu.at[:, 0].add(lam_bar * h0)
    a = jnp.broadcast_to(lam_bar, bu.shape)
    _, h = lax.associative_scan(_linear_combine, (a, bu), axis=1)
    return h


def s5_readout(h, c_re, c_im):
    c_mat = lax.complex(c_re.astype(jnp.float32), c_im.astype(jnp.float32))
    return jnp.einsum('gnp,blgp->blgn', c_mat, h).real


def s5_mixer(u, disc, c_re, c_im, d, h0, want_output):
    b, n, _ = u.shape
    uf = u.astype(jnp.float32)
    ug = uf.reshape(b, n, S5_GROUPS, S5_GROUP)
    h_f = s5_states(ug, disc[0][0], disc[0][1], h0[0])
    h_b = s5_states(ug[:, ::-1], disc[1][0], disc[1][1], h0[1])
    finals = (h_f[:, -1], h_b[:, -1])
    if not want_output:
        return None, finals
    y = s5_readout(h_f, c_re[0], c_im[0]) + s5_readout(h_b, c_re[1], c_im[1])[:, ::-1]
    y = y.reshape(b, n, S5_WIDTH) + d.astype(jnp.float32) * uf
    return y, finals


def s5_glu(y, w_glu):
    y = jax.nn.gelu(y)
    return y * jax.nn.sigmoid(y @ w_glu.astype(jnp.float32))


def hyena_filters(n_tok, w1, b1, w2, b2, w3, freq):
    f32 = jnp.float32
    t = jnp.linspace(0.0, 1.0, n_tok, dtype=f32)[:, None]
    w = 2.0 * math.pi * jnp.arange(n_tok, dtype=f32)[:, None] / n_tok
    bands = jnp.linspace(1e-4, HY_BANDS - 1, HY_BANDS, dtype=f32)[None, :]
    z = jnp.concatenate([t, jnp.cos(bands * w), -jnp.sin(bands * w)], axis=-1)
    fq = freq.astype(f32)
    hid = jnp.sin(fq * (z @ w1.astype(f32) + b1.astype(f32)))
    hid = jnp.sin(fq * (hid @ w2.astype(f32) + b2.astype(f32)))
    filt = hid @ w3.astype(f32)
    deltas = jnp.abs(jnp.linspace(math.log(HY_DECAY_TARGET) / HY_DECAY_SHORT,
                                  math.log(HY_DECAY_TARGET) / HY_DECAY_LONG, HY_FILTER_OUT, dtype=f32))
    return filt * (jnp.exp(-t * deltas) + HY_DECAY_SHIFT)


def short_conv(z, w, b):
    zp = jnp.pad(z, ((0, 0), (1, 1), (0, 0)))
    return zp[:, :-2] * w[0] + zp[:, 1:-1] * w[1] + zp[:, 2:] * w[2] + b


def long_conv(u, h_fwd, h_bwd, bias):
    n = u.shape[1]
    n_fft = 2 * n
    spec_f = jnp.fft.rfft(h_fwd, n=n_fft, axis=0)
    spec_b = jnp.fft.rfft(h_bwd, n=n_fft, axis=0)
    y_f = jnp.fft.irfft(jnp.fft.rfft(u, n=n_fft, axis=1) * spec_f, n=n_fft, axis=1)[:, :n]
    y_b = jnp.fft.irfft(jnp.fft.rfft(u[:, ::-1], n=n_fft, axis=1) * spec_b, n=n_fft, axis=1)[:, :n][:, ::-1]
    return y_f + y_b + u * bias


def hyena_mixer(z, filt, conv_w, conv_b, bias):
    b, n, _ = z.shape
    u = short_conv(z, conv_w, conv_b).astype(jnp.float32)
    v, g1, g2 = jnp.split(u, 3, axis=-1)
    filt = filt.reshape(n, HY_ORDER, 2, HY_WIDTH)
    bias = bias.astype(jnp.float32)
    y = v
    for o, g in enumerate((g1, g2)):
        y = g * long_conv(y, filt[:, o, 0], filt[:, o, 1], bias[o])
    return y.astype(z.dtype)


def merge_branches(z, outs, w_branches, w_out):
    gates = jax.nn.sigmoid(z[..., OFF_GATE:].astype(jnp.float32))
    gates = gates.reshape(z.shape[:-1] + (N_BRANCH, D_MODEL))
    merged = gates[..., 0, :] * (outs[0] @ w_branches[0]).astype(jnp.float32)
    for j in range(1, N_BRANCH):
        merged = merged + gates[..., j, :] * (outs[j] @ w_branches[j]).astype(jnp.float32)
    return merged.astype(z.dtype) @ w_out


def swiglu(h, w_gu, w_down):
    gu = h @ w_gu
    return (jax.nn.silu(gu[..., :D_FF]) * gu[..., D_FF:]) @ w_down


def setup_inputs(seed: int = 0) -> dict:
    key = jax.random.key(seed)
    keys = iter(jax.random.split(key, 48))

    def nrm(shape, scale):
        return scale * jax.random.normal(next(keys), shape, jnp.float32)

    L = DEPTH
    D = D_MODEL
    x = nrm((BATCH, SEQ, D), 1.0)
    c = nrm((BATCH, D), 1.0)
    ctx = nrm((BATCH, CTX_LEN, D), 1.0)
    c_ctx = nrm((D,), 1.0)
    ada_w = nrm((L, D, N_MOD * D), 0.5 * D ** -0.5)
    ada_b = nrm((L, N_MOD * D), 0.01)
    norm_mix = 1.0 + nrm((L, D), 0.01)
    w_in = nrm((L, D, IN_WIDTH), D ** -0.5)
    mla_q_norm = 1.0 + nrm((L, MLA_Q_LORA), 0.01)
    mla_w_uq = nrm((L, MLA_Q_LORA, MLA_HEADS * (MLA_NOPE + MLA_ROPE)), MLA_Q_LORA ** -0.5)
    mla_kv_norm = 1.0 + nrm((L, MLA_KV_LORA), 0.01)
    mla_w_ukv = nrm((L, MLA_KV_LORA, MLA_HEADS * (MLA_NOPE + MLA_V)), MLA_KV_LORA ** -0.5)
    s5_lam_re = -0.5 + nrm((L, 2, S5_GROUPS, S5_STATE), 0.01)
    s5_lam_im = math.pi * jnp.arange(S5_STATE, dtype=jnp.float32) + nrm((L, 2, S5_GROUPS, S5_STATE), 0.01)
    s5_log_dt = jax.random.uniform(next(keys), (L, 2, S5_GROUPS), jnp.float32,
                                   math.log(S5_DT_MIN), math.log(S5_DT_MAX))
    s5_b_re = nrm((L, 2, S5_GROUPS, S5_STATE, S5_GROUP), (2 * S5_GROUP) ** -0.5)
    s5_b_im = nrm((L, 2, S5_GROUPS, S5_STATE, S5_GROUP), (2 * S5_GROUP) ** -0.5)
    s5_c_re = nrm((L, 2, S5_GROUPS, S5_GROUP, S5_STATE), S5_STATE ** -0.5)
    s5_c_im = nrm((L, 2, S5_GROUPS, S5_GROUP, S5_STATE), S5_STATE ** -0.5)
    s5_d = nrm((L, S5_WIDTH), 0.5)
    s5_w_glu = nrm((L, S5_WIDTH, S5_WIDTH), S5_WIDTH ** -0.5)
    hy_conv_w = nrm((L, 3, 3 * HY_WIDTH), 0.5)
    hy_conv_b = nrm((L, 3 * HY_WIDTH), 0.01)
    hy_f_w1 = nrm((L, HY_POS_DIM, HY_FILTER_HIDDEN), HY_POS_DIM ** -0.5)
    hy_f_b1 = nrm((L, HY_FILTER_HIDDEN), 0.1)
    hy_f_w2 = nrm((L, HY_FILTER_HIDDEN, HY_FILTER_HIDDEN), HY_FILTER_HIDDEN ** -0.5)
    hy_f_b2 = nrm((L, HY_FILTER_HIDDEN), 0.1)
    hy_f_w3 = nrm((L, HY_FILTER_HIDDEN, HY_FILTER_OUT), 0.03 * HY_FILTER_HIDDEN ** -0.5)
    hy_f_freq = 1.0 + nrm((L, HY_FILTER_HIDDEN), 0.01)
    hy_bias = nrm((L, HY_ORDER, HY_WIDTH), 0.5)
    w_branch_mla = nrm((L, MLA_HEADS * MLA_V, D), (MLA_HEADS * MLA_V) ** -0.5)
    w_branch_s5 = nrm((L, S5_WIDTH, D), S5_WIDTH ** -0.5)
    w_branch_hy = nrm((L, HY_WIDTH, D), HY_WIDTH ** -0.5)
    w_out = nrm((L, D, D), D ** -0.5)
    norm_ffn = 1.0 + nrm((L, D), 0.01)
    ffn_w_gu = nrm((L, D, 2 * D_FF), D ** -0.5)
    ffn_w_down = nrm((L, D_FF, D), D_FF ** -0.5)
    final_norm = 1.0 + nrm((D,), 0.01)
    return {
        "x": x, "c": c, "ctx": ctx, "c_ctx": c_ctx,
        "ada_w": ada_w, "ada_b": ada_b, "norm_mix": norm_mix, "w_in": w_in,
        "mla_q_norm": mla_q_norm, "mla_w_uq": mla_w_uq, "mla_kv_norm": mla_kv_norm, "mla_w_ukv": mla_w_ukv,
        "s5_lam_re": s5_lam_re, "s5_lam_im": s5_lam_im, "s5_log_dt": s5_log_dt,
        "s5_b_re": s5_b_re, "s5_b_im": s5_b_im, "s5_c_re": s5_c_re, "s5_c_im": s5_c_im,
        "s5_d": s5_d, "s5_w_glu": s5_w_glu,
        "hy_conv_w": hy_conv_w, "hy_conv_b": hy_conv_b,
        "hy_f_w1": hy_f_w1, "hy_f_b1": hy_f_b1, "hy_f_w2": hy_f_w2, "hy_f_b2": hy_f_b2,
        "hy_f_w3": hy_f_w3, "hy_f_freq": hy_f_freq, "hy_bias": hy_bias,
        "w_branch_mla": w_branch_mla, "w_branch_s5": w_branch_s5, "w_branch_hy": w_branch_hy,
        "w_out": w_out, "norm_ffn": norm_ffn, "ffn_w_gu": ffn_w_gu, "ffn_w_down": ffn_w_down,
        "final_norm": final_norm,
    }


def reference(x, c, ctx, c_ctx, ada_w, ada_b, norm_mix, w_in,
              mla_q_norm, mla_w_uq, mla_kv_norm, mla_w_ukv,
              s5_lam_re, s5_lam_im, s5_log_dt, s5_b_re, s5_b_im, s5_c_re, s5_c_im, s5_d, s5_w_glu,
              hy_conv_w, hy_conv_b, hy_f_w1, hy_f_b1, hy_f_w2, hy_f_b2, hy_f_w3, hy_f_freq, hy_bias,
              w_branch_mla, w_branch_s5, w_branch_hy, w_out,
              norm_ffn, ffn_w_gu, ffn_w_down, final_norm):
    bsz, n_tok, _ = x.shape
    n_ctx = ctx.shape[1]
    rope = axial_rope(n_tok)
    h0 = jnp.zeros((bsz, S5_GROUPS, S5_STATE), jnp.complex64)
    for i in range(DEPTH):
        last = i == DEPTH - 1
        mx = (jax.nn.silu(c) @ ada_w[i] + ada_b[i]).reshape(bsz, N_MOD, 1, D_MODEL)
        mc = (jax.nn.silu(c_ctx) @ ada_w[i] + ada_b[i]).reshape(N_MOD, D_MODEL)

        hx = modulate(rmsnorm(x, norm_mix[i]), mx[:, 0], mx[:, 1])
        hc = modulate(rmsnorm(ctx, norm_mix[i]), mc[0], mc[1])
        zx = hx @ w_in[i]
        zc = hc @ (w_in[i][:, :OFF_HY] if last else w_in[i])

        k_c, v_c = mla_keys_values(zc, mla_kv_norm[i], mla_w_ukv[i], None)
        k_x, v_x = mla_keys_values(zx, mla_kv_norm[i], mla_w_ukv[i], rope)
        q_x = mla_queries(zx, mla_q_norm[i], mla_w_uq[i], rope)
        a_x = blocked_attend(q_x, jnp.concatenate([k_c, k_x], axis=1), jnp.concatenate([v_c, v_x], axis=1))

        disc = tuple(s5_discretize(s5_lam_re[i, d], s5_lam_im[i, d], s5_log_dt[i, d],
                                   s5_b_re[i, d], s5_b_im[i, d]) for d in range(2))
        y_c, finals = s5_mixer(zc[..., OFF_S5:OFF_HY], disc, s5_c_re[i], s5_c_im[i], s5_d[i],
                               (h0, h0), not last)
        y_x, _ = s5_mixer(zx[..., OFF_S5:OFF_HY], disc, s5_c_re[i], s5_c_im[i], s5_d[i], finals, True)
        s_x = s5_glu(y_x, s5_w_glu[i]).astype(x.dtype)

        fparams = (hy_f_w1[i], hy_f_b1[i], hy_f_w2[i], hy_f_b2[i], hy_f_w3[i], hy_f_freq[i])
        e_x = hyena_mixer(zx[..., OFF_HY:OFF_GATE], hyena_filters(n_tok, *fparams),
                          hy_conv_w[i], hy_conv_b[i], hy_bias[i])

        branches = (w_branch_mla[i], w_branch_s5[i], w_branch_hy[i])
        x = x + mx[:, 2] * merge_branches(zx, (a_x, s_x, e_x), branches, w_out[i])
        x = x + mx[:, 5] * swiglu(modulate(rmsnorm(x, norm_ffn[i]), mx[:, 3], mx[:, 4]),
                                  ffn_w_gu[i], ffn_w_down[i])

        if not last:
            a_c = attend(mla_queries(zc, mla_q_norm[i], mla_w_uq[i], None), k_c, v_c).reshape(bsz, n_ctx, -1)
            s_c = s5_glu(y_c, s5_w_glu[i]).astype(ctx.dtype)
            e_c = hyena_mixer(zc[..., OFF_HY:OFF_GATE], hyena_filters(n_ctx, *fparams),
                              hy_conv_w[i], hy_conv_b[i], hy_bias[i])
            ctx = ctx + mc[2] * merge_branches(zc, (a_c, s_c, e_c), branches, w_out[i])
            ctx = ctx + mc[5] * swiglu(modulate(rmsnorm(ctx, norm_ffn[i]), mc[3], mc[4]),
                                       ffn_w_gu[i], ffn_w_down[i])
    return rmsnorm(x, final_norm)
```

```python
import functools
import math

import jax
import jax.numpy as jnp
import numpy as np
from jax import lax
from jax.experimental import pallas as pl
from jax.experimental.pallas import tpu as pltpu

F32 = jnp.float32
BF16 = jnp.bfloat16
HIGHEST = lax.Precision.HIGHEST

D_MODEL = 1024
GRID_W = 64
NORM_EPS = 1e-6
N_MOD = 6

MLA_HEADS = 8
MLA_NOPE = 64
MLA_ROPE = 32
MLA_V = 64
MLA_Q_LORA = 256
MLA_KV_LORA = 128
ROPE_BASE = 10000.0
HEAD_W = 128

S5_WIDTH = 512
S5_GROUP = 16
S5_GROUPS = 32
S5_STATE = 64
S5_HALF = 256
S5_HSTATE = 1024
S5_CHUNK = 128

HY_WIDTH = 512
HY_ORDER = 2
HY_BANDS = 16
HY_POS_DIM = 1 + 2 * HY_BANDS
HY_POS_PAD = 64
HY_FILTER_HIDDEN = 64
HY_FILTER_OUT = HY_ORDER * 2 * HY_WIDTH
HY_DECAY_TARGET = 1e-2
HY_DECAY_SHORT = 0.3
HY_DECAY_LONG = 1.5
HY_DECAY_SHIFT = 0.05

D_FF = 2816

OFF_CQ = 0
OFF_CKV = OFF_CQ + MLA_Q_LORA
OFF_KR = OFF_CKV + MLA_KV_LORA
OFF_S5 = OFF_KR + MLA_ROPE
OFF_HY = OFF_S5 + S5_WIDTH
OFF_GATE = OFF_HY + 3 * HY_WIDTH

Z_GATE = 0
Z_HY = 3072
Z_S5 = 4608
Z_MLA = 5120
Z_WIDTH = 5632

VMEM_LIMIT_V7X = 52 * 1024 * 1024


def _cparams(sem):
    return pltpu.CompilerParams(dimension_semantics=sem, vmem_limit_bytes=VMEM_LIMIT_V7X)


def _dot(a, b):
    return jnp.dot(a, b, preferred_element_type=F32)


def _rms(x, g):
    return x * lax.rsqrt(jnp.mean(x * x, axis=-1, keepdims=True) + NORM_EPS) * g


def _norm_mod(x, g, shift, scale):
    return _rms(x, g) * (1.0 + scale) + shift


def _proj_kernel(x_ref, g_ref, sh_ref, sc_ref, w_ref, o_ref, h_ref):
    @pl.when(pl.program_id(2) == 0)
    def _():
        h_ref[...] = _norm_mod(x_ref[0], g_ref[...], sh_ref[0], sc_ref[0]).astype(BF16)

    o_ref[0] = _dot(h_ref[...], w_ref[...])


def _proj_in(x, g, shift, scale, w):
    bsz, n, d = x.shape
    nz = w.shape[1]
    tm = min(n, 1024)
    tn = 512
    return pl.pallas_call(
        _proj_kernel,
        grid=(bsz, n // tm, nz // tn),
        in_specs=[
            pl.BlockSpec((1, tm, d), lambda b, i, j: (b, i, 0)),
            pl.BlockSpec((1, d), lambda b, i, j: (0, 0)),
            pl.BlockSpec((1, 1, d), lambda b, i, j: (b, 0, 0)),
            pl.BlockSpec((1, 1, d), lambda b, i, j: (b, 0, 0)),
            pl.BlockSpec((d, tn), lambda b, i, j: (0, j)),
        ],
        out_specs=pl.BlockSpec((1, tm, tn), lambda b, i, j: (b, i, j)),
        out_shape=jax.ShapeDtypeStruct((bsz, n, nz), F32),
        scratch_shapes=[pltpu.VMEM((tm, d), BF16)],
        compiler_params=_cparams(("parallel", "parallel", "arbitrary")),
        name="proj_in",
    )(x, g, shift, scale, w)


def _mla_prep_kernel(z_ref, qn_ref, kvn_ref, wq_ref, wqs_ref, wkv_ref, c_ref, s_ref, q_ref, k_ref, v_ref):
    z = z_ref[0]
    hq = _rms(z[:, :MLA_Q_LORA], qn_ref[...]).astype(BF16)
    hkv = _rms(z[:, MLA_Q_LORA:MLA_Q_LORA + MLA_KV_LORA], kvn_ref[...]).astype(BF16)
    krb = z[:, MLA_Q_LORA + MLA_KV_LORA:]
    qa = _dot(hq, wq_ref[...])
    qb = _dot(hq, wqs_ref[...])
    kv = _dot(hkv, wkv_ref[...])
    ck = c_ref[...]
    sn = s_ref[...]
    lane = lax.broadcasted_iota(jnp.int32, ck.shape, 1)
    cq = jnp.where(lane < MLA_ROPE, ck, 1.0)
    kr = krb * ck + pltpu.roll(krb, HEAD_W - MLA_ROPE // 2, 1) * sn
    scale = (MLA_NOPE + MLA_ROPE) ** -0.5
    for h in range(MLA_HEADS):
        sl = slice(h * HEAD_W, (h + 1) * HEAD_W)
        q_ref[0, h] = ((qa[:, sl] * cq + qb[:, sl] * sn) * scale).astype(BF16)
        kvh = kv[:, sl]
        k_ref[0, h] = jnp.where(lane < MLA_NOPE, kr, kvh).astype(BF16)
        v_ref[0, h] = kvh[:, :MLA_V].astype(BF16)


def _mla_prep(z, qn, kvn, wq, wqs, wkv, ctab, stab):
    bsz, n, _ = z.shape
    tm = min(n, 512)
    hw = MLA_HEADS * HEAD_W
    zb = Z_MLA // 512
    full = lambda shape: pl.BlockSpec(shape, lambda b, i: (0,) * len(shape))
    return pl.pallas_call(
        _mla_prep_kernel,
        grid=(bsz, n // tm),
        in_specs=[
            pl.BlockSpec((1, tm, 512), lambda b, i: (b, i, zb)),
            full((1, MLA_Q_LORA)),
            full((1, MLA_KV_LORA)),
            full((MLA_Q_LORA, hw)),
            full((MLA_Q_LORA, hw)),
            full((MLA_KV_LORA, hw)),
            pl.BlockSpec((tm, HEAD_W), lambda b, i: (i, 0)),
            pl.BlockSpec((tm, HEAD_W), lambda b, i: (i, 0)),
        ],
        out_specs=[
            pl.BlockSpec((1, MLA_HEADS, tm, HEAD_W), lambda b, i: (b, 0, i, 0)),
            pl.BlockSpec((1, MLA_HEADS, tm, HEAD_W), lambda b, i: (b, 0, i, 0)),
            pl.BlockSpec((1, MLA_HEADS, tm, MLA_V), lambda b, i: (b, 0, i, 0)),
        ],
        out_shape=[
            jax.ShapeDtypeStruct((bsz, MLA_HEADS, n, HEAD_W), BF16),
            jax.ShapeDtypeStruct((bsz, MLA_HEADS, n, HEAD_W), BF16),
            jax.ShapeDtypeStruct((bsz, MLA_HEADS, n, MLA_V), BF16),
        ],
        compiler_params=_cparams(("parallel", "parallel")),
        name="mla_prep",
    )(z, qn, kvn, wq, wqs, wkv, ctab, stab)


def _attn_kernel(q_ref, k_ref, v_ref, o_ref, *, tk, nkc):
    q = q_ref[0, 0]
    tq = q.shape[0]

    def body(j, carry):
        m, l, acc = carry
        off = pl.multiple_of(j * tk, tk)
        kc = k_ref[0, 0, pl.ds(off, tk), :]
        vc = v_ref[0, 0, pl.ds(off, tk), :]
        s = lax.dot_general(q, kc, (((1,), (1,)), ((), ())), preferred_element_type=F32)
        mn = jnp.maximum(m, jnp.max(s, axis=-1, keepdims=True))
        p = jnp.exp(s - mn)
        alpha = jnp.exp(m - mn)
        l = alpha * l + jnp.sum(p, axis=-1, keepdims=True)
        acc = alpha * acc + _dot(p.astype(BF16), vc)
        return mn, l, acc

    m0 = jnp.full((tq, 1), -1e30, F32)
    l0 = jnp.zeros((tq, 1), F32)
    a0 = jnp.zeros((tq, MLA_V), F32)
    _, l, acc = lax.fori_loop(0, nkc, body, (m0, l0, a0))
    o_ref[0, 0] = (acc / l).astype(BF16)


def _kv_chunk(nk):
    for tk in (1280, 1024, 640, 512, 256, 128):
        if nk % tk == 0:
            return tk
    raise ValueError(f"unsupported key count {nk}")


def _attention(q, k, v):
    bsz, nh, nq, _ = q.shape
    nk = k.shape[2]
    tq = min(nq, 256)
    tk = _kv_chunk(nk)
    return pl.pallas_call(
        functools.partial(_attn_kernel, tk=tk, nkc=nk // tk),
        grid=(bsz, nh, nq // tq),
        in_specs=[
            pl.BlockSpec((1, 1, tq, HEAD_W), lambda b, h, i: (b, h, i, 0)),
            pl.BlockSpec((1, 1, nk, HEAD_W), lambda b, h, i: (b, h, 0, 0)),
            pl.BlockSpec((1, 1, nk, MLA_V), lambda b, h, i: (b, h, 0, 0)),
        ],
        out_specs=pl.BlockSpec((1, 1, tq, MLA_V), lambda b, h, i: (b, h, i, 0)),
        out_shape=jax.ShapeDtypeStruct((bsz, nh, nq, MLA_V), BF16),
        compiler_params=_cparams(("parallel", "parallel", "arbitrary")),
        name="attention",
    )(q, k, v)


def _cmul(ar, ai, br, bi):
    return ar * br - ai * bi, ar * bi + ai * br


def _s5_kernel(u_ref, h0_ref, bm_ref, cm_ref, wn_ref, wp_ref, l1_ref, tri_ref, y_ref, hf_ref, carry_ref, *, tc, nc):
    d = pl.program_id(1)
    i = pl.program_id(2)

    @pl.when(i == 0)
    def _():
        carry_ref[...] = h0_ref[0, 0]

    u = u_ref[0]
    tri = tri_ref[0]
    ys = []
    for hf in range(2):
        ub = u[:, hf * S5_HALF:(hf + 1) * S5_HALF].astype(BF16)
        bu = _dot(ub, bm_ref[0, hf])
        xr, xi = _cmul(wn_ref[0, hf, 0], wn_ref[0, hf, 1], bu[:, :S5_HSTATE], bu[:, S5_HSTATE:])
        s = _dot(tri, jnp.concatenate([xr, xi], axis=1).astype(BF16))
        cr, ci = _cmul(l1_ref[0, hf, 0], l1_ref[0, hf, 1],
                       carry_ref[2 * hf:2 * hf + 1, :], carry_ref[2 * hf + 1:2 * hf + 2, :])
        hr, hi = _cmul(wp_ref[0, hf, 0], wp_ref[0, hf, 1], s[:, :S5_HSTATE] + cr, s[:, S5_HSTATE:] + ci)
        ys.append(_dot(jnp.concatenate([hr, hi], axis=1).astype(BF16), cm_ref[0, hf]))
        carry_ref[2 * hf:2 * hf + 1, :] = jnp.where(d == 0, hr[tc - 1:tc], hr[0:1])
        carry_ref[2 * hf + 1:2 * hf + 2, :] = jnp.where(d == 0, hi[tc - 1:tc], hi[0:1])
    y_ref[0, 0] = jnp.concatenate(ys, axis=1)

    @pl.when(i == nc - 1)
    def _():
        hf_ref[0, 0] = carry_ref[...]


def _s5_scan(z, h0, tabs):
    bm, cm, wn, wp, l1, tri = tabs
    bsz, n, _ = z.shape
    tc = S5_CHUNK
    nc = n // tc
    zb = Z_S5 // S5_WIDTH

    def chunk(d, i):
        return jnp.where(d == 0, i, nc - 1 - i)

    return pl.pallas_call(
        functools.partial(_s5_kernel, tc=tc, nc=nc),
        grid=(bsz, 2, nc),
        in_specs=[
            pl.BlockSpec((1, tc, S5_WIDTH), lambda b, d, i: (b, chunk(d, i), zb)),
            pl.BlockSpec((1, 1, 4, S5_HSTATE), lambda b, d, i: (b, d, 0, 0)),
            pl.BlockSpec((1, 2, S5_HALF, 2 * S5_HSTATE), lambda b, d, i: (d, 0, 0, 0)),
            pl.BlockSpec((1, 2, 2 * S5_HSTATE, S5_HALF), lambda b, d, i: (d, 0, 0, 0)),
            pl.BlockSpec((1, 2, 2, tc, S5_HSTATE), lambda b, d, i: (d, 0, 0, 0, 0)),
            pl.BlockSpec((1, 2, 2, tc, S5_HSTATE), lambda b, d, i: (d, 0, 0, 0, 0)),
            pl.BlockSpec((1, 2, 2, 1, S5_HSTATE), lambda b, d, i: (d, 0, 0, 0, 0)),
            pl.BlockSpec((1, tc, tc), lambda b, d, i: (d, 0, 0)),
        ],
        out_specs=[
            pl.BlockSpec((1, 1, tc, S5_WIDTH), lambda b, d, i: (d, b, chunk(d, i), 0)),
            pl.BlockSpec((1, 1, 4, S5_HSTATE), lambda b, d, i: (b, d, 0, 0)),
        ],
        out_shape=[
            jax.ShapeDtypeStruct((2, bsz, n, S5_WIDTH), F32),
            jax.ShapeDtypeStruct((bsz, 2, 4, S5_HSTATE), F32),
        ],
        scratch_shapes=[pltpu.VMEM((4, S5_HSTATE), F32)],
        compiler_params=_cparams(("parallel", "parallel", "arbitrary")),
        name="s5_scan",
    )(z, h0, bm, cm, wn, wp, l1, tri)


def _s5_tables(lam_re, lam_im, log_dt, b_re, b_im, c_re, c_im):
    tc = S5_CHUNK
    dt = jnp.exp(log_dt)[..., None]
    zr, zi = lam_re * dt, lam_im * dt
    mag = jnp.exp(zr)
    lbr, lbi = mag * jnp.cos(zi), mag * jnp.sin(zi)
    den = lam_re * lam_re + lam_im * lam_im
    nr, ni = lbr - 1.0, lbi
    cfr = (nr * lam_re + ni * lam_im) / den
    cfi = (ni * lam_re - nr * lam_im) / den
    bbr = cfr[..., None] * b_re - cfi[..., None] * b_im
    bbi = cfr[..., None] * b_im + cfi[..., None] * b_re
    eye = jnp.eye(S5_GROUP, dtype=F32)

    def blockdiag_in(b):
        b = b.reshape(2, 2, S5_GROUP, S5_STATE, S5_GROUP)
        return jnp.einsum('gk,dhgpn->dhgnkp', eye, b).reshape(2, 2, S5_HALF, S5_HSTATE)

    def blockdiag_out(c):
        c = c.reshape(2, 2, S5_GROUP, S5_GROUP, S5_STATE)
        return jnp.einsum('gk,dhgnp->dhgpkn', eye, c).reshape(2, 2, S5_HSTATE, S5_HALF)

    bm = jnp.concatenate([blockdiag_in(bbr), blockdiag_in(bbi)], axis=-1).astype(BF16)
    cm = jnp.concatenate([blockdiag_out(c_re), blockdiag_out(-c_im)], axis=-2).astype(BF16)

    def powers(k):
        zr_ = zr.reshape(2, 2, 1, S5_HSTATE)
        zi_ = zi.reshape(2, 2, 1, S5_HSTATE)
        kk = k[:, None, :, None]
        m = jnp.exp(kk * zr_)
        return jnp.stack([m * jnp.cos(kk * zi_), m * jnp.sin(kk * zi_)], axis=2)

    t = jnp.arange(tc, dtype=F32)
    wn = powers(jnp.stack([-t, -(tc - 1 - t)]))
    wp = powers(jnp.stack([t, tc - 1 - t]))
    l1 = powers(jnp.ones((2, 1), F32))
    r = jnp.arange(tc)
    tri = jnp.stack([r[:, None] >= r[None, :], r[:, None] <= r[None, :]]).astype(BF16)
    return bm, cm, wn, wp, l1, tri


def _sconv_kernel(z_ref, zp_ref, zn_ref, w_ref, b_ref, o_ref, *, nt):
    i = pl.program_id(1)
    z = z_ref[0]
    tm = z.shape[0]
    row = lax.broadcasted_iota(jnp.int32, z.shape, 0)
    prev = jnp.where(i > 0, zp_ref[0, 7:8, :], 0.0)
    nxt = jnp.where(i < nt - 1, zn_ref[0, 0:1, :], 0.0)
    up = jnp.where(row == 0, prev, pltpu.roll(z, 1, 0))
    dn = jnp.where(row == tm - 1, nxt, pltpu.roll(z, tm - 1, 0))
    o_ref[0] = up * w_ref[0:1, :] + z * w_ref[1:2, :] + dn * w_ref[2:3, :] + b_ref[...]


def _short_conv(z, w, b):
    bsz, n, _ = z.shape
    cw = 3 * HY_WIDTH
    tm = min(n, 512)
    nt = n // tm
    zb = Z_HY // cw
    r8 = tm // 8
    return pl.pallas_call(
        functools.partial(_sconv_kernel, nt=nt),
        grid=(bsz, nt),
        in_specs=[
            pl.BlockSpec((1, tm, cw), lambda b_, i: (b_, i, zb)),
            pl.BlockSpec((1, 8, cw), lambda b_, i: (b_, jnp.maximum(i * r8 - 1, 0), zb)),
            pl.BlockSpec((1, 8, cw), lambda b_, i: (b_, jnp.minimum((i + 1) * r8, n // 8 - 1), zb)),
            pl.BlockSpec((3, cw), lambda b_, i: (0, 0)),
            pl.BlockSpec((1, cw), lambda b_, i: (0, 0)),
        ],
        out_specs=pl.BlockSpec((1, tm, cw), lambda b_, i: (b_, i, 0)),
        out_shape=jax.ShapeDtypeStruct((bsz, n, cw), F32),
        compiler_params=_cparams(("parallel", "parallel")),
        name="short_conv",
    )(z, z, z, w, b)


def _filt_kernel(f_ref, w1_ref, b1_ref, w2_ref, b2_ref, w3_ref, fq_ref, dl_ref, o_ref, *, n_tok):
    z = f_ref[...]
    tm = z.shape[0]
    fq = fq_ref[...]
    hid = jnp.sin(fq * (jnp.dot(z, w1_ref[...], precision=HIGHEST, preferred_element_type=F32) + b1_ref[...]))
    hid = jnp.sin(fq * (jnp.dot(hid, w2_ref[...], precision=HIGHEST, preferred_element_type=F32) + b2_ref[...]))
    filt = jnp.dot(hid, w3_ref[...], precision=HIGHEST, preferred_element_type=F32)
    filt = filt * (jnp.exp(-z[:, 0:1] * dl_ref[...]) + HY_DECAY_SHIFT)
    m = pl.program_id(0) * tm + lax.broadcasted_iota(jnp.int32, (tm, HY_WIDTH), 0)
    for o in range(HY_ORDER):
        base = o * 2 * HY_WIDTH
        fwd = filt[:, base:base + HY_WIDTH]
        bwd = filt[:, base + HY_WIDTH:base + 2 * HY_WIDTH]
        o_ref[o] = jnp.where(m < n_tok, fwd, 0.0) + jnp.where((m > n_tok) | (m == 0), bwd, 0.0)


def _hyena_filter_feats(n_tok):
    m = jnp.arange(2 * n_tok)
    lag = jnp.where(m < n_tok, m, jnp.where(m > n_tok, 2 * n_tok - m, 0))
    t = jnp.linspace(0.0, 1.0, n_tok, dtype=F32)[lag][:, None]
    w = (2.0 * math.pi * lag.astype(F32) / n_tok)[:, None]
    bands = jnp.linspace(1e-4, HY_BANDS - 1, HY_BANDS, dtype=F32)[None, :]
    feats = jnp.concatenate([t, jnp.cos(bands * w), -jnp.sin(bands * w)], axis=-1)
    return jnp.pad(feats, ((0, 0), (0, HY_POS_PAD - HY_POS_DIM)))


def _hyena_filters(n_tok, w1, b1, w2, b2, w3, freq):
    feats = _hyena_filter_feats(n_tok)
    deltas = jnp.abs(jnp.linspace(math.log(HY_DECAY_TARGET) / HY_DECAY_SHORT,
                                  math.log(HY_DECAY_TARGET) / HY_DECAY_LONG, HY_FILTER_OUT, dtype=F32))[None, :]
    w1p = jnp.pad(w1, ((0, HY_POS_PAD - HY_POS_DIM), (0, 0)))
    n2 = 2 * n_tok
    tm = min(n2, 512)
    full = lambda shape: pl.BlockSpec(shape, lambda i: (0,) * len(shape))
    hh = HY_FILTER_HIDDEN
    return pl.pallas_call(
        functools.partial(_filt_kernel, n_tok=n_tok),
        grid=(n2 // tm,),
        in_specs=[
            pl.BlockSpec((tm, HY_POS_PAD), lambda i: (i, 0)),
            full((HY_POS_PAD, hh)), full((1, hh)), full((hh, hh)), full((1, hh)),
            full((hh, HY_FILTER_OUT)), full((1, hh)), full((1, HY_FILTER_OUT)),
        ],
        out_specs=pl.BlockSpec((HY_ORDER, tm, HY_WIDTH), lambda i: (0, i, 0)),
        out_shape=jax.ShapeDtypeStruct((HY_ORDER, n2, HY_WIDTH), F32),
        compiler_params=_cparams(("parallel",)),
        name="hyena_filter",
    )(feats, w1p, b1[None], w2, b2[None], w3, freq[None], deltas)


def _fft_split(n_fft):
    if n_fft <= 1024:
        return n_fft, 1
    n1 = 1 << (int(math.log2(n_fft)) // 2)
    return n1, n_fft // n1


def _cis(num, den, sign):
    ang = (2.0 * math.pi / den) * num.astype(F32)
    return jnp.cos(ang), sign * jnp.sin(ang)


def _fft_tables(n_fft):
    n1, n2 = _fft_split(n_fft)
    a = jnp.arange(n1)
    f1r, f1i = _cis((a[:, None] * a[None, :]) % n1, n1, -1.0)
    s1 = jnp.concatenate([f1r, f1i], axis=0).astype(BF16)
    half = n1 // 2
    s3 = (jnp.concatenate([f1r[:half], -f1i[:half]], axis=0) / n_fft).astype(BF16)
    if n2 == 1:
        return s1, s3, None, None
    c = jnp.arange(n1)[:, None, None]
    dd = jnp.arange(n2)[None, :, None]
    bb = jnp.arange(n2)[None, None, :]
    fr, fi = _cis((bb * (c + n1 * dd)) % n_fft, n_fft, -1.0)
    ft = jnp.concatenate([fr, fi], axis=1).astype(BF16)
    frt, fit = jnp.swapaxes(fr, 1, 2), jnp.swapaxes(fi, 1, 2)
    it = jnp.concatenate([frt, -fit], axis=1).astype(BF16)
    return s1, s3, ft, it


def _stacked_cdot(s, xr, xi, m):
    p = _dot(s, xr.astype(BF16))
    if xi is None:
        return p[:m], p[m:]
    q = _dot(s, xi.astype(BF16))
    return p[:m] - q[m:], q[:m] + p[m:]


def _fft_a_kernel(*refs, parts, n1, has_g):
    if has_g:
        x_ref, s_ref, g_ref, y_ref = refs
    else:
        x_ref, s_ref, y_ref = refs
    yr, yi = _stacked_cdot(s_ref[...], x_ref[0], x_ref[1] if parts == 2 else None, n1)
    if has_g:
        yr, yi = _cmul(yr, yi, g_ref[0], g_ref[1])
    y_ref[0] = yr
    y_ref[1] = yi


def _fft_a(x, s1, g=None):
    parts, a_rows, cols = x.shape
    n1 = s1.shape[0] // 2
    tc = min(cols, 2048)
    in_specs = [
        pl.BlockSpec((parts, a_rows, tc), lambda j: (0, 0, j)),
        pl.BlockSpec((2 * n1, a_rows), lambda j: (0, 0)),
    ]
    args = [x, s1[:, :a_rows]]
    if g is not None:
        in_specs.append(pl.BlockSpec((2, n1, tc), lambda j: (0, 0, j)))
        args.append(g)
    return pl.pallas_call(
        functools.partial(_fft_a_kernel, parts=parts, n1=n1, has_g=g is not None),
        grid=(cols // tc,),
        in_specs=in_specs,
        out_specs=pl.BlockSpec((2, n1, tc), lambda j: (0, 0, j)),
        out_shape=jax.ShapeDtypeStruct((2, n1, cols), F32),
        compiler_params=_cparams(("parallel",)),
        name="fft_stage_a",
    )(*args)


def _fft_b_kernel(*refs, n2, inverse):
    if inverse:
        y_ref, ft_ref, it_ref, g_ref, o_ref = refs
    else:
        y_ref, ft_ref, o_ref = refs
    xr, xi = _stacked_cdot(ft_ref[0], y_ref[0, 0], y_ref[1, 0], n2)
    if inverse:
        zr, zi = _cmul(xr, xi, g_ref[0, 0], g_ref[1, 0])
        xr, xi = _stacked_cdot(it_ref[0], zr, zi, n2)
    o_ref[0, 0] = xr
    o_ref[1, 0] = xi


def _fft_b(y, ft, it=None, g=None):
    _, n1, n2, ch = y.shape
    inverse = it is not None
    blk = pl.BlockSpec((2, 1, n2, ch), lambda c: (0, c, 0, 0))
    mat = pl.BlockSpec((1, 2 * n2, n2), lambda c: (c, 0, 0))
    in_specs, args = [blk, mat], [y, ft]
    if inverse:
        in_specs += [mat, blk]
        args += [it, g]
    return pl.pallas_call(
        functools.partial(_fft_b_kernel, n2=n2, inverse=inverse),
        grid=(n1,),
        in_specs=in_specs,
        out_specs=blk,
        out_shape=jax.ShapeDtypeStruct(y.shape, F32),
        compiler_params=_cparams(("parallel",)),
        name="fft_stage_b",
    )(*args)


def _fft_c_kernel(y_ref, s_ref, x_ref, g_ref, b_ref, o_ref, *, half):
    cr, ci = _stacked_cdot(s_ref[...], y_ref[0], y_ref[1], half)
    bias = b_ref[...]
    o_ref[0] = g_ref[0] * (cr + bias * x_ref[0])
    o_ref[1] = g_ref[1] * (ci + bias * x_ref[1])


def _fft_c(y, s3, x, gate, bias_cols):
    _, n1, cols = y.shape
    half = n1 // 2
    tc = min(cols, 2048)
    io = pl.BlockSpec((2, half, tc), lambda j: (0, 0, j))
    return pl.pallas_call(
        functools.partial(_fft_c_kernel, half=half),
        grid=(cols // tc,),
        in_specs=[
            pl.BlockSpec((2, n1, tc), lambda j: (0, 0, j)),
            pl.BlockSpec((n1, n1), lambda j: (0, 0)),
            io, io,
            pl.BlockSpec((1, tc), lambda j: (0, j)),
        ],
        out_specs=io,
        out_shape=jax.ShapeDtypeStruct((2, half, cols), F32),
        compiler_params=_cparams(("parallel",)),
        name="fft_stage_c",
    )(y, s3, x, gate, bias_cols)


def _hyena_mixer(u3, gfilt, bias):
    bsz, n, _ = u3.shape
    assert bsz == 2, "the complex packing of the long convolution pairs exactly two batch rows"
    ch = HY_WIDTH
    n_fft = 2 * n
    n1, n2 = _fft_split(n_fft)
    s1, s3, ft, it = _fft_tables(n_fft)
    cols = n2 * ch
    y = u3[:, :, :ch].reshape(2, n1 // 2, cols)
    for o in range(HY_ORDER):
        gate = u3[:, :, (o + 1) * ch:(o + 2) * ch].reshape(2, n1 // 2, cols)
        gspec = _fft_a(gfilt[o].reshape(1, n1, cols), s1)
        bias_cols = jnp.tile(bias[o], n2)[None, :]
        if n2 == 1:
            spec = _fft_a(y, s1, gspec)
        else:
            gspec = _fft_b(gspec.reshape(2, n1, n2, ch), ft)
            spec = _fft_b(_fft_a(y, s1).reshape(2, n1, n2, ch), ft, it, gspec).reshape(2, n1, cols)
        y = _fft_c(spec, s3, y, gate, bias_cols)
    return y.reshape(2, n, ch)


def _gelu_tanh(x):
    return 0.5 * x * (1.0 + jnp.tanh(math.sqrt(2.0 / math.pi) * (x + 0.044715 * (x * x * x))))


def _merge_kernel(x_ref, g0_ref, g1_ref, g2_ref, a_ref, yf_ref, yb_ref, u_ref, e_ref, d_ref,
                  wglu_ref, wm_ref, ws_ref, wh_ref, wo_ref, gt_ref, o_ref):
    att = _dot(a_ref[0, 0], wm_ref[0])
    for h in range(1, MLA_HEADS):
        att = att + _dot(a_ref[0, h], wm_ref[h])
    y = yf_ref[0, 0] + yb_ref[0, 0] + d_ref[...] * u_ref[0]
    gy = _gelu_tanh(y)
    s5 = gy * jax.nn.sigmoid(_dot(gy.astype(BF16), wglu_ref[...]))
    s5 = _dot(s5.astype(BF16), ws_ref[...])
    hy = _dot(e_ref[0].astype(BF16), wh_ref[...])
    merged = jax.nn.sigmoid(g0_ref[0]) * att
    merged = merged + jax.nn.sigmoid(g1_ref[0]) * s5
    merged = merged + jax.nn.sigmoid(g2_ref[0]) * hy
    o_ref[0] = x_ref[0] + gt_ref[0] * _dot(merged.astype(BF16), wo_ref[...])


def _merge(x, z, att, y5, e, s5_d, w_glu, w_mla, w_s5, w_hy, w_out, gate):
    bsz, n, d = x.shape
    tm = min(n, 256)
    full = lambda shape: pl.BlockSpec(shape, lambda b, i: (0,) * len(shape))
    zs5 = Z_S5 // S5_WIDTH
    return pl.pallas_call(
        _merge_kernel,
        grid=(bsz, n // tm),
        in_specs=[
            pl.BlockSpec((1, tm, d), lambda b, i: (b, i, 0)),
            pl.BlockSpec((1, tm, d), lambda b, i: (b, i, 0)),
            pl.BlockSpec((1, tm, d), lambda b, i: (b, i, 1)),
            pl.BlockSpec((1, tm, d), lambda b, i: (b, i, 2)),
            pl.BlockSpec((1, MLA_HEADS, tm, MLA_V), lambda b, i: (b, 0, i, 0)),
            pl.BlockSpec((1, 1, tm, S5_WIDTH), lambda b, i: (0, b, i, 0)),
            pl.BlockSpec((1, 1, tm, S5_WIDTH), lambda b, i: (1, b, i, 0)),
            pl.BlockSpec((1, tm, S5_WIDTH), lambda b, i: (b, i, zs5)),
            pl.BlockSpec((1, tm, HY_WIDTH), lambda b, i: (b, i, 0)),
            full((1, S5_WIDTH)),
            full((S5_WIDTH, S5_WIDTH)),
            full((MLA_HEADS, MLA_V, d)),
            full((S5_WIDTH, d)),
            full((HY_WIDTH, d)),
            full((d, d)),
            pl.BlockSpec((1, 1, d), lambda b, i: (b, 0, 0)),
        ],
        out_specs=pl.BlockSpec((1, tm, d), lambda b, i: (b, i, 0)),
        out_shape=jax.ShapeDtypeStruct((bsz, n, d), F32),
        compiler_params=_cparams(("parallel", "parallel")),
        name="merge",
    )(x, z, z, z, att, y5, y5, z, e, s5_d, w_glu, w_mla, w_s5, w_hy, w_out, gate)


def _ffn_kernel(x_ref, g_ref, sh_ref, sc_ref, gt_ref, wg_ref, wu_ref, wd_ref, fg_ref, o_ref, h_ref, acc_ref,
                *, nk, final):
    k = pl.program_id(2)

    @pl.when(k == 0)
    def _():
        h_ref[...] = _norm_mod(x_ref[0], g_ref[...], sh_ref[0], sc_ref[0]).astype(BF16)
        acc_ref[...] = jnp.zeros_like(acc_ref)

    h = h_ref[...]
    act = jax.nn.silu(_dot(h, wg_ref[...])) * _dot(h, wu_ref[...])
    acc_ref[...] += _dot(act.astype(BF16), wd_ref[...])

    @pl.when(k == nk - 1)
    def _():
        r = x_ref[0] + gt_ref[0] * acc_ref[...]
        o_ref[0] = _rms(r, fg_ref[...]) if final else r


def _ffn(x, g, shift, scale, gate, w_g, w_u, w_d, final_g, final):
    bsz, n, d = x.shape
    dff = w_g.shape[1]
    tm = min(n, 1024)
    tf = 256
    nk = dff // tf
    vec = pl.BlockSpec((1, 1, d), lambda b, i, k: (b, 0, 0))
    row = pl.BlockSpec((1, d), lambda b, i, k: (0, 0))
    return pl.pallas_call(
        functools.partial(_ffn_kernel, nk=nk, final=final),
        grid=(bsz, n // tm, nk),
        in_specs=[
            pl.BlockSpec((1, tm, d), lambda b, i, k: (b, i, 0)),
            row, vec, vec, vec,
            pl.BlockSpec((d, tf), lambda b, i, k: (0, k)),
            pl.BlockSpec((d, tf), lambda b, i, k: (0, k)),
            pl.BlockSpec((tf, d), lambda b, i, k: (k, 0)),
            row,
        ],
        out_specs=pl.BlockSpec((1, tm, d), lambda b, i, k: (b, i, 0)),
        out_shape=jax.ShapeDtypeStruct((bsz, n, d), F32),
        scratch_shapes=[pltpu.VMEM((tm, d), BF16), pltpu.VMEM((tm, d), F32)],
        compiler_params=_cparams(("parallel", "parallel", "arbitrary")),
        name="ffn",
    )(x, g, shift, scale, gate, w_g, w_u, w_d, final_g)


def _rope_tables(n_tok):
    rows = n_tok // GRID_W
    row = jnp.broadcast_to(jnp.arange(rows, dtype=F32)[:, None], (rows, GRID_W)).reshape(-1)
    col = jnp.broadcast_to(jnp.arange(GRID_W, dtype=F32)[None, :], (rows, GRID_W)).reshape(-1)
    n_freq = MLA_ROPE // 4
    inv = ROPE_BASE ** (-jnp.arange(n_freq, dtype=F32) / n_freq)
    ang = jnp.concatenate([row[:, None] * inv, col[:, None] * inv], axis=-1)
    cos, sin = jnp.cos(ang), jnp.sin(ang)
    pad = jnp.zeros((n_tok, HEAD_W - MLA_ROPE), F32)
    return jnp.concatenate([cos, cos, pad], axis=-1), jnp.concatenate([-sin, sin, pad], axis=-1)


def _identity_rope_tables(n_tok):
    one = jnp.ones((n_tok, MLA_ROPE), F32)
    pad = jnp.zeros((n_tok, HEAD_W - MLA_ROPE), F32)
    return jnp.concatenate([one, pad], axis=-1), jnp.zeros((n_tok, HEAD_W), F32)


def _layout_w_in(w):
    kr = w[:, OFF_KR:OFF_S5]
    x1, x2 = kr[:, 0::2], kr[:, 1::2]
    pad = jnp.zeros((w.shape[0], HEAD_W - 3 * (MLA_ROPE // 2)), w.dtype)
    return jnp.concatenate([w[:, OFF_GATE:], w[:, OFF_HY:OFF_GATE], w[:, OFF_S5:OFF_HY],
                            w[:, OFF_CQ:OFF_CKV], w[:, OFF_CKV:OFF_KR], x1, x2, x1, pad], axis=1).astype(BF16)


def _layout_w_uq(w):
    w = w.reshape(MLA_Q_LORA, MLA_HEADS, MLA_NOPE + MLA_ROPE)
    nope, rope = w[..., :MLA_NOPE], w[..., MLA_NOPE:]
    x1, x2 = rope[..., 0::2], rope[..., 1::2]
    z32 = jnp.zeros_like(rope)
    wq = jnp.concatenate([x1, x2, z32, nope], axis=-1)
    wqs = jnp.concatenate([x2, x1, z32, jnp.zeros_like(nope)], axis=-1)
    shape = (MLA_Q_LORA, MLA_HEADS * HEAD_W)
    return wq.reshape(shape).astype(BF16), wqs.reshape(shape).astype(BF16)


def _layout_w_ukv(w):
    w = w.reshape(MLA_KV_LORA, MLA_HEADS, MLA_NOPE + MLA_V)
    return jnp.concatenate([w[..., MLA_NOPE:], w[..., :MLA_NOPE]], axis=-1).reshape(
        MLA_KV_LORA, MLA_HEADS * HEAD_W).astype(BF16)


def kernel(x, c, ctx, c_ctx, ada_w, ada_b, norm_mix, w_in, mla_q_norm, mla_w_uq, mla_kv_norm, mla_w_ukv,
           s5_lam_re, s5_lam_im, s5_log_dt, s5_b_re, s5_b_im, s5_c_re, s5_c_im, s5_d, s5_w_glu,
           hy_conv_w, hy_conv_b, hy_f_w1, hy_f_b1, hy_f_w2, hy_f_b2, hy_f_w3, hy_f_freq, hy_bias,
           w_branch_mla, w_branch_s5, w_branch_hy, w_out, norm_ffn, ffn_w_gu, ffn_w_down, final_norm):
    bsz, n_tok, d = x.shape
    n_ctx = ctx.shape[1]
    depth = ada_w.shape[0]
    rope_x = _rope_tables(n_tok)
    rope_c = _identity_rope_tables(n_ctx)
    h_zero = jnp.zeros((bsz, 2, 4, S5_HSTATE), F32)

    for i in range(depth):
        last = i == depth - 1
        mx = (jax.nn.silu(c) @ ada_w[i] + ada_b[i]).reshape(bsz, N_MOD, 1, d)
        mc = jnp.broadcast_to((jax.nn.silu(c_ctx) @ ada_w[i] + ada_b[i]).reshape(1, N_MOD, 1, d),
                              (bsz, N_MOD, 1, d))
        w_z = _layout_w_in(w_in[i])
        wq, wqs = _layout_w_uq(mla_w_uq[i])
        wkv = _layout_w_ukv(mla_w_ukv[i])
        qn, kvn = mla_q_norm[i][None], mla_kv_norm[i][None]
        s5_tabs = _s5_tables(s5_lam_re[i], s5_lam_im[i], s5_log_dt[i], s5_b_re[i], s5_b_im[i],
                             s5_c_re[i], s5_c_im[i])
        fparams = (hy_f_w1[i], hy_f_b1[i], hy_f_w2[i], hy_f_b2[i], hy_f_w3[i], hy_f_freq[i])
        merge_w = (s5_d[i][None], s5_w_glu[i].astype(BF16),
                   w_branch_mla[i].reshape(MLA_HEADS, MLA_V, d).astype(BF16),
                   w_branch_s5[i].astype(BF16), w_branch_hy[i].astype(BF16), w_out[i].astype(BF16))
        w_g = ffn_w_gu[i][:, :D_FF].astype(BF16)
        w_u = ffn_w_gu[i][:, D_FF:].astype(BF16)
        w_d = ffn_w_down[i].astype(BF16)
        g_mix, g_ffn = norm_mix[i][None], norm_ffn[i][None]

        zx = _proj_in(x, g_mix, mx[:, 0], mx[:, 1], w_z)
        zc = _proj_in(ctx, g_mix, mc[:, 0], mc[:, 1], w_z)

        q_c, k_c, v_c = _mla_prep(zc, qn, kvn, wq, wqs, wkv, *rope_c)
        q_x, k_x, v_x = _mla_prep(zx, qn, kvn, wq, wqs, wkv, *rope_x)
        a_x = _attention(q_x, jnp.concatenate([k_c, k_x], axis=2), jnp.concatenate([v_c, v_x], axis=2))

        y5_c, finals = _s5_scan(zc, h_zero, s5_tabs)
        y5_x, _ = _s5_scan(zx, finals, s5_tabs)

        e_x = _hyena_mixer(_short_conv(zx, hy_conv_w[i], hy_conv_b[i][None]),
                           _hyena_filters(n_tok, *fparams), hy_bias[i])

        x = _merge(x, zx, a_x, y5_x, e_x, *merge_w, mx[:, 2])
        x = _ffn(x, g_ffn, mx[:, 3], mx[:, 4], mx[:, 5], w_g, w_u, w_d, final_norm[None], last)

        if not last:
            a_c = _attention(q_c, k_c, v_c)
            e_c = _hyena_mixer(_short_conv(zc, hy_conv_w[i], hy_conv_b[i][None]),
                               _hyena_filters(n_ctx, *fparams), hy_bias[i])
            ctx = _merge(ctx, zc, a_c, y5_c, e_c, *merge_w, mc[:, 2])
            ctx = _ffn(ctx, g_ffn, mc[:, 3], mc[:, 4], mc[:, 5], w_g, w_u, w_d, final_norm[None], False)
    return x
```

```python
import functools
import math

import jax
import jax.numpy as jnp
import numpy as np
from jax import lax
from jax.experimental import pallas as pl
from jax.experimental.pallas import tpu as pltpu

F32 = jnp.float32
BF16 = jnp.bfloat16
HIGHEST = lax.Precision.HIGHEST

D_MODEL = 1024
GRID_W = 64
NORM_EPS = 1e-6
N_MOD = 6

MLA_HEADS = 8
MLA_NOPE = 64
MLA_ROPE = 32
MLA_V = 64
MLA_Q_LORA = 256
MLA_KV_LORA = 128
ROPE_BASE = 10000.0
HEAD_W = 128
QK_SCALE_LOG2 = (MLA_NOPE + MLA_ROPE) ** -0.5 * math.log2(math.e)

S5_WIDTH = 512
S5_GROUP = 16
S5_GROUPS = 32
S5_STATE = 64
S5_HALF = 256
S5_HSTATE = 1024
S5_CHUNK = 128

HY_WIDTH = 512
HY_ORDER = 2
HY_BANDS = 16
HY_POS_DIM = 1 + 2 * HY_BANDS
HY_POS_PAD = 64
HY_FILTER_HIDDEN = 64
HY_FILTER_OUT = HY_ORDER * 2 * HY_WIDTH
HY_DECAY_TARGET = 1e-2
HY_DECAY_SHORT = 0.3
HY_DECAY_LONG = 1.5
HY_DECAY_SHIFT = 0.05

D_FF = 2816

OFF_CQ = 0
OFF_CKV = OFF_CQ + MLA_Q_LORA
OFF_KR = OFF_CKV + MLA_KV_LORA
OFF_S5 = OFF_KR + MLA_ROPE
OFF_HY = OFF_S5 + S5_WIDTH
OFF_GATE = OFF_HY + 3 * HY_WIDTH

Z_GATE = 0
Z_HY = 3072
Z_S5 = 4608
Z_MLA = 5120
Z_WIDTH = 5632

VMEM_LIMIT_V7X = 52 * 1024 * 1024


def _cparams(sem):
    return pltpu.CompilerParams(dimension_semantics=sem, vmem_limit_bytes=VMEM_LIMIT_V7X)


def _dot(a, b):
    return jnp.dot(a, b, preferred_element_type=F32)


def _rms(x, g):
    return x * lax.rsqrt(jnp.mean(x * x, axis=-1, keepdims=True) + NORM_EPS) * g


def _norm_mod(x, g, shift, scale):
    return _rms(x, g) * (1.0 + scale) + shift


def _proj_kernel(x_ref, g_ref, sh_ref, sc_ref, w_ref, o_ref, h_ref):
    @pl.when(pl.program_id(2) == 0)
    def _():
        h_ref[...] = _norm_mod(x_ref[0], g_ref[...], sh_ref[0], sc_ref[0]).astype(BF16)

    o_ref[0] = _dot(h_ref[...], w_ref[...])


def _proj_in(x, g, shift, scale, w):
    bsz, n, d = x.shape
    nz = w.shape[1]
    tm = min(n, 1024)
    tn = 512
    return pl.pallas_call(
        _proj_kernel,
        grid=(bsz, n // tm, nz // tn),
        in_specs=[
            pl.BlockSpec((1, tm, d), lambda b, i, j: (b, i, 0)),
            pl.BlockSpec((1, d), lambda b, i, j: (0, 0)),
            pl.BlockSpec((1, 1, d), lambda b, i, j: (b, 0, 0)),
            pl.BlockSpec((1, 1, d), lambda b, i, j: (b, 0, 0)),
            pl.BlockSpec((d, tn), lambda b, i, j: (0, j)),
        ],
        out_specs=pl.BlockSpec((1, tm, tn), lambda b, i, j: (b, i, j)),
        out_shape=jax.ShapeDtypeStruct((bsz, n, nz), F32),
        scratch_shapes=[pltpu.VMEM((tm, d), BF16)],
        compiler_params=_cparams(("parallel", "parallel", "arbitrary")),
        name="proj_in",
    )(x, g, shift, scale, w)


def _mla_prep_kernel(z_ref, qn_ref, kvn_ref, wq_ref, wqs_ref, wk_ref, wvt_ref, c_ref, s_ref, q_ref, k_ref, vt_ref):
    z = z_ref[0]
    hq = _rms(z[:, :MLA_Q_LORA], qn_ref[...]).astype(BF16)
    hkv = _rms(z[:, MLA_Q_LORA:MLA_Q_LORA + MLA_KV_LORA], kvn_ref[...]).astype(BF16)
    krb = z[:, MLA_Q_LORA + MLA_KV_LORA:]
    qa = _dot(hq, wq_ref[...])
    qb = _dot(hq, wqs_ref[...])
    kn = _dot(hkv, wk_ref[...])
    vt = lax.dot_general(wvt_ref[...], hkv, (((1,), (1,)), ((), ())), preferred_element_type=F32)
    ck = c_ref[...]
    sn = s_ref[...]
    lane = lax.broadcasted_iota(jnp.int32, ck.shape, 1)
    cq = jnp.where(lane < MLA_ROPE, ck, 1.0)
    kr = krb * ck + pltpu.roll(krb, HEAD_W - MLA_ROPE // 2, 1) * sn
    row = lax.broadcasted_iota(jnp.int32, (HEAD_W, z.shape[0]), 0)
    for h in range(MLA_HEADS):
        sl = slice(h * HEAD_W, (h + 1) * HEAD_W)
        q_ref[0, h] = ((qa[:, sl] * cq + qb[:, sl] * sn) * QK_SCALE_LOG2).astype(BF16)
        k_ref[0, h] = (kr + kn[:, sl]).astype(BF16)
        vt_ref[0, h] = jnp.where(row == MLA_V, 1.0, vt[sl, :]).astype(BF16)


def _mla_prep(z, qn, kvn, wq, wqs, wk, wvt, ctab, stab):
    bsz, n, _ = z.shape
    tm = min(n, 512)
    hw = MLA_HEADS * HEAD_W
    zb = Z_MLA // 512
    full = lambda shape: pl.BlockSpec(shape, lambda b, i: (0,) * len(shape))
    return pl.pallas_call(
        _mla_prep_kernel,
        grid=(bsz, n // tm),
        in_specs=[
            pl.BlockSpec((1, tm, 512), lambda b, i: (b, i, zb)),
            full((1, MLA_Q_LORA)),
            full((1, MLA_KV_LORA)),
            full((MLA_Q_LORA, hw)),
            full((MLA_Q_LORA, hw)),
            full((MLA_KV_LORA, hw)),
            full((hw, MLA_KV_LORA)),
            pl.BlockSpec((tm, HEAD_W), lambda b, i: (i, 0)),
            pl.BlockSpec((tm, HEAD_W), lambda b, i: (i, 0)),
        ],
        out_specs=[
            pl.BlockSpec((1, MLA_HEADS, tm, HEAD_W), lambda b, i: (b, 0, i, 0)),
            pl.BlockSpec((1, MLA_HEADS, tm, HEAD_W), lambda b, i: (b, 0, i, 0)),
            pl.BlockSpec((1, MLA_HEADS, HEAD_W, tm), lambda b, i: (b, 0, 0, i)),
        ],
        out_shape=[
            jax.ShapeDtypeStruct((bsz, MLA_HEADS, n, HEAD_W), BF16),
            jax.ShapeDtypeStruct((bsz, MLA_HEADS, n, HEAD_W), BF16),
            jax.ShapeDtypeStruct((bsz, MLA_HEADS, HEAD_W, n), BF16),
        ],
        compiler_params=_cparams(("parallel", "parallel")),
        name="mla_prep",
    )(z, qn, kvn, wq, wqs, wk, wvt, ctab, stab)


def _attn_kernel(q_ref, k_ref, vt_ref, o_ref, s0_ref, s1_ref, *, tk, nkc):
    q = q_ref[0, 0]
    tq = q.shape[0]

    def scores(j, s_ref):
        off = pl.multiple_of(j * tk, tk)
        s = lax.dot_general(k_ref[0, 0, pl.ds(off, tk), :], q, (((1,), (1,)), ((), ())),
                            preferred_element_type=F32)
        s_ref[...] = s
        return jnp.max(s, axis=0, keepdims=True)

    def update(j, s_ref, cmax, m, acc):
        mn = jnp.maximum(m, cmax)
        p = jnp.exp2(s_ref[...] - mn).astype(BF16)
        off = pl.multiple_of(j * tk, tk)
        return mn, jnp.exp2(m - mn) * acc + _dot(vt_ref[0, 0, :, pl.ds(off, tk)], p)

    def body(t, carry):
        cm0, m, acc = carry
        cm1 = scores(2 * t + 1, s1_ref)
        m, acc = update(2 * t, s0_ref, cm0, m, acc)
        cm0 = scores(2 * t + 2, s0_ref)
        m, acc = update(2 * t + 1, s1_ref, cm1, m, acc)
        return cm0, m, acc

    npairs = (nkc - 1) // 2
    m0 = jnp.full((1, tq), -1e30, F32)
    a0 = jnp.zeros((HEAD_W, tq), F32)
    cm0, m, acc = lax.fori_loop(0, npairs, body, (scores(0, s0_ref), m0, a0))
    j = 2 * npairs
    two_left = nkc - j == 2
    if two_left:
        cm1 = scores(j + 1, s1_ref)
    m, acc = update(j, s0_ref, cm0, m, acc)
    if two_left:
        m, acc = update(j + 1, s1_ref, cm1, m, acc)
    out = (acc * (1.0 / acc[MLA_V:MLA_V + 1, :])).T
    o_ref[0, 0] = out[:, :MLA_V].astype(BF16)


def _kv_chunk(nk):
    for tk in (1280, 1024, 640, 512, 256, 128):
        if nk % tk == 0:
            return tk
    raise ValueError(f"unsupported key count {nk}")


def _attention(q, k, vt):
    bsz, nh, nq, _ = q.shape
    nk = k.shape[2]
    tq = min(nq, 256)
    tk = _kv_chunk(nk)
    return pl.pallas_call(
        functools.partial(_attn_kernel, tk=tk, nkc=nk // tk),
        grid=(bsz, nh, nq // tq),
        in_specs=[
            pl.BlockSpec((1, 1, tq, HEAD_W), lambda b, h, i: (b, h, i, 0)),
            pl.BlockSpec((1, 1, nk, HEAD_W), lambda b, h, i: (b, h, 0, 0)),
            pl.BlockSpec((1, 1, HEAD_W, nk), lambda b, h, i: (b, h, 0, 0)),
        ],
        out_specs=pl.BlockSpec((1, 1, tq, MLA_V), lambda b, h, i: (b, h, i, 0)),
        out_shape=jax.ShapeDtypeStruct((bsz, nh, nq, MLA_V), BF16),
        scratch_shapes=[pltpu.VMEM((tk, tq), F32), pltpu.VMEM((tk, tq), F32)],
        compiler_params=_cparams(("parallel", "parallel", "arbitrary")),
        name="attention",
    )(q, k, vt)


def _cmul(ar, ai, br, bi):
    return ar * br - ai * bi, ar * bi + ai * br


def _s5_kernel(u_ref, h0_ref, bm_ref, cm_ref, wn_ref, wp_ref, l1_ref, tri_ref, y_ref, hf_ref, carry_ref, *, tc, nc):
    d = pl.program_id(1)
    i = pl.program_id(2)

    @pl.when(i == 0)
    def _():
        carry_ref[...] = h0_ref[0, 0]

    u = u_ref[0]
    tri = tri_ref[0]
    ys = []
    for hf in range(2):
        ub = u[:, hf * S5_HALF:(hf + 1) * S5_HALF].astype(BF16)
        bu = _dot(ub, bm_ref[0, hf])
        xr, xi = _cmul(wn_ref[0, hf, 0], wn_ref[0, hf, 1], bu[:, :S5_HSTATE], bu[:, S5_HSTATE:])
        s = _dot(tri, jnp.concatenate([xr, xi], axis=1).astype(BF16))
        cr, ci = _cmul(l1_ref[0, hf, 0], l1_ref[0, hf, 1],
                       carry_ref[2 * hf:2 * hf + 1, :], carry_ref[2 * hf + 1:2 * hf + 2, :])
        hr, hi = _cmul(wp_ref[0, hf, 0], wp_ref[0, hf, 1], s[:, :S5_HSTATE] + cr, s[:, S5_HSTATE:] + ci)
        ys.append(_dot(jnp.concatenate([hr, hi], axis=1).astype(BF16), cm_ref[0, hf]))
        carry_ref[2 * hf:2 * hf + 1, :] = jnp.where(d == 0, hr[tc - 1:tc], hr[0:1])
        carry_ref[2 * hf + 1:2 * hf + 2, :] = jnp.where(d == 0, hi[tc - 1:tc], hi[0:1])
    y_ref[0, 0] = jnp.concatenate(ys, axis=1)

    @pl.when(i == nc - 1)
    def _():
        hf_ref[0, 0] = carry_ref[...]


def _s5_scan(z, h0, tabs):
    bm, cm, wn, wp, l1, tri = tabs
    bsz, n, _ = z.shape
    tc = S5_CHUNK
    nc = n // tc
    zb = Z_S5 // S5_WIDTH

    def chunk(d, i):
        return jnp.where(d == 0, i, nc - 1 - i)

    return pl.pallas_call(
        functools.partial(_s5_kernel, tc=tc, nc=nc),
        grid=(bsz, 2, nc),
        in_specs=[
            pl.BlockSpec((1, tc, S5_WIDTH), lambda b, d, i: (b, chunk(d, i), zb)),
            pl.BlockSpec((1, 1, 4, S5_HSTATE), lambda b, d, i: (b, d, 0, 0)),
            pl.BlockSpec((1, 2, S5_HALF, 2 * S5_HSTATE), lambda b, d, i: (d, 0, 0, 0)),
            pl.BlockSpec((1, 2, 2 * S5_HSTATE, S5_HALF), lambda b, d, i: (d, 0, 0, 0)),
            pl.BlockSpec((1, 2, 2, tc, S5_HSTATE), lambda b, d, i: (d, 0, 0, 0, 0)),
            pl.BlockSpec((1, 2, 2, tc, S5_HSTATE), lambda b, d, i: (d, 0, 0, 0, 0)),
            pl.BlockSpec((1, 2, 2, 1, S5_HSTATE), lambda b, d, i: (d, 0, 0, 0, 0)),
            pl.BlockSpec((1, tc, tc), lambda b, d, i: (d, 0, 0)),
        ],
        out_specs=[
            pl.BlockSpec((1, 1, tc, S5_WIDTH), lambda b, d, i: (d, b, chunk(d, i), 0)),
            pl.BlockSpec((1, 1, 4, S5_HSTATE), lambda b, d, i: (b, d, 0, 0)),
        ],
        out_shape=[
            jax.ShapeDtypeStruct((2, bsz, n, S5_WIDTH), F32),
            jax.ShapeDtypeStruct((bsz, 2, 4, S5_HSTATE), F32),
        ],
        scratch_shapes=[pltpu.VMEM((4, S5_HSTATE), F32)],
        compiler_params=_cparams(("parallel", "parallel", "arbitrary")),
        name="s5_scan",
    )(z, h0, bm, cm, wn, wp, l1, tri)


def _s5_tables(lam_re, lam_im, log_dt, b_re, b_im, c_re, c_im):
    tc = S5_CHUNK
    dt = jnp.exp(log_dt)[..., None]
    zr, zi = lam_re * dt, lam_im * dt
    mag = jnp.exp(zr)
    lbr, lbi = mag * jnp.cos(zi), mag * jnp.sin(zi)
    den = lam_re * lam_re + lam_im * lam_im
    nr, ni = lbr - 1.0, lbi
    cfr = (nr * lam_re + ni * lam_im) / den
    cfi = (ni * lam_re - nr * lam_im) / den
    bbr = cfr[..., None] * b_re - cfi[..., None] * b_im
    bbi = cfr[..., None] * b_im + cfi[..., None] * b_re
    eye = jnp.eye(S5_GROUP, dtype=F32)

    def blockdiag_in(b):
        b = b.reshape(2, 2, S5_GROUP, S5_STATE, S5_GROUP)
        return jnp.einsum('gk,dhgpn->dhgnkp', eye, b).reshape(2, 2, S5_HALF, S5_HSTATE)

    def blockdiag_out(c):
        c = c.reshape(2, 2, S5_GROUP, S5_GROUP, S5_STATE)
        return jnp.einsum('gk,dhgnp->dhgpkn', eye, c).reshape(2, 2, S5_HSTATE, S5_HALF)

    bm = jnp.concatenate([blockdiag_in(bbr), blockdiag_in(bbi)], axis=-1).astype(BF16)
    cm = jnp.concatenate([blockdiag_out(c_re), blockdiag_out(-c_im)], axis=-2).astype(BF16)

    def powers(k):
        zr_ = zr.reshape(2, 2, 1, S5_HSTATE)
        zi_ = zi.reshape(2, 2, 1, S5_HSTATE)
        kk = k[:, None, :, None]
        m = jnp.exp(kk * zr_)
        return jnp.stack([m * jnp.cos(kk * zi_), m * jnp.sin(kk * zi_)], axis=2)

    t = jnp.arange(tc, dtype=F32)
    wn = powers(jnp.stack([-t, -(tc - 1 - t)]))
    wp = powers(jnp.stack([t, tc - 1 - t]))
    l1 = powers(jnp.ones((2, 1), F32))
    r = jnp.arange(tc)
    tri = jnp.stack([r[:, None] >= r[None, :], r[:, None] <= r[None, :]]).astype(BF16)
    return bm, cm, wn, wp, l1, tri


def _sconv_kernel(z_ref, zp_ref, zn_ref, w_ref, b_ref, o_ref, *, nt):
    i = pl.program_id(1)
    z = z_ref[0]
    tm = z.shape[0]
    row = lax.broadcasted_iota(jnp.int32, z.shape, 0)
    prev = jnp.where(i > 0, zp_ref[0, 7:8, :], 0.0)
    nxt = jnp.where(i < nt - 1, zn_ref[0, 0:1, :], 0.0)
    up = jnp.where(row == 0, prev, pltpu.roll(z, 1, 0))
    dn = jnp.where(row == tm - 1, nxt, pltpu.roll(z, tm - 1, 0))
    o_ref[0] = up * w_ref[0:1, :] + z * w_ref[1:2, :] + dn * w_ref[2:3, :] + b_ref[...]


def _short_conv(z, w, b):
    bsz, n, _ = z.shape
    cw = 3 * HY_WIDTH
    tm = min(n, 512)
    nt = n // tm
    zb = Z_HY // cw
    r8 = tm // 8
    return pl.pallas_call(
        functools.partial(_sconv_kernel, nt=nt),
        grid=(bsz, nt),
        in_specs=[
            pl.BlockSpec((1, tm, cw), lambda b_, i: (b_, i, zb)),
            pl.BlockSpec((1, 8, cw), lambda b_, i: (b_, jnp.maximum(i * r8 - 1, 0), zb)),
            pl.BlockSpec((1, 8, cw), lambda b_, i: (b_, jnp.minimum((i + 1) * r8, n // 8 - 1), zb)),
            pl.BlockSpec((3, cw), lambda b_, i: (0, 0)),
            pl.BlockSpec((1, cw), lambda b_, i: (0, 0)),
        ],
        out_specs=pl.BlockSpec((1, tm, cw), lambda b_, i: (b_, i, 0)),
        out_shape=jax.ShapeDtypeStruct((bsz, n, cw), F32),
        compiler_params=_cparams(("parallel", "parallel")),
        name="short_conv",
    )(z, z, z, w, b)


def _filt_kernel(f_ref, w1_ref, b1_ref, w2_ref, b2_ref, w3_ref, fq_ref, dl_ref, o_ref, *, n_tok):
    z = f_ref[...]
    tm = z.shape[0]
    fq = fq_ref[...]
    hid = jnp.sin(fq * (jnp.dot(z, w1_ref[...], precision=HIGHEST, preferred_element_type=F32) + b1_ref[...]))
    hid = jnp.sin(fq * (jnp.dot(hid, w2_ref[...], precision=HIGHEST, preferred_element_type=F32) + b2_ref[...]))
    filt = jnp.dot(hid, w3_ref[...], precision=HIGHEST, preferred_element_type=F32)
    filt = filt * (jnp.exp(-z[:, 0:1] * dl_ref[...]) + HY_DECAY_SHIFT)
    m = pl.program_id(0) * tm + lax.broadcasted_iota(jnp.int32, (tm, HY_WIDTH), 0)
    for o in range(HY_ORDER):
        base = o * 2 * HY_WIDTH
        fwd = filt[:, base:base + HY_WIDTH]
        bwd = filt[:, base + HY_WIDTH:base + 2 * HY_WIDTH]
        o_ref[o] = jnp.where(m < n_tok, fwd, 0.0) + jnp.where((m > n_tok) | (m == 0), bwd, 0.0)


def _hyena_filter_feats(n_tok):
    m = jnp.arange(2 * n_tok)
    lag = jnp.where(m < n_tok, m, jnp.where(m > n_tok, 2 * n_tok - m, 0))
    t = jnp.linspace(0.0, 1.0, n_tok, dtype=F32)[lag][:, None]
    w = (2.0 * math.pi * lag.astype(F32) / n_tok)[:, None]
    bands = jnp.linspace(1e-4, HY_BANDS - 1, HY_BANDS, dtype=F32)[None, :]
    feats = jnp.concatenate([t, jnp.cos(bands * w), -jnp.sin(bands * w)], axis=-1)
    return jnp.pad(feats, ((0, 0), (0, HY_POS_PAD - HY_POS_DIM)))


def _hyena_filters(n_tok, w1, b1, w2, b2, w3, freq):
    feats = _hyena_filter_feats(n_tok)
    deltas = jnp.abs(jnp.linspace(math.log(HY_DECAY_TARGET) / HY_DECAY_SHORT,
                                  math.log(HY_DECAY_TARGET) / HY_DECAY_LONG, HY_FILTER_OUT, dtype=F32))[None, :]
    w1p = jnp.pad(w1, ((0, HY_POS_PAD - HY_POS_DIM), (0, 0)))
    n2 = 2 * n_tok
    tm = min(n2, 512)
    full = lambda shape: pl.BlockSpec(shape, lambda i: (0,) * len(shape))
    hh = HY_FILTER_HIDDEN
    return pl.pallas_call(
        functools.partial(_filt_kernel, n_tok=n_tok),
        grid=(n2 // tm,),
        in_specs=[
            pl.BlockSpec((tm, HY_POS_PAD), lambda i: (i, 0)),
            full((HY_POS_PAD, hh)), full((1, hh)), full((hh, hh)), full((1, hh)),
            full((hh, HY_FILTER_OUT)), full((1, hh)), full((1, HY_FILTER_OUT)),
        ],
        out_specs=pl.BlockSpec((HY_ORDER, tm, HY_WIDTH), lambda i: (0, i, 0)),
        out_shape=jax.ShapeDtypeStruct((HY_ORDER, n2, HY_WIDTH), F32),
        compiler_params=_cparams(("parallel",)),
        name="hyena_filter",
    )(feats, w1p, b1[None], w2, b2[None], w3, freq[None], deltas)


def _fft_split(n_fft):
    if n_fft <= 1024:
        return n_fft, 1
    n1 = 1 << (int(math.log2(n_fft)) // 2)
    return n1, n_fft // n1


def _cis(num, den, sign):
    ang = (2.0 * math.pi / den) * num.astype(F32)
    return jnp.cos(ang), sign * jnp.sin(ang)


def _fft_tables(n_fft):
    n1, n2 = _fft_split(n_fft)
    a = jnp.arange(n1)
    f1r, f1i = _cis((a[:, None] * a[None, :]) % n1, n1, -1.0)
    s1 = jnp.concatenate([f1r, f1i], axis=0).astype(BF16)
    half = n1 // 2
    s3 = (jnp.concatenate([f1r[:half], -f1i[:half]], axis=0) / n_fft).astype(BF16)
    if n2 == 1:
        return s1, s3, None, None
    c = jnp.arange(n1)[:, None, None]
    dd = jnp.arange(n2)[None, :, None]
    bb = jnp.arange(n2)[None, None, :]
    fr, fi = _cis((bb * (c + n1 * dd)) % n_fft, n_fft, -1.0)
    ft = jnp.concatenate([fr, fi], axis=1).astype(BF16)
    frt, fit = jnp.swapaxes(fr, 1, 2), jnp.swapaxes(fi, 1, 2)
    it = jnp.concatenate([frt, -fit], axis=1).astype(BF16)
    return s1, s3, ft, it


def _stacked_cdot(s, xr, xi, m):
    p = _dot(s, xr.astype(BF16))
    if xi is None:
        return p[:m], p[m:]
    q = _dot(s, xi.astype(BF16))
    return p[:m] - q[m:], q[:m] + p[m:]


def _fft_a_kernel(*refs, parts, n1, has_g):
    if has_g:
        x_ref, s_ref, g_ref, y_ref = refs
    else:
        x_ref, s_ref, y_ref = refs
    yr, yi = _stacked_cdot(s_ref[...], x_ref[0], x_ref[1] if parts == 2 else None, n1)
    if has_g:
        yr, yi = _cmul(yr, yi, g_ref[0], g_ref[1])
    y_ref[0] = yr
    y_ref[1] = yi


def _fft_a(x, s1, g=None):
    parts, a_rows, cols = x.shape
    n1 = s1.shape[0] // 2
    tc = min(cols, 2048)
    in_specs = [
        pl.BlockSpec((parts, a_rows, tc), lambda j: (0, 0, j)),
        pl.BlockSpec((2 * n1, a_rows), lambda j: (0, 0)),
    ]
    args = [x, s1[:, :a_rows]]
    if g is not None:
        in_specs.append(pl.BlockSpec((2, n1, tc), lambda j: (0, 0, j)))
        args.append(g)
    return pl.pallas_call(
        functools.partial(_fft_a_kernel, parts=parts, n1=n1, has_g=g is not None),
        grid=(cols // tc,),
        in_specs=in_specs,
        out_specs=pl.BlockSpec((2, n1, tc), lambda j: (0, 0, j)),
        out_shape=jax.ShapeDtypeStruct((2, n1, cols), F32),
        compiler_params=_cparams(("parallel",)),
        name="fft_stage_a",
    )(*args)


def _fft_b_kernel(*refs, n2, inverse):
    if inverse:
        y_ref, ft_ref, it_ref, g_ref, o_ref = refs
    else:
        y_ref, ft_ref, o_ref = refs
    xr, xi = _stacked_cdot(ft_ref[0], y_ref[0, 0], y_ref[1, 0], n2)
    if inverse:
        zr, zi = _cmul(xr, xi, g_ref[0, 0], g_ref[1, 0])
        xr, xi = _stacked_cdot(it_ref[0], zr, zi, n2)
    o_ref[0, 0] = xr
    o_ref[1, 0] = xi


def _fft_b(y, ft, it=None, g=None):
    _, n1, n2, ch = y.shape
    inverse = it is not None
    blk = pl.BlockSpec((2, 1, n2, ch), lambda c: (0, c, 0, 0))
    mat = pl.BlockSpec((1, 2 * n2, n2), lambda c: (c, 0, 0))
    in_specs, args = [blk, mat], [y, ft]
    if inverse:
        in_specs += [mat, blk]
        args += [it, g]
    return pl.pallas_call(
        functools.partial(_fft_b_kernel, n2=n2, inverse=inverse),
        grid=(n1,),
        in_specs=in_specs,
        out_specs=blk,
        out_shape=jax.ShapeDtypeStruct(y.shape, F32),
        compiler_params=_cparams(("parallel",)),
        name="fft_stage_b",
    )(*args)


def _fft_c_kernel(y_ref, s_ref, x_ref, g_ref, b_ref, o_ref, *, half):
    cr, ci = _stacked_cdot(s_ref[...], y_ref[0], y_ref[1], half)
    bias = b_ref[...]
    o_ref[0] = g_ref[0] * (cr + bias * x_ref[0])
    o_ref[1] = g_ref[1] * (ci + bias * x_ref[1])


def _fft_c(y, s3, x, gate, bias_cols):
    _, n1, cols = y.shape
    half = n1 // 2
    tc = min(cols, 2048)
    io = pl.BlockSpec((2, half, tc), lambda j: (0, 0, j))
    return pl.pallas_call(
        functools.partial(_fft_c_kernel, half=half),
        grid=(cols // tc,),
        in_specs=[
            pl.BlockSpec((2, n1, tc), lambda j: (0, 0, j)),
            pl.BlockSpec((n1, n1), lambda j: (0, 0)),
            io, io,
            pl.BlockSpec((1, tc), lambda j: (0, j)),
        ],
        out_specs=io,
        out_shape=jax.ShapeDtypeStruct((2, half, cols), F32),
        compiler_params=_cparams(("parallel",)),
        name="fft_stage_c",
    )(y, s3, x, gate, bias_cols)


def _hyena_mixer(u3, gfilt, bias):
    bsz, n, _ = u3.shape
    assert bsz == 2, "the complex packing of the long convolution pairs exactly two batch rows"
    ch = HY_WIDTH
    n_fft = 2 * n
    n1, n2 = _fft_split(n_fft)
    s1, s3, ft, it = _fft_tables(n_fft)
    cols = n2 * ch
    y = u3[:, :, :ch].reshape(2, n1 // 2, cols)
    for o in range(HY_ORDER):
        gate = u3[:, :, (o + 1) * ch:(o + 2) * ch].reshape(2, n1 // 2, cols)
        gspec = _fft_a(gfilt[o].reshape(1, n1, cols), s1)
        bias_cols = jnp.tile(bias[o], n2)[None, :]
        if n2 == 1:
            spec = _fft_a(y, s1, gspec)
        else:
            gspec = _fft_b(gspec.reshape(2, n1, n2, ch), ft)
            spec = _fft_b(_fft_a(y, s1).reshape(2, n1, n2, ch), ft, it, gspec).reshape(2, n1, cols)
        y = _fft_c(spec, s3, y, gate, bias_cols)
    return y.reshape(2, n, ch)


def _gelu_tanh(x):
    return 0.5 * x * (1.0 + jnp.tanh(math.sqrt(2.0 / math.pi) * (x + 0.044715 * (x * x * x))))


def _merge_kernel(x_ref, g0_ref, g1_ref, g2_ref, a_ref, yf_ref, yb_ref, u_ref, e_ref, d_ref,
                  wglu_ref, wm_ref, ws_ref, wh_ref, wo_ref, gt_ref, o_ref):
    att = _dot(a_ref[0, 0], wm_ref[0])
    for h in range(1, MLA_HEADS):
        att = att + _dot(a_ref[0, h], wm_ref[h])
    y = yf_ref[0, 0] + yb_ref[0, 0] + d_ref[...] * u_ref[0]
    gy = _gelu_tanh(y)
    s5 = gy * jax.nn.sigmoid(_dot(gy.astype(BF16), wglu_ref[...]))
    s5 = _dot(s5.astype(BF16), ws_ref[...])
    hy = _dot(e_ref[0].astype(BF16), wh_ref[...])
    merged = jax.nn.sigmoid(g0_ref[0]) * att
    merged = merged + jax.nn.sigmoid(g1_ref[0]) * s5
    merged = merged + jax.nn.sigmoid(g2_ref[0]) * hy
    o_ref[0] = x_ref[0] + gt_ref[0] * _dot(merged.astype(BF16), wo_ref[...])


def _merge(x, z, att, y5, e, s5_d, w_glu, w_mla, w_s5, w_hy, w_out, gate):
    bsz, n, d = x.shape
    tm = min(n, 256)
    full = lambda shape: pl.BlockSpec(shape, lambda b, i: (0,) * len(shape))
    zs5 = Z_S5 // S5_WIDTH
    return pl.pallas_call(
        _merge_kernel,
        grid=(bsz, n // tm),
        in_specs=[
            pl.BlockSpec((1, tm, d), lambda b, i: (b, i, 0)),
            pl.BlockSpec((1, tm, d), lambda b, i: (b, i, 0)),
            pl.BlockSpec((1, tm, d), lambda b, i: (b, i, 1)),
            pl.BlockSpec((1, tm, d), lambda b, i: (b, i, 2)),
            pl.BlockSpec((1, MLA_HEADS, tm, MLA_V), lambda b, i: (b, 0, i, 0)),
            pl.BlockSpec((1, 1, tm, S5_WIDTH), lambda b, i: (0, b, i, 0)),
            pl.BlockSpec((1, 1, tm, S5_WIDTH), lambda b, i: (1, b, i, 0)),
            pl.BlockSpec((1, tm, S5_WIDTH), lambda b, i: (b, i, zs5)),
            pl.BlockSpec((1, tm, HY_WIDTH), lambda b, i: (b, i, 0)),
            full((1, S5_WIDTH)),
            full((S5_WIDTH, S5_WIDTH)),
            full((MLA_HEADS, MLA_V, d)),
            full((S5_WIDTH, d)),
            full((HY_WIDTH, d)),
            full((d, d)),
            pl.BlockSpec((1, 1, d), lambda b, i: (b, 0, 0)),
        ],
        out_specs=pl.BlockSpec((1, tm, d), lambda b, i: (b, i, 0)),
        out_shape=jax.ShapeDtypeStruct((bsz, n, d), F32),
        compiler_params=_cparams(("parallel", "parallel")),
        name="merge",
    )(x, z, z, z, att, y5, y5, z, e, s5_d, w_glu, w_mla, w_s5, w_hy, w_out, gate)


def _ffn_kernel(x_ref, g_ref, sh_ref, sc_ref, gt_ref, wg_ref, wu_ref, wd_ref, fg_ref, o_ref, h_ref, acc_ref,
                *, nk, final):
    k = pl.program_id(2)

    @pl.when(k == 0)
    def _():
        h_ref[...] = _norm_mod(x_ref[0], g_ref[...], sh_ref[0], sc_ref[0]).astype(BF16)
        acc_ref[...] = jnp.zeros_like(acc_ref)

    h = h_ref[...]
    act = jax.nn.silu(_dot(h, wg_ref[...])) * _dot(h, wu_ref[...])
    acc_ref[...] += _dot(act.astype(BF16), wd_ref[...])

    @pl.when(k == nk - 1)
    def _():
        r = x_ref[0] + gt_ref[0] * acc_ref[...]
        o_ref[0] = _rms(r, fg_ref[...]) if final else r


def _ffn(x, g, shift, scale, gate, w_g, w_u, w_d, final_g, final):
    bsz, n, d = x.shape
    dff = w_g.shape[1]
    tm = min(n, 1024)
    tf = 256
    nk = dff // tf
    vec = pl.BlockSpec((1, 1, d), lambda b, i, k: (b, 0, 0))
    row = pl.BlockSpec((1, d), lambda b, i, k: (0, 0))
    return pl.pallas_call(
        functools.partial(_ffn_kernel, nk=nk, final=final),
        grid=(bsz, n // tm, nk),
        in_specs=[
            pl.BlockSpec((1, tm, d), lambda b, i, k: (b, i, 0)),
            row, vec, vec, vec,
            pl.BlockSpec((d, tf), lambda b, i, k: (0, k)),
            pl.BlockSpec((d, tf), lambda b, i, k: (0, k)),
            pl.BlockSpec((tf, d), lambda b, i, k: (k, 0)),
            row,
        ],
        out_specs=pl.BlockSpec((1, tm, d), lambda b, i, k: (b, i, 0)),
        out_shape=jax.ShapeDtypeStruct((bsz, n, d), F32),
        scratch_shapes=[pltpu.VMEM((tm, d), BF16), pltpu.VMEM((tm, d), F32)],
        compiler_params=_cparams(("parallel", "parallel", "arbitrary")),
        name="ffn",
    )(x, g, shift, scale, gate, w_g, w_u, w_d, final_g)


def _rope_tables(n_tok):
    rows = n_tok // GRID_W
    row = jnp.broadcast_to(jnp.arange(rows, dtype=F32)[:, None], (rows, GRID_W)).reshape(-1)
    col = jnp.broadcast_to(jnp.arange(GRID_W, dtype=F32)[None, :], (rows, GRID_W)).reshape(-1)
    n_freq = MLA_ROPE // 4
    inv = ROPE_BASE ** (-jnp.arange(n_freq, dtype=F32) / n_freq)
    ang = jnp.concatenate([row[:, None] * inv, col[:, None] * inv], axis=-1)
    cos, sin = jnp.cos(ang), jnp.sin(ang)
    pad = jnp.zeros((n_tok, HEAD_W - MLA_ROPE), F32)
    return jnp.concatenate([cos, cos, pad], axis=-1), jnp.concatenate([-sin, sin, pad], axis=-1)


def _identity_rope_tables(n_tok):
    one = jnp.ones((n_tok, MLA_ROPE), F32)
    pad = jnp.zeros((n_tok, HEAD_W - MLA_ROPE), F32)
    return jnp.concatenate([one, pad], axis=-1), jnp.zeros((n_tok, HEAD_W), F32)


def _layout_w_in(w):
    kr = w[:, OFF_KR:OFF_S5]
    x1, x2 = kr[:, 0::2], kr[:, 1::2]
    pad = jnp.zeros((w.shape[0], HEAD_W - 3 * (MLA_ROPE // 2)), w.dtype)
    return jnp.concatenate([w[:, OFF_GATE:], w[:, OFF_HY:OFF_GATE], w[:, OFF_S5:OFF_HY],
                            w[:, OFF_CQ:OFF_CKV], w[:, OFF_CKV:OFF_KR], x1, x2, x1, pad], axis=1).astype(BF16)


def _layout_w_uq(w):
    w = w.reshape(MLA_Q_LORA, MLA_HEADS, MLA_NOPE + MLA_ROPE)
    nope, rope = w[..., :MLA_NOPE], w[..., MLA_NOPE:]
    x1, x2 = rope[..., 0::2], rope[..., 1::2]
    z32 = jnp.zeros_like(rope)
    wq = jnp.concatenate([x1, x2, z32, nope], axis=-1)
    wqs = jnp.concatenate([x2, x1, z32, jnp.zeros_like(nope)], axis=-1)
    shape = (MLA_Q_LORA, MLA_HEADS * HEAD_W)
    return wq.reshape(shape).astype(BF16), wqs.reshape(shape).astype(BF16)


def _layout_w_ukv(w):
    w = w.reshape(MLA_KV_LORA, MLA_HEADS, MLA_NOPE + MLA_V)
    nope, val = w[..., :MLA_NOPE], w[..., MLA_NOPE:]
    wk = jnp.concatenate([jnp.zeros_like(nope), nope], axis=-1).reshape(MLA_KV_LORA, MLA_HEADS * HEAD_W)
    wvt = jnp.concatenate([val, jnp.zeros_like(val)], axis=-1).reshape(MLA_KV_LORA, MLA_HEADS * HEAD_W).T
    return wk.astype(BF16), wvt.astype(BF16)


def kernel(x, c, ctx, c_ctx, ada_w, ada_b, norm_mix, w_in, mla_q_norm, mla_w_uq, mla_kv_norm, mla_w_ukv,
           s5_lam_re, s5_lam_im, s5_log_dt, s5_b_re, s5_b_im, s5_c_re, s5_c_im, s5_d, s5_w_glu,
           hy_conv_w, hy_conv_b, hy_f_w1, hy_f_b1, hy_f_w2, hy_f_b2, hy_f_w3, hy_f_freq, hy_bias,
           w_branch_mla, w_branch_s5, w_branch_hy, w_out, norm_ffn, ffn_w_gu, ffn_w_down, final_norm):
    bsz, n_tok, d = x.shape
    n_ctx = ctx.shape[1]
    depth = ada_w.shape[0]
    rope_x = _rope_tables(n_tok)
    rope_c = _identity_rope_tables(n_ctx)
    h_zero = jnp.zeros((bsz, 2, 4, S5_HSTATE), F32)

    for i in range(depth):
        last = i == depth - 1
        mx = (jax.nn.silu(c) @ ada_w[i] + ada_b[i]).reshape(bsz, N_MOD, 1, d)
        mc = jnp.broadcast_to((jax.nn.silu(c_ctx) @ ada_w[i] + ada_b[i]).reshape(1, N_MOD, 1, d),
                              (bsz, N_MOD, 1, d))
        w_z = _layout_w_in(w_in[i])
        wq, wqs = _layout_w_uq(mla_w_uq[i])
        wk, wvt = _layout_w_ukv(mla_w_ukv[i])
        qn, kvn = mla_q_norm[i][None], mla_kv_norm[i][None]
        s5_tabs = _s5_tables(s5_lam_re[i], s5_lam_im[i], s5_log_dt[i], s5_b_re[i], s5_b_im[i],
                             s5_c_re[i], s5_c_im[i])
        fparams = (hy_f_w1[i], hy_f_b1[i], hy_f_w2[i], hy_f_b2[i], hy_f_w3[i], hy_f_freq[i])
        merge_w = (s5_d[i][None], s5_w_glu[i].astype(BF16),
                   w_branch_mla[i].reshape(MLA_HEADS, MLA_V, d).astype(BF16),
                   w_branch_s5[i].astype(BF16), w_branch_hy[i].astype(BF16), w_out[i].astype(BF16))
        w_g = ffn_w_gu[i][:, :D_FF].astype(BF16)
        w_u = ffn_w_gu[i][:, D_FF:].astype(BF16)
        w_d = ffn_w_down[i].astype(BF16)
        g_mix, g_ffn = norm_mix[i][None], norm_ffn[i][None]

        zx = _proj_in(x, g_mix, mx[:, 0], mx[:, 1], w_z)
        zc = _proj_in(ctx, g_mix, mc[:, 0], mc[:, 1], w_z)

        q_c, k_c, vt_c = _mla_prep(zc, qn, kvn, wq, wqs, wk, wvt, *rope_c)
        q_x, k_x, vt_x = _mla_prep(zx, qn, kvn, wq, wqs, wk, wvt, *rope_x)
        a_x = _attention(q_x, jnp.concatenate([k_c, k_x], axis=2), jnp.concatenate([vt_c, vt_x], axis=3))

        y5_c, finals = _s5_scan(zc, h_zero, s5_tabs)
        y5_x, _ = _s5_scan(zx, finals, s5_tabs)

        e_x = _hyena_mixer(_short_conv(zx, hy_conv_w[i], hy_conv_b[i][None]),
                           _hyena_filters(n_tok, *fparams), hy_bias[i])

        x = _merge(x, zx, a_x, y5_x, e_x, *merge_w, mx[:, 2])
        x = _ffn(x, g_ffn, mx[:, 3], mx[:, 4], mx[:, 5], w_g, w_u, w_d, final_norm[None], last)

        if not last:
            a_c = _attention(q_c, k_c, vt_c)
            e_c = _hyena_mixer(_short_conv(zc, hy_conv_w[i], hy_conv_b[i][None]),
                               _hyena_filters(n_ctx, *fparams), hy_bias[i])
            ctx = _merge(ctx, zc, a_c, y5_c, e_c, *merge_w, mc[:, 2])
            ctx = _ffn(ctx, g_ffn, mc[:, 3], mc[:, 4], mc[:, 5], w_g, w_u, w_d, final_norm[None], False)
    return x
```

```python
import functools
import math

import jax
import jax.numpy as jnp
import numpy as np
from jax import lax
from jax.experimental import pallas as pl
from jax.experimental.pallas import tpu as pltpu

F32 = jnp.float32
BF16 = jnp.bfloat16
HIGHEST = lax.Precision.HIGHEST

D_MODEL = 1024
GRID_W = 64
NORM_EPS = 1e-6
N_MOD = 6

MLA_HEADS = 8
MLA_NOPE = 64
MLA_ROPE = 32
MLA_V = 64
MLA_Q_LORA = 256
MLA_KV_LORA = 128
ROPE_BASE = 10000.0
HEAD_W = 128
QK_SCALE_LOG2 = (MLA_NOPE + MLA_ROPE) ** -0.5 * math.log2(math.e)
ATTN_SUB = 256
ATTN_UNROLL = 4

S5_WIDTH = 512
S5_GROUP = 16
S5_GROUPS = 32
S5_STATE = 64
S5_HALF = 256
S5_HSTATE = 1024
S5_CHUNK = 128

HY_WIDTH = 512
HY_ORDER = 2
HY_BANDS = 16
HY_POS_DIM = 1 + 2 * HY_BANDS
HY_POS_PAD = 64
HY_FILTER_HIDDEN = 64
HY_FILTER_OUT = HY_ORDER * 2 * HY_WIDTH
HY_DECAY_TARGET = 1e-2
HY_DECAY_SHORT = 0.3
HY_DECAY_LONG = 1.5
HY_DECAY_SHIFT = 0.05

D_FF = 2816

OFF_CQ = 0
OFF_CKV = OFF_CQ + MLA_Q_LORA
OFF_KR = OFF_CKV + MLA_KV_LORA
OFF_S5 = OFF_KR + MLA_ROPE
OFF_HY = OFF_S5 + S5_WIDTH
OFF_GATE = OFF_HY + 3 * HY_WIDTH

Z_GATE = 0
Z_HY = 3072
Z_S5 = 4608
Z_MLA = 5120
Z_WIDTH = 5632

VMEM_LIMIT_V7X = 52 * 1024 * 1024


def _cparams(sem):
    return pltpu.CompilerParams(dimension_semantics=sem, vmem_limit_bytes=VMEM_LIMIT_V7X)


def _dot(a, b):
    return jnp.dot(a, b, preferred_element_type=F32)


def _rms(x, g):
    return x * lax.rsqrt(jnp.mean(x * x, axis=-1, keepdims=True) + NORM_EPS) * g


def _norm_mod(x, g, shift, scale):
    return _rms(x, g) * (1.0 + scale) + shift


def _proj_kernel(x_ref, g_ref, sh_ref, sc_ref, w_ref, o_ref, h_ref):
    @pl.when(pl.program_id(2) == 0)
    def _():
        h_ref[...] = _norm_mod(x_ref[0], g_ref[...], sh_ref[0], sc_ref[0]).astype(BF16)

    o_ref[0] = _dot(h_ref[...], w_ref[...])


def _proj_in(x, g, shift, scale, w):
    bsz, n, d = x.shape
    nz = w.shape[1]
    tm = min(n, 1024)
    tn = 512
    return pl.pallas_call(
        _proj_kernel,
        grid=(bsz, n // tm, nz // tn),
        in_specs=[
            pl.BlockSpec((1, tm, d), lambda b, i, j: (b, i, 0)),
            pl.BlockSpec((1, d), lambda b, i, j: (0, 0)),
            pl.BlockSpec((1, 1, d), lambda b, i, j: (b, 0, 0)),
            pl.BlockSpec((1, 1, d), lambda b, i, j: (b, 0, 0)),
            pl.BlockSpec((d, tn), lambda b, i, j: (0, j)),
        ],
        out_specs=pl.BlockSpec((1, tm, tn), lambda b, i, j: (b, i, j)),
        out_shape=jax.ShapeDtypeStruct((bsz, n, nz), F32),
        scratch_shapes=[pltpu.VMEM((tm, d), BF16)],
        compiler_params=_cparams(("parallel", "parallel", "arbitrary")),
        name="proj_in",
    )(x, g, shift, scale, w)


def _mla_prep_kernel(z_ref, qn_ref, kvn_ref, wq_ref, wqs_ref, wk_ref, wvt_ref, c_ref, s_ref, q_ref, k_ref, vt_ref):
    z = z_ref[0]
    hq = _rms(z[:, :MLA_Q_LORA], qn_ref[...]).astype(BF16)
    hkv = _rms(z[:, MLA_Q_LORA:MLA_Q_LORA + MLA_KV_LORA], kvn_ref[...]).astype(BF16)
    krb = z[:, MLA_Q_LORA + MLA_KV_LORA:]
    qa = _dot(hq, wq_ref[...])
    qb = _dot(hq, wqs_ref[...])
    kn = _dot(hkv, wk_ref[...])
    vt = lax.dot_general(wvt_ref[...], hkv, (((1,), (1,)), ((), ())), preferred_element_type=F32)
    ck = c_ref[...]
    sn = s_ref[...]
    lane = lax.broadcasted_iota(jnp.int32, ck.shape, 1)
    cq = jnp.where(lane < MLA_ROPE, ck, 1.0)
    kr = krb * ck + pltpu.roll(krb, HEAD_W - MLA_ROPE // 2, 1) * sn
    row = lax.broadcasted_iota(jnp.int32, (HEAD_W, z.shape[0]), 0)
    for h in range(MLA_HEADS):
        sl = slice(h * HEAD_W, (h + 1) * HEAD_W)
        q_ref[0, h] = ((qa[:, sl] * cq + qb[:, sl] * sn) * QK_SCALE_LOG2).astype(BF16)
        k_ref[0, h] = (kr + kn[:, sl]).astype(BF16)
        vt_ref[0, h] = jnp.where(row == MLA_V, 1.0, vt[sl, :]).astype(BF16)


def _mla_prep(z, qn, kvn, wq, wqs, wk, wvt, ctab, stab):
    bsz, n, _ = z.shape
    tm = min(n, 512)
    hw = MLA_HEADS * HEAD_W
    zb = Z_MLA // 512
    full = lambda shape: pl.BlockSpec(shape, lambda b, i: (0,) * len(shape))
    return pl.pallas_call(
        _mla_prep_kernel,
        grid=(bsz, n // tm),
        in_specs=[
            pl.BlockSpec((1, tm, 512), lambda b, i: (b, i, zb)),
            full((1, MLA_Q_LORA)),
            full((1, MLA_KV_LORA)),
            full((MLA_Q_LORA, hw)),
            full((MLA_Q_LORA, hw)),
            full((MLA_KV_LORA, hw)),
            full((hw, MLA_KV_LORA)),
            pl.BlockSpec((tm, HEAD_W), lambda b, i: (i, 0)),
            pl.BlockSpec((tm, HEAD_W), lambda b, i: (i, 0)),
        ],
        out_specs=[
            pl.BlockSpec((1, MLA_HEADS, tm, HEAD_W), lambda b, i: (b, 0, i, 0)),
            pl.BlockSpec((1, MLA_HEADS, tm, HEAD_W), lambda b, i: (b, 0, i, 0)),
            pl.BlockSpec((1, MLA_HEADS, HEAD_W, tm), lambda b, i: (b, 0, 0, i)),
        ],
        out_shape=[
            jax.ShapeDtypeStruct((bsz, MLA_HEADS, n, HEAD_W), BF16),
            jax.ShapeDtypeStruct((bsz, MLA_HEADS, n, HEAD_W), BF16),
            jax.ShapeDtypeStruct((bsz, MLA_HEADS, HEAD_W, n), BF16),
        ],
        compiler_params=_cparams(("parallel", "parallel")),
        name="mla_prep",
    )(z, qn, kvn, wq, wqs, wk, wvt, ctab, stab)


def _attn_kernel(q_ref, k_ref, vt_ref, o_ref, s0_ref, s1_ref, p0_ref, p1_ref, *, tk, nkc):
    q = q_ref[0, 0]
    tq = q.shape[0]
    s_refs = (s0_ref, s1_ref)
    p_refs = (p0_ref, p1_ref)

    ts = min(tk, ATTN_SUB)
    subs = [slice(r, r + ts) for r in range(0, tk, ts)]

    def scores_sub(j, s_ref, sub, cmax):
        off = pl.multiple_of(j * tk + sub.start, ts)
        s = lax.dot_general(k_ref[0, 0, pl.ds(off, ts), :], q, (((1,), (1,)), ((), ())),
                            preferred_element_type=F32)
        s_ref[sub, :] = s
        cm = jnp.max(s, axis=0, keepdims=True)
        return cm if cmax is None else jnp.maximum(cmax, cm)

    def softmax_sub(par, sub, mn):
        p_refs[par][sub, :] = jnp.exp2(s_refs[par][sub, :] - mn).astype(BF16)

    def step(i, par, carry, do_softmax, do_scores):
        cm, m, alpha, acc = carry
        mn = jnp.maximum(m, cm)
        cm_new, pv = None, None
        for sub in subs:
            if do_scores:
                cm_new = scores_sub(i + 2, s_refs[par], sub, cm_new)
            if do_softmax:
                softmax_sub(1 - par, sub, mn)
            off = pl.multiple_of(i * tk + sub.start, ts)
            d = _dot(vt_ref[0, 0, :, pl.ds(off, ts)], p_refs[par][sub, :])
            pv = d if pv is None else pv + d
        acc = alpha * acc + pv
        if do_softmax:
            alpha, m = jnp.exp2(m - mn), mn
        return (cm_new if do_scores else cm), m, alpha, acc

    m = jnp.full((1, tq), -1e30, F32)
    acc = jnp.zeros((HEAD_W, tq), F32)
    cm, cm_next = None, None
    for sub in subs:
        cm = scores_sub(0, s0_ref, sub, cm)
        if nkc > 1:
            cm_next = scores_sub(1, s1_ref, sub, cm_next)
    mn = jnp.maximum(m, cm)
    for sub in subs:
        softmax_sub(0, sub, mn)
    carry = (cm_next if nkc > 1 else cm, mn, jnp.exp2(m - mn), acc)

    def body(t, carry):
        for r in range(ATTN_UNROLL):
            carry = step(ATTN_UNROLL * t + r, r % 2, carry, True, True)
        return carry

    nloop = max(nkc - 2, 0) // ATTN_UNROLL
    carry = lax.fori_loop(0, nloop, body, carry)
    for i in range(ATTN_UNROLL * nloop, nkc):
        carry = step(i, i % 2, carry, i + 1 < nkc, i + 2 < nkc)
    acc = carry[3]
    out = (acc * (1.0 / acc[MLA_V:MLA_V + 1, :])).T
    o_ref[0, 0] = out[:, :MLA_V].astype(BF16)


def _kv_chunk(nk):
    for tk in (1280, 1024, 640, 512, 256, 128):
        if nk % tk == 0:
            return tk
    raise ValueError(f"unsupported key count {nk}")


def _attention(q, k, vt):
    bsz, nh, nq, _ = q.shape
    nk = k.shape[2]
    tq = min(nq, 512)
    tk = _kv_chunk(nk)
    return pl.pallas_call(
        functools.partial(_attn_kernel, tk=tk, nkc=nk // tk),
        grid=(bsz, nh, nq // tq),
        in_specs=[
            pl.BlockSpec((1, 1, tq, HEAD_W), lambda b, h, i: (b, h, i, 0)),
            pl.BlockSpec((1, 1, nk, HEAD_W), lambda b, h, i: (b, h, 0, 0)),
            pl.BlockSpec((1, 1, HEAD_W, nk), lambda b, h, i: (b, h, 0, 0)),
        ],
        out_specs=pl.BlockSpec((1, 1, tq, MLA_V), lambda b, h, i: (b, h, i, 0)),
        out_shape=jax.ShapeDtypeStruct((bsz, nh, nq, MLA_V), BF16),
        scratch_shapes=[pltpu.VMEM((tk, tq), F32), pltpu.VMEM((tk, tq), F32),
                        pltpu.VMEM((tk, tq), BF16), pltpu.VMEM((tk, tq), BF16)],
        compiler_params=_cparams(("parallel", "parallel", "arbitrary")),
        name="attention",
    )(q, k, vt)


def _cmul(ar, ai, br, bi):
    return ar * br - ai * bi, ar * bi + ai * br


def _s5_kernel(u_ref, h0_ref, bm_ref, cm_ref, wn_ref, wp_ref, l1_ref, tri_ref, y_ref, hf_ref, carry_ref, *, tc, nc):
    d = pl.program_id(1)
    i = pl.program_id(2)

    @pl.when(i == 0)
    def _():
        carry_ref[...] = h0_ref[0, 0]

    u = u_ref[0]
    tri = tri_ref[0]
    ys = []
    for hf in range(2):
        ub = u[:, hf * S5_HALF:(hf + 1) * S5_HALF].astype(BF16)
        bu = _dot(ub, bm_ref[0, hf])
        xr, xi = _cmul(wn_ref[0, hf, 0], wn_ref[0, hf, 1], bu[:, :S5_HSTATE], bu[:, S5_HSTATE:])
        s = _dot(tri, jnp.concatenate([xr, xi], axis=1).astype(BF16))
        cr, ci = _cmul(l1_ref[0, hf, 0], l1_ref[0, hf, 1],
                       carry_ref[2 * hf:2 * hf + 1, :], carry_ref[2 * hf + 1:2 * hf + 2, :])
        hr, hi = _cmul(wp_ref[0, hf, 0], wp_ref[0, hf, 1], s[:, :S5_HSTATE] + cr, s[:, S5_HSTATE:] + ci)
        ys.append(_dot(jnp.concatenate([hr, hi], axis=1).astype(BF16), cm_ref[0, hf]))
        carry_ref[2 * hf:2 * hf + 1, :] = jnp.where(d == 0, hr[tc - 1:tc], hr[0:1])
        carry_ref[2 * hf + 1:2 * hf + 2, :] = jnp.where(d == 0, hi[tc - 1:tc], hi[0:1])
    y_ref[0, 0] = jnp.concatenate(ys, axis=1)

    @pl.when(i == nc - 1)
    def _():
        hf_ref[0, 0] = carry_ref[...]


def _s5_scan(z, h0, tabs):
    bm, cm, wn, wp, l1, tri = tabs
    bsz, n, _ = z.shape
    tc = S5_CHUNK
    nc = n // tc
    zb = Z_S5 // S5_WIDTH

    def chunk(d, i):
        return jnp.where(d == 0, i, nc - 1 - i)

    return pl.pallas_call(
        functools.partial(_s5_kernel, tc=tc, nc=nc),
        grid=(bsz, 2, nc),
        in_specs=[
            pl.BlockSpec((1, tc, S5_WIDTH), lambda b, d, i: (b, chunk(d, i), zb)),
            pl.BlockSpec((1, 1, 4, S5_HSTATE), lambda b, d, i: (b, d, 0, 0)),
            pl.BlockSpec((1, 2, S5_HALF, 2 * S5_HSTATE), lambda b, d, i: (d, 0, 0, 0)),
            pl.BlockSpec((1, 2, 2 * S5_HSTATE, S5_HALF), lambda b, d, i: (d, 0, 0, 0)),
            pl.BlockSpec((1, 2, 2, tc, S5_HSTATE), lambda b, d, i: (d, 0, 0, 0, 0)),
            pl.BlockSpec((1, 2, 2, tc, S5_HSTATE), lambda b, d, i: (d, 0, 0, 0, 0)),
            pl.BlockSpec((1, 2, 2, 1, S5_HSTATE), lambda b, d, i: (d, 0, 0, 0, 0)),
            pl.BlockSpec((1, tc, tc), lambda b, d, i: (d, 0, 0)),
        ],
        out_specs=[
            pl.BlockSpec((1, 1, tc, S5_WIDTH), lambda b, d, i: (d, b, chunk(d, i), 0)),
            pl.BlockSpec((1, 1, 4, S5_HSTATE), lambda b, d, i: (b, d, 0, 0)),
        ],
        out_shape=[
            jax.ShapeDtypeStruct((2, bsz, n, S5_WIDTH), F32),
            jax.ShapeDtypeStruct((bsz, 2, 4, S5_HSTATE), F32),
        ],
        scratch_shapes=[pltpu.VMEM((4, S5_HSTATE), F32)],
        compiler_params=_cparams(("parallel", "parallel", "arbitrary")),
        name="s5_scan",
    )(z, h0, bm, cm, wn, wp, l1, tri)


def _s5_tables(lam_re, lam_im, log_dt, b_re, b_im, c_re, c_im):
    tc = S5_CHUNK
    dt = jnp.exp(log_dt)[..., None]
    zr, zi = lam_re * dt, lam_im * dt
    mag = jnp.exp(zr)
    lbr, lbi = mag * jnp.cos(zi), mag * jnp.sin(zi)
    den = lam_re * lam_re + lam_im * lam_im
    nr, ni = lbr - 1.0, lbi
    cfr = (nr * lam_re + ni * lam_im) / den
    cfi = (ni * lam_re - nr * lam_im) / den
    bbr = cfr[..., None] * b_re - cfi[..., None] * b_im
    bbi = cfr[..., None] * b_im + cfi[..., None] * b_re
    eye = jnp.eye(S5_GROUP, dtype=F32)

    def blockdiag_in(b):
        b = b.reshape(2, 2, S5_GROUP, S5_STATE, S5_GROUP)
        return jnp.einsum('gk,dhgpn->dhgnkp', eye, b).reshape(2, 2, S5_HALF, S5_HSTATE)

    def blockdiag_out(c):
        c = c.reshape(2, 2, S5_GROUP, S5_GROUP, S5_STATE)
        return jnp.einsum('gk,dhgnp->dhgpkn', eye, c).reshape(2, 2, S5_HSTATE, S5_HALF)

    bm = jnp.concatenate([blockdiag_in(bbr), blockdiag_in(bbi)], axis=-1).astype(BF16)
    cm = jnp.concatenate([blockdiag_out(c_re), blockdiag_out(-c_im)], axis=-2).astype(BF16)

    def powers(k):
        zr_ = zr.reshape(2, 2, 1, S5_HSTATE)
        zi_ = zi.reshape(2, 2, 1, S5_HSTATE)
        kk = k[:, None, :, None]
        m = jnp.exp(kk * zr_)
        return jnp.stack([m * jnp.cos(kk * zi_), m * jnp.sin(kk * zi_)], axis=2)

    t = jnp.arange(tc, dtype=F32)
    wn = powers(jnp.stack([-t, -(tc - 1 - t)]))
    wp = powers(jnp.stack([t, tc - 1 - t]))
    l1 = powers(jnp.ones((2, 1), F32))
    r = jnp.arange(tc)
    tri = jnp.stack([r[:, None] >= r[None, :], r[:, None] <= r[None, :]]).astype(BF16)
    return bm, cm, wn, wp, l1, tri


def _sconv_kernel(z_ref, zp_ref, zn_ref, w_ref, b_ref, o_ref, *, nt):
    i = pl.program_id(1)
    z = z_ref[0]
    tm = z.shape[0]
    row = lax.broadcasted_iota(jnp.int32, z.shape, 0)
    prev = jnp.where(i > 0, zp_ref[0, 7:8, :], 0.0)
    nxt = jnp.where(i < nt - 1, zn_ref[0, 0:1, :], 0.0)
    up = jnp.where(row == 0, prev, pltpu.roll(z, 1, 0))
    dn = jnp.where(row == tm - 1, nxt, pltpu.roll(z, tm - 1, 0))
    o_ref[0] = up * w_ref[0:1, :] + z * w_ref[1:2, :] + dn * w_ref[2:3, :] + b_ref[...]


def _short_conv(z, w, b):
    bsz, n, _ = z.shape
    cw = 3 * HY_WIDTH
    tm = min(n, 512)
    nt = n // tm
    zb = Z_HY // cw
    r8 = tm // 8
    return pl.pallas_call(
        functools.partial(_sconv_kernel, nt=nt),
        grid=(bsz, nt),
        in_specs=[
            pl.BlockSpec((1, tm, cw), lambda b_, i: (b_, i, zb)),
            pl.BlockSpec((1, 8, cw), lambda b_, i: (b_, jnp.maximum(i * r8 - 1, 0), zb)),
            pl.BlockSpec((1, 8, cw), lambda b_, i: (b_, jnp.minimum((i + 1) * r8, n // 8 - 1), zb)),
            pl.BlockSpec((3, cw), lambda b_, i: (0, 0)),
            pl.BlockSpec((1, cw), lambda b_, i: (0, 0)),
        ],
        out_specs=pl.BlockSpec((1, tm, cw), lambda b_, i: (b_, i, 0)),
        out_shape=jax.ShapeDtypeStruct((bsz, n, cw), F32),
        compiler_params=_cparams(("parallel", "parallel")),
        name="short_conv",
    )(z, z, z, w, b)


def _filt_kernel(f_ref, w1_ref, b1_ref, w2_ref, b2_ref, w3_ref, fq_ref, dl_ref, o_ref, *, n_tok):
    z = f_ref[...]
    tm = z.shape[0]
    fq = fq_ref[...]
    hid = jnp.sin(fq * (jnp.dot(z, w1_ref[...], precision=HIGHEST, preferred_element_type=F32) + b1_ref[...]))
    hid = jnp.sin(fq * (jnp.dot(hid, w2_ref[...], precision=HIGHEST, preferred_element_type=F32) + b2_ref[...]))
    filt = jnp.dot(hid, w3_ref[...], precision=HIGHEST, preferred_element_type=F32)
    filt = filt * (jnp.exp(-z[:, 0:1] * dl_ref[...]) + HY_DECAY_SHIFT)
    m = pl.program_id(0) * tm + lax.broadcasted_iota(jnp.int32, (tm, HY_WIDTH), 0)
    for o in range(HY_ORDER):
        base = o * 2 * HY_WIDTH
        fwd = filt[:, base:base + HY_WIDTH]
        bwd = filt[:, base + HY_WIDTH:base + 2 * HY_WIDTH]
        o_ref[o] = jnp.where(m < n_tok, fwd, 0.0) + jnp.where((m > n_tok) | (m == 0), bwd, 0.0)


def _hyena_filter_feats(n_tok):
    m = jnp.arange(2 * n_tok)
    lag = jnp.where(m < n_tok, m, jnp.where(m > n_tok, 2 * n_tok - m, 0))
    t = jnp.linspace(0.0, 1.0, n_tok, dtype=F32)[lag][:, None]
    w = (2.0 * math.pi * lag.astype(F32) / n_tok)[:, None]
    bands = jnp.linspace(1e-4, HY_BANDS - 1, HY_BANDS, dtype=F32)[None, :]
    feats = jnp.concatenate([t, jnp.cos(bands * w), -jnp.sin(bands * w)], axis=-1)
    return jnp.pad(feats, ((0, 0), (0, HY_POS_PAD - HY_POS_DIM)))


def _hyena_filters(n_tok, w1, b1, w2, b2, w3, freq):
    feats = _hyena_filter_feats(n_tok)
    deltas = jnp.abs(jnp.linspace(math.log(HY_DECAY_TARGET) / HY_DECAY_SHORT,
                                  math.log(HY_DECAY_TARGET) / HY_DECAY_LONG, HY_FILTER_OUT, dtype=F32))[None, :]
    w1p = jnp.pad(w1, ((0, HY_POS_PAD - HY_POS_DIM), (0, 0)))
    n2 = 2 * n_tok
    tm = min(n2, 512)
    full = lambda shape: pl.BlockSpec(shape, lambda i: (0,) * len(shape))
    hh = HY_FILTER_HIDDEN
    return pl.pallas_call(
        functools.partial(_filt_kernel, n_tok=n_tok),
        grid=(n2 // tm,),
        in_specs=[
            pl.BlockSpec((tm, HY_POS_PAD), lambda i: (i, 0)),
            full((HY_POS_PAD, hh)), full((1, hh)), full((hh, hh)), full((1, hh)),
            full((hh, HY_FILTER_OUT)), full((1, hh)), full((1, HY_FILTER_OUT)),
        ],
        out_specs=pl.BlockSpec((HY_ORDER, tm, HY_WIDTH), lambda i: (0, i, 0)),
        out_shape=jax.ShapeDtypeStruct((HY_ORDER, n2, HY_WIDTH), F32),
        compiler_params=_cparams(("parallel",)),
        name="hyena_filter",
    )(feats, w1p, b1[None], w2, b2[None], w3, freq[None], deltas)


def _fft_split(n_fft):
    if n_fft <= 1024:
        return n_fft, 1
    n1 = 1 << (int(math.log2(n_fft)) // 2)
    return n1, n_fft // n1


def _cis(num, den, sign):
    ang = (2.0 * math.pi / den) * num.astype(F32)
    return jnp.cos(ang), sign * jnp.sin(ang)


def _fft_tables(n_fft):
    n1, n2 = _fft_split(n_fft)
    a = jnp.arange(n1)
    f1r, f1i = _cis((a[:, None] * a[None, :]) % n1, n1, -1.0)
    s1 = jnp.concatenate([f1r, f1i], axis=0).astype(BF16)
    half = n1 // 2
    s3 = (jnp.concatenate([f1r[:half], -f1i[:half]], axis=0) / n_fft).astype(BF16)
    if n2 == 1:
        return s1, s3, None, None
    c = jnp.arange(n1)[:, None, None]
    dd = jnp.arange(n2)[None, :, None]
    bb = jnp.arange(n2)[None, None, :]
    fr, fi = _cis((bb * (c + n1 * dd)) % n_fft, n_fft, -1.0)
    ft = jnp.concatenate([fr, fi], axis=1).astype(BF16)
    frt, fit = jnp.swapaxes(fr, 1, 2), jnp.swapaxes(fi, 1, 2)
    it = jnp.concatenate([frt, -fit], axis=1).astype(BF16)
    return s1, s3, ft, it


def _stacked_cdot(s, xr, xi, m):
    p = _dot(s, xr.astype(BF16))
    if xi is None:
        return p[:m], p[m:]
    q = _dot(s, xi.astype(BF16))
    return p[:m] - q[m:], q[:m] + p[m:]


def _fft_a_kernel(*refs, parts, n1, has_g):
    if has_g:
        x_ref, s_ref, g_ref, y_ref = refs
    else:
        x_ref, s_ref, y_ref = refs
    yr, yi = _stacked_cdot(s_ref[...], x_ref[0], x_ref[1] if parts == 2 else None, n1)
    if has_g:
        yr, yi = _cmul(yr, yi, g_ref[0], g_ref[1])
    y_ref[0] = yr
    y_ref[1] = yi


def _fft_a(x, s1, g=None):
    parts, a_rows, cols = x.shape
    n1 = s1.shape[0] // 2
    tc = min(cols, 2048)
    in_specs = [
        pl.BlockSpec((parts, a_rows, tc), lambda j: (0, 0, j)),
        pl.BlockSpec((2 * n1, a_rows), lambda j: (0, 0)),
    ]
    args = [x, s1[:, :a_rows]]
    if g is not None:
        in_specs.append(pl.BlockSpec((2, n1, tc), lambda j: (0, 0, j)))
        args.append(g)
    return pl.pallas_call(
        functools.partial(_fft_a_kernel, parts=parts, n1=n1, has_g=g is not None),
        grid=(cols // tc,),
        in_specs=in_specs,
        out_specs=pl.BlockSpec((2, n1, tc), lambda j: (0, 0, j)),
        out_shape=jax.ShapeDtypeStruct((2, n1, cols), F32),
        compiler_params=_cparams(("parallel",)),
        name="fft_stage_a",
    )(*args)


def _fft_b_kernel(*refs, n2, inverse):
    if inverse:
        y_ref, ft_ref, it_ref, g_ref, o_ref = refs
    else:
        y_ref, ft_ref, o_ref = refs
    xr, xi = _stacked_cdot(ft_ref[0], y_ref[0, 0], y_ref[1, 0], n2)
    if inverse:
        zr, zi = _cmul(xr, xi, g_ref[0, 0], g_ref[1, 0])
        xr, xi = _stacked_cdot(it_ref[0], zr, zi, n2)
    o_ref[0, 0] = xr
    o_ref[1, 0] = xi


def _fft_b(y, ft, it=None, g=None):
    _, n1, n2, ch = y.shape
    inverse = it is not None
    blk = pl.BlockSpec((2, 1, n2, ch), lambda c: (0, c, 0, 0))
    mat = pl.BlockSpec((1, 2 * n2, n2), lambda c: (c, 0, 0))
    in_specs, args = [blk, mat], [y, ft]
    if inverse:
        in_specs += [mat, blk]
        args += [it, g]
    return pl.pallas_call(
        functools.partial(_fft_b_kernel, n2=n2, inverse=inverse),
        grid=(n1,),
        in_specs=in_specs,
        out_specs=blk,
        out_shape=jax.ShapeDtypeStruct(y.shape, F32),
        compiler_params=_cparams(("parallel",)),
        name="fft_stage_b",
    )(*args)


def _fft_c_kernel(y_ref, s_ref, x_ref, g_ref, b_ref, o_ref, *, half):
    cr, ci = _stacked_cdot(s_ref[...], y_ref[0], y_ref[1], half)
    bias = b_ref[...]
    o_ref[0] = g_ref[0] * (cr + bias * x_ref[0])
    o_ref[1] = g_ref[1] * (ci + bias * x_ref[1])


def _fft_c(y, s3, x, gate, bias_cols):
    _, n1, cols = y.shape
    half = n1 // 2
    tc = min(cols, 2048)
    io = pl.BlockSpec((2, half, tc), lambda j: (0, 0, j))
    return pl.pallas_call(
        functools.partial(_fft_c_kernel, half=half),
        grid=(cols // tc,),
        in_specs=[
            pl.BlockSpec((2, n1, tc), lambda j: (0, 0, j)),
            pl.BlockSpec((n1, n1), lambda j: (0, 0)),
            io, io,
            pl.BlockSpec((1, tc), lambda j: (0, j)),
        ],
        out_specs=io,
        out_shape=jax.ShapeDtypeStruct((2, half, cols), F32),
        compiler_params=_cparams(("parallel",)),
        name="fft_stage_c",
    )(y, s3, x, gate, bias_cols)


def _hyena_mixer(u3, gfilt, bias):
    bsz, n, _ = u3.shape
    assert bsz == 2, "the complex packing of the long convolution pairs exactly two batch rows"
    ch = HY_WIDTH
    n_fft = 2 * n
    n1, n2 = _fft_split(n_fft)
    s1, s3, ft, it = _fft_tables(n_fft)
    cols = n2 * ch
    y = u3[:, :, :ch].reshape(2, n1 // 2, cols)
    for o in range(HY_ORDER):
        gate = u3[:, :, (o + 1) * ch:(o + 2) * ch].reshape(2, n1 // 2, cols)
        gspec = _fft_a(gfilt[o].reshape(1, n1, cols), s1)
        bias_cols = jnp.tile(bias[o], n2)[None, :]
        if n2 == 1:
            spec = _fft_a(y, s1, gspec)
        else:
            gspec = _fft_b(gspec.reshape(2, n1, n2, ch), ft)
            spec = _fft_b(_fft_a(y, s1).reshape(2, n1, n2, ch), ft, it, gspec).reshape(2, n1, cols)
        y = _fft_c(spec, s3, y, gate, bias_cols)
    return y.reshape(2, n, ch)


def _gelu_tanh(x):
    return 0.5 * x * (1.0 + jnp.tanh(math.sqrt(2.0 / math.pi) * (x + 0.044715 * (x * x * x))))


def _merge_kernel(x_ref, g0_ref, g1_ref, g2_ref, a_ref, yf_ref, yb_ref, u_ref, e_ref, d_ref,
                  wglu_ref, wm_ref, ws_ref, wh_ref, wo_ref, gt_ref, o_ref):
    att = _dot(a_ref[0, 0], wm_ref[0])
    for h in range(1, MLA_HEADS):
        att = att + _dot(a_ref[0, h], wm_ref[h])
    y = yf_ref[0, 0] + yb_ref[0, 0] + d_ref[...] * u_ref[0]
    gy = _gelu_tanh(y)
    s5 = gy * jax.nn.sigmoid(_dot(gy.astype(BF16), wglu_ref[...]))
    s5 = _dot(s5.astype(BF16), ws_ref[...])
    hy = _dot(e_ref[0].astype(BF16), wh_ref[...])
    merged = jax.nn.sigmoid(g0_ref[0]) * att
    merged = merged + jax.nn.sigmoid(g1_ref[0]) * s5
    merged = merged + jax.nn.sigmoid(g2_ref[0]) * hy
    o_ref[0] = x_ref[0] + gt_ref[0] * _dot(merged.astype(BF16), wo_ref[...])


def _merge(x, z, att, y5, e, s5_d, w_glu, w_mla, w_s5, w_hy, w_out, gate):
    bsz, n, d = x.shape
    tm = min(n, 256)
    full = lambda shape: pl.BlockSpec(shape, lambda b, i: (0,) * len(shape))
    zs5 = Z_S5 // S5_WIDTH
    return pl.pallas_call(
        _merge_kernel,
        grid=(bsz, n // tm),
        in_specs=[
            pl.BlockSpec((1, tm, d), lambda b, i: (b, i, 0)),
            pl.BlockSpec((1, tm, d), lambda b, i: (b, i, 0)),
            pl.BlockSpec((1, tm, d), lambda b, i: (b, i, 1)),
            pl.BlockSpec((1, tm, d), lambda b, i: (b, i, 2)),
            pl.BlockSpec((1, MLA_HEADS, tm, MLA_V), lambda b, i: (b, 0, i, 0)),
            pl.BlockSpec((1, 1, tm, S5_WIDTH), lambda b, i: (0, b, i, 0)),
            pl.BlockSpec((1, 1, tm, S5_WIDTH), lambda b, i: (1, b, i, 0)),
            pl.BlockSpec((1, tm, S5_WIDTH), lambda b, i: (b, i, zs5)),
            pl.BlockSpec((1, tm, HY_WIDTH), lambda b, i: (b, i, 0)),
            full((1, S5_WIDTH)),
            full((S5_WIDTH, S5_WIDTH)),
            full((MLA_HEADS, MLA_V, d)),
            full((S5_WIDTH, d)),
            full((HY_WIDTH, d)),
            full((d, d)),
            pl.BlockSpec((1, 1, d), lambda b, i: (b, 0, 0)),
        ],
        out_specs=pl.BlockSpec((1, tm, d), lambda b, i: (b, i, 0)),
        out_shape=jax.ShapeDtypeStruct((bsz, n, d), F32),
        compiler_params=_cparams(("parallel", "parallel")),
        name="merge",
    )(x, z, z, z, att, y5, y5, z, e, s5_d, w_glu, w_mla, w_s5, w_hy, w_out, gate)


def _ffn_kernel(x_ref, g_ref, sh_ref, sc_ref, gt_ref, wg_ref, wu_ref, wd_ref, fg_ref, o_ref, h_ref, acc_ref,
                *, nk, final):
    k = pl.program_id(2)

    @pl.when(k == 0)
    def _():
        h_ref[...] = _norm_mod(x_ref[0], g_ref[...], sh_ref[0], sc_ref[0]).astype(BF16)
        acc_ref[...] = jnp.zeros_like(acc_ref)

    h = h_ref[...]
    act = jax.nn.silu(_dot(h, wg_ref[...])) * _dot(h, wu_ref[...])
    acc_ref[...] += _dot(act.astype(BF16), wd_ref[...])

    @pl.when(k == nk - 1)
    def _():
        r = x_ref[0] + gt_ref[0] * acc_ref[...]
        o_ref[0] = _rms(r, fg_ref[...]) if final else r


def _ffn(x, g, shift, scale, gate, w_g, w_u, w_d, final_g, final):
    bsz, n, d = x.shape
    dff = w_g.shape[1]
    tm = min(n, 1024)
    tf = 256
    nk = dff // tf
    vec = pl.BlockSpec((1, 1, d), lambda b, i, k: (b, 0, 0))
    row = pl.BlockSpec((1, d), lambda b, i, k: (0, 0))
    return pl.pallas_call(
        functools.partial(_ffn_kernel, nk=nk, final=final),
        grid=(bsz, n // tm, nk),
        in_specs=[
            pl.BlockSpec((1, tm, d), lambda b, i, k: (b, i, 0)),
            row, vec, vec, vec,
            pl.BlockSpec((d, tf), lambda b, i, k: (0, k)),
            pl.BlockSpec((d, tf), lambda b, i, k: (0, k)),
            pl.BlockSpec((tf, d), lambda b, i, k: (k, 0)),
            row,
        ],
        out_specs=pl.BlockSpec((1, tm, d), lambda b, i, k: (b, i, 0)),
        out_shape=jax.ShapeDtypeStruct((bsz, n, d), F32),
        scratch_shapes=[pltpu.VMEM((tm, d), BF16), pltpu.VMEM((tm, d), F32)],
        compiler_params=_cparams(("parallel", "parallel", "arbitrary")),
        name="ffn",
    )(x, g, shift, scale, gate, w_g, w_u, w_d, final_g)


def _rope_tables(n_tok):
    rows = n_tok // GRID_W
    row = jnp.broadcast_to(jnp.arange(rows, dtype=F32)[:, None], (rows, GRID_W)).reshape(-1)
    col = jnp.broadcast_to(jnp.arange(GRID_W, dtype=F32)[None, :], (rows, GRID_W)).reshape(-1)
    n_freq = MLA_ROPE // 4
    inv = ROPE_BASE ** (-jnp.arange(n_freq, dtype=F32) / n_freq)
    ang = jnp.concatenate([row[:, None] * inv, col[:, None] * inv], axis=-1)
    cos, sin = jnp.cos(ang), jnp.sin(ang)
    pad = jnp.zeros((n_tok, HEAD_W - MLA_ROPE), F32)
    return jnp.concatenate([cos, cos, pad], axis=-1), jnp.concatenate([-sin, sin, pad], axis=-1)


def _identity_rope_tables(n_tok):
    one = jnp.ones((n_tok, MLA_ROPE), F32)
    pad = jnp.zeros((n_tok, HEAD_W - MLA_ROPE), F32)
    return jnp.concatenate([one, pad], axis=-1), jnp.zeros((n_tok, HEAD_W), F32)


def _layout_w_in(w):
    kr = w[:, OFF_KR:OFF_S5]
    x1, x2 = kr[:, 0::2], kr[:, 1::2]
    pad = jnp.zeros((w.shape[0], HEAD_W - 3 * (MLA_ROPE // 2)), w.dtype)
    return jnp.concatenate([w[:, OFF_GATE:], w[:, OFF_HY:OFF_GATE], w[:, OFF_S5:OFF_HY],
                            w[:, OFF_CQ:OFF_CKV], w[:, OFF_CKV:OFF_KR], x1, x2, x1, pad], axis=1).astype(BF16)


def _layout_w_uq(w):
    w = w.reshape(MLA_Q_LORA, MLA_HEADS, MLA_NOPE + MLA_ROPE)
    nope, rope = w[..., :MLA_NOPE], w[..., MLA_NOPE:]
    x1, x2 = rope[..., 0::2], rope[..., 1::2]
    z32 = jnp.zeros_like(rope)
    wq = jnp.concatenate([x1, x2, z32, nope], axis=-1)
    wqs = jnp.concatenate([x2, x1, z32, jnp.zeros_like(nope)], axis=-1)
    shape = (MLA_Q_LORA, MLA_HEADS * HEAD_W)
    return wq.reshape(shape).astype(BF16), wqs.reshape(shape).astype(BF16)


def _layout_w_ukv(w):
    w = w.reshape(MLA_KV_LORA, MLA_HEADS, MLA_NOPE + MLA_V)
    nope, val = w[..., :MLA_NOPE], w[..., MLA_NOPE:]
    wk = jnp.concatenate([jnp.zeros_like(nope), nope], axis=-1).reshape(MLA_KV_LORA, MLA_HEADS * HEAD_W)
    wvt = jnp.concatenate([val, jnp.zeros_like(val)], axis=-1).reshape(MLA_KV_LORA, MLA_HEADS * HEAD_W).T
    return wk.astype(BF16), wvt.astype(BF16)


def kernel(x, c, ctx, c_ctx, ada_w, ada_b, norm_mix, w_in, mla_q_norm, mla_w_uq, mla_kv_norm, mla_w_ukv,
           s5_lam_re, s5_lam_im, s5_log_dt, s5_b_re, s5_b_im, s5_c_re, s5_c_im, s5_d, s5_w_glu,
           hy_conv_w, hy_conv_b, hy_f_w1, hy_f_b1, hy_f_w2, hy_f_b2, hy_f_w3, hy_f_freq, hy_bias,
           w_branch_mla, w_branch_s5, w_branch_hy, w_out, norm_ffn, ffn_w_gu, ffn_w_down, final_norm):
    bsz, n_tok, d = x.shape
    n_ctx = ctx.shape[1]
    depth = ada_w.shape[0]
    rope_x = _rope_tables(n_tok)
    rope_c = _identity_rope_tables(n_ctx)
    h_zero = jnp.zeros((bsz, 2, 4, S5_HSTATE), F32)

    for i in range(depth):
        last = i == depth - 1
        mx = (jax.nn.silu(c) @ ada_w[i] + ada_b[i]).reshape(bsz, N_MOD, 1, d)
        mc = jnp.broadcast_to((jax.nn.silu(c_ctx) @ ada_w[i] + ada_b[i]).reshape(1, N_MOD, 1, d),
                              (bsz, N_MOD, 1, d))
        w_z = _layout_w_in(w_in[i])
        wq, wqs = _layout_w_uq(mla_w_uq[i])
        wk, wvt = _layout_w_ukv(mla_w_ukv[i])
        qn, kvn = mla_q_norm[i][None], mla_kv_norm[i][None]
        s5_tabs = _s5_tables(s5_lam_re[i], s5_lam_im[i], s5_log_dt[i], s5_b_re[i], s5_b_im[i],
                             s5_c_re[i], s5_c_im[i])
        fparams = (hy_f_w1[i], hy_f_b1[i], hy_f_w2[i], hy_f_b2[i], hy_f_w3[i], hy_f_freq[i])
        merge_w = (s5_d[i][None], s5_w_glu[i].astype(BF16),
                   w_branch_mla[i].reshape(MLA_HEADS, MLA_V, d).astype(BF16),
                   w_branch_s5[i].astype(BF16), w_branch_hy[i].astype(BF16), w_out[i].astype(BF16))
        w_g = ffn_w_gu[i][:, :D_FF].astype(BF16)
        w_u = ffn_w_gu[i][:, D_FF:].astype(BF16)
        w_d = ffn_w_down[i].astype(BF16)
        g_mix, g_ffn = norm_mix[i][None], norm_ffn[i][None]

        zx = _proj_in(x, g_mix, mx[:, 0], mx[:, 1], w_z)
        zc = _proj_in(ctx, g_mix, mc[:, 0], mc[:, 1], w_z)

        q_c, k_c, vt_c = _mla_prep(zc, qn, kvn, wq, wqs, wk, wvt, *rope_c)
        q_x, k_x, vt_x = _mla_prep(zx, qn, kvn, wq, wqs, wk, wvt, *rope_x)
        a_x = _attention(q_x, jnp.concatenate([k_c, k_x], axis=2), jnp.concatenate([vt_c, vt_x], axis=3))

        y5_c, finals = _s5_scan(zc, h_zero, s5_tabs)
        y5_x, _ = _s5_scan(zx, finals, s5_tabs)

        e_x = _hyena_mixer(_short_conv(zx, hy_conv_w[i], hy_conv_b[i][None]),
                           _hyena_filters(n_tok, *fparams), hy_bias[i])

        x = _merge(x, zx, a_x, y5_x, e_x, *merge_w, mx[:, 2])
        x = _ffn(x, g_ffn, mx[:, 3], mx[:, 4], mx[:, 5], w_g, w_u, w_d, final_norm[None], last)

        if not last:
            a_c = _attention(q_c, k_c, vt_c)
            e_c = _hyena_mixer(_short_conv(zc, hy_conv_w[i], hy_conv_b[i][None]),
                               _hyena_filters(n_ctx, *fparams), hy_bias[i])
            ctx = _merge(ctx, zc, a_c, y5_c, e_c, *merge_w, mc[:, 2])
            ctx = _ffn(ctx, g_ffn, mc[:, 3], mc[:, 4], mc[:, 5], w_g, w_u, w_d, final_norm[None], False)
    return x
```

```python
import functools
import math

import jax
import jax.numpy as jnp
import numpy as np
from jax import lax
from jax.experimental import pallas as pl
from jax.experimental.pallas import tpu as pltpu

F32 = jnp.float32
BF16 = jnp.bfloat16
HIGHEST = lax.Precision.HIGHEST

D_MODEL = 1024
GRID_W = 64
NORM_EPS = 1e-6
N_MOD = 6

MLA_HEADS = 8
MLA_NOPE = 64
MLA_ROPE = 32
MLA_V = 64
MLA_Q_LORA = 256
MLA_KV_LORA = 128
ROPE_BASE = 10000.0
HEAD_W = 128
VT_ROWS = 80
QK_SCALE_LOG2 = (MLA_NOPE + MLA_ROPE) ** -0.5 * math.log2(math.e)
ATTN_SUB = 256
ATTN_UNROLL = 3

S5_WIDTH = 512
S5_GROUP = 16
S5_GROUPS = 32
S5_STATE = 64
S5_HALF = 256
S5_HSTATE = 1024
S5_CHUNK = 128

HY_WIDTH = 512
HY_ORDER = 2
HY_BANDS = 16
HY_POS_DIM = 1 + 2 * HY_BANDS
HY_POS_PAD = 64
HY_FILTER_HIDDEN = 64
HY_FILTER_OUT = HY_ORDER * 2 * HY_WIDTH
HY_DECAY_TARGET = 1e-2
HY_DECAY_SHORT = 0.3
HY_DECAY_LONG = 1.5
HY_DECAY_SHIFT = 0.05

D_FF = 2816

OFF_CQ = 0
OFF_CKV = OFF_CQ + MLA_Q_LORA
OFF_KR = OFF_CKV + MLA_KV_LORA
OFF_S5 = OFF_KR + MLA_ROPE
OFF_HY = OFF_S5 + S5_WIDTH
OFF_GATE = OFF_HY + 3 * HY_WIDTH

Z_GATE = 0
Z_HY = 3072
Z_S5 = 4608
Z_MLA = 5120
Z_WIDTH = 5632

VMEM_LIMIT_V7X = 52 * 1024 * 1024


def _cparams(sem, flags=None):
    return pltpu.CompilerParams(dimension_semantics=sem, vmem_limit_bytes=VMEM_LIMIT_V7X, flags=flags)


def _dot(a, b):
    return jnp.dot(a, b, preferred_element_type=F32)


def _rms(x, g):
    return x * lax.rsqrt(jnp.mean(x * x, axis=-1, keepdims=True) + NORM_EPS) * g


def _norm_mod(x, g, shift, scale):
    return _rms(x, g) * (1.0 + scale) + shift


def _proj_kernel(x_ref, g_ref, sh_ref, sc_ref, w_ref, o_ref, h_ref):
    @pl.when(pl.program_id(2) == 0)
    def _():
        h_ref[...] = _norm_mod(x_ref[0], g_ref[...], sh_ref[0], sc_ref[0]).astype(BF16)

    o_ref[0] = _dot(h_ref[...], w_ref[...])


def _proj_in(x, g, shift, scale, w):
    bsz, n, d = x.shape
    nz = w.shape[1]
    tm = min(n, 1024)
    tn = 512
    return pl.pallas_call(
        _proj_kernel,
        grid=(bsz, n // tm, nz // tn),
        in_specs=[
            pl.BlockSpec((1, tm, d), lambda b, i, j: (b, i, 0)),
            pl.BlockSpec((1, d), lambda b, i, j: (0, 0)),
            pl.BlockSpec((1, 1, d), lambda b, i, j: (b, 0, 0)),
            pl.BlockSpec((1, 1, d), lambda b, i, j: (b, 0, 0)),
            pl.BlockSpec((d, tn), lambda b, i, j: (0, j)),
        ],
        out_specs=pl.BlockSpec((1, tm, tn), lambda b, i, j: (b, i, j)),
        out_shape=jax.ShapeDtypeStruct((bsz, n, nz), F32),
        scratch_shapes=[pltpu.VMEM((tm, d), BF16)],
        compiler_params=_cparams(("parallel", "parallel", "arbitrary")),
        name="proj_in",
    )(x, g, shift, scale, w)


def _mla_prep_kernel(z_ref, qn_ref, kvn_ref, wq_ref, wqs_ref, wk_ref, wvt_ref, c_ref, s_ref, q_ref, k_ref, vt_ref):
    z = z_ref[0]
    hq = _rms(z[:, :MLA_Q_LORA], qn_ref[...]).astype(BF16)
    hkv = _rms(z[:, MLA_Q_LORA:MLA_Q_LORA + MLA_KV_LORA], kvn_ref[...]).astype(BF16)
    krb = z[:, MLA_Q_LORA + MLA_KV_LORA:]
    qa = _dot(hq, wq_ref[...])
    qb = _dot(hq, wqs_ref[...])
    kn = _dot(hkv, wk_ref[...])
    vt = lax.dot_general(wvt_ref[...], hkv, (((1,), (1,)), ((), ())), preferred_element_type=F32)
    ck = c_ref[...]
    sn = s_ref[...]
    lane = lax.broadcasted_iota(jnp.int32, ck.shape, 1)
    cq = jnp.where(lane < MLA_ROPE, ck, 1.0)
    kr = krb * ck + pltpu.roll(krb, HEAD_W - MLA_ROPE // 2, 1) * sn
    row = lax.broadcasted_iota(jnp.int32, (VT_ROWS, z.shape[0]), 0)
    for h in range(MLA_HEADS):
        sl = slice(h * HEAD_W, (h + 1) * HEAD_W)
        q_ref[0, h] = ((qa[:, sl] * cq + qb[:, sl] * sn) * QK_SCALE_LOG2).astype(BF16)
        k_ref[0, h] = (kr + kn[:, sl]).astype(BF16)
        vt_ref[0, h] = jnp.where(row == MLA_V, 1.0, vt[h * VT_ROWS:(h + 1) * VT_ROWS, :]).astype(BF16)


def _mla_prep(z, qn, kvn, wq, wqs, wk, wvt, ctab, stab):
    bsz, n, _ = z.shape
    tm = min(n, 512)
    hw = MLA_HEADS * HEAD_W
    zb = Z_MLA // 512
    full = lambda shape: pl.BlockSpec(shape, lambda b, i: (0,) * len(shape))
    return pl.pallas_call(
        _mla_prep_kernel,
        grid=(bsz, n // tm),
        in_specs=[
            pl.BlockSpec((1, tm, 512), lambda b, i: (b, i, zb)),
            full((1, MLA_Q_LORA)),
            full((1, MLA_KV_LORA)),
            full((MLA_Q_LORA, hw)),
            full((MLA_Q_LORA, hw)),
            full((MLA_KV_LORA, hw)),
            full((MLA_HEADS * VT_ROWS, MLA_KV_LORA)),
            pl.BlockSpec((tm, HEAD_W), lambda b, i: (i, 0)),
            pl.BlockSpec((tm, HEAD_W), lambda b, i: (i, 0)),
        ],
        out_specs=[
            pl.BlockSpec((1, MLA_HEADS, tm, HEAD_W), lambda b, i: (b, 0, i, 0)),
            pl.BlockSpec((1, MLA_HEADS, tm, HEAD_W), lambda b, i: (b, 0, i, 0)),
            pl.BlockSpec((1, MLA_HEADS, VT_ROWS, tm), lambda b, i: (b, 0, 0, i)),
        ],
        out_shape=[
            jax.ShapeDtypeStruct((bsz, MLA_HEADS, n, HEAD_W), BF16),
            jax.ShapeDtypeStruct((bsz, MLA_HEADS, n, HEAD_W), BF16),
            jax.ShapeDtypeStruct((bsz, MLA_HEADS, VT_ROWS, n), BF16),
        ],
        compiler_params=_cparams(("parallel", "parallel")),
        name="mla_prep",
    )(z, qn, kvn, wq, wqs, wk, wvt, ctab, stab)


def _attn_kernel(q_ref, k_ref, vt_ref, o_ref, s0_ref, s1_ref, s2_ref, p0_ref, p1_ref, p2_ref, acc_ref, *, tk, nkc):
    q = q_ref[0, 0]
    tq = q.shape[0]
    s_refs = (s0_ref, s1_ref, s2_ref)
    p_refs = (p0_ref, p1_ref, p2_ref)

    ts = min(tk, ATTN_SUB)
    subs = [slice(r, r + ts) for r in range(0, tk, ts)]

    def scores_sub(j, s_ref, sub, cmax):
        off = pl.multiple_of(j * tk + sub.start, ts)
        s = lax.dot_general(k_ref[0, 0, pl.ds(off, ts), :], q, (((1,), (1,)), ((), ())),
                            preferred_element_type=F32)
        s_ref[sub, :] = s
        cm = jnp.max(s, axis=0, keepdims=True)
        return cm if cmax is None else jnp.maximum(cmax, cm)

    def softmax_sub(slot, sub, mn):
        p_refs[slot][sub, :] = jnp.exp2(s_refs[slot][sub, :] - mn).astype(BF16)

    def step(i, slot, carry, do_softmax, do_scores):
        cm, m, alpha = carry
        mn = jnp.maximum(m, cm)
        off = pl.multiple_of(i * tk, tk)
        acc_ref[...] = alpha * acc_ref[...] + _dot(vt_ref[0, 0, :, pl.ds(off, tk)], p_refs[slot][...])
        cm_new = None
        for sub in subs:
            if do_softmax:
                softmax_sub((slot + 1) % 3, sub, mn)
            if do_scores:
                cm_new = scores_sub(i + 2, s_refs[(slot + 2) % 3], sub, cm_new)
        if do_softmax:
            alpha, m = jnp.exp2(m - mn), mn
        return (cm_new if do_scores else cm), m, alpha

    m = jnp.full((1, tq), -1e30, F32)
    acc_ref[...] = jnp.zeros_like(acc_ref)
    cm, cm_next = None, None
    for sub in subs:
        cm = scores_sub(0, s0_ref, sub, cm)
        if nkc > 1:
            cm_next = scores_sub(1, s1_ref, sub, cm_next)
    mn = jnp.maximum(m, cm)
    for sub in subs:
        softmax_sub(0, sub, mn)
    carry = (cm_next if nkc > 1 else cm, mn, jnp.exp2(m - mn))

    def body(t, carry):
        for r in range(ATTN_UNROLL):
            carry = step(ATTN_UNROLL * t + r, r % 3, carry, True, True)
        return carry

    nloop = max(nkc - 2, 0) // ATTN_UNROLL
    carry = lax.fori_loop(0, nloop, body, carry)
    for i in range(ATTN_UNROLL * nloop, nkc):
        carry = step(i, i % 3, carry, i + 1 < nkc, i + 2 < nkc)
    acc = acc_ref[...]
    out = acc * (1.0 / acc[MLA_V:MLA_V + 1, :])
    out = jnp.concatenate([out, jnp.zeros((HEAD_W - VT_ROWS, tq), F32)], axis=0).T
    o_ref[0, 0] = out[:, :MLA_V].astype(BF16)


def _kv_chunk(nk):
    for tk in (1280, 1024, 640, 512, 256, 128):
        if nk % tk == 0:
            return tk
    raise ValueError(f"unsupported key count {nk}")


def _attention(q, k, vt):
    bsz, nh, nq, _ = q.shape
    nk = k.shape[2]
    tq = min(nq, 512)
    tk = _kv_chunk(nk)
    return pl.pallas_call(
        functools.partial(_attn_kernel, tk=tk, nkc=nk // tk),
        grid=(bsz, nh, nq // tq),
        in_specs=[
            pl.BlockSpec((1, 1, tq, HEAD_W), lambda b, h, i: (b, h, i, 0)),
            pl.BlockSpec((1, 1, nk, HEAD_W), lambda b, h, i: (b, h, 0, 0)),
            pl.BlockSpec((1, 1, VT_ROWS, nk), lambda b, h, i: (b, h, 0, 0)),
        ],
        out_specs=pl.BlockSpec((1, 1, tq, MLA_V), lambda b, h, i: (b, h, i, 0)),
        out_shape=jax.ShapeDtypeStruct((bsz, nh, nq, MLA_V), BF16),
        scratch_shapes=[pltpu.VMEM((tk, tq), F32)] * 3 + [pltpu.VMEM((tk, tq), BF16)] * 3
        + [pltpu.VMEM((VT_ROWS, tq), F32)],
        compiler_params=_cparams(("parallel", "parallel", "arbitrary")),
        name="attention",
    )(q, k, vt)


def _cmul(ar, ai, br, bi):
    return ar * br - ai * bi, ar * bi + ai * br


def _s5_kernel(u_ref, h0_ref, bm_ref, cm_ref, wn_ref, wp_ref, l1_ref, tri_ref, y_ref, hf_ref, carry_ref, *, tc, nc):
    d = pl.program_id(1)
    i = pl.program_id(2)

    @pl.when(i == 0)
    def _():
        carry_ref[...] = h0_ref[0, 0]

    u = u_ref[0]
    tri = tri_ref[0]
    ys = []
    for hf in range(2):
        ub = u[:, hf * S5_HALF:(hf + 1) * S5_HALF].astype(BF16)
        bu = _dot(ub, bm_ref[0, hf])
        xr, xi = _cmul(wn_ref[0, hf, 0], wn_ref[0, hf, 1], bu[:, :S5_HSTATE], bu[:, S5_HSTATE:])
        s = _dot(tri, jnp.concatenate([xr, xi], axis=1).astype(BF16))
        cr, ci = _cmul(l1_ref[0, hf, 0], l1_ref[0, hf, 1],
                       carry_ref[2 * hf:2 * hf + 1, :], carry_ref[2 * hf + 1:2 * hf + 2, :])
        hr, hi = _cmul(wp_ref[0, hf, 0], wp_ref[0, hf, 1], s[:, :S5_HSTATE] + cr, s[:, S5_HSTATE:] + ci)
        ys.append(_dot(jnp.concatenate([hr, hi], axis=1).astype(BF16), cm_ref[0, hf]))
        carry_ref[2 * hf:2 * hf + 1, :] = jnp.where(d == 0, hr[tc - 1:tc], hr[0:1])
        carry_ref[2 * hf + 1:2 * hf + 2, :] = jnp.where(d == 0, hi[tc - 1:tc], hi[0:1])
    y_ref[0, 0] = jnp.concatenate(ys, axis=1)

    @pl.when(i == nc - 1)
    def _():
        hf_ref[0, 0] = carry_ref[...]


def _s5_scan(z, h0, tabs):
    bm, cm, wn, wp, l1, tri = tabs
    bsz, n, _ = z.shape
    tc = S5_CHUNK
    nc = n // tc
    zb = Z_S5 // S5_WIDTH

    def chunk(d, i):
        return jnp.where(d == 0, i, nc - 1 - i)

    return pl.pallas_call(
        functools.partial(_s5_kernel, tc=tc, nc=nc),
        grid=(bsz, 2, nc),
        in_specs=[
            pl.BlockSpec((1, tc, S5_WIDTH), lambda b, d, i: (b, chunk(d, i), zb)),
            pl.BlockSpec((1, 1, 4, S5_HSTATE), lambda b, d, i: (b, d, 0, 0)),
            pl.BlockSpec((1, 2, S5_HALF, 2 * S5_HSTATE), lambda b, d, i: (d, 0, 0, 0)),
            pl.BlockSpec((1, 2, 2 * S5_HSTATE, S5_HALF), lambda b, d, i: (d, 0, 0, 0)),
            pl.BlockSpec((1, 2, 2, tc, S5_HSTATE), lambda b, d, i: (d, 0, 0, 0, 0)),
            pl.BlockSpec((1, 2, 2, tc, S5_HSTATE), lambda b, d, i: (d, 0, 0, 0, 0)),
            pl.BlockSpec((1, 2, 2, 1, S5_HSTATE), lambda b, d, i: (d, 0, 0, 0, 0)),
            pl.BlockSpec((1, tc, tc), lambda b, d, i: (d, 0, 0)),
        ],
        out_specs=[
            pl.BlockSpec((1, 1, tc, S5_WIDTH), lambda b, d, i: (d, b, chunk(d, i), 0)),
            pl.BlockSpec((1, 1, 4, S5_HSTATE), lambda b, d, i: (b, d, 0, 0)),
        ],
        out_shape=[
            jax.ShapeDtypeStruct((2, bsz, n, S5_WIDTH), F32),
            jax.ShapeDtypeStruct((bsz, 2, 4, S5_HSTATE), F32),
        ],
        scratch_shapes=[pltpu.VMEM((4, S5_HSTATE), F32)],
        compiler_params=_cparams(("parallel", "parallel", "arbitrary")),
        name="s5_scan",
    )(z, h0, bm, cm, wn, wp, l1, tri)


def _s5_tables(lam_re, lam_im, log_dt, b_re, b_im, c_re, c_im):
    tc = S5_CHUNK
    dt = jnp.exp(log_dt)[..., None]
    zr, zi = lam_re * dt, lam_im * dt
    mag = jnp.exp(zr)
    lbr, lbi = mag * jnp.cos(zi), mag * jnp.sin(zi)
    den = lam_re * lam_re + lam_im * lam_im
    nr, ni = lbr - 1.0, lbi
    cfr = (nr * lam_re + ni * lam_im) / den
    cfi = (ni * lam_re - nr * lam_im) / den
    bbr = cfr[..., None] * b_re - cfi[..., None] * b_im
    bbi = cfr[..., None] * b_im + cfi[..., None] * b_re
    eye = jnp.eye(S5_GROUP, dtype=F32)

    def blockdiag_in(b):
        b = b.reshape(2, 2, S5_GROUP, S5_STATE, S5_GROUP)
        return jnp.einsum('gk,dhgpn->dhgnkp', eye, b).reshape(2, 2, S5_HALF, S5_HSTATE)

    def blockdiag_out(c):
        c = c.reshape(2, 2, S5_GROUP, S5_GROUP, S5_STATE)
        return jnp.einsum('gk,dhgnp->dhgpkn', eye, c).reshape(2, 2, S5_HSTATE, S5_HALF)

    bm = jnp.concatenate([blockdiag_in(bbr), blockdiag_in(bbi)], axis=-1).astype(BF16)
    cm = jnp.concatenate([blockdiag_out(c_re), blockdiag_out(-c_im)], axis=-2).astype(BF16)

    def powers(k):
        zr_ = zr.reshape(2, 2, 1, S5_HSTATE)
        zi_ = zi.reshape(2, 2, 1, S5_HSTATE)
        kk = k[:, None, :, None]
        m = jnp.exp(kk * zr_)
        return jnp.stack([m * jnp.cos(kk * zi_), m * jnp.sin(kk * zi_)], axis=2)

    t = jnp.arange(tc, dtype=F32)
    wn = powers(jnp.stack([-t, -(tc - 1 - t)]))
    wp = powers(jnp.stack([t, tc - 1 - t]))
    l1 = powers(jnp.ones((2, 1), F32))
    r = jnp.arange(tc)
    tri = jnp.stack([r[:, None] >= r[None, :], r[:, None] <= r[None, :]]).astype(BF16)
    return bm, cm, wn, wp, l1, tri


def _sconv_kernel(z_ref, zp_ref, zn_ref, w_ref, b_ref, v_ref, g1_ref, g2_ref, *, nt):
    i = pl.program_id(1)
    z = z_ref[0]
    tm = z.shape[0]
    row = lax.broadcasted_iota(jnp.int32, z.shape, 0)
    prev = jnp.where(i > 0, zp_ref[0, 7:8, :], 0.0)
    nxt = jnp.where(i < nt - 1, zn_ref[0, 0:1, :], 0.0)
    up = jnp.where(row == 0, prev, pltpu.roll(z, 1, 0))
    dn = jnp.where(row == tm - 1, nxt, pltpu.roll(z, tm - 1, 0))
    u = up * w_ref[0:1, :] + z * w_ref[1:2, :] + dn * w_ref[2:3, :] + b_ref[...]
    for k, o_ref in enumerate((v_ref, g1_ref, g2_ref)):
        o_ref[0] = u[:, k * HY_WIDTH:(k + 1) * HY_WIDTH]


def _short_conv(z, w, b):
    bsz, n, _ = z.shape
    cw = 3 * HY_WIDTH
    tm = min(n, 512)
    nt = n // tm
    zb = Z_HY // cw
    r8 = tm // 8
    out = pl.BlockSpec((1, tm, HY_WIDTH), lambda b_, i: (b_, i, 0))
    return pl.pallas_call(
        functools.partial(_sconv_kernel, nt=nt),
        grid=(bsz, nt),
        in_specs=[
            pl.BlockSpec((1, tm, cw), lambda b_, i: (b_, i, zb)),
            pl.BlockSpec((1, 8, cw), lambda b_, i: (b_, jnp.maximum(i * r8 - 1, 0), zb)),
            pl.BlockSpec((1, 8, cw), lambda b_, i: (b_, jnp.minimum((i + 1) * r8, n // 8 - 1), zb)),
            pl.BlockSpec((3, cw), lambda b_, i: (0, 0)),
            pl.BlockSpec((1, cw), lambda b_, i: (0, 0)),
        ],
        out_specs=[out] * 3,
        out_shape=[jax.ShapeDtypeStruct((bsz, n, HY_WIDTH), F32)] * 3,
        compiler_params=_cparams(("parallel", "parallel")),
        name="short_conv",
    )(z, z, z, w, b)


def _filt_kernel(f_ref, w1_ref, b1_ref, w2_ref, b2_ref, w3h_ref, w3l_ref, fq_ref, dl_ref, o_ref, *, n_tok):
    z = f_ref[...]
    tm = z.shape[0]
    fq = fq_ref[...]
    hid = jnp.sin(fq * (jnp.dot(z, w1_ref[...], precision=HIGHEST, preferred_element_type=F32) + b1_ref[...]))
    hid = jnp.sin(fq * (jnp.dot(hid, w2_ref[...], precision=HIGHEST, preferred_element_type=F32) + b2_ref[...]))
    hid_hi = hid.astype(BF16)
    hid_lo = (hid - hid_hi.astype(F32)).astype(BF16)
    filt = _dot(hid_hi, w3h_ref[...]) + (_dot(hid_hi, w3l_ref[...]) + _dot(hid_lo, w3h_ref[...]))
    filt = filt * (jnp.exp(-z[:, 0:1] * dl_ref[...]) + HY_DECAY_SHIFT)
    m = pl.program_id(0) * tm + lax.broadcasted_iota(jnp.int32, (tm, HY_WIDTH), 0)
    for o in range(HY_ORDER):
        base = o * 2 * HY_WIDTH
        fwd = filt[:, base:base + HY_WIDTH]
        bwd = filt[:, base + HY_WIDTH:base + 2 * HY_WIDTH]
        o_ref[o] = jnp.where(m < n_tok, fwd, 0.0) + jnp.where((m > n_tok) | (m == 0), bwd, 0.0)


def _hyena_filter_feats(n_tok):
    m = jnp.arange(2 * n_tok)
    lag = jnp.where(m < n_tok, m, jnp.where(m > n_tok, 2 * n_tok - m, 0))
    t = (lag.astype(F32) / (n_tok - 1))[:, None]
    w = (2.0 * math.pi * lag.astype(F32) / n_tok)[:, None]
    bands = jnp.linspace(1e-4, HY_BANDS - 1, HY_BANDS, dtype=F32)[None, :]
    feats = jnp.concatenate([t, jnp.cos(bands * w), -jnp.sin(bands * w)], axis=-1)
    return jnp.pad(feats, ((0, 0), (0, HY_POS_PAD - HY_POS_DIM)))


def _hyena_filters(n_tok, w1, b1, w2, b2, w3, freq):
    feats = _hyena_filter_feats(n_tok)
    deltas = jnp.abs(jnp.linspace(math.log(HY_DECAY_TARGET) / HY_DECAY_SHORT,
                                  math.log(HY_DECAY_TARGET) / HY_DECAY_LONG, HY_FILTER_OUT, dtype=F32))[None, :]
    w1p = jnp.pad(w1, ((0, HY_POS_PAD - HY_POS_DIM), (0, 0)))
    w3_hi = w3.astype(BF16)
    w3_lo = (w3 - w3_hi.astype(F32)).astype(BF16)
    n2 = 2 * n_tok
    tm = min(n2, 512)
    full = lambda shape: pl.BlockSpec(shape, lambda i: (0,) * len(shape))
    hh = HY_FILTER_HIDDEN
    return pl.pallas_call(
        functools.partial(_filt_kernel, n_tok=n_tok),
        grid=(n2 // tm,),
        in_specs=[
            pl.BlockSpec((tm, HY_POS_PAD), lambda i: (i, 0)),
            full((HY_POS_PAD, hh)), full((1, hh)), full((hh, hh)), full((1, hh)),
            full((hh, HY_FILTER_OUT)), full((hh, HY_FILTER_OUT)), full((1, hh)), full((1, HY_FILTER_OUT)),
        ],
        out_specs=pl.BlockSpec((HY_ORDER, tm, HY_WIDTH), lambda i: (0, i, 0)),
        out_shape=jax.ShapeDtypeStruct((HY_ORDER, n2, HY_WIDTH), F32),
        compiler_params=_cparams(("parallel",)),
        name="hyena_filter",
    )(feats, w1p, b1[None], w2, b2[None], w3_hi, w3_lo, freq[None], deltas)


def _fft_split(n_fft):
    if n_fft <= 1024:
        return n_fft, 1
    n1 = 1 << (int(math.log2(n_fft)) // 2)
    return n1, n_fft // n1


def _cis(num, den, sign):
    ang = (2.0 * math.pi / den) * num.astype(F32)
    return jnp.cos(ang), sign * jnp.sin(ang)


def _fft_tables(n_fft):
    n1, n2 = _fft_split(n_fft)
    a = jnp.arange(n1)
    f1r, f1i = _cis((a[:, None] * a[None, :]) % n1, n1, -1.0)
    s1 = jnp.concatenate([f1r, f1i], axis=0).astype(BF16)
    half = n1 // 2
    s3 = (jnp.concatenate([f1r[:half], -f1i[:half]], axis=0) / n_fft).astype(BF16)
    if n2 == 1:
        return s1, s3, None, None
    c = jnp.arange(n1)[:, None, None]
    dd = jnp.arange(n2)[None, :, None]
    bb = jnp.arange(n2)[None, None, :]
    fr, fi = _cis((bb * (c + n1 * dd)) % n_fft, n_fft, -1.0)
    ft = jnp.concatenate([fr, fi], axis=1).astype(BF16)
    frt, fit = jnp.swapaxes(fr, 1, 2), jnp.swapaxes(fi, 1, 2)
    it = jnp.concatenate([frt, -fit], axis=1).astype(BF16)
    return s1, s3, ft, it


def _stacked_cdot(s, xr, xi, m):
    p = _dot(s, xr.astype(BF16))
    if xi is None:
        return p[:m], p[m:]
    q = _dot(s, xi.astype(BF16))
    return p[:m] - q[m:], q[:m] + p[m:]


def _fft_a_kernel(*refs, parts, n1, has_g):
    if has_g:
        x_ref, s_ref, g_ref, y_ref = refs
    else:
        x_ref, s_ref, y_ref = refs
    yr, yi = _stacked_cdot(s_ref[...], x_ref[0], x_ref[1] if parts == 2 else None, n1)
    if has_g:
        yr, yi = _cmul(yr, yi, g_ref[0], g_ref[1])
    y_ref[0] = yr
    y_ref[1] = yi


def _fft_a(x, s1, g=None):
    parts, a_rows, cols = x.shape
    n1 = s1.shape[0] // 2
    tc = min(cols, 2048)
    in_specs = [
        pl.BlockSpec((parts, a_rows, tc), lambda j: (0, 0, j)),
        pl.BlockSpec((2 * n1, a_rows), lambda j: (0, 0)),
    ]
    args = [x, s1[:, :a_rows]]
    if g is not None:
        in_specs.append(pl.BlockSpec((2, n1, tc), lambda j: (0, 0, j)))
        args.append(g)
    return pl.pallas_call(
        functools.partial(_fft_a_kernel, parts=parts, n1=n1, has_g=g is not None),
        grid=(cols // tc,),
        in_specs=in_specs,
        out_specs=pl.BlockSpec((2, n1, tc), lambda j: (0, 0, j)),
        out_shape=jax.ShapeDtypeStruct((2, n1, cols), F32),
        compiler_params=_cparams(("parallel",)),
        name="fft_stage_a",
    )(*args)


def _fft_b_kernel(*refs, n2, inverse):
    if inverse:
        y_ref, ft_ref, it_ref, g_ref, o_ref = refs
    else:
        y_ref, ft_ref, o_ref = refs
    xr, xi = _stacked_cdot(ft_ref[0], y_ref[0, 0], y_ref[1, 0], n2)
    if inverse:
        zr, zi = _cmul(xr, xi, g_ref[0, 0], g_ref[1, 0])
        xr, xi = _stacked_cdot(it_ref[0], zr, zi, n2)
    o_ref[0, 0] = xr.astype(o_ref.dtype)
    o_ref[1, 0] = xi.astype(o_ref.dtype)


def _fft_b(y, ft, it=None, g=None):
    _, n1, n2, ch = y.shape
    inverse = it is not None
    blk = pl.BlockSpec((2, 1, n2, ch), lambda c: (0, c, 0, 0))
    mat = pl.BlockSpec((1, 2 * n2, n2), lambda c: (c, 0, 0))
    in_specs, args = [blk, mat], [y, ft]
    if inverse:
        in_specs += [mat, blk]
        args += [it, g]
    return pl.pallas_call(
        functools.partial(_fft_b_kernel, n2=n2, inverse=inverse),
        grid=(n1,),
        in_specs=in_specs,
        out_specs=blk,
        out_shape=jax.ShapeDtypeStruct(y.shape, BF16 if inverse else F32),
        compiler_params=_cparams(("parallel",)),
        name="fft_stage_b",
    )(*args)


def _fft_c_kernel(y_ref, s_ref, x_ref, g_ref, b_ref, o_ref, *, half):
    cr, ci = _stacked_cdot(s_ref[...], y_ref[0], y_ref[1], half)
    bias = b_ref[...]
    o_ref[0] = g_ref[0] * (cr + bias * x_ref[0])
    o_ref[1] = g_ref[1] * (ci + bias * x_ref[1])


def _fft_c(y, s3, x, gate, bias_cols):
    _, n1, cols = y.shape
    half = n1 // 2
    tc = min(cols, 2048)
    io = pl.BlockSpec((2, half, tc), lambda j: (0, 0, j))
    return pl.pallas_call(
        functools.partial(_fft_c_kernel, half=half),
        grid=(cols // tc,),
        in_specs=[
            pl.BlockSpec((2, n1, tc), lambda j: (0, 0, j)),
            pl.BlockSpec((n1, n1), lambda j: (0, 0)),
            io, io,
            pl.BlockSpec((1, tc), lambda j: (0, j)),
        ],
        out_specs=io,
        out_shape=jax.ShapeDtypeStruct((2, half, cols), F32),
        compiler_params=_cparams(("parallel",)),
        name="fft_stage_c",
    )(y, s3, x, gate, bias_cols)


FFT_BT = 16


def _to_fine_major(x):
    return pltpu.einshape("abc->bac", x)


def _fft_a4_kernel(x_ref, s_ref, y_ref, *, parts, n1):
    s = s_ref[...]
    xr = _to_fine_major(x_ref[0].astype(BF16))
    xi = _to_fine_major(x_ref[1].astype(BF16)) if parts == 2 else None
    out = [_stacked_cdot(s, xr[b], None if xi is None else xi[b], n1) for b in range(FFT_BT)]
    y_ref[0] = pltpu.einshape("bac->abc", jnp.stack([o[0] for o in out])).astype(BF16)
    y_ref[1] = pltpu.einshape("bac->abc", jnp.stack([o[1] for o in out])).astype(BF16)


def _fft_a4(x, s1):
    parts, a_rows, n2, ch = x.shape
    n1 = s1.shape[0] // 2
    return pl.pallas_call(
        functools.partial(_fft_a4_kernel, parts=parts, n1=n1),
        grid=(n2 // FFT_BT,),
        in_specs=[
            pl.BlockSpec((parts, a_rows, FFT_BT, ch), lambda j: (0, 0, j, 0)),
            pl.BlockSpec((2 * n1, a_rows), lambda j: (0, 0)),
        ],
        out_specs=pl.BlockSpec((2, n1, FFT_BT, ch), lambda j: (0, 0, j, 0)),
        out_shape=jax.ShapeDtypeStruct((2, n1, n2, ch), BF16),
        compiler_params=_cparams(("parallel",)),
        name="fft_stage_a",
    )(x, s1[:, :a_rows])


def _fft_c4_kernel(y_ref, s_ref, x_ref, g_ref, b_ref, o_ref, c_ref, *, half):
    s = s_ref[...]
    yr = _to_fine_major(y_ref[0].astype(BF16))
    yi = _to_fine_major(y_ref[1].astype(BF16))
    out = [_stacked_cdot(s, yr[b], yi[b], half) for b in range(FFT_BT)]
    c_ref[0] = pltpu.einshape("bac->abc", jnp.stack([o[0] for o in out]))
    c_ref[1] = pltpu.einshape("bac->abc", jnp.stack([o[1] for o in out]))
    o_ref[...] = g_ref[...] * (c_ref[...] + b_ref[...][None, None] * x_ref[...])


def _fft_c4(y, s3, x, gate, bias):
    _, n1, n2, ch = y.shape
    half = n1 // 2
    io = pl.BlockSpec((2, half, FFT_BT, ch), lambda j: (0, 0, j, 0))
    return pl.pallas_call(
        functools.partial(_fft_c4_kernel, half=half),
        grid=(n2 // FFT_BT,),
        in_specs=[
            pl.BlockSpec((2, n1, FFT_BT, ch), lambda j: (0, 0, j, 0)),
            pl.BlockSpec((n1, n1), lambda j: (0, 0)),
            io, io,
            pl.BlockSpec((1, ch), lambda j: (0, 0)),
        ],
        out_specs=io,
        out_shape=jax.ShapeDtypeStruct((2, half, n2, ch), F32),
        scratch_shapes=[pltpu.VMEM((2, half, FFT_BT, ch), F32)],
        compiler_params=_cparams(("parallel",)),
        name="fft_stage_c",
    )(y, s3, x, gate, bias)


def _hyena_mixer(v, gates, gfilt, bias):
    bsz, n, ch = v.shape
    assert bsz == 2, "the complex packing of the long convolution pairs exactly two batch rows"
    n_fft = 2 * n
    n1, n2 = _fft_split(n_fft)
    s1, s3, ft, it = _fft_tables(n_fft)
    if n2 == 1:
        y = v
        for o in range(HY_ORDER):
            gspec = _fft_a(gfilt[o][None], s1)
            y = _fft_c(_fft_a(y, s1, gspec), s3, y, gates[o], bias[o][None])
        return y
    half = n1 // 2
    y = v.reshape(2, half, n2, ch)
    for o in range(HY_ORDER):
        gspec = _fft_b(_fft_a4(gfilt[o].reshape(1, n1, n2, ch), s1), ft)
        spec = _fft_b(_fft_a4(y, s1), ft, it, gspec)
        y = _fft_c4(spec, s3, y, gates[o].reshape(2, half, n2, ch), bias[o][None])
    return y.reshape(2, n, ch)


def _gelu_tanh(x):
    return 0.5 * x * (1.0 + jnp.tanh(math.sqrt(2.0 / math.pi) * (x + 0.044715 * (x * x * x))))


def _merge_kernel(x_ref, g0_ref, g1_ref, g2_ref, a_ref, yf_ref, yb_ref, u_ref, e_ref, d_ref,
                  wglu_ref, wm_ref, ws_ref, wh_ref, wo_ref, gt_ref, o_ref):
    att = _dot(a_ref[0, 0], wm_ref[0])
    for h in range(1, MLA_HEADS):
        att = att + _dot(a_ref[0, h], wm_ref[h])
    y = yf_ref[0, 0] + yb_ref[0, 0] + d_ref[...] * u_ref[0]
    gy = _gelu_tanh(y)
    s5 = gy * jax.nn.sigmoid(_dot(gy.astype(BF16), wglu_ref[...]))
    s5 = _dot(s5.astype(BF16), ws_ref[...])
    hy = _dot(e_ref[0].astype(BF16), wh_ref[...])
    merged = jax.nn.sigmoid(g0_ref[0]) * att
    merged = merged + jax.nn.sigmoid(g1_ref[0]) * s5
    merged = merged + jax.nn.sigmoid(g2_ref[0]) * hy
    o_ref[0] = x_ref[0] + gt_ref[0] * _dot(merged.astype(BF16), wo_ref[...])


def _merge(x, z, att, y5, e, s5_d, w_glu, w_mla, w_s5, w_hy, w_out, gate):
    bsz, n, d = x.shape
    tm = min(n, 256)
    full = lambda shape: pl.BlockSpec(shape, lambda b, i: (0,) * len(shape))
    zs5 = Z_S5 // S5_WIDTH
    return pl.pallas_call(
        _merge_kernel,
        grid=(bsz, n // tm),
        in_specs=[
            pl.BlockSpec((1, tm, d), lambda b, i: (b, i, 0)),
            pl.BlockSpec((1, tm, d), lambda b, i: (b, i, 0)),
            pl.BlockSpec((1, tm, d), lambda b, i: (b, i, 1)),
            pl.BlockSpec((1, tm, d), lambda b, i: (b, i, 2)),
            pl.BlockSpec((1, MLA_HEADS, tm, MLA_V), lambda b, i: (b, 0, i, 0)),
            pl.BlockSpec((1, 1, tm, S5_WIDTH), lambda b, i: (0, b, i, 0)),
            pl.BlockSpec((1, 1, tm, S5_WIDTH), lambda b, i: (1, b, i, 0)),
            pl.BlockSpec((1, tm, S5_WIDTH), lambda b, i: (b, i, zs5)),
            pl.BlockSpec((1, tm, HY_WIDTH), lambda b, i: (b, i, 0)),
            full((1, S5_WIDTH)),
            full((S5_WIDTH, S5_WIDTH)),
            full((MLA_HEADS, MLA_V, d)),
            full((S5_WIDTH, d)),
            full((HY_WIDTH, d)),
            full((d, d)),
            pl.BlockSpec((1, 1, d), lambda b, i: (b, 0, 0)),
        ],
        out_specs=pl.BlockSpec((1, tm, d), lambda b, i: (b, i, 0)),
        out_shape=jax.ShapeDtypeStruct((bsz, n, d), F32),
        compiler_params=_cparams(("parallel", "parallel")),
        name="merge",
    )(x, z, z, z, att, y5, y5, z, e, s5_d, w_glu, w_mla, w_s5, w_hy, w_out, gate)


def _ffn_kernel(x_ref, g_ref, sh_ref, sc_ref, gt_ref, wg_ref, wu_ref, wd_ref, fg_ref, o_ref, h_ref, acc_ref,
                *, nk, final):
    k = pl.program_id(2)

    @pl.when(k == 0)
    def _():
        h_ref[...] = _norm_mod(x_ref[0], g_ref[...], sh_ref[0], sc_ref[0]).astype(BF16)
        acc_ref[...] = jnp.zeros_like(acc_ref)

    h = h_ref[...]
    act = jax.nn.silu(_dot(h, wg_ref[...])) * _dot(h, wu_ref[...])
    acc_ref[...] += _dot(act.astype(BF16), wd_ref[...])

    @pl.when(k == nk - 1)
    def _():
        r = x_ref[0] + gt_ref[0] * acc_ref[...]
        o_ref[0] = _rms(r, fg_ref[...]) if final else r


def _ffn(x, g, shift, scale, gate, w_g, w_u, w_d, final_g, final):
    bsz, n, d = x.shape
    dff = w_g.shape[1]
    tm = min(n, 1024)
    tf = 256
    nk = dff // tf
    vec = pl.BlockSpec((1, 1, d), lambda b, i, k: (b, 0, 0))
    row = pl.BlockSpec((1, d), lambda b, i, k: (0, 0))
    return pl.pallas_call(
        functools.partial(_ffn_kernel, nk=nk, final=final),
        grid=(bsz, n // tm, nk),
        in_specs=[
            pl.BlockSpec((1, tm, d), lambda b, i, k: (b, i, 0)),
            row, vec, vec, vec,
            pl.BlockSpec((d, tf), lambda b, i, k: (0, k)),
            pl.BlockSpec((d, tf), lambda b, i, k: (0, k)),
            pl.BlockSpec((tf, d), lambda b, i, k: (k, 0)),
            row,
        ],
        out_specs=pl.BlockSpec((1, tm, d), lambda b, i, k: (b, i, 0)),
        out_shape=jax.ShapeDtypeStruct((bsz, n, d), F32),
        scratch_shapes=[pltpu.VMEM((tm, d), BF16), pltpu.VMEM((tm, d), F32)],
        compiler_params=_cparams(("parallel", "parallel", "arbitrary")),
        name="ffn",
    )(x, g, shift, scale, gate, w_g, w_u, w_d, final_g)


def _rope_tables(n_tok):
    rows = n_tok // GRID_W
    row = jnp.broadcast_to(jnp.arange(rows, dtype=F32)[:, None], (rows, GRID_W)).reshape(-1)
    col = jnp.broadcast_to(jnp.arange(GRID_W, dtype=F32)[None, :], (rows, GRID_W)).reshape(-1)
    n_freq = MLA_ROPE // 4
    inv = ROPE_BASE ** (-jnp.arange(n_freq, dtype=F32) / n_freq)
    ang = jnp.concatenate([row[:, None] * inv, col[:, None] * inv], axis=-1)
    cos, sin = jnp.cos(ang), jnp.sin(ang)
    pad = jnp.zeros((n_tok, HEAD_W - MLA_ROPE), F32)
    return jnp.concatenate([cos, cos, pad], axis=-1), jnp.concatenate([-sin, sin, pad], axis=-1)


def _identity_rope_tables(n_tok):
    one = jnp.ones((n_tok, MLA_ROPE), F32)
    pad = jnp.zeros((n_tok, HEAD_W - MLA_ROPE), F32)
    return jnp.concatenate([one, pad], axis=-1), jnp.zeros((n_tok, HEAD_W), F32)


def _layout_w_in(w):
    kr = w[:, OFF_KR:OFF_S5]
    x1, x2 = kr[:, 0::2], kr[:, 1::2]
    pad = jnp.zeros((w.shape[0], HEAD_W - 3 * (MLA_ROPE // 2)), w.dtype)
    return jnp.concatenate([w[:, OFF_GATE:], w[:, OFF_HY:OFF_GATE], w[:, OFF_S5:OFF_HY],
                            w[:, OFF_CQ:OFF_CKV], w[:, OFF_CKV:OFF_KR], x1, x2, x1, pad], axis=1).astype(BF16)


def _layout_w_uq(w):
    w = w.reshape(MLA_Q_LORA, MLA_HEADS, MLA_NOPE + MLA_ROPE)
    nope, rope = w[..., :MLA_NOPE], w[..., MLA_NOPE:]
    x1, x2 = rope[..., 0::2], rope[..., 1::2]
    z32 = jnp.zeros_like(rope)
    wq = jnp.concatenate([x1, x2, z32, nope], axis=-1)
    wqs = jnp.concatenate([x2, x1, z32, jnp.zeros_like(nope)], axis=-1)
    shape = (MLA_Q_LORA, MLA_HEADS * HEAD_W)
    return wq.reshape(shape).astype(BF16), wqs.reshape(shape).astype(BF16)


def _layout_w_ukv(w):
    w = w.reshape(MLA_KV_LORA, MLA_HEADS, MLA_NOPE + MLA_V)
    nope, val = w[..., :MLA_NOPE], w[..., MLA_NOPE:]
    wk = jnp.concatenate([jnp.zeros_like(nope), nope], axis=-1).reshape(MLA_KV_LORA, MLA_HEADS * HEAD_W)
    wvt = jnp.concatenate([val, jnp.zeros_like(val[..., :VT_ROWS - MLA_V])], axis=-1).reshape(
        MLA_KV_LORA, MLA_HEADS * VT_ROWS).T
    return wk.astype(BF16), wvt.astype(BF16)


def kernel(x, c, ctx, c_ctx, ada_w, ada_b, norm_mix, w_in, mla_q_norm, mla_w_uq, mla_kv_norm, mla_w_ukv,
           s5_lam_re, s5_lam_im, s5_log_dt, s5_b_re, s5_b_im, s5_c_re, s5_c_im, s5_d, s5_w_glu,
           hy_conv_w, hy_conv_b, hy_f_w1, hy_f_b1, hy_f_w2, hy_f_b2, hy_f_w3, hy_f_freq, hy_bias,
           w_branch_mla, w_branch_s5, w_branch_hy, w_out, norm_ffn, ffn_w_gu, ffn_w_down, final_norm):
    bsz, n_tok, d = x.shape
    n_ctx = ctx.shape[1]
    depth = ada_w.shape[0]
    rope_x = _rope_tables(n_tok)
    rope_c = _identity_rope_tables(n_ctx)
    h_zero = jnp.zeros((bsz, 2, 4, S5_HSTATE), F32)

    for i in range(depth):
        last = i == depth - 1
        mx = (jax.nn.silu(c) @ ada_w[i] + ada_b[i]).reshape(bsz, N_MOD, 1, d)
        mc = jnp.broadcast_to((jax.nn.silu(c_ctx) @ ada_w[i] + ada_b[i]).reshape(1, N_MOD, 1, d),
                              (bsz, N_MOD, 1, d))
        w_z = _layout_w_in(w_in[i])
        wq, wqs = _layout_w_uq(mla_w_uq[i])
        wk, wvt = _layout_w_ukv(mla_w_ukv[i])
        qn, kvn = mla_q_norm[i][None], mla_kv_norm[i][None]
        s5_tabs = _s5_tables(s5_lam_re[i], s5_lam_im[i], s5_log_dt[i], s5_b_re[i], s5_b_im[i],
                             s5_c_re[i], s5_c_im[i])
        fparams = (hy_f_w1[i], hy_f_b1[i], hy_f_w2[i], hy_f_b2[i], hy_f_w3[i], hy_f_freq[i])
        merge_w = (s5_d[i][None], s5_w_glu[i].astype(BF16),
                   w_branch_mla[i].reshape(MLA_HEADS, MLA_V, d).astype(BF16),
                   w_branch_s5[i].astype(BF16), w_branch_hy[i].astype(BF16), w_out[i].astype(BF16))
        w_g = ffn_w_gu[i][:, :D_FF].astype(BF16)
        w_u = ffn_w_gu[i][:, D_FF:].astype(BF16)
        w_d = ffn_w_down[i].astype(BF16)
        g_mix, g_ffn = norm_mix[i][None], norm_ffn[i][None]

        zx = _proj_in(x, g_mix, mx[:, 0], mx[:, 1], w_z)
        zc = _proj_in(ctx, g_mix, mc[:, 0], mc[:, 1], w_z)

        q_c, k_c, vt_c = _mla_prep(zc, qn, kvn, wq, wqs, wk, wvt, *rope_c)
        q_x, k_x, vt_x = _mla_prep(zx, qn, kvn, wq, wqs, wk, wvt, *rope_x)
        a_x = _attention(q_x, jnp.concatenate([k_c, k_x], axis=2), jnp.concatenate([vt_c, vt_x], axis=3))

        y5_c, finals = _s5_scan(zc, h_zero, s5_tabs)
        y5_x, _ = _s5_scan(zx, finals, s5_tabs)

        v_x, g1_x, g2_x = _short_conv(zx, hy_conv_w[i], hy_conv_b[i][None])
        e_x = _hyena_mixer(v_x, (g1_x, g2_x), _hyena_filters(n_tok, *fparams), hy_bias[i])

        x = _merge(x, zx, a_x, y5_x, e_x, *merge_w, mx[:, 2])
        x = _ffn(x, g_ffn, mx[:, 3], mx[:, 4], mx[:, 5], w_g, w_u, w_d, final_norm[None], last)

        if not last:
            a_c = _attention(q_c, k_c, vt_c)
            v_c, g1_c, g2_c = _short_conv(zc, hy_conv_w[i], hy_conv_b[i][None])
            e_c = _hyena_mixer(v_c, (g1_c, g2_c), _hyena_filters(n_ctx, *fparams), hy_bias[i])
            ctx = _merge(ctx, zc, a_c, y5_c, e_c, *merge_w, mc[:, 2])
            ctx = _ffn(ctx, g_ffn, mc[:, 3], mc[:, 4], mc[:, 5], w_g, w_u, w_d, final_norm[None], False)
    return x
```

```python
import functools
import math

import jax
import jax.numpy as jnp
import numpy as np
from jax import lax
from jax.experimental import pallas as pl
from jax.experimental.pallas import tpu as pltpu

F32 = jnp.float32
BF16 = jnp.bfloat16
HIGHEST = lax.Precision.HIGHEST

D_MODEL = 1024
GRID_W = 64
NORM_EPS = 1e-6
N_MOD = 6

MLA_HEADS = 8
MLA_NOPE = 64
MLA_ROPE = 32
MLA_V = 64
MLA_Q_LORA = 256
MLA_KV_LORA = 128
ROPE_BASE = 10000.0
HEAD_W = 128
VT_ROWS = 80
QK_SCALE_LOG2 = (MLA_NOPE + MLA_ROPE) ** -0.5 * math.log2(math.e)
ATTN_SUB = 256
ATTN_LAG = 2
ATTN_UNROLL = 3

S5_WIDTH = 512
S5_GROUP = 16
S5_GROUPS = 32
S5_STATE = 64
S5_HALF = 256
S5_HSTATE = 1024
S5_CHUNK = 128

HY_WIDTH = 512
HY_ORDER = 2
HY_BANDS = 16
HY_POS_DIM = 1 + 2 * HY_BANDS
HY_POS_PAD = 64
HY_FILTER_HIDDEN = 64
HY_FILTER_OUT = HY_ORDER * 2 * HY_WIDTH
HY_DECAY_TARGET = 1e-2
HY_DECAY_SHORT = 0.3
HY_DECAY_LONG = 1.5
HY_DECAY_SHIFT = 0.05

D_FF = 2816

OFF_CQ = 0
OFF_CKV = OFF_CQ + MLA_Q_LORA
OFF_KR = OFF_CKV + MLA_KV_LORA
OFF_S5 = OFF_KR + MLA_ROPE
OFF_HY = OFF_S5 + S5_WIDTH
OFF_GATE = OFF_HY + 3 * HY_WIDTH

Z_GATE = 0
Z_HY = 3072
Z_S5 = 4608
Z_MLA = 5120
Z_WIDTH = 5632

VMEM_LIMIT_V7X = 52 * 1024 * 1024


def _cparams(sem, flags=None):
    return pltpu.CompilerParams(dimension_semantics=sem, vmem_limit_bytes=VMEM_LIMIT_V7X, flags=flags)


def _dot(a, b):
    return jnp.dot(a, b, preferred_element_type=F32)


def _rms(x, g):
    return x * lax.rsqrt(jnp.mean(x * x, axis=-1, keepdims=True) + NORM_EPS) * g


def _norm_mod(x, g, shift, scale):
    return _rms(x, g) * (1.0 + scale) + shift


def _proj_kernel(x_ref, g_ref, sh_ref, sc_ref, w_ref, o_ref, h_ref):
    @pl.when(pl.program_id(2) == 0)
    def _():
        h_ref[...] = _norm_mod(x_ref[0], g_ref[...], sh_ref[0], sc_ref[0]).astype(BF16)

    o_ref[0] = _dot(h_ref[...], w_ref[...]).astype(BF16)


def _proj_in(x, g, shift, scale, w):
    bsz, n, d = x.shape
    nz = w.shape[1]
    tm = min(n, 2048)
    tn = 512
    return pl.pallas_call(
        _proj_kernel,
        grid=(bsz, n // tm, nz // tn),
        in_specs=[
            pl.BlockSpec((1, tm, d), lambda b, i, j: (b, i, 0)),
            pl.BlockSpec((1, d), lambda b, i, j: (0, 0)),
            pl.BlockSpec((1, 1, d), lambda b, i, j: (b, 0, 0)),
            pl.BlockSpec((1, 1, d), lambda b, i, j: (b, 0, 0)),
            pl.BlockSpec((d, tn), lambda b, i, j: (0, j)),
        ],
        out_specs=pl.BlockSpec((1, tm, tn), lambda b, i, j: (b, i, j)),
        out_shape=jax.ShapeDtypeStruct((bsz, n, nz), BF16),
        scratch_shapes=[pltpu.VMEM((tm, d), BF16)],
        compiler_params=_cparams(("parallel", "parallel", "arbitrary")),
        name="proj_in",
    )(x, g, shift, scale, w)


def _mla_prep_kernel(z_ref, qn_ref, kvn_ref, wq_ref, wqs_ref, wk_ref, wvt_ref, c_ref, s_ref, q_ref, k_ref, vt_ref):
    z = z_ref[0].astype(F32)
    hq = _rms(z[:, :MLA_Q_LORA], qn_ref[...]).astype(BF16)
    hkv = _rms(z[:, MLA_Q_LORA:MLA_Q_LORA + MLA_KV_LORA], kvn_ref[...]).astype(BF16)
    krb = z[:, MLA_Q_LORA + MLA_KV_LORA:]
    qa = _dot(hq, wq_ref[...])
    qb = _dot(hq, wqs_ref[...])
    kn = _dot(hkv, wk_ref[...])
    vt = lax.dot_general(wvt_ref[...], hkv, (((1,), (1,)), ((), ())), preferred_element_type=F32)
    ck = c_ref[...]
    sn = s_ref[...]
    lane = lax.broadcasted_iota(jnp.int32, ck.shape, 1)
    cq = jnp.where(lane < MLA_ROPE, ck, 1.0)
    kr = krb * ck + pltpu.roll(krb, HEAD_W - MLA_ROPE // 2, 1) * sn
    row = lax.broadcasted_iota(jnp.int32, (VT_ROWS, z.shape[0]), 0)
    for h in range(MLA_HEADS):
        sl = slice(h * HEAD_W, (h + 1) * HEAD_W)
        q_ref[0, h] = ((qa[:, sl] * cq + qb[:, sl] * sn) * QK_SCALE_LOG2).astype(BF16)
        k_ref[0, h] = (kr + kn[:, sl]).astype(BF16)
        vt_ref[0, h] = jnp.where(row == MLA_V, 1.0, vt[h * VT_ROWS:(h + 1) * VT_ROWS, :]).astype(BF16)


def _mla_prep(z, qn, kvn, wq, wqs, wk, wvt, ctab, stab):
    bsz, n, _ = z.shape
    tm = min(n, 512)
    hw = MLA_HEADS * HEAD_W
    zb = Z_MLA // 512
    full = lambda shape: pl.BlockSpec(shape, lambda b, i: (0,) * len(shape))
    return pl.pallas_call(
        _mla_prep_kernel,
        grid=(bsz, n // tm),
        in_specs=[
            pl.BlockSpec((1, tm, 512), lambda b, i: (b, i, zb)),
            full((1, MLA_Q_LORA)),
            full((1, MLA_KV_LORA)),
            full((MLA_Q_LORA, hw)),
            full((MLA_Q_LORA, hw)),
            full((MLA_KV_LORA, hw)),
            full((MLA_HEADS * VT_ROWS, MLA_KV_LORA)),
            pl.BlockSpec((tm, HEAD_W), lambda b, i: (i, 0)),
            pl.BlockSpec((tm, HEAD_W), lambda b, i: (i, 0)),
        ],
        out_specs=[
            pl.BlockSpec((1, MLA_HEADS, tm, HEAD_W), lambda b, i: (b, 0, i, 0)),
            pl.BlockSpec((1, MLA_HEADS, tm, HEAD_W), lambda b, i: (b, 0, i, 0)),
            pl.BlockSpec((1, MLA_HEADS, VT_ROWS, tm), lambda b, i: (b, 0, 0, i)),
        ],
        out_shape=[
            jax.ShapeDtypeStruct((bsz, MLA_HEADS, n, HEAD_W), BF16),
            jax.ShapeDtypeStruct((bsz, MLA_HEADS, n, HEAD_W), BF16),
            jax.ShapeDtypeStruct((bsz, MLA_HEADS, VT_ROWS, n), BF16),
        ],
        compiler_params=_cparams(("parallel", "parallel")),
        name="mla_prep",
    )(z, qn, kvn, wq, wqs, wk, wvt, ctab, stab)


def _attn_kernel(q_ref, k_ref, vt_ref, o_ref, s0_ref, s1_ref, s2_ref, p0_ref, p1_ref, p2_ref, acc_ref, *, tk, nkc):
    q = q_ref[0, 0]
    tq = q.shape[0]
    s_refs = (s0_ref, s1_ref, s2_ref)
    p_refs = (p0_ref, p1_ref, p2_ref)

    ts = min(tk, ATTN_SUB)
    subs = [slice(r, r + ts) for r in range(0, tk, ts)]

    def scores_sub(j, s_ref, sub, cmax):
        off = pl.multiple_of(j * tk + sub.start, ts)
        s = lax.dot_general(k_ref[0, 0, pl.ds(off, ts), :], q, (((1,), (1,)), ((), ())),
                            preferred_element_type=F32)
        s_ref[sub, :] = s
        cm = jnp.max(s, axis=0, keepdims=True)
        return cm if cmax is None else jnp.maximum(cmax, cm)

    def softmax_sub(slot, sub, mn):
        p_refs[slot][sub, :] = jnp.exp2(s_refs[slot][sub, :] - mn).astype(BF16)

    def step(i, slot, carry, do_softmax, do_scores):
        cm, m, alpha = carry
        mn = jnp.maximum(m, cm)
        off = pl.multiple_of(i * tk, tk)
        acc_ref[...] = alpha * acc_ref[...] + _dot(vt_ref[0, 0, :, pl.ds(off, tk)], p_refs[slot][...])
        cm_new, cms = None, []
        for r, sub in enumerate(subs):
            mn_sub = mn
            if do_scores:
                cm_new = scores_sub(i + 2, s_refs[(slot + 2) % 3], sub, cm_new)
                cms.append(cm_new)
                if r >= ATTN_LAG:
                    mn_sub = jnp.maximum(mn, cms[r - ATTN_LAG] - 1e30)
            if do_softmax:
                softmax_sub((slot + 1) % 3, sub, mn_sub)
        if do_softmax:
            alpha, m = jnp.exp2(m - mn), mn
        return (cm_new if do_scores else cm), m, alpha

    m = jnp.full((1, tq), -1e30, F32)
    acc_ref[...] = jnp.zeros_like(acc_ref)
    cm, cm_next = None, None
    for sub in subs:
        cm = scores_sub(0, s0_ref, sub, cm)
        if nkc > 1:
            cm_next = scores_sub(1, s1_ref, sub, cm_next)
    mn = jnp.maximum(m, cm)
    for sub in subs:
        softmax_sub(0, sub, mn)
    carry = (cm_next if nkc > 1 else cm, mn, jnp.exp2(m - mn))

    def body(t, carry):
        for r in range(ATTN_UNROLL):
            carry = step(ATTN_UNROLL * t + r, r % 3, carry, True, True)
        return carry

    nloop = max(nkc - 2, 0) // ATTN_UNROLL
    carry = lax.fori_loop(0, nloop, body, carry)
    for i in range(ATTN_UNROLL * nloop, nkc):
        carry = step(i, i % 3, carry, i + 1 < nkc, i + 2 < nkc)
    acc = acc_ref[...]
    out = acc * (1.0 / acc[MLA_V:MLA_V + 1, :])
    out = jnp.concatenate([out, jnp.zeros((HEAD_W - VT_ROWS, tq), F32)], axis=0).T
    o_ref[0, 0] = out[:, :MLA_V].astype(BF16)


def _kv_chunk(nk):
    for tk in (1280, 1024, 640, 512, 256, 128):
        if nk % tk == 0:
            return tk
    raise ValueError(f"unsupported key count {nk}")


def _attention(q, k, vt):
    bsz, nh, nq, _ = q.shape
    nk = k.shape[2]
    tq = min(nq, 512)
    tk = _kv_chunk(nk)
    return pl.pallas_call(
        functools.partial(_attn_kernel, tk=tk, nkc=nk // tk),
        grid=(bsz, nh, nq // tq),
        in_specs=[
            pl.BlockSpec((1, 1, tq, HEAD_W), lambda b, h, i: (b, h, i, 0)),
            pl.BlockSpec((1, 1, nk, HEAD_W), lambda b, h, i: (b, h, 0, 0)),
            pl.BlockSpec((1, 1, VT_ROWS, nk), lambda b, h, i: (b, h, 0, 0)),
        ],
        out_specs=pl.BlockSpec((1, 1, tq, MLA_V), lambda b, h, i: (b, h, i, 0)),
        out_shape=jax.ShapeDtypeStruct((bsz, nh, nq, MLA_V), BF16),
        scratch_shapes=[pltpu.VMEM((tk, tq), F32)] * 3 + [pltpu.VMEM((tk, tq), BF16)] * 3
        + [pltpu.VMEM((VT_ROWS, tq), F32)],
        compiler_params=_cparams(("parallel", "parallel", "arbitrary")),
        name="attention",
    )(q, k, vt)


def _cmul(ar, ai, br, bi):
    return ar * br - ai * bi, ar * bi + ai * br


def _s5_kernel(u_ref, h0_ref, bm_ref, cm_ref, wn_ref, wp_ref, l1_ref, tri_ref, y_ref, hf_ref, carry_ref, *, tc, nc):
    d = pl.program_id(1)
    i = pl.program_id(2)

    @pl.when(i == 0)
    def _():
        carry_ref[...] = h0_ref[0, 0]

    u = u_ref[0]
    tri = tri_ref[0]
    ys = []
    for hf in range(2):
        ub = u[:, hf * S5_HALF:(hf + 1) * S5_HALF].astype(BF16)
        bu = _dot(ub, bm_ref[0, hf])
        xr, xi = _cmul(wn_ref[0, hf, 0], wn_ref[0, hf, 1], bu[:, :S5_HSTATE], bu[:, S5_HSTATE:])
        s = _dot(tri, jnp.concatenate([xr, xi], axis=1).astype(BF16))
        cr, ci = _cmul(l1_ref[0, hf, 0], l1_ref[0, hf, 1],
                       carry_ref[2 * hf:2 * hf + 1, :], carry_ref[2 * hf + 1:2 * hf + 2, :])
        hr, hi = _cmul(wp_ref[0, hf, 0], wp_ref[0, hf, 1], s[:, :S5_HSTATE] + cr, s[:, S5_HSTATE:] + ci)
        ys.append(_dot(jnp.concatenate([hr, hi], axis=1).astype(BF16), cm_ref[0, hf]))
        carry_ref[2 * hf:2 * hf + 1, :] = jnp.where(d == 0, hr[tc - 1:tc], hr[0:1])
        carry_ref[2 * hf + 1:2 * hf + 2, :] = jnp.where(d == 0, hi[tc - 1:tc], hi[0:1])
    y_ref[0, 0] = jnp.concatenate(ys, axis=1)

    @pl.when(i == nc - 1)
    def _():
        hf_ref[0, 0] = carry_ref[...]


def _s5_scan(z, h0, tabs):
    bm, cm, wn, wp, l1, tri = tabs
    bsz, n, _ = z.shape
    tc = S5_CHUNK
    nc = n // tc
    zb = Z_S5 // S5_WIDTH

    def chunk(d, i):
        return jnp.where(d == 0, i, nc - 1 - i)

    return pl.pallas_call(
        functools.partial(_s5_kernel, tc=tc, nc=nc),
        grid=(bsz, 2, nc),
        in_specs=[
            pl.BlockSpec((1, tc, S5_WIDTH), lambda b, d, i: (b, chunk(d, i), zb)),
            pl.BlockSpec((1, 1, 4, S5_HSTATE), lambda b, d, i: (b, d, 0, 0)),
            pl.BlockSpec((1, 2, S5_HALF, 2 * S5_HSTATE), lambda b, d, i: (d, 0, 0, 0)),
            pl.BlockSpec((1, 2, 2 * S5_HSTATE, S5_HALF), lambda b, d, i: (d, 0, 0, 0)),
            pl.BlockSpec((1, 2, 2, tc, S5_HSTATE), lambda b, d, i: (d, 0, 0, 0, 0)),
            pl.BlockSpec((1, 2, 2, tc, S5_HSTATE), lambda b, d, i: (d, 0, 0, 0, 0)),
            pl.BlockSpec((1, 2, 2, 1, S5_HSTATE), lambda b, d, i: (d, 0, 0, 0, 0)),
            pl.BlockSpec((1, tc, tc), lambda b, d, i: (d, 0, 0)),
        ],
        out_specs=[
            pl.BlockSpec((1, 1, tc, S5_WIDTH), lambda b, d, i: (d, b, chunk(d, i), 0)),
            pl.BlockSpec((1, 1, 4, S5_HSTATE), lambda b, d, i: (b, d, 0, 0)),
        ],
        out_shape=[
            jax.ShapeDtypeStruct((2, bsz, n, S5_WIDTH), F32),
            jax.ShapeDtypeStruct((bsz, 2, 4, S5_HSTATE), F32),
        ],
        scratch_shapes=[pltpu.VMEM((4, S5_HSTATE), F32)],
        compiler_params=_cparams(("parallel", "parallel", "arbitrary")),
        name="s5_scan",
    )(z, h0, bm, cm, wn, wp, l1, tri)


def _s5_tables(lam_re, lam_im, log_dt, b_re, b_im, c_re, c_im):
    tc = S5_CHUNK
    dt = jnp.exp(log_dt)[..., None]
    zr, zi = lam_re * dt, lam_im * dt
    mag = jnp.exp(zr)
    lbr, lbi = mag * jnp.cos(zi), mag * jnp.sin(zi)
    den = lam_re * lam_re + lam_im * lam_im
    nr, ni = lbr - 1.0, lbi
    cfr = (nr * lam_re + ni * lam_im) / den
    cfi = (ni * lam_re - nr * lam_im) / den
    bbr = cfr[..., None] * b_re - cfi[..., None] * b_im
    bbi = cfr[..., None] * b_im + cfi[..., None] * b_re
    eye = jnp.eye(S5_GROUP, dtype=F32)

    def blockdiag_in(b):
        b = b.reshape(2, 2, S5_GROUP, S5_STATE, S5_GROUP)
        return jnp.einsum('gk,dhgpn->dhgnkp', eye, b).reshape(2, 2, S5_HALF, S5_HSTATE)

    def blockdiag_out(c):
        c = c.reshape(2, 2, S5_GROUP, S5_GROUP, S5_STATE)
        return jnp.einsum('gk,dhgnp->dhgpkn', eye, c).reshape(2, 2, S5_HSTATE, S5_HALF)

    bm = jnp.concatenate([blockdiag_in(bbr), blockdiag_in(bbi)], axis=-1).astype(BF16)
    cm = jnp.concatenate([blockdiag_out(c_re), blockdiag_out(-c_im)], axis=-2).astype(BF16)

    def powers(k):
        zr_ = zr.reshape(2, 2, 1, S5_HSTATE)
        zi_ = zi.reshape(2, 2, 1, S5_HSTATE)
        kk = k[:, None, :, None]
        m = jnp.exp(kk * zr_)
        return jnp.stack([m * jnp.cos(kk * zi_), m * jnp.sin(kk * zi_)], axis=2)

    t = jnp.arange(tc, dtype=F32)
    wn = powers(jnp.stack([-t, -(tc - 1 - t)]))
    wp = powers(jnp.stack([t, tc - 1 - t]))
    l1 = powers(jnp.ones((2, 1), F32))
    r = jnp.arange(tc)
    tri = jnp.stack([r[:, None] >= r[None, :], r[:, None] <= r[None, :]]).astype(BF16)
    return bm, cm, wn, wp, l1, tri


HALO = 16


def _sconv_kernel(z_ref, zp_ref, zn_ref, w_ref, b_ref, v_ref, g1_ref, g2_ref, *, nt):
    i = pl.program_id(1)
    z = z_ref[0].astype(F32)
    tm = z.shape[0]
    row = lax.broadcasted_iota(jnp.int32, z.shape, 0)
    prev = jnp.where(i > 0, zp_ref[0, HALO - 1:HALO, :].astype(F32), 0.0)
    nxt = jnp.where(i < nt - 1, zn_ref[0, 0:1, :].astype(F32), 0.0)
    up = jnp.where(row == 0, prev, pltpu.roll(z, 1, 0))
    dn = jnp.where(row == tm - 1, nxt, pltpu.roll(z, tm - 1, 0))
    u = up * w_ref[0:1, :] + z * w_ref[1:2, :] + dn * w_ref[2:3, :] + b_ref[...]
    for k, o_ref in enumerate((v_ref, g1_ref, g2_ref)):
        o_ref[0] = u[:, k * HY_WIDTH:(k + 1) * HY_WIDTH]


def _short_conv(z, w, b):
    bsz, n, _ = z.shape
    cw = 3 * HY_WIDTH
    tm = min(n, 512)
    nt = n // tm
    zb = Z_HY // cw
    rh = tm // HALO
    out = pl.BlockSpec((1, tm, HY_WIDTH), lambda b_, i: (b_, i, 0))
    return pl.pallas_call(
        functools.partial(_sconv_kernel, nt=nt),
        grid=(bsz, nt),
        in_specs=[
            pl.BlockSpec((1, tm, cw), lambda b_, i: (b_, i, zb)),
            pl.BlockSpec((1, HALO, cw), lambda b_, i: (b_, jnp.maximum(i * rh - 1, 0), zb)),
            pl.BlockSpec((1, HALO, cw), lambda b_, i: (b_, jnp.minimum((i + 1) * rh, n // HALO - 1), zb)),
            pl.BlockSpec((3, cw), lambda b_, i: (0, 0)),
            pl.BlockSpec((1, cw), lambda b_, i: (0, 0)),
        ],
        out_specs=[out] * 3,
        out_shape=[jax.ShapeDtypeStruct((bsz, n, HY_WIDTH), F32)] * 3,
        compiler_params=_cparams(("parallel", "parallel")),
        name="short_conv",
    )(z, z, z, w, b)


def _filt_kernel(f_ref, w1_ref, b1_ref, w2_ref, b2_ref, w3h_ref, w3l_ref, fq_ref, dl_ref, o_ref, *, n_tok):
    z = f_ref[...]
    tm = z.shape[0]
    fq = fq_ref[...]
    hid = jnp.sin(fq * (jnp.dot(z, w1_ref[...], precision=HIGHEST, preferred_element_type=F32) + b1_ref[...]))
    hid = jnp.sin(fq * (jnp.dot(hid, w2_ref[...], precision=HIGHEST, preferred_element_type=F32) + b2_ref[...]))
    hid_hi = hid.astype(BF16)
    hid_lo = (hid - hid_hi.astype(F32)).astype(BF16)
    filt = _dot(hid_hi, w3h_ref[...]) + (_dot(hid_hi, w3l_ref[...]) + _dot(hid_lo, w3h_ref[...]))
    filt = filt * (jnp.exp(-z[:, 0:1] * dl_ref[...]) + HY_DECAY_SHIFT)
    m = pl.program_id(0) * tm + lax.broadcasted_iota(jnp.int32, (tm, HY_WIDTH), 0)
    for o in range(HY_ORDER):
        base = o * 2 * HY_WIDTH
        fwd = filt[:, base:base + HY_WIDTH]
        bwd = filt[:, base + HY_WIDTH:base + 2 * HY_WIDTH]
        o_ref[o] = jnp.where(m < n_tok, fwd, 0.0) + jnp.where((m > n_tok) | (m == 0), bwd, 0.0)


def _hyena_filter_feats(n_tok):
    m = jnp.arange(2 * n_tok)
    lag = jnp.where(m < n_tok, m, jnp.where(m > n_tok, 2 * n_tok - m, 0))
    t = (lag.astype(F32) / (n_tok - 1))[:, None]
    w = (2.0 * math.pi * lag.astype(F32) / n_tok)[:, None]
    bands = jnp.linspace(1e-4, HY_BANDS - 1, HY_BANDS, dtype=F32)[None, :]
    feats = jnp.concatenate([t, jnp.cos(bands * w), -jnp.sin(bands * w)], axis=-1)
    return jnp.pad(feats, ((0, 0), (0, HY_POS_PAD - HY_POS_DIM)))


def _hyena_filters(n_tok, w1, b1, w2, b2, w3, freq):
    feats = _hyena_filter_feats(n_tok)
    deltas = jnp.abs(jnp.linspace(math.log(HY_DECAY_TARGET) / HY_DECAY_SHORT,
                                  math.log(HY_DECAY_TARGET) / HY_DECAY_LONG, HY_FILTER_OUT, dtype=F32))[None, :]
    w1p = jnp.pad(w1, ((0, HY_POS_PAD - HY_POS_DIM), (0, 0)))
    w3_hi = w3.astype(BF16)
    w3_lo = (w3 - w3_hi.astype(F32)).astype(BF16)
    n2 = 2 * n_tok
    tm = min(n2, 512)
    full = lambda shape: pl.BlockSpec(shape, lambda i: (0,) * len(shape))
    hh = HY_FILTER_HIDDEN
    return pl.pallas_call(
        functools.partial(_filt_kernel, n_tok=n_tok),
        grid=(n2 // tm,),
        in_specs=[
            pl.BlockSpec((tm, HY_POS_PAD), lambda i: (i, 0)),
            full((HY_POS_PAD, hh)), full((1, hh)), full((hh, hh)), full((1, hh)),
            full((hh, HY_FILTER_OUT)), full((hh, HY_FILTER_OUT)), full((1, hh)), full((1, HY_FILTER_OUT)),
        ],
        out_specs=pl.BlockSpec((HY_ORDER, tm, HY_WIDTH), lambda i: (0, i, 0)),
        out_shape=jax.ShapeDtypeStruct((HY_ORDER, n2, HY_WIDTH), F32),
        compiler_params=_cparams(("parallel",)),
        name="hyena_filter",
    )(feats, w1p, b1[None], w2, b2[None], w3_hi, w3_lo, freq[None], deltas)


def _fft_split(n_fft):
    if n_fft <= 1024:
        return n_fft, 1
    n1 = 1 << (int(math.log2(n_fft)) // 2)
    return n1, n_fft // n1


def _cis(num, den, sign):
    ang = (2.0 * math.pi / den) * num.astype(F32)
    return jnp.cos(ang), sign * jnp.sin(ang)


def _fft_tables(n_fft):
    n1, n2 = _fft_split(n_fft)
    a = jnp.arange(n1)
    f1r, f1i = _cis((a[:, None] * a[None, :]) % n1, n1, -1.0)
    s1 = jnp.concatenate([f1r, f1i], axis=0).astype(BF16)
    half = n1 // 2
    s3 = (jnp.concatenate([f1r[:half], -f1i[:half]], axis=0) / n_fft).astype(BF16)
    if n2 == 1:
        return s1, s3, None, None
    b = jnp.arange(n2)
    f2r, f2i = _cis((b[:, None] * b[None, :]) % n2, n2, -1.0)
    twr, twi = _cis((jnp.arange(n1)[:, None] * b[None, :]) % n_fft, n_fft, -1.0)
    fr, fi = _cmul(f2r[None], f2i[None], twr[:, None, :], twi[:, None, :])
    ft = jnp.concatenate([fr, fi], axis=1).astype(BF16)
    frt, fit = _cmul(f2r[None], f2i[None], twr[:, :, None], twi[:, :, None])
    it = jnp.concatenate([frt, -fit], axis=1).astype(BF16)
    return s1, s3, ft, it


def _stacked_cdot(s, xr, xi, m):
    p = _dot(s, xr.astype(BF16))
    if xi is None:
        return p[:m], p[m:]
    q = _dot(s, xi.astype(BF16))
    return p[:m] - q[m:], q[:m] + p[m:]


def _fft_a_kernel(*refs, parts, n1, has_g):
    if has_g:
        x_ref, s_ref, g_ref, y_ref = refs
    else:
        x_ref, s_ref, y_ref = refs
    yr, yi = _stacked_cdot(s_ref[...], x_ref[0], x_ref[1] if parts == 2 else None, n1)
    if has_g:
        yr, yi = _cmul(yr, yi, g_ref[0], g_ref[1])
    y_ref[0] = yr
    y_ref[1] = yi


def _fft_a(x, s1, g=None):
    parts, a_rows, cols = x.shape
    n1 = s1.shape[0] // 2
    tc = min(cols, 2048)
    in_specs = [
        pl.BlockSpec((parts, a_rows, tc), lambda j: (0, 0, j)),
        pl.BlockSpec((2 * n1, a_rows), lambda j: (0, 0)),
    ]
    args = [x, s1[:, :a_rows]]
    if g is not None:
        in_specs.append(pl.BlockSpec((2, n1, tc), lambda j: (0, 0, j)))
        args.append(g)
    return pl.pallas_call(
        functools.partial(_fft_a_kernel, parts=parts, n1=n1, has_g=g is not None),
        grid=(cols // tc,),
        in_specs=in_specs,
        out_specs=pl.BlockSpec((2, n1, tc), lambda j: (0, 0, j)),
        out_shape=jax.ShapeDtypeStruct((2, n1, cols), F32),
        compiler_params=_cparams(("parallel",)),
        name="fft_stage_a",
    )(*args)


def _fft_b_kernel(*refs, n2, inverse):
    if inverse:
        y_ref, ft_ref, it_ref, g_ref, o_ref = refs
    else:
        y_ref, ft_ref, o_ref = refs
    xr, xi = _stacked_cdot(ft_ref[0], y_ref[0, 0], y_ref[1, 0], n2)
    if inverse:
        zr, zi = _cmul(xr, xi, g_ref[0, 0], g_ref[1, 0])
        xr, xi = _stacked_cdot(it_ref[0], zr, zi, n2)
    o_ref[0, 0] = xr.astype(o_ref.dtype)
    o_ref[1, 0] = xi.astype(o_ref.dtype)


def _fft_b(y, ft, it=None, g=None):
    _, n1, n2, ch = y.shape
    inverse = it is not None
    blk = pl.BlockSpec((2, 1, n2, ch), lambda c: (0, c, 0, 0))
    mat = pl.BlockSpec((1, 2 * n2, n2), lambda c: (c, 0, 0))
    in_specs, args = [blk, mat], [y, ft]
    if inverse:
        in_specs += [mat, blk]
        args += [it, g]
    return pl.pallas_call(
        functools.partial(_fft_b_kernel, n2=n2, inverse=inverse),
        grid=(n1,),
        in_specs=in_specs,
        out_specs=blk,
        out_shape=jax.ShapeDtypeStruct(y.shape, BF16 if inverse else F32),
        compiler_params=_cparams(("parallel",)),
        name="fft_stage_b",
    )(*args)


def _fft_c_kernel(y_ref, s_ref, x_ref, g_ref, b_ref, o_ref, *, half):
    cr, ci = _stacked_cdot(s_ref[...], y_ref[0], y_ref[1], half)
    bias = b_ref[...]
    o_ref[0] = g_ref[0] * (cr + bias * x_ref[0])
    o_ref[1] = g_ref[1] * (ci + bias * x_ref[1])


def _fft_c(y, s3, x, gate, bias_cols):
    _, n1, cols = y.shape
    half = n1 // 2
    tc = min(cols, 2048)
    io = pl.BlockSpec((2, half, tc), lambda j: (0, 0, j))
    return pl.pallas_call(
        functools.partial(_fft_c_kernel, half=half),
        grid=(cols // tc,),
        in_specs=[
            pl.BlockSpec((2, n1, tc), lambda j: (0, 0, j)),
            pl.BlockSpec((n1, n1), lambda j: (0, 0)),
            io, io,
            pl.BlockSpec((1, tc), lambda j: (0, j)),
        ],
        out_specs=io,
        out_shape=jax.ShapeDtypeStruct((2, half, cols), F32),
        compiler_params=_cparams(("parallel",)),
        name="fft_stage_c",
    )(y, s3, x, gate, bias_cols)


FFT_BT = 16


def _to_fine_major(x):
    return pltpu.einshape("abc->bac", x)


def _fft_a4_kernel(x_ref, s_ref, y_ref, *, parts, n1):
    s = s_ref[...]
    xr = _to_fine_major(x_ref[0].astype(BF16))
    xi = _to_fine_major(x_ref[1].astype(BF16)) if parts == 2 else None
    out = [_stacked_cdot(s, xr[b], None if xi is None else xi[b], n1) for b in range(FFT_BT)]
    y_ref[0] = pltpu.einshape("bac->abc", jnp.stack([o[0] for o in out])).astype(BF16)
    y_ref[1] = pltpu.einshape("bac->abc", jnp.stack([o[1] for o in out])).astype(BF16)


def _fft_a4(x, s1):
    parts, a_rows, n2, ch = x.shape
    n1 = s1.shape[0] // 2
    return pl.pallas_call(
        functools.partial(_fft_a4_kernel, parts=parts, n1=n1),
        grid=(n2 // FFT_BT,),
        in_specs=[
            pl.BlockSpec((parts, a_rows, FFT_BT, ch), lambda j: (0, 0, j, 0)),
            pl.BlockSpec((2 * n1, a_rows), lambda j: (0, 0)),
        ],
        out_specs=pl.BlockSpec((2, n1, FFT_BT, ch), lambda j: (0, 0, j, 0)),
        out_shape=jax.ShapeDtypeStruct((2, n1, n2, ch), BF16),
        compiler_params=_cparams(("parallel",)),
        name="fft_stage_a",
    )(x, s1[:, :a_rows])


def _fft_c4_kernel(y_ref, s_ref, x_ref, g_ref, b_ref, o_ref, c_ref, *, half):
    s = s_ref[...]
    yr = _to_fine_major(y_ref[0].astype(BF16))
    yi = _to_fine_major(y_ref[1].astype(BF16))
    out = [_stacked_cdot(s, yr[b], yi[b], half) for b in range(FFT_BT)]
    c_ref[0] = pltpu.einshape("bac->abc", jnp.stack([o[0] for o in out]))
    c_ref[1] = pltpu.einshape("bac->abc", jnp.stack([o[1] for o in out]))
    o_ref[...] = g_ref[...] * (c_ref[...] + b_ref[...][None, None] * x_ref[...])


def _fft_c4(y, s3, x, gate, bias):
    _, n1, n2, ch = y.shape
    half = n1 // 2
    io = pl.BlockSpec((2, half, FFT_BT, ch), lambda j: (0, 0, j, 0))
    return pl.pallas_call(
        functools.partial(_fft_c4_kernel, half=half),
        grid=(n2 // FFT_BT,),
        in_specs=[
            pl.BlockSpec((2, n1, FFT_BT, ch), lambda j: (0, 0, j, 0)),
            pl.BlockSpec((n1, n1), lambda j: (0, 0)),
            io, io,
            pl.BlockSpec((1, ch), lambda j: (0, 0)),
        ],
        out_specs=io,
        out_shape=jax.ShapeDtypeStruct((2, half, n2, ch), F32),
        scratch_shapes=[pltpu.VMEM((2, half, FFT_BT, ch), F32)],
        compiler_params=_cparams(("parallel",)),
        name="fft_stage_c",
    )(y, s3, x, gate, bias)


def _hyena_mixer(v, gates, gfilt, bias):
    bsz, n, ch = v.shape
    assert bsz == 2, "the complex packing of the long convolution pairs exactly two batch rows"
    n_fft = 2 * n
    n1, n2 = _fft_split(n_fft)
    s1, s3, ft, it = _fft_tables(n_fft)
    if n2 == 1:
        y = v
        for o in range(HY_ORDER):
            gspec = _fft_a(gfilt[o][None], s1)
            y = _fft_c(_fft_a(y, s1, gspec), s3, y, gates[o], bias[o][None])
        return y
    half = n1 // 2
    y = v.reshape(2, half, n2, ch)
    for o in range(HY_ORDER):
        gspec = _fft_b(_fft_a4(gfilt[o].reshape(1, n1, n2, ch), s1), ft)
        spec = _fft_b(_fft_a4(y, s1), ft, it, gspec)
        y = _fft_c4(spec, s3, y, gates[o].reshape(2, half, n2, ch), bias[o][None])
    return y.reshape(2, n, ch)


def _gelu_tanh(x):
    return 0.5 * x * (1.0 + jnp.tanh(math.sqrt(2.0 / math.pi) * (x + 0.044715 * (x * x * x))))


def _merge_kernel(x_ref, g0_ref, g1_ref, g2_ref, a_ref, yf_ref, yb_ref, u_ref, e_ref, d_ref,
                  wglu_ref, wm_ref, ws_ref, wh_ref, wo_ref, gt_ref, o_ref):
    att = _dot(a_ref[0, 0], wm_ref[0])
    for h in range(1, MLA_HEADS):
        att = att + _dot(a_ref[0, h], wm_ref[h])
    y = yf_ref[0, 0] + yb_ref[0, 0] + d_ref[...] * u_ref[0].astype(F32)
    gy = _gelu_tanh(y)
    s5 = gy * jax.nn.sigmoid(_dot(gy.astype(BF16), wglu_ref[...]))
    s5 = _dot(s5.astype(BF16), ws_ref[...])
    hy = _dot(e_ref[0].astype(BF16), wh_ref[...])
    merged = jax.nn.sigmoid(g0_ref[0].astype(F32)) * att
    merged = merged + jax.nn.sigmoid(g1_ref[0].astype(F32)) * s5
    merged = merged + jax.nn.sigmoid(g2_ref[0].astype(F32)) * hy
    o_ref[0] = x_ref[0] + gt_ref[0] * _dot(merged.astype(BF16), wo_ref[...])


def _merge(x, z, att, y5, e, s5_d, w_glu, w_mla, w_s5, w_hy, w_out, gate):
    bsz, n, d = x.shape
    tm = min(n, 256)
    full = lambda shape: pl.BlockSpec(shape, lambda b, i: (0,) * len(shape))
    zs5 = Z_S5 // S5_WIDTH
    return pl.pallas_call(
        _merge_kernel,
        grid=(bsz, n // tm),
        in_specs=[
            pl.BlockSpec((1, tm, d), lambda b, i: (b, i, 0)),
            pl.BlockSpec((1, tm, d), lambda b, i: (b, i, 0)),
            pl.BlockSpec((1, tm, d), lambda b, i: (b, i, 1)),
            pl.BlockSpec((1, tm, d), lambda b, i: (b, i, 2)),
            pl.BlockSpec((1, MLA_HEADS, tm, MLA_V), lambda b, i: (b, 0, i, 0)),
            pl.BlockSpec((1, 1, tm, S5_WIDTH), lambda b, i: (0, b, i, 0)),
            pl.BlockSpec((1, 1, tm, S5_WIDTH), lambda b, i: (1, b, i, 0)),
            pl.BlockSpec((1, tm, S5_WIDTH), lambda b, i: (b, i, zs5)),
            pl.BlockSpec((1, tm, HY_WIDTH), lambda b, i: (b, i, 0)),
            full((1, S5_WIDTH)),
            full((S5_WIDTH, S5_WIDTH)),
            full((MLA_HEADS, MLA_V, d)),
            full((S5_WIDTH, d)),
            full((HY_WIDTH, d)),
            full((d, d)),
            pl.BlockSpec((1, 1, d), lambda b, i: (b, 0, 0)),
        ],
        out_specs=pl.BlockSpec((1, tm, d), lambda b, i: (b, i, 0)),
        out_shape=jax.ShapeDtypeStruct((bsz, n, d), F32),
        compiler_params=_cparams(("parallel", "parallel")),
        name="merge",
    )(x, z, z, z, att, y5, y5, z, e, s5_d, w_glu, w_mla, w_s5, w_hy, w_out, gate)


def _ffn_kernel(x_ref, g_ref, sh_ref, sc_ref, gt_ref, wg_ref, wu_ref, wd_ref, fg_ref, o_ref, h_ref, acc_ref,
                *, nk, final):
    k = pl.program_id(2)

    @pl.when(k == 0)
    def _():
        h_ref[...] = _norm_mod(x_ref[0], g_ref[...], sh_ref[0], sc_ref[0]).astype(BF16)
        acc_ref[...] = jnp.zeros_like(acc_ref)

    h = h_ref[...]
    act = jax.nn.silu(_dot(h, wg_ref[...])) * _dot(h, wu_ref[...])
    acc_ref[...] += _dot(act.astype(BF16), wd_ref[...])

    @pl.when(k == nk - 1)
    def _():
        r = x_ref[0] + gt_ref[0] * acc_ref[...]
        o_ref[0] = _rms(r, fg_ref[...]) if final else r


def _ffn(x, g, shift, scale, gate, w_g, w_u, w_d, final_g, final):
    bsz, n, d = x.shape
    dff = w_g.shape[1]
    tm = min(n, 1024)
    tf = 256
    nk = dff // tf
    vec = pl.BlockSpec((1, 1, d), lambda b, i, k: (b, 0, 0))
    row = pl.BlockSpec((1, d), lambda b, i, k: (0, 0))
    return pl.pallas_call(
        functools.partial(_ffn_kernel, nk=nk, final=final),
        grid=(bsz, n // tm, nk),
        in_specs=[
            pl.BlockSpec((1, tm, d), lambda b, i, k: (b, i, 0)),
            row, vec, vec, vec,
            pl.BlockSpec((d, tf), lambda b, i, k: (0, k)),
            pl.BlockSpec((d, tf), lambda b, i, k: (0, k)),
            pl.BlockSpec((tf, d), lambda b, i, k: (k, 0)),
            row,
        ],
        out_specs=pl.BlockSpec((1, tm, d), lambda b, i, k: (b, i, 0)),
        out_shape=jax.ShapeDtypeStruct((bsz, n, d), F32),
        scratch_shapes=[pltpu.VMEM((tm, d), BF16), pltpu.VMEM((tm, d), F32)],
        compiler_params=_cparams(("parallel", "parallel", "arbitrary")),
        name="ffn",
    )(x, g, shift, scale, gate, w_g, w_u, w_d, final_g)


def _rope_tables(n_tok):
    rows = n_tok // GRID_W
    row = jnp.broadcast_to(jnp.arange(rows, dtype=F32)[:, None], (rows, GRID_W)).reshape(-1)
    col = jnp.broadcast_to(jnp.arange(GRID_W, dtype=F32)[None, :], (rows, GRID_W)).reshape(-1)
    n_freq = MLA_ROPE // 4
    inv = ROPE_BASE ** (-jnp.arange(n_freq, dtype=F32) / n_freq)
    ang = jnp.concatenate([row[:, None] * inv, col[:, None] * inv], axis=-1)
    cos, sin = jnp.cos(ang), jnp.sin(ang)
    pad = jnp.zeros((n_tok, HEAD_W - MLA_ROPE), F32)
    return jnp.concatenate([cos, cos, pad], axis=-1), jnp.concatenate([-sin, sin, pad], axis=-1)


def _identity_rope_tables(n_tok):
    one = jnp.ones((n_tok, MLA_ROPE), F32)
    pad = jnp.zeros((n_tok, HEAD_W - MLA_ROPE), F32)
    return jnp.concatenate([one, pad], axis=-1), jnp.zeros((n_tok, HEAD_W), F32)


def _layout_w_in(w):
    kr = w[:, OFF_KR:OFF_S5]
    x1, x2 = kr[:, 0::2], kr[:, 1::2]
    pad = jnp.zeros((w.shape[0], HEAD_W - 3 * (MLA_ROPE // 2)), w.dtype)
    return jnp.concatenate([w[:, OFF_GATE:], w[:, OFF_HY:OFF_GATE], w[:, OFF_S5:OFF_HY],
                            w[:, OFF_CQ:OFF_CKV], w[:, OFF_CKV:OFF_KR], x1, x2, x1, pad], axis=1).astype(BF16)


def _layout_w_uq(w):
    w = w.reshape(MLA_Q_LORA, MLA_HEADS, MLA_NOPE + MLA_ROPE)
    nope, rope = w[..., :MLA_NOPE], w[..., MLA_NOPE:]
    x1, x2 = rope[..., 0::2], rope[..., 1::2]
    z32 = jnp.zeros_like(rope)
    wq = jnp.concatenate([x1, x2, z32, nope], axis=-1)
    wqs = jnp.concatenate([x2, x1, z32, jnp.zeros_like(nope)], axis=-1)
    shape = (MLA_Q_LORA, MLA_HEADS * HEAD_W)
    return wq.reshape(shape).astype(BF16), wqs.reshape(shape).astype(BF16)


def _layout_w_ukv(w):
    w = w.reshape(MLA_KV_LORA, MLA_HEADS, MLA_NOPE + MLA_V)
    nope, val = w[..., :MLA_NOPE], w[..., MLA_NOPE:]
    wk = jnp.concatenate([jnp.zeros_like(nope), nope], axis=-1).reshape(MLA_KV_LORA, MLA_HEADS * HEAD_W)
    wvt = jnp.concatenate([val, jnp.zeros_like(val[..., :VT_ROWS - MLA_V])], axis=-1).reshape(
        MLA_KV_LORA, MLA_HEADS * VT_ROWS).T
    return wk.astype(BF16), wvt.astype(BF16)


def kernel(x, c, ctx, c_ctx, ada_w, ada_b, norm_mix, w_in, mla_q_norm, mla_w_uq, mla_kv_norm, mla_w_ukv,
           s5_lam_re, s5_lam_im, s5_log_dt, s5_b_re, s5_b_im, s5_c_re, s5_c_im, s5_d, s5_w_glu,
           hy_conv_w, hy_conv_b, hy_f_w1, hy_f_b1, hy_f_w2, hy_f_b2, hy_f_w3, hy_f_freq, hy_bias,
           w_branch_mla, w_branch_s5, w_branch_hy, w_out, norm_ffn, ffn_w_gu, ffn_w_down, final_norm):
    bsz, n_tok, d = x.shape
    n_ctx = ctx.shape[1]
    depth = ada_w.shape[0]
    rope_x = _rope_tables(n_tok)
    rope_c = _identity_rope_tables(n_ctx)
    h_zero = jnp.zeros((bsz, 2, 4, S5_HSTATE), F32)

    for i in range(depth):
        last = i == depth - 1
        mx = (jax.nn.silu(c) @ ada_w[i] + ada_b[i]).reshape(bsz, N_MOD, 1, d)
        mc = jnp.broadcast_to((jax.nn.silu(c_ctx) @ ada_w[i] + ada_b[i]).reshape(1, N_MOD, 1, d),
                              (bsz, N_MOD, 1, d))
        w_z = _layout_w_in(w_in[i])
        wq, wqs = _layout_w_uq(mla_w_uq[i])
        wk, wvt = _layout_w_ukv(mla_w_ukv[i])
        qn, kvn = mla_q_norm[i][None], mla_kv_norm[i][None]
        s5_tabs = _s5_tables(s5_lam_re[i], s5_lam_im[i], s5_log_dt[i], s5_b_re[i], s5_b_im[i],
                             s5_c_re[i], s5_c_im[i])
        fparams = (hy_f_w1[i], hy_f_b1[i], hy_f_w2[i], hy_f_b2[i], hy_f_w3[i], hy_f_freq[i])
        merge_w = (s5_d[i][None], s5_w_glu[i].astype(BF16),
                   w_branch_mla[i].reshape(MLA_HEADS, MLA_V, d).astype(BF16),
                   w_branch_s5[i].astype(BF16), w_branch_hy[i].astype(BF16), w_out[i].astype(BF16))
        w_g = ffn_w_gu[i][:, :D_FF].astype(BF16)
        w_u = ffn_w_gu[i][:, D_FF:].astype(BF16)
        w_d = ffn_w_down[i].astype(BF16)
        g_mix, g_ffn = norm_mix[i][None], norm_ffn[i][None]

        zx = _proj_in(x, g_mix, mx[:, 0], mx[:, 1], w_z)
        zc = _proj_in(ctx, g_mix, mc[:, 0], mc[:, 1], w_z)

        q_c, k_c, vt_c = _mla_prep(zc, qn, kvn, wq, wqs, wk, wvt, *rope_c)
        q_x, k_x, vt_x = _mla_prep(zx, qn, kvn, wq, wqs, wk, wvt, *rope_x)
        a_x = _attention(q_x, jnp.concatenate([k_c, k_x], axis=2), jnp.concatenate([vt_c, vt_x], axis=3))

        y5_c, finals = _s5_scan(zc, h_zero, s5_tabs)
        y5_x, _ = _s5_scan(zx, finals, s5_tabs)

        v_x, g1_x, g2_x = _short_conv(zx, hy_conv_w[i], hy_conv_b[i][None])
        e_x = _hyena_mixer(v_x, (g1_x, g2_x), _hyena_filters(n_tok, *fparams), hy_bias[i])

        x = _merge(x, zx, a_x, y5_x, e_x, *merge_w, mx[:, 2])
        x = _ffn(x, g_ffn, mx[:, 3], mx[:, 4], mx[:, 5], w_g, w_u, w_d, final_norm[None], last)

        if not last:
            a_c = _attention(q_c, k_c, vt_c)
            v_c, g1_c, g2_c = _short_conv(zc, hy_conv_w[i], hy_conv_b[i][None])
            e_c = _hyena_mixer(v_c, (g1_c, g2_c), _hyena_filters(n_ctx, *fparams), hy_bias[i])
            ctx = _merge(ctx, zc, a_c, y5_c, e_c, *merge_w, mc[:, 2])
            ctx = _ffn(ctx, g_ffn, mc[:, 3], mc[:, 4], mc[:, 5], w_g, w_u, w_d, final_norm[None], False)
    return x
```

```python
import functools
import math

import jax
import jax.numpy as jnp
import numpy as np
from jax import lax
from jax.experimental import pallas as pl
from jax.experimental.pallas import tpu as pltpu

F32 = jnp.float32
BF16 = jnp.bfloat16
HIGHEST = lax.Precision.HIGHEST

D_MODEL = 1024
GRID_W = 64
NORM_EPS = 1e-6
N_MOD = 6

MLA_HEADS = 8
MLA_NOPE = 64
MLA_ROPE = 32
MLA_V = 64
MLA_Q_LORA = 256
MLA_KV_LORA = 128
ROPE_BASE = 10000.0
HEAD_W = 128
VT_ROWS = 80
QK_SCALE_LOG2 = (MLA_NOPE + MLA_ROPE) ** -0.5 * math.log2(math.e)
ATTN_SUB = 256
ATTN_LAG = 2
ATTN_UNROLL = 3

S5_WIDTH = 512
S5_GROUP = 16
S5_GROUPS = 32
S5_STATE = 64
S5_HALF = 256
S5_HSTATE = 1024
S5_CHUNK = 128
S5_CHUNKS_PER_STEP = 4

HY_WIDTH = 512
HY_ORDER = 2
HY_BANDS = 16
HY_POS_DIM = 1 + 2 * HY_BANDS
HY_POS_PAD = 64
HY_FILTER_HIDDEN = 64
HY_FILTER_OUT = HY_ORDER * 2 * HY_WIDTH
HY_DECAY_TARGET = 1e-2
HY_DECAY_SHORT = 0.3
HY_DECAY_LONG = 1.5
HY_DECAY_SHIFT = 0.05

D_FF = 2816

OFF_CQ = 0
OFF_CKV = OFF_CQ + MLA_Q_LORA
OFF_KR = OFF_CKV + MLA_KV_LORA
OFF_S5 = OFF_KR + MLA_ROPE
OFF_HY = OFF_S5 + S5_WIDTH
OFF_GATE = OFF_HY + 3 * HY_WIDTH

Z_GATE = 0
Z_HY = 3072
Z_S5 = 4608
Z_MLA = 5120
Z_WIDTH = 5632

VMEM_LIMIT_V7X = 52 * 1024 * 1024


def _cparams(sem, flags=None):
    return pltpu.CompilerParams(dimension_semantics=sem, vmem_limit_bytes=VMEM_LIMIT_V7X, flags=flags)


def _dot(a, b):
    return jnp.dot(a, b, preferred_element_type=F32)


def _rms(x, g):
    return x * lax.rsqrt(jnp.mean(x * x, axis=-1, keepdims=True) + NORM_EPS) * g


def _norm_mod(x, g, shift, scale):
    return _rms(x, g) * (1.0 + scale) + shift


def _proj_kernel(x_ref, g_ref, sh_ref, sc_ref, w_ref, o_ref, h_ref):
    @pl.when(pl.program_id(2) == 0)
    def _():
        h_ref[...] = _norm_mod(x_ref[0], g_ref[...], sh_ref[0], sc_ref[0]).astype(BF16)

    o_ref[0] = _dot(h_ref[...], w_ref[...]).astype(BF16)


def _proj_in(x, g, shift, scale, w):
    bsz, n, d = x.shape
    nz = w.shape[1]
    tm = min(n, 2048)
    tn = 512
    return pl.pallas_call(
        _proj_kernel,
        grid=(bsz, n // tm, nz // tn),
        in_specs=[
            pl.BlockSpec((1, tm, d), lambda b, i, j: (b, i, 0)),
            pl.BlockSpec((1, d), lambda b, i, j: (0, 0)),
            pl.BlockSpec((1, 1, d), lambda b, i, j: (b, 0, 0)),
            pl.BlockSpec((1, 1, d), lambda b, i, j: (b, 0, 0)),
            pl.BlockSpec((d, tn), lambda b, i, j: (0, j)),
        ],
        out_specs=pl.BlockSpec((1, tm, tn), lambda b, i, j: (b, i, j)),
        out_shape=jax.ShapeDtypeStruct((bsz, n, nz), BF16),
        scratch_shapes=[pltpu.VMEM((tm, d), BF16)],
        compiler_params=_cparams(("parallel", "parallel", "arbitrary")),
        name="proj_in",
    )(x, g, shift, scale, w)


def _mla_prep_kernel(z_ref, qn_ref, kvn_ref, wq_ref, wqs_ref, wk_ref, wvt_ref, c_ref, s_ref, q_ref, k_ref, vt_ref):
    z = z_ref[0].astype(F32)
    hq = _rms(z[:, :MLA_Q_LORA], qn_ref[...]).astype(BF16)
    hkv = _rms(z[:, MLA_Q_LORA:MLA_Q_LORA + MLA_KV_LORA], kvn_ref[...]).astype(BF16)
    krb = z[:, MLA_Q_LORA + MLA_KV_LORA:]
    qa = _dot(hq, wq_ref[...])
    qb = _dot(hq, wqs_ref[...])
    kn = _dot(hkv, wk_ref[...])
    vt = lax.dot_general(wvt_ref[...], hkv, (((1,), (1,)), ((), ())), preferred_element_type=F32)
    ck = c_ref[...]
    sn = s_ref[...]
    lane = lax.broadcasted_iota(jnp.int32, ck.shape, 1)
    cq = jnp.where(lane < MLA_ROPE, ck, 1.0)
    kr = krb * ck + pltpu.roll(krb, HEAD_W - MLA_ROPE // 2, 1) * sn
    row = lax.broadcasted_iota(jnp.int32, (VT_ROWS, z.shape[0]), 0)
    for h in range(MLA_HEADS):
        sl = slice(h * HEAD_W, (h + 1) * HEAD_W)
        q_ref[0, h] = ((qa[:, sl] * cq + qb[:, sl] * sn) * QK_SCALE_LOG2).astype(BF16)
        k_ref[0, h] = (kr + kn[:, sl]).astype(BF16)
        vt_ref[0, h] = jnp.where(row == MLA_V, 1.0, vt[h * VT_ROWS:(h + 1) * VT_ROWS, :]).astype(BF16)


def _mla_prep(z, qn, kvn, wq, wqs, wk, wvt, ctab, stab):
    bsz, n, _ = z.shape
    tm = min(n, 512)
    hw = MLA_HEADS * HEAD_W
    zb = Z_MLA // 512
    full = lambda shape: pl.BlockSpec(shape, lambda b, i: (0,) * len(shape))
    return pl.pallas_call(
        _mla_prep_kernel,
        grid=(bsz, n // tm),
        in_specs=[
            pl.BlockSpec((1, tm, 512), lambda b, i: (b, i, zb)),
            full((1, MLA_Q_LORA)),
            full((1, MLA_KV_LORA)),
            full((MLA_Q_LORA, hw)),
            full((MLA_Q_LORA, hw)),
            full((MLA_KV_LORA, hw)),
            full((MLA_HEADS * VT_ROWS, MLA_KV_LORA)),
            pl.BlockSpec((tm, HEAD_W), lambda b, i: (i, 0)),
            pl.BlockSpec((tm, HEAD_W), lambda b, i: (i, 0)),
        ],
        out_specs=[
            pl.BlockSpec((1, MLA_HEADS, tm, HEAD_W), lambda b, i: (b, 0, i, 0)),
            pl.BlockSpec((1, MLA_HEADS, tm, HEAD_W), lambda b, i: (b, 0, i, 0)),
            pl.BlockSpec((1, MLA_HEADS, VT_ROWS, tm), lambda b, i: (b, 0, 0, i)),
        ],
        out_shape=[
            jax.ShapeDtypeStruct((bsz, MLA_HEADS, n, HEAD_W), BF16),
            jax.ShapeDtypeStruct((bsz, MLA_HEADS, n, HEAD_W), BF16),
            jax.ShapeDtypeStruct((bsz, MLA_HEADS, VT_ROWS, n), BF16),
        ],
        compiler_params=_cparams(("parallel", "parallel")),
        name="mla_prep",
    )(z, qn, kvn, wq, wqs, wk, wvt, ctab, stab)


def _attn_kernel(q_ref, k_ref, vt_ref, o_ref, s0_ref, s1_ref, s2_ref, p0_ref, p1_ref, p2_ref, acc_ref, *, tk, nkc):
    q = q_ref[0, 0]
    tq = q.shape[0]
    s_refs = (s0_ref, s1_ref, s2_ref)
    p_refs = (p0_ref, p1_ref, p2_ref)

    ts = min(tk, ATTN_SUB)
    subs = [slice(r, r + ts) for r in range(0, tk, ts)]

    def scores_sub(j, s_ref, sub, cmax):
        off = pl.multiple_of(j * tk + sub.start, ts)
        s = lax.dot_general(k_ref[0, 0, pl.ds(off, ts), :], q, (((1,), (1,)), ((), ())),
                            preferred_element_type=F32)
        s_ref[sub, :] = s
        cm = jnp.max(s, axis=0, keepdims=True)
        return cm if cmax is None else jnp.maximum(cmax, cm)

    def softmax_sub(slot, sub, mn):
        p_refs[slot][sub, :] = jnp.exp2(s_refs[slot][sub, :] - mn).astype(BF16)

    def step(i, slot, carry, do_softmax, do_scores):
        cm, m, alpha = carry
        mn = jnp.maximum(m, cm)
        off = pl.multiple_of(i * tk, tk)
        acc_ref[...] = alpha * acc_ref[...] + _dot(vt_ref[0, 0, :, pl.ds(off, tk)], p_refs[slot][...])
        cm_new, cms = None, []
        for r, sub in enumerate(subs):
            mn_sub = mn
            if do_scores:
                cm_new = scores_sub(i + 2, s_refs[(slot + 2) % 3], sub, cm_new)
                cms.append(cm_new)
                if r >= ATTN_LAG:
                    mn_sub = jnp.maximum(mn, cms[r - ATTN_LAG] - 1e30)
            if do_softmax:
                softmax_sub((slot + 1) % 3, sub, mn_sub)
        if do_softmax:
            alpha, m = jnp.exp2(m - mn), mn
        return (cm_new if do_scores else cm), m, alpha

    m = jnp.full((1, tq), -1e30, F32)
    acc_ref[...] = jnp.zeros_like(acc_ref)
    cm, cm_next = None, None
    for sub in subs:
        cm = scores_sub(0, s0_ref, sub, cm)
        if nkc > 1:
            cm_next = scores_sub(1, s1_ref, sub, cm_next)
    mn = jnp.maximum(m, cm)
    for sub in subs:
        softmax_sub(0, sub, mn)
    carry = (cm_next if nkc > 1 else cm, mn, jnp.exp2(m - mn))

    def body(t, carry):
        for r in range(ATTN_UNROLL):
            carry = step(ATTN_UNROLL * t + r, r % 3, carry, True, True)
        return carry

    nloop = max(nkc - 2, 0) // ATTN_UNROLL
    carry = lax.fori_loop(0, nloop, body, carry)
    for i in range(ATTN_UNROLL * nloop, nkc):
        carry = step(i, i % 3, carry, i + 1 < nkc, i + 2 < nkc)
    acc = acc_ref[...]
    out = acc * (1.0 / acc[MLA_V:MLA_V + 1, :])
    out = jnp.concatenate([out, jnp.zeros((HEAD_W - VT_ROWS, tq), F32)], axis=0).T
    o_ref[0, 0] = out[:, :MLA_V].astype(BF16)


def _kv_chunk(nk):
    for tk in (1280, 1024, 640, 512, 256, 128):
        if nk % tk == 0:
            return tk
    raise ValueError(f"unsupported key count {nk}")


def _attention(q, k, vt):
    bsz, nh, nq, _ = q.shape
    nk = k.shape[2]
    tq = min(nq, 512)
    tk = _kv_chunk(nk)
    return pl.pallas_call(
        functools.partial(_attn_kernel, tk=tk, nkc=nk // tk),
        grid=(bsz, nh, nq // tq),
        in_specs=[
            pl.BlockSpec((1, 1, tq, HEAD_W), lambda b, h, i: (b, h, i, 0)),
            pl.BlockSpec((1, 1, nk, HEAD_W), lambda b, h, i: (b, h, 0, 0)),
            pl.BlockSpec((1, 1, VT_ROWS, nk), lambda b, h, i: (b, h, 0, 0)),
        ],
        out_specs=pl.BlockSpec((1, 1, tq, MLA_V), lambda b, h, i: (b, h, i, 0)),
        out_shape=jax.ShapeDtypeStruct((bsz, nh, nq, MLA_V), BF16),
        scratch_shapes=[pltpu.VMEM((tk, tq), F32)] * 3 + [pltpu.VMEM((tk, tq), BF16)] * 3
        + [pltpu.VMEM((VT_ROWS, tq), F32)],
        compiler_params=_cparams(("parallel", "parallel", "arbitrary")),
        name="attention",
    )(q, k, vt)


def _cmul(ar, ai, br, bi):
    return ar * br - ai * bi, ar * bi + ai * br


def _s5_kernel(u_ref, h0_ref, bm_ref, cm_ref, wn_ref, wp_ref, l1_ref, tri_ref, y_ref, hf_ref,
               carry_ref, h0cat_ref, h1cat_ref, *, tc, nr, ns, reverse):
    i = pl.program_id(1)

    @pl.when(i == 0)
    def _():
        carry_ref[...] = h0_ref[0]

    u = u_ref[0]
    tri = tri_ref[...]
    order = range(nr - 1, -1, -1) if reverse else range(nr)
    edge = 0 if reverse else tc - 1
    ys = []
    for hf, hcat_ref in enumerate((h0cat_ref, h1cat_ref)):
        bu = _dot(u[:, hf * S5_HALF:(hf + 1) * S5_HALF], bm_ref[hf])
        hr_in, hi_in = carry_ref[2 * hf:2 * hf + 1, :], carry_ref[2 * hf + 1:2 * hf + 2, :]
        for c in order:
            rows = slice(c * tc, (c + 1) * tc)
            xr, xi = _cmul(wn_ref[hf, 0], wn_ref[hf, 1], bu[rows, :S5_HSTATE], bu[rows, S5_HSTATE:])
            s = _dot(tri, jnp.concatenate([xr, xi], axis=1).astype(BF16))
            cr, ci = _cmul(l1_ref[hf, 0], l1_ref[hf, 1], hr_in, hi_in)
            hr, hi = _cmul(wp_ref[hf, 0], wp_ref[hf, 1], s[:, :S5_HSTATE] + cr, s[:, S5_HSTATE:] + ci)
            hcat_ref[rows, :S5_HSTATE] = hr.astype(BF16)
            hcat_ref[rows, S5_HSTATE:] = hi.astype(BF16)
            hr_in, hi_in = hr[edge:edge + 1], hi[edge:edge + 1]
        carry_ref[2 * hf:2 * hf + 1, :] = hr_in
        carry_ref[2 * hf + 1:2 * hf + 2, :] = hi_in
        ys.append(_dot(hcat_ref[...], cm_ref[hf]))
    y_ref[0] = jnp.concatenate(ys, axis=1)

    @pl.when(i == ns - 1)
    def _():
        hf_ref[0] = carry_ref[...]


def _s5_scan_dir(z, h0, tabs, reverse):
    bm, cm, wn, wp, l1, tri = tabs
    bsz, n, _ = z.shape
    tc = S5_CHUNK
    nr = min(S5_CHUNKS_PER_STEP, n // tc)
    rows = nr * tc
    ns = n // rows
    zb = Z_S5 // S5_WIDTH
    blk = (lambda i: ns - 1 - i) if reverse else (lambda i: i)
    full = lambda shape: pl.BlockSpec(shape, lambda b, i: (0,) * len(shape))
    return pl.pallas_call(
        functools.partial(_s5_kernel, tc=tc, nr=nr, ns=ns, reverse=reverse),
        grid=(bsz, ns),
        in_specs=[
            pl.BlockSpec((1, rows, S5_WIDTH), lambda b, i: (b, blk(i), zb)),
            pl.BlockSpec((1, 4, S5_HSTATE), lambda b, i: (b, 0, 0)),
            full((2, S5_HALF, 2 * S5_HSTATE)),
            full((2, 2 * S5_HSTATE, S5_HALF)),
            full((2, 2, tc, S5_HSTATE)),
            full((2, 2, tc, S5_HSTATE)),
            full((2, 2, 1, S5_HSTATE)),
            full((tc, tc)),
        ],
        out_specs=[
            pl.BlockSpec((1, rows, S5_WIDTH), lambda b, i: (b, blk(i), 0)),
            pl.BlockSpec((1, 4, S5_HSTATE), lambda b, i: (b, 0, 0)),
        ],
        out_shape=[
            jax.ShapeDtypeStruct((bsz, n, S5_WIDTH), F32),
            jax.ShapeDtypeStruct((bsz, 4, S5_HSTATE), F32),
        ],
        scratch_shapes=[pltpu.VMEM((4, S5_HSTATE), F32),
                        pltpu.VMEM((rows, 2 * S5_HSTATE), BF16), pltpu.VMEM((rows, 2 * S5_HSTATE), BF16)],
        compiler_params=_cparams(("parallel", "arbitrary")),
        name="s5_scan",
    )(z, h0, bm, cm, wn, wp, l1, tri)


def _s5_scan(z, h0, tabs):
    outs = [_s5_scan_dir(z, h0[:, d], [t[d] for t in tabs], reverse=bool(d)) for d in range(2)]
    return outs[0][0], outs[1][0], jnp.stack([outs[0][1], outs[1][1]], axis=1)


def _s5_tables(lam_re, lam_im, log_dt, b_re, b_im, c_re, c_im):
    tc = S5_CHUNK
    dt = jnp.exp(log_dt)[..., None]
    zr, zi = lam_re * dt, lam_im * dt
    mag = jnp.exp(zr)
    lbr, lbi = mag * jnp.cos(zi), mag * jnp.sin(zi)
    den = lam_re * lam_re + lam_im * lam_im
    nr, ni = lbr - 1.0, lbi
    cfr = (nr * lam_re + ni * lam_im) / den
    cfi = (ni * lam_re - nr * lam_im) / den
    bbr = cfr[..., None] * b_re - cfi[..., None] * b_im
    bbi = cfr[..., None] * b_im + cfi[..., None] * b_re
    eye = jnp.eye(S5_GROUP, dtype=F32)

    def blockdiag_in(b):
        b = b.reshape(2, 2, S5_GROUP, S5_STATE, S5_GROUP)
        return jnp.einsum('gk,dhgpn->dhgnkp', eye, b).reshape(2, 2, S5_HALF, S5_HSTATE)

    def blockdiag_out(c):
        c = c.reshape(2, 2, S5_GROUP, S5_GROUP, S5_STATE)
        return jnp.einsum('gk,dhgnp->dhgpkn', eye, c).reshape(2, 2, S5_HSTATE, S5_HALF)

    bm = jnp.concatenate([blockdiag_in(bbr), blockdiag_in(bbi)], axis=-1).astype(BF16)
    cm = jnp.concatenate([blockdiag_out(c_re), blockdiag_out(-c_im)], axis=-2).astype(BF16)

    def powers(k):
        zr_ = zr.reshape(2, 2, 1, S5_HSTATE)
        zi_ = zi.reshape(2, 2, 1, S5_HSTATE)
        kk = k[:, None, :, None]
        m = jnp.exp(kk * zr_)
        return jnp.stack([m * jnp.cos(kk * zi_), m * jnp.sin(kk * zi_)], axis=2)

    t = jnp.arange(tc, dtype=F32)
    wn = powers(jnp.stack([-t, -(tc - 1 - t)]))
    wp = powers(jnp.stack([t, tc - 1 - t]))
    l1 = powers(jnp.ones((2, 1), F32))
    r = jnp.arange(tc)
    tri = jnp.stack([r[:, None] >= r[None, :], r[:, None] <= r[None, :]]).astype(BF16)
    return bm, cm, wn, wp, l1, tri


HALO = 16


def _sconv_kernel(z_ref, zp_ref, zn_ref, w_ref, b_ref, v_ref, g1_ref, g2_ref, *, nt):
    i = pl.program_id(1)
    z = z_ref[0].astype(F32)
    tm = z.shape[0]
    row = lax.broadcasted_iota(jnp.int32, z.shape, 0)
    prev = jnp.where(i > 0, zp_ref[0, HALO - 1:HALO, :].astype(F32), 0.0)
    nxt = jnp.where(i < nt - 1, zn_ref[0, 0:1, :].astype(F32), 0.0)
    up = jnp.where(row == 0, prev, pltpu.roll(z, 1, 0))
    dn = jnp.where(row == tm - 1, nxt, pltpu.roll(z, tm - 1, 0))
    u = up * w_ref[0:1, :] + z * w_ref[1:2, :] + dn * w_ref[2:3, :] + b_ref[...]
    for k, o_ref in enumerate((v_ref, g1_ref, g2_ref)):
        o_ref[0] = u[:, k * HY_WIDTH:(k + 1) * HY_WIDTH]


def _short_conv(z, w, b):
    bsz, n, _ = z.shape
    cw = 3 * HY_WIDTH
    tm = min(n, 512)
    nt = n // tm
    zb = Z_HY // cw
    rh = tm // HALO
    out = pl.BlockSpec((1, tm, HY_WIDTH), lambda b_, i: (b_, i, 0))
    return pl.pallas_call(
        functools.partial(_sconv_kernel, nt=nt),
        grid=(bsz, nt),
        in_specs=[
            pl.BlockSpec((1, tm, cw), lambda b_, i: (b_, i, zb)),
            pl.BlockSpec((1, HALO, cw), lambda b_, i: (b_, jnp.maximum(i * rh - 1, 0), zb)),
            pl.BlockSpec((1, HALO, cw), lambda b_, i: (b_, jnp.minimum((i + 1) * rh, n // HALO - 1), zb)),
            pl.BlockSpec((3, cw), lambda b_, i: (0, 0)),
            pl.BlockSpec((1, cw), lambda b_, i: (0, 0)),
        ],
        out_specs=[out] * 3,
        out_shape=[jax.ShapeDtypeStruct((bsz, n, HY_WIDTH), F32)] * 3,
        compiler_params=_cparams(("parallel", "parallel")),
        name="short_conv",
    )(z, z, z, w, b)


def _filt_kernel(f_ref, w1_ref, b1_ref, w2_ref, b2_ref, w3h_ref, w3l_ref, fq_ref, dl_ref, o_ref, *, n_tok):
    z = f_ref[...]
    tm = z.shape[0]
    fq = fq_ref[...]
    hid = jnp.sin(fq * (jnp.dot(z, w1_ref[...], precision=HIGHEST, preferred_element_type=F32) + b1_ref[...]))
    hid = jnp.sin(fq * (jnp.dot(hid, w2_ref[...], precision=HIGHEST, preferred_element_type=F32) + b2_ref[...]))
    hid_hi = hid.astype(BF16)
    hid_lo = (hid - hid_hi.astype(F32)).astype(BF16)
    filt = _dot(hid_hi, w3h_ref[...]) + (_dot(hid_hi, w3l_ref[...]) + _dot(hid_lo, w3h_ref[...]))
    filt = filt * (jnp.exp(-z[:, 0:1] * dl_ref[...]) + HY_DECAY_SHIFT)
    m = pl.program_id(0) * tm + lax.broadcasted_iota(jnp.int32, (tm, HY_WIDTH), 0)
    for o in range(HY_ORDER):
        base = o * 2 * HY_WIDTH
        fwd = filt[:, base:base + HY_WIDTH]
        bwd = filt[:, base + HY_WIDTH:base + 2 * HY_WIDTH]
        o_ref[o] = jnp.where(m < n_tok, fwd, 0.0) + jnp.where((m > n_tok) | (m == 0), bwd, 0.0)


def _hyena_filter_feats(n_tok):
    m = jnp.arange(2 * n_tok)
    lag = jnp.where(m < n_tok, m, jnp.where(m > n_tok, 2 * n_tok - m, 0))
    t = (lag.astype(F32) / (n_tok - 1))[:, None]
    w = (2.0 * math.pi * lag.astype(F32) / n_tok)[:, None]
    bands = jnp.linspace(1e-4, HY_BANDS - 1, HY_BANDS, dtype=F32)[None, :]
    feats = jnp.concatenate([t, jnp.cos(bands * w), -jnp.sin(bands * w)], axis=-1)
    return jnp.pad(feats, ((0, 0), (0, HY_POS_PAD - HY_POS_DIM)))


def _hyena_filters(n_tok, w1, b1, w2, b2, w3, freq):
    feats = _hyena_filter_feats(n_tok)
    deltas = jnp.abs(jnp.linspace(math.log(HY_DECAY_TARGET) / HY_DECAY_SHORT,
                                  math.log(HY_DECAY_TARGET) / HY_DECAY_LONG, HY_FILTER_OUT, dtype=F32))[None, :]
    w1p = jnp.pad(w1, ((0, HY_POS_PAD - HY_POS_DIM), (0, 0)))
    w3_hi = w3.astype(BF16)
    w3_lo = (w3 - w3_hi.astype(F32)).astype(BF16)
    n2 = 2 * n_tok
    tm = min(n2, 512)
    full = lambda shape: pl.BlockSpec(shape, lambda i: (0,) * len(shape))
    hh = HY_FILTER_HIDDEN
    return pl.pallas_call(
        functools.partial(_filt_kernel, n_tok=n_tok),
        grid=(n2 // tm,),
        in_specs=[
            pl.BlockSpec((tm, HY_POS_PAD), lambda i: (i, 0)),
            full((HY_POS_PAD, hh)), full((1, hh)), full((hh, hh)), full((1, hh)),
            full((hh, HY_FILTER_OUT)), full((hh, HY_FILTER_OUT)), full((1, hh)), full((1, HY_FILTER_OUT)),
        ],
        out_specs=pl.BlockSpec((HY_ORDER, tm, HY_WIDTH), lambda i: (0, i, 0)),
        out_shape=jax.ShapeDtypeStruct((HY_ORDER, n2, HY_WIDTH), F32),
        compiler_params=_cparams(("parallel",)),
        name="hyena_filter",
    )(feats, w1p, b1[None], w2, b2[None], w3_hi, w3_lo, freq[None], deltas)


def _fft_split(n_fft):
    if n_fft <= 1024:
        return n_fft, 1
    n1 = 1 << (int(math.log2(n_fft)) // 2)
    return n1, n_fft // n1


def _cis(num, den, sign):
    ang = (2.0 * math.pi / den) * num.astype(F32)
    return jnp.cos(ang), sign * jnp.sin(ang)


def _fft_tables(n_fft):
    n1, n2 = _fft_split(n_fft)
    a = jnp.arange(n1)
    f1r, f1i = _cis((a[:, None] * a[None, :]) % n1, n1, -1.0)
    s1 = jnp.concatenate([f1r, f1i], axis=0).astype(BF16)
    half = n1 // 2
    s3 = (jnp.concatenate([f1r[:half], -f1i[:half]], axis=0) / n_fft).astype(BF16)
    if n2 == 1:
        return s1, s3, None, None
    b = jnp.arange(n2)
    f2r, f2i = _cis((b[:, None] * b[None, :]) % n2, n2, -1.0)
    twr, twi = _cis((jnp.arange(n1)[:, None] * b[None, :]) % n_fft, n_fft, -1.0)
    fr, fi = _cmul(f2r[None], f2i[None], twr[:, None, :], twi[:, None, :])
    ft = jnp.concatenate([fr, fi], axis=1).astype(BF16)
    frt, fit = _cmul(f2r[None], f2i[None], twr[:, :, None], twi[:, :, None])
    it = jnp.concatenate([frt, -fit], axis=1).astype(BF16)
    return s1, s3, ft, it


def _stacked_cdot(s, xr, xi, m):
    p = _dot(s, xr.astype(BF16))
    if xi is None:
        return p[:m], p[m:]
    q = _dot(s, xi.astype(BF16))
    return p[:m] - q[m:], q[:m] + p[m:]


def _fft_a_kernel(*refs, parts, n1, has_g):
    if has_g:
        x_ref, s_ref, g_ref, y_ref = refs
    else:
        x_ref, s_ref, y_ref = refs
    yr, yi = _stacked_cdot(s_ref[...], x_ref[0], x_ref[1] if parts == 2 else None, n1)
    if has_g:
        yr, yi = _cmul(yr, yi, g_ref[0], g_ref[1])
    y_ref[0] = yr
    y_ref[1] = yi


def _fft_a(x, s1, g=None):
    parts, a_rows, cols = x.shape
    n1 = s1.shape[0] // 2
    tc = min(cols, 2048)
    in_specs = [
        pl.BlockSpec((parts, a_rows, tc), lambda j: (0, 0, j)),
        pl.BlockSpec((2 * n1, a_rows), lambda j: (0, 0)),
    ]
    args = [x, s1[:, :a_rows]]
    if g is not None:
        in_specs.append(pl.BlockSpec((2, n1, tc), lambda j: (0, 0, j)))
        args.append(g)
    return pl.pallas_call(
        functools.partial(_fft_a_kernel, parts=parts, n1=n1, has_g=g is not None),
        grid=(cols // tc,),
        in_specs=in_specs,
        out_specs=pl.BlockSpec((2, n1, tc), lambda j: (0, 0, j)),
        out_shape=jax.ShapeDtypeStruct((2, n1, cols), F32),
        compiler_params=_cparams(("parallel",)),
        name="fft_stage_a",
    )(*args)


def _fft_b_kernel(*refs, n2, inverse):
    if inverse:
        y_ref, ft_ref, it_ref, g_ref, o_ref = refs
    else:
        y_ref, ft_ref, o_ref = refs
    xr, xi = _stacked_cdot(ft_ref[0], y_ref[0, 0], y_ref[1, 0], n2)
    if inverse:
        zr, zi = _cmul(xr, xi, g_ref[0, 0], g_ref[1, 0])
        xr, xi = _stacked_cdot(it_ref[0], zr, zi, n2)
    o_ref[0, 0] = xr.astype(o_ref.dtype)
    o_ref[1, 0] = xi.astype(o_ref.dtype)


def _fft_b(y, ft, it=None, g=None):
    _, n1, n2, ch = y.shape
    inverse = it is not None
    blk = pl.BlockSpec((2, 1, n2, ch), lambda c: (0, c, 0, 0))
    mat = pl.BlockSpec((1, 2 * n2, n2), lambda c: (c, 0, 0))
    in_specs, args = [blk, mat], [y, ft]
    if inverse:
        in_specs += [mat, blk]
        args += [it, g]
    return pl.pallas_call(
        functools.partial(_fft_b_kernel, n2=n2, inverse=inverse),
        grid=(n1,),
        in_specs=in_specs,
        out_specs=blk,
        out_shape=jax.ShapeDtypeStruct(y.shape, BF16 if inverse else F32),
        compiler_params=_cparams(("parallel",)),
        name="fft_stage_b",
    )(*args)


def _fft_c_kernel(y_ref, s_ref, x_ref, g_ref, b_ref, o_ref, *, half):
    cr, ci = _stacked_cdot(s_ref[...], y_ref[0], y_ref[1], half)
    bias = b_ref[...]
    o_ref[0] = g_ref[0] * (cr + bias * x_ref[0])
    o_ref[1] = g_ref[1] * (ci + bias * x_ref[1])


def _fft_c(y, s3, x, gate, bias_cols):
    _, n1, cols = y.shape
    half = n1 // 2
    tc = min(cols, 2048)
    io = pl.BlockSpec((2, half, tc), lambda j: (0, 0, j))
    return pl.pallas_call(
        functools.partial(_fft_c_kernel, half=half),
        grid=(cols // tc,),
        in_specs=[
            pl.BlockSpec((2, n1, tc), lambda j: (0, 0, j)),
            pl.BlockSpec((n1, n1), lambda j: (0, 0)),
            io, io,
            pl.BlockSpec((1, tc), lambda j: (0, j)),
        ],
        out_specs=io,
        out_shape=jax.ShapeDtypeStruct((2, half, cols), F32),
        compiler_params=_cparams(("parallel",)),
        name="fft_stage_c",
    )(y, s3, x, gate, bias_cols)


FFT_BT = 16


def _to_fine_major(x):
    return pltpu.einshape("abc->bac", x)


def _fft_a4_kernel(x_ref, s_ref, y_ref, *, parts, n1):
    s = s_ref[...]
    xr = _to_fine_major(x_ref[0].astype(BF16))
    xi = _to_fine_major(x_ref[1].astype(BF16)) if parts == 2 else None
    out = [_stacked_cdot(s, xr[b], None if xi is None else xi[b], n1) for b in range(FFT_BT)]
    y_ref[0] = pltpu.einshape("bac->abc", jnp.stack([o[0] for o in out])).astype(BF16)
    y_ref[1] = pltpu.einshape("bac->abc", jnp.stack([o[1] for o in out])).astype(BF16)


def _fft_a4(x, s1):
    parts, a_rows, n2, ch = x.shape
    n1 = s1.shape[0] // 2
    return pl.pallas_call(
        functools.partial(_fft_a4_kernel, parts=parts, n1=n1),
        grid=(n2 // FFT_BT,),
        in_specs=[
            pl.BlockSpec((parts, a_rows, FFT_BT, ch), lambda j: (0, 0, j, 0)),
            pl.BlockSpec((2 * n1, a_rows), lambda j: (0, 0)),
        ],
        out_specs=pl.BlockSpec((2, n1, FFT_BT, ch), lambda j: (0, 0, j, 0)),
        out_shape=jax.ShapeDtypeStruct((2, n1, n2, ch), BF16),
        compiler_params=_cparams(("parallel",)),
        name="fft_stage_a",
    )(x, s1[:, :a_rows])


def _fft_c4_kernel(y_ref, s_ref, x_ref, g_ref, b_ref, o_ref, c_ref, *, half):
    s = s_ref[...]
    yr = _to_fine_major(y_ref[0].astype(BF16))
    yi = _to_fine_major(y_ref[1].astype(BF16))
    out = [_stacked_cdot(s, yr[b], yi[b], half) for b in range(FFT_BT)]
    c_ref[0] = pltpu.einshape("bac->abc", jnp.stack([o[0] for o in out]))
    c_ref[1] = pltpu.einshape("bac->abc", jnp.stack([o[1] for o in out]))
    o_ref[...] = g_ref[...] * (c_ref[...] + b_ref[...][None, None] * x_ref[...])


def _fft_c4(y, s3, x, gate, bias):
    _, n1, n2, ch = y.shape
    half = n1 // 2
    io = pl.BlockSpec((2, half, FFT_BT, ch), lambda j: (0, 0, j, 0))
    return pl.pallas_call(
        functools.partial(_fft_c4_kernel, half=half),
        grid=(n2 // FFT_BT,),
        in_specs=[
            pl.BlockSpec((2, n1, FFT_BT, ch), lambda j: (0, 0, j, 0)),
            pl.BlockSpec((n1, n1), lambda j: (0, 0)),
            io, io,
            pl.BlockSpec((1, ch), lambda j: (0, 0)),
        ],
        out_specs=io,
        out_shape=jax.ShapeDtypeStruct((2, half, n2, ch), F32),
        scratch_shapes=[pltpu.VMEM((2, half, FFT_BT, ch), F32)],
        compiler_params=_cparams(("parallel",)),
        name="fft_stage_c",
    )(y, s3, x, gate, bias)


def _hyena_mixer(v, gates, gfilt, bias):
    bsz, n, ch = v.shape
    assert bsz == 2, "the complex packing of the long convolution pairs exactly two batch rows"
    n_fft = 2 * n
    n1, n2 = _fft_split(n_fft)
    s1, s3, ft, it = _fft_tables(n_fft)
    if n2 == 1:
        y = v
        for o in range(HY_ORDER):
            gspec = _fft_a(gfilt[o][None], s1)
            y = _fft_c(_fft_a(y, s1, gspec), s3, y, gates[o], bias[o][None])
        return y
    half = n1 // 2
    y = v.reshape(2, half, n2, ch)
    for o in range(HY_ORDER):
        gspec = _fft_b(_fft_a4(gfilt[o].reshape(1, n1, n2, ch), s1), ft)
        spec = _fft_b(_fft_a4(y, s1), ft, it, gspec)
        y = _fft_c4(spec, s3, y, gates[o].reshape(2, half, n2, ch), bias[o][None])
    return y.reshape(2, n, ch)


def _gelu_tanh(x):
    return 0.5 * x * (1.0 + jnp.tanh(math.sqrt(2.0 / math.pi) * (x + 0.044715 * (x * x * x))))


def _merge_kernel(x_ref, g0_ref, g1_ref, g2_ref, a_ref, yf_ref, yb_ref, u_ref, e_ref, d_ref,
                  wglu_ref, wm_ref, ws_ref, wh_ref, wo_ref, gt_ref, o_ref):
    att = _dot(jnp.concatenate([a_ref[0, h] for h in range(MLA_HEADS)], axis=-1), wm_ref[...])
    y = yf_ref[0] + yb_ref[0] + d_ref[...] * u_ref[0].astype(F32)
    gy = _gelu_tanh(y)
    s5 = gy * jax.nn.sigmoid(_dot(gy.astype(BF16), wglu_ref[...]))
    s5 = _dot(s5.astype(BF16), ws_ref[...])
    hy = _dot(e_ref[0].astype(BF16), wh_ref[...])
    merged = jax.nn.sigmoid(g0_ref[0].astype(F32)) * att
    merged = merged + jax.nn.sigmoid(g1_ref[0].astype(F32)) * s5
    merged = merged + jax.nn.sigmoid(g2_ref[0].astype(F32)) * hy
    o_ref[0] = x_ref[0] + gt_ref[0] * _dot(merged.astype(BF16), wo_ref[...])


def _merge(x, z, att, y5, e, s5_d, w_glu, w_mla, w_s5, w_hy, w_out, gate):
    bsz, n, d = x.shape
    tm = min(n, 512)
    full = lambda shape: pl.BlockSpec(shape, lambda b, i: (0,) * len(shape))
    zs5 = Z_S5 // S5_WIDTH
    return pl.pallas_call(
        _merge_kernel,
        grid=(bsz, n // tm),
        in_specs=[
            pl.BlockSpec((1, tm, d), lambda b, i: (b, i, 0)),
            pl.BlockSpec((1, tm, d), lambda b, i: (b, i, 0)),
            pl.BlockSpec((1, tm, d), lambda b, i: (b, i, 1)),
            pl.BlockSpec((1, tm, d), lambda b, i: (b, i, 2)),
            pl.BlockSpec((1, MLA_HEADS, tm, MLA_V), lambda b, i: (b, 0, i, 0)),
            pl.BlockSpec((1, tm, S5_WIDTH), lambda b, i: (b, i, 0)),
            pl.BlockSpec((1, tm, S5_WIDTH), lambda b, i: (b, i, 0)),
            pl.BlockSpec((1, tm, S5_WIDTH), lambda b, i: (b, i, zs5)),
            pl.BlockSpec((1, tm, HY_WIDTH), lambda b, i: (b, i, 0)),
            full((1, S5_WIDTH)),
            full((S5_WIDTH, S5_WIDTH)),
            full((MLA_HEADS * MLA_V, d)),
            full((S5_WIDTH, d)),
            full((HY_WIDTH, d)),
            full((d, d)),
            pl.BlockSpec((1, 1, d), lambda b, i: (b, 0, 0)),
        ],
        out_specs=pl.BlockSpec((1, tm, d), lambda b, i: (b, i, 0)),
        out_shape=jax.ShapeDtypeStruct((bsz, n, d), F32),
        compiler_params=_cparams(("parallel", "parallel")),
        name="merge",
    )(x, z, z, z, att, y5[0], y5[1], z, e, s5_d, w_glu, w_mla, w_s5, w_hy, w_out, gate)


def _ffn_kernel(x_ref, g_ref, sh_ref, sc_ref, gt_ref, wg_ref, wu_ref, wd_ref, fg_ref, o_ref, h_ref, acc_ref,
                *, nk, final):
    k = pl.program_id(2)

    @pl.when(k == 0)
    def _():
        h_ref[...] = _norm_mod(x_ref[0], g_ref[...], sh_ref[0], sc_ref[0]).astype(BF16)
        acc_ref[...] = jnp.zeros_like(acc_ref)

    h = h_ref[...]
    act = jax.nn.silu(_dot(h, wg_ref[...])) * _dot(h, wu_ref[...])
    acc_ref[...] += _dot(act.astype(BF16), wd_ref[...])

    @pl.when(k == nk - 1)
    def _():
        r = x_ref[0] + gt_ref[0] * acc_ref[...]
        o_ref[0] = _rms(r, fg_ref[...]) if final else r


def _ffn(x, g, shift, scale, gate, w_g, w_u, w_d, final_g, final):
    bsz, n, d = x.shape
    dff = w_g.shape[1]
    tm = min(n, 1024)
    tf = 256
    nk = dff // tf
    vec = pl.BlockSpec((1, 1, d), lambda b, i, k: (b, 0, 0))
    row = pl.BlockSpec((1, d), lambda b, i, k: (0, 0))
    return pl.pallas_call(
        functools.partial(_ffn_kernel, nk=nk, final=final),
        grid=(bsz, n // tm, nk),
        in_specs=[
            pl.BlockSpec((1, tm, d), lambda b, i, k: (b, i, 0)),
            row, vec, vec, vec,
            pl.BlockSpec((d, tf), lambda b, i, k: (0, k)),
            pl.BlockSpec((d, tf), lambda b, i, k: (0, k)),
            pl.BlockSpec((tf, d), lambda b, i, k: (k, 0)),
            row,
        ],
        out_specs=pl.BlockSpec((1, tm, d), lambda b, i, k: (b, i, 0)),
        out_shape=jax.ShapeDtypeStruct((bsz, n, d), F32),
        scratch_shapes=[pltpu.VMEM((tm, d), BF16), pltpu.VMEM((tm, d), F32)],
        compiler_params=_cparams(("parallel", "parallel", "arbitrary")),
        name="ffn",
    )(x, g, shift, scale, gate, w_g, w_u, w_d, final_g)


def _rope_tables(n_tok):
    rows = n_tok // GRID_W
    row = jnp.broadcast_to(jnp.arange(rows, dtype=F32)[:, None], (rows, GRID_W)).reshape(-1)
    col = jnp.broadcast_to(jnp.arange(GRID_W, dtype=F32)[None, :], (rows, GRID_W)).reshape(-1)
    n_freq = MLA_ROPE // 4
    inv = ROPE_BASE ** (-jnp.arange(n_freq, dtype=F32) / n_freq)
    ang = jnp.concatenate([row[:, None] * inv, col[:, None] * inv], axis=-1)
    cos, sin = jnp.cos(ang), jnp.sin(ang)
    pad = jnp.zeros((n_tok, HEAD_W - MLA_ROPE), F32)
    return jnp.concatenate([cos, cos, pad], axis=-1), jnp.concatenate([-sin, sin, pad], axis=-1)


def _identity_rope_tables(n_tok):
    one = jnp.ones((n_tok, MLA_ROPE), F32)
    pad = jnp.zeros((n_tok, HEAD_W - MLA_ROPE), F32)
    return jnp.concatenate([one, pad], axis=-1), jnp.zeros((n_tok, HEAD_W), F32)


def _layout_w_in(w):
    kr = w[:, OFF_KR:OFF_S5]
    x1, x2 = kr[:, 0::2], kr[:, 1::2]
    pad = jnp.zeros((w.shape[0], HEAD_W - 3 * (MLA_ROPE // 2)), w.dtype)
    return jnp.concatenate([w[:, OFF_GATE:], w[:, OFF_HY:OFF_GATE], w[:, OFF_S5:OFF_HY],
                            w[:, OFF_CQ:OFF_CKV], w[:, OFF_CKV:OFF_KR], x1, x2, x1, pad], axis=1).astype(BF16)


def _layout_w_uq(w):
    w = w.reshape(MLA_Q_LORA, MLA_HEADS, MLA_NOPE + MLA_ROPE)
    nope, rope = w[..., :MLA_NOPE], w[..., MLA_NOPE:]
    x1, x2 = rope[..., 0::2], rope[..., 1::2]
    z32 = jnp.zeros_like(rope)
    wq = jnp.concatenate([x1, x2, z32, nope], axis=-1)
    wqs = jnp.concatenate([x2, x1, z32, jnp.zeros_like(nope)], axis=-1)
    shape = (MLA_Q_LORA, MLA_HEADS * HEAD_W)
    return wq.reshape(shape).astype(BF16), wqs.reshape(shape).astype(BF16)


def _layout_w_ukv(w):
    w = w.reshape(MLA_KV_LORA, MLA_HEADS, MLA_NOPE + MLA_V)
    nope, val = w[..., :MLA_NOPE], w[..., MLA_NOPE:]
    wk = jnp.concatenate([jnp.zeros_like(nope), nope], axis=-1).reshape(MLA_KV_LORA, MLA_HEADS * HEAD_W)
    wvt = jnp.concatenate([val, jnp.zeros_like(val[..., :VT_ROWS - MLA_V])], axis=-1).reshape(
        MLA_KV_LORA, MLA_HEADS * VT_ROWS).T
    return wk.astype(BF16), wvt.astype(BF16)


def kernel(x, c, ctx, c_ctx, ada_w, ada_b, norm_mix, w_in, mla_q_norm, mla_w_uq, mla_kv_norm, mla_w_ukv,
           s5_lam_re, s5_lam_im, s5_log_dt, s5_b_re, s5_b_im, s5_c_re, s5_c_im, s5_d, s5_w_glu,
           hy_conv_w, hy_conv_b, hy_f_w1, hy_f_b1, hy_f_w2, hy_f_b2, hy_f_w3, hy_f_freq, hy_bias,
           w_branch_mla, w_branch_s5, w_branch_hy, w_out, norm_ffn, ffn_w_gu, ffn_w_down, final_norm):
    bsz, n_tok, d = x.shape
    n_ctx = ctx.shape[1]
    depth = ada_w.shape[0]
    rope_x = _rope_tables(n_tok)
    rope_c = _identity_rope_tables(n_ctx)
    h_zero = jnp.zeros((bsz, 2, 4, S5_HSTATE), F32)

    for i in range(depth):
        last = i == depth - 1
        mx = (jax.nn.silu(c) @ ada_w[i] + ada_b[i]).reshape(bsz, N_MOD, 1, d)
        mc = jnp.broadcast_to((jax.nn.silu(c_ctx) @ ada_w[i] + ada_b[i]).reshape(1, N_MOD, 1, d),
                              (bsz, N_MOD, 1, d))
        w_z = _layout_w_in(w_in[i])
        wq, wqs = _layout_w_uq(mla_w_uq[i])
        wk, wvt = _layout_w_ukv(mla_w_ukv[i])
        qn, kvn = mla_q_norm[i][None], mla_kv_norm[i][None]
        s5_tabs = _s5_tables(s5_lam_re[i], s5_lam_im[i], s5_log_dt[i], s5_b_re[i], s5_b_im[i],
                             s5_c_re[i], s5_c_im[i])
        fparams = (hy_f_w1[i], hy_f_b1[i], hy_f_w2[i], hy_f_b2[i], hy_f_w3[i], hy_f_freq[i])
        merge_w = (s5_d[i][None], s5_w_glu[i].astype(BF16),
                   w_branch_mla[i].astype(BF16),
                   w_branch_s5[i].astype(BF16), w_branch_hy[i].astype(BF16), w_out[i].astype(BF16))
        w_g = ffn_w_gu[i][:, :D_FF].astype(BF16)
        w_u = ffn_w_gu[i][:, D_FF:].astype(BF16)
        w_d = ffn_w_down[i].astype(BF16)
        g_mix, g_ffn = norm_mix[i][None], norm_ffn[i][None]

        zx = _proj_in(x, g_mix, mx[:, 0], mx[:, 1], w_z)
        zc = _proj_in(ctx, g_mix, mc[:, 0], mc[:, 1], w_z)

        q_c, k_c, vt_c = _mla_prep(zc, qn, kvn, wq, wqs, wk, wvt, *rope_c)
        q_x, k_x, vt_x = _mla_prep(zx, qn, kvn, wq, wqs, wk, wvt, *rope_x)
        a_x = _attention(q_x, jnp.concatenate([k_c, k_x], axis=2), jnp.concatenate([vt_c, vt_x], axis=3))

        *y5_c, finals = _s5_scan(zc, h_zero, s5_tabs)
        *y5_x, _ = _s5_scan(zx, finals, s5_tabs)

        v_x, g1_x, g2_x = _short_conv(zx, hy_conv_w[i], hy_conv_b[i][None])
        e_x = _hyena_mixer(v_x, (g1_x, g2_x), _hyena_filters(n_tok, *fparams), hy_bias[i])

        x = _merge(x, zx, a_x, y5_x, e_x, *merge_w, mx[:, 2])
        x = _ffn(x, g_ffn, mx[:, 3], mx[:, 4], mx[:, 5], w_g, w_u, w_d, final_norm[None], last)

        if not last:
            a_c = _attention(q_c, k_c, vt_c)
            v_c, g1_c, g2_c = _short_conv(zc, hy_conv_w[i], hy_conv_b[i][None])
            e_c = _hyena_mixer(v_c, (g1_c, g2_c), _hyena_filters(n_ctx, *fparams), hy_bias[i])
            ctx = _merge(ctx, zc, a_c, y5_c, e_c, *merge_w, mc[:, 2])
            ctx = _ffn(ctx, g_ffn, mc[:, 3], mc[:, 4], mc[:, 5], w_g, w_u, w_d, final_norm[None], False)
    return x
```

```python
import functools
import math

import jax
import jax.numpy as jnp
import numpy as np
from jax import lax
from jax.experimental import pallas as pl
from jax.experimental.pallas import tpu as pltpu

F32 = jnp.float32
BF16 = jnp.bfloat16
HIGHEST = lax.Precision.HIGHEST

D_MODEL = 1024
GRID_W = 64
NORM_EPS = 1e-6
N_MOD = 6

MLA_HEADS = 8
MLA_NOPE = 64
MLA_ROPE = 32
MLA_V = 64
MLA_Q_LORA = 256
MLA_KV_LORA = 128
ROPE_BASE = 10000.0
HEAD_W = 128
VT_ROWS = 80
QK_SCALE_LOG2 = (MLA_NOPE + MLA_ROPE) ** -0.5 * math.log2(math.e)
ATTN_SUB = 128
ATTN_LAG = 2
ATTN_UNROLL = 12

S5_WIDTH = 512
S5_GROUP = 16
S5_GROUPS = 32
S5_STATE = 64
S5_HALF = 256
S5_HSTATE = 1024
S5_CHUNK = 128
S5_CHUNKS_PER_STEP = 4

HY_WIDTH = 512
HY_ORDER = 2
HY_BANDS = 16
HY_POS_DIM = 1 + 2 * HY_BANDS
HY_POS_PAD = 64
HY_FILTER_HIDDEN = 64
HY_FILTER_OUT = HY_ORDER * 2 * HY_WIDTH
HY_DECAY_TARGET = 1e-2
HY_DECAY_SHORT = 0.3
HY_DECAY_LONG = 1.5
HY_DECAY_SHIFT = 0.05

D_FF = 2816

OFF_CQ = 0
OFF_CKV = OFF_CQ + MLA_Q_LORA
OFF_KR = OFF_CKV + MLA_KV_LORA
OFF_S5 = OFF_KR + MLA_ROPE
OFF_HY = OFF_S5 + S5_WIDTH
OFF_GATE = OFF_HY + 3 * HY_WIDTH

Z_GATE = 0
Z_HY = 3072
Z_S5 = 4608
Z_MLA = 5120
Z_WIDTH = 5632

VMEM_LIMIT_V7X = 52 * 1024 * 1024


def _cparams(sem, flags=None):
    return pltpu.CompilerParams(dimension_semantics=sem, vmem_limit_bytes=VMEM_LIMIT_V7X, flags=flags)


def _dot(a, b):
    return jnp.dot(a, b, preferred_element_type=F32)


def _rms(x, g):
    return x * lax.rsqrt(jnp.mean(x * x, axis=-1, keepdims=True) + NORM_EPS) * g


def _norm_mod(x, g, shift, scale):
    return _rms(x, g) * (1.0 + scale) + shift


def _proj_kernel(x_ref, g_ref, sh_ref, sc_ref, w_ref, o_ref, h_ref):
    @pl.when(pl.program_id(2) == 0)
    def _():
        h_ref[...] = _norm_mod(x_ref[0], g_ref[...], sh_ref[0], sc_ref[0]).astype(BF16)

    o_ref[0] = _dot(h_ref[...], w_ref[...]).astype(BF16)


def _proj_in(x, g, shift, scale, w):
    bsz, n, d = x.shape
    nz = w.shape[1]
    tm = min(n, 2048)
    tn = 512
    return pl.pallas_call(
        _proj_kernel,
        grid=(bsz, n // tm, nz // tn),
        in_specs=[
            pl.BlockSpec((1, tm, d), lambda b, i, j: (b, i, 0)),
            pl.BlockSpec((1, d), lambda b, i, j: (0, 0)),
            pl.BlockSpec((1, 1, d), lambda b, i, j: (b, 0, 0)),
            pl.BlockSpec((1, 1, d), lambda b, i, j: (b, 0, 0)),
            pl.BlockSpec((d, tn), lambda b, i, j: (0, j)),
        ],
        out_specs=pl.BlockSpec((1, tm, tn), lambda b, i, j: (b, i, j)),
        out_shape=jax.ShapeDtypeStruct((bsz, n, nz), BF16),
        scratch_shapes=[pltpu.VMEM((tm, d), BF16)],
        compiler_params=_cparams(("parallel", "parallel", "arbitrary")),
        name="proj_in",
    )(x, g, shift, scale, w)


def _mla_prep_kernel(z_ref, qn_ref, kvn_ref, wq_ref, wqs_ref, wk_ref, wvt_ref, c_ref, s_ref, q_ref, k_ref, vt_ref):
    z = z_ref[0].astype(F32)
    hq = _rms(z[:, :MLA_Q_LORA], qn_ref[...]).astype(BF16)
    hkv = _rms(z[:, MLA_Q_LORA:MLA_Q_LORA + MLA_KV_LORA], kvn_ref[...]).astype(BF16)
    krb = z[:, MLA_Q_LORA + MLA_KV_LORA:]
    qa = _dot(hq, wq_ref[...])
    qb = _dot(hq, wqs_ref[...])
    kn = _dot(hkv, wk_ref[...])
    vt = lax.dot_general(wvt_ref[...], hkv, (((1,), (1,)), ((), ())), preferred_element_type=F32)
    ck = c_ref[...]
    sn = s_ref[...]
    lane = lax.broadcasted_iota(jnp.int32, ck.shape, 1)
    cq = jnp.where(lane < MLA_ROPE, ck, 1.0)
    kr = krb * ck + pltpu.roll(krb, HEAD_W - MLA_ROPE // 2, 1) * sn
    row = lax.broadcasted_iota(jnp.int32, (VT_ROWS, z.shape[0]), 0)
    for h in range(MLA_HEADS):
        sl = slice(h * HEAD_W, (h + 1) * HEAD_W)
        q_ref[0, h] = ((qa[:, sl] * cq + qb[:, sl] * sn) * QK_SCALE_LOG2).astype(BF16)
        k_ref[0, h] = (kr + kn[:, sl]).astype(BF16)
        vt_ref[0, h] = jnp.where(row == MLA_V, 1.0, vt[h * VT_ROWS:(h + 1) * VT_ROWS, :]).astype(BF16)


def _mla_prep(z, qn, kvn, wq, wqs, wk, wvt, ctab, stab):
    bsz, n, _ = z.shape
    tm = min(n, 512)
    hw = MLA_HEADS * HEAD_W
    zb = Z_MLA // 512
    full = lambda shape: pl.BlockSpec(shape, lambda b, i: (0,) * len(shape))
    return pl.pallas_call(
        _mla_prep_kernel,
        grid=(bsz, n // tm),
        in_specs=[
            pl.BlockSpec((1, tm, 512), lambda b, i: (b, i, zb)),
            full((1, MLA_Q_LORA)),
            full((1, MLA_KV_LORA)),
            full((MLA_Q_LORA, hw)),
            full((MLA_Q_LORA, hw)),
            full((MLA_KV_LORA, hw)),
            full((MLA_HEADS * VT_ROWS, MLA_KV_LORA)),
            pl.BlockSpec((tm, HEAD_W), lambda b, i: (i, 0)),
            pl.BlockSpec((tm, HEAD_W), lambda b, i: (i, 0)),
        ],
        out_specs=[
            pl.BlockSpec((1, MLA_HEADS, tm, HEAD_W), lambda b, i: (b, 0, i, 0)),
            pl.BlockSpec((1, MLA_HEADS, tm, HEAD_W), lambda b, i: (b, 0, i, 0)),
            pl.BlockSpec((1, MLA_HEADS, VT_ROWS, tm), lambda b, i: (b, 0, 0, i)),
        ],
        out_shape=[
            jax.ShapeDtypeStruct((bsz, MLA_HEADS, n, HEAD_W), BF16),
            jax.ShapeDtypeStruct((bsz, MLA_HEADS, n, HEAD_W), BF16),
            jax.ShapeDtypeStruct((bsz, MLA_HEADS, VT_ROWS, n), BF16),
        ],
        compiler_params=_cparams(("parallel", "parallel")),
        name="mla_prep",
    )(z, qn, kvn, wq, wqs, wk, wvt, ctab, stab)


def _attn_kernel(q_ref, k_ref, vt_ref, o_ref, s0_ref, s1_ref, s2_ref, p0_ref, p1_ref, p2_ref, acc_ref, *, tk, nkc):
    q = q_ref[0, 0]
    tq = q.shape[0]
    s_refs = (s0_ref, s1_ref, s2_ref)
    p_refs = (p0_ref, p1_ref, p2_ref)

    ts = min(tk, ATTN_SUB)
    subs = [slice(r, r + ts) for r in range(0, tk, ts)]

    def scores_sub(j, s_ref, sub, cmax):
        off = pl.multiple_of(j * tk + sub.start, ts)
        s = lax.dot_general(k_ref[0, 0, pl.ds(off, ts), :], q, (((1,), (1,)), ((), ())),
                            preferred_element_type=F32)
        s_ref[sub, :] = s
        cm = jnp.max(s, axis=0, keepdims=True)
        return cm if cmax is None else jnp.maximum(cmax, cm)

    def softmax_sub(slot, sub, mn):
        p_refs[slot][sub, :] = jnp.exp2(s_refs[slot][sub, :] - mn).astype(BF16)

    def step(i, slot, carry, do_softmax, do_scores):
        cm, m, alpha = carry
        mn = jnp.maximum(m, cm)
        off = pl.multiple_of(i * tk, tk)
        acc_ref[...] = alpha * acc_ref[...] + _dot(vt_ref[0, 0, :, pl.ds(off, tk)], p_refs[slot][...])
        cm_new, cms = None, []
        for r, sub in enumerate(subs):
            mn_sub = mn
            if do_scores:
                cm_new = scores_sub(i + 2, s_refs[(slot + 2) % 3], sub, cm_new)
                cms.append(cm_new)
                if r >= ATTN_LAG:
                    mn_sub = jnp.maximum(mn, cms[r - ATTN_LAG] - 1e30)
            if do_softmax:
                softmax_sub((slot + 1) % 3, sub, mn_sub)
        if do_softmax:
            alpha, m = jnp.exp2(m - mn), mn
        return (cm_new if do_scores else cm), m, alpha

    m = jnp.full((1, tq), -1e30, F32)
    acc_ref[...] = jnp.zeros_like(acc_ref)
    cm, cm_next = None, None
    for sub in subs:
        cm = scores_sub(0, s0_ref, sub, cm)
        if nkc > 1:
            cm_next = scores_sub(1, s1_ref, sub, cm_next)
    mn = jnp.maximum(m, cm)
    for sub in subs:
        softmax_sub(0, sub, mn)
    carry = (cm_next if nkc > 1 else cm, mn, jnp.exp2(m - mn))

    def body(t, carry):
        for r in range(ATTN_UNROLL):
            carry = step(ATTN_UNROLL * t + r, r % 3, carry, True, True)
        return carry

    nloop = max(nkc - 2, 0) // ATTN_UNROLL
    carry = lax.fori_loop(0, nloop, body, carry)
    for i in range(ATTN_UNROLL * nloop, nkc):
        carry = step(i, i % 3, carry, i + 1 < nkc, i + 2 < nkc)
    acc = acc_ref[...]
    out = acc * (1.0 / acc[MLA_V:MLA_V + 1, :])
    out = jnp.concatenate([out, jnp.zeros((HEAD_W - VT_ROWS, tq), F32)], axis=0).T
    o_ref[0, 0] = out[:, :MLA_V].astype(BF16)


def _kv_chunk(nk):
    for tk in (640, 512, 256, 128):
        if nk % tk == 0:
            return tk
    raise ValueError(f"unsupported key count {nk}")


def _attention(q, k, vt):
    bsz, nh, nq, _ = q.shape
    nk = k.shape[2]
    tq = min(nq, 1024)
    tk = _kv_chunk(nk)
    return pl.pallas_call(
        functools.partial(_attn_kernel, tk=tk, nkc=nk // tk),
        grid=(bsz, nh, nq // tq),
        in_specs=[
            pl.BlockSpec((1, 1, tq, HEAD_W), lambda b, h, i: (b, h, i, 0)),
            pl.BlockSpec((1, 1, nk, HEAD_W), lambda b, h, i: (b, h, 0, 0)),
            pl.BlockSpec((1, 1, VT_ROWS, nk), lambda b, h, i: (b, h, 0, 0)),
        ],
        out_specs=pl.BlockSpec((1, 1, tq, MLA_V), lambda b, h, i: (b, h, i, 0)),
        out_shape=jax.ShapeDtypeStruct((bsz, nh, nq, MLA_V), BF16),
        scratch_shapes=[pltpu.VMEM((tk, tq), F32)] * 3 + [pltpu.VMEM((tk, tq), BF16)] * 3
        + [pltpu.VMEM((VT_ROWS, tq), F32)],
        compiler_params=_cparams(("parallel", "parallel", "arbitrary")),
        name="attention",
    )(q, k, vt)


def _cmul(ar, ai, br, bi):
    return ar * br - ai * bi, ar * bi + ai * br


def _s5_kernel(u_ref, h0_ref, bm_ref, cm_ref, wn_ref, wp_ref, l1_ref, tri_ref, y_ref, hf_ref,
               carry_ref, h0cat_ref, h1cat_ref, *, tc, nr, ns, reverse):
    i = pl.program_id(1)

    @pl.when(i == 0)
    def _():
        carry_ref[...] = h0_ref[0]

    u = u_ref[0]
    tri = tri_ref[...]
    order = range(nr - 1, -1, -1) if reverse else range(nr)
    edge = 0 if reverse else tc - 1
    ys = []
    for hf, hcat_ref in enumerate((h0cat_ref, h1cat_ref)):
        bu = _dot(u[:, hf * S5_HALF:(hf + 1) * S5_HALF], bm_ref[hf])
        hr_in, hi_in = carry_ref[2 * hf:2 * hf + 1, :], carry_ref[2 * hf + 1:2 * hf + 2, :]
        for c in order:
            rows = slice(c * tc, (c + 1) * tc)
            xr, xi = _cmul(wn_ref[hf, 0], wn_ref[hf, 1], bu[rows, :S5_HSTATE], bu[rows, S5_HSTATE:])
            s = _dot(tri, jnp.concatenate([xr, xi], axis=1).astype(BF16))
            cr, ci = _cmul(l1_ref[hf, 0], l1_ref[hf, 1], hr_in, hi_in)
            hr, hi = _cmul(wp_ref[hf, 0], wp_ref[hf, 1], s[:, :S5_HSTATE] + cr, s[:, S5_HSTATE:] + ci)
            hcat_ref[rows, :S5_HSTATE] = hr.astype(BF16)
            hcat_ref[rows, S5_HSTATE:] = hi.astype(BF16)
            hr_in, hi_in = hr[edge:edge + 1], hi[edge:edge + 1]
        carry_ref[2 * hf:2 * hf + 1, :] = hr_in
        carry_ref[2 * hf + 1:2 * hf + 2, :] = hi_in
        ys.append(_dot(hcat_ref[...], cm_ref[hf]))
    y_ref[0] = jnp.concatenate(ys, axis=1)

    @pl.when(i == ns - 1)
    def _():
        hf_ref[0] = carry_ref[...]


def _s5_scan_dir(z, h0, tabs, reverse):
    bm, cm, wn, wp, l1, tri = tabs
    bsz, n, _ = z.shape
    tc = S5_CHUNK
    nr = min(S5_CHUNKS_PER_STEP, n // tc)
    rows = nr * tc
    ns = n // rows
    zb = Z_S5 // S5_WIDTH
    blk = (lambda i: ns - 1 - i) if reverse else (lambda i: i)
    full = lambda shape: pl.BlockSpec(shape, lambda b, i: (0,) * len(shape))
    return pl.pallas_call(
        functools.partial(_s5_kernel, tc=tc, nr=nr, ns=ns, reverse=reverse),
        grid=(bsz, ns),
        in_specs=[
            pl.BlockSpec((1, rows, S5_WIDTH), lambda b, i: (b, blk(i), zb)),
            pl.BlockSpec((1, 4, S5_HSTATE), lambda b, i: (b, 0, 0)),
            full((2, S5_HALF, 2 * S5_HSTATE)),
            full((2, 2 * S5_HSTATE, S5_HALF)),
            full((2, 2, tc, S5_HSTATE)),
            full((2, 2, tc, S5_HSTATE)),
            full((2, 2, 1, S5_HSTATE)),
            full((tc, tc)),
        ],
        out_specs=[
            pl.BlockSpec((1, rows, S5_WIDTH), lambda b, i: (b, blk(i), 0)),
            pl.BlockSpec((1, 4, S5_HSTATE), lambda b, i: (b, 0, 0)),
        ],
        out_shape=[
            jax.ShapeDtypeStruct((bsz, n, S5_WIDTH), F32),
            jax.ShapeDtypeStruct((bsz, 4, S5_HSTATE), F32),
        ],
        scratch_shapes=[pltpu.VMEM((4, S5_HSTATE), F32),
                        pltpu.VMEM((rows, 2 * S5_HSTATE), BF16), pltpu.VMEM((rows, 2 * S5_HSTATE), BF16)],
        compiler_params=_cparams(("parallel", "arbitrary")),
        name="s5_scan",
    )(z, h0, bm, cm, wn, wp, l1, tri)


def _s5_scan(z, h0, tabs):
    outs = [_s5_scan_dir(z, h0[:, d], [t[d] for t in tabs], reverse=bool(d)) for d in range(2)]
    return outs[0][0], outs[1][0], jnp.stack([outs[0][1], outs[1][1]], axis=1)


def _s5_tables(lam_re, lam_im, log_dt, b_re, b_im, c_re, c_im):
    tc = S5_CHUNK
    dt = jnp.exp(log_dt)[..., None]
    zr, zi = lam_re * dt, lam_im * dt
    mag = jnp.exp(zr)
    lbr, lbi = mag * jnp.cos(zi), mag * jnp.sin(zi)
    den = lam_re * lam_re + lam_im * lam_im
    nr, ni = lbr - 1.0, lbi
    cfr = (nr * lam_re + ni * lam_im) / den
    cfi = (ni * lam_re - nr * lam_im) / den
    bbr = cfr[..., None] * b_re - cfi[..., None] * b_im
    bbi = cfr[..., None] * b_im + cfi[..., None] * b_re
    eye = jnp.eye(S5_GROUP, dtype=F32)

    def blockdiag_in(b):
        b = b.reshape(2, 2, S5_GROUP, S5_STATE, S5_GROUP)
        return jnp.einsum('gk,dhgpn->dhgnkp', eye, b).reshape(2, 2, S5_HALF, S5_HSTATE)

    def blockdiag_out(c):
        c = c.reshape(2, 2, S5_GROUP, S5_GROUP, S5_STATE)
        return jnp.einsum('gk,dhgnp->dhgpkn', eye, c).reshape(2, 2, S5_HSTATE, S5_HALF)

    bm = jnp.concatenate([blockdiag_in(bbr), blockdiag_in(bbi)], axis=-1).astype(BF16)
    cm = jnp.concatenate([blockdiag_out(c_re), blockdiag_out(-c_im)], axis=-2).astype(BF16)

    def powers(k):
        zr_ = zr.reshape(2, 2, 1, S5_HSTATE)
        zi_ = zi.reshape(2, 2, 1, S5_HSTATE)
        kk = k[:, None, :, None]
        m = jnp.exp(kk * zr_)
        return jnp.stack([m * jnp.cos(kk * zi_), m * jnp.sin(kk * zi_)], axis=2)

    t = jnp.arange(tc, dtype=F32)
    wn = powers(jnp.stack([-t, -(tc - 1 - t)]))
    wp = powers(jnp.stack([t, tc - 1 - t]))
    l1 = powers(jnp.ones((2, 1), F32))
    r = jnp.arange(tc)
    tri = jnp.stack([r[:, None] >= r[None, :], r[:, None] <= r[None, :]]).astype(BF16)
    return bm, cm, wn, wp, l1, tri


HALO = 16


def _sconv_kernel(z_ref, zp_ref, zn_ref, w_ref, b_ref, v_ref, g1_ref, g2_ref, *, nt):
    i = pl.program_id(1)
    z = z_ref[0].astype(F32)
    tm = z.shape[0]
    row = lax.broadcasted_iota(jnp.int32, z.shape, 0)
    prev = jnp.where(i > 0, zp_ref[0, HALO - 1:HALO, :].astype(F32), 0.0)
    nxt = jnp.where(i < nt - 1, zn_ref[0, 0:1, :].astype(F32), 0.0)
    up = jnp.where(row == 0, prev, pltpu.roll(z, 1, 0))
    dn = jnp.where(row == tm - 1, nxt, pltpu.roll(z, tm - 1, 0))
    u = up * w_ref[0:1, :] + z * w_ref[1:2, :] + dn * w_ref[2:3, :] + b_ref[...]
    for k, o_ref in enumerate((v_ref, g1_ref, g2_ref)):
        o_ref[0] = u[:, k * HY_WIDTH:(k + 1) * HY_WIDTH]


def _short_conv(z, w, b):
    bsz, n, _ = z.shape
    cw = 3 * HY_WIDTH
    tm = min(n, 512)
    nt = n // tm
    zb = Z_HY // cw
    rh = tm // HALO
    out = pl.BlockSpec((1, tm, HY_WIDTH), lambda b_, i: (b_, i, 0))
    return pl.pallas_call(
        functools.partial(_sconv_kernel, nt=nt),
        grid=(bsz, nt),
        in_specs=[
            pl.BlockSpec((1, tm, cw), lambda b_, i: (b_, i, zb)),
            pl.BlockSpec((1, HALO, cw), lambda b_, i: (b_, jnp.maximum(i * rh - 1, 0), zb)),
            pl.BlockSpec((1, HALO, cw), lambda b_, i: (b_, jnp.minimum((i + 1) * rh, n // HALO - 1), zb)),
            pl.BlockSpec((3, cw), lambda b_, i: (0, 0)),
            pl.BlockSpec((1, cw), lambda b_, i: (0, 0)),
        ],
        out_specs=[out] * 3,
        out_shape=[jax.ShapeDtypeStruct((bsz, n, HY_WIDTH), F32)] * 3,
        compiler_params=_cparams(("parallel", "parallel")),
        name="short_conv",
    )(z, z, z, w, b)


def _filt_kernel(f_ref, w1_ref, b1_ref, w2_ref, b2_ref, w3h_ref, w3l_ref, fq_ref, dl_ref, o_ref, *, n_tok):
    z = f_ref[...]
    tm = z.shape[0]
    fq = fq_ref[...]
    hid = jnp.sin(fq * (jnp.dot(z, w1_ref[...], precision=HIGHEST, preferred_element_type=F32) + b1_ref[...]))
    hid = jnp.sin(fq * (jnp.dot(hid, w2_ref[...], precision=HIGHEST, preferred_element_type=F32) + b2_ref[...]))
    hid_hi = hid.astype(BF16)
    hid_lo = (hid - hid_hi.astype(F32)).astype(BF16)
    filt = _dot(hid_hi, w3h_ref[...]) + (_dot(hid_hi, w3l_ref[...]) + _dot(hid_lo, w3h_ref[...]))
    filt = filt * (jnp.exp(-z[:, 0:1] * dl_ref[...]) + HY_DECAY_SHIFT)
    m = pl.program_id(0) * tm + lax.broadcasted_iota(jnp.int32, (tm, HY_WIDTH), 0)
    for o in range(HY_ORDER):
        base = o * 2 * HY_WIDTH
        fwd = filt[:, base:base + HY_WIDTH]
        bwd = filt[:, base + HY_WIDTH:base + 2 * HY_WIDTH]
        o_ref[o] = jnp.where(m < n_tok, fwd, 0.0) + jnp.where((m > n_tok) | (m == 0), bwd, 0.0)


def _hyena_filter_feats(n_tok):
    m = jnp.arange(2 * n_tok)
    lag = jnp.where(m < n_tok, m, jnp.where(m > n_tok, 2 * n_tok - m, 0))
    t = (lag.astype(F32) / (n_tok - 1))[:, None]
    w = (2.0 * math.pi * lag.astype(F32) / n_tok)[:, None]
    bands = jnp.linspace(1e-4, HY_BANDS - 1, HY_BANDS, dtype=F32)[None, :]
    feats = jnp.concatenate([t, jnp.cos(bands * w), -jnp.sin(bands * w)], axis=-1)
    return jnp.pad(feats, ((0, 0), (0, HY_POS_PAD - HY_POS_DIM)))


def _hyena_filters(n_tok, w1, b1, w2, b2, w3, freq):
    feats = _hyena_filter_feats(n_tok)
    deltas = jnp.abs(jnp.linspace(math.log(HY_DECAY_TARGET) / HY_DECAY_SHORT,
                                  math.log(HY_DECAY_TARGET) / HY_DECAY_LONG, HY_FILTER_OUT, dtype=F32))[None, :]
    w1p = jnp.pad(w1, ((0, HY_POS_PAD - HY_POS_DIM), (0, 0)))
    w3_hi = w3.astype(BF16)
    w3_lo = (w3 - w3_hi.astype(F32)).astype(BF16)
    n2 = 2 * n_tok
    tm = min(n2, 512)
    full = lambda shape: pl.BlockSpec(shape, lambda i: (0,) * len(shape))
    hh = HY_FILTER_HIDDEN
    return pl.pallas_call(
        functools.partial(_filt_kernel, n_tok=n_tok),
        grid=(n2 // tm,),
        in_specs=[
            pl.BlockSpec((tm, HY_POS_PAD), lambda i: (i, 0)),
            full((HY_POS_PAD, hh)), full((1, hh)), full((hh, hh)), full((1, hh)),
            full((hh, HY_FILTER_OUT)), full((hh, HY_FILTER_OUT)), full((1, hh)), full((1, HY_FILTER_OUT)),
        ],
        out_specs=pl.BlockSpec((HY_ORDER, tm, HY_WIDTH), lambda i: (0, i, 0)),
        out_shape=jax.ShapeDtypeStruct((HY_ORDER, n2, HY_WIDTH), F32),
        compiler_params=_cparams(("parallel",)),
        name="hyena_filter",
    )(feats, w1p, b1[None], w2, b2[None], w3_hi, w3_lo, freq[None], deltas)


def _fft_split(n_fft):
    if n_fft <= 1024:
        return n_fft, 1
    n1 = 1 << (int(math.log2(n_fft)) // 2)
    return n1, n_fft // n1


def _cis(num, den, sign):
    ang = (2.0 * math.pi / den) * num.astype(F32)
    return jnp.cos(ang), sign * jnp.sin(ang)


def _fft_tables(n_fft):
    n1, n2 = _fft_split(n_fft)
    a = jnp.arange(n1)
    f1r, f1i = _cis((a[:, None] * a[None, :]) % n1, n1, -1.0)
    s1 = jnp.concatenate([f1r, f1i], axis=0).astype(BF16)
    half = n1 // 2
    s3 = (jnp.concatenate([f1r[:half], -f1i[:half]], axis=0) / n_fft).astype(BF16)
    if n2 == 1:
        return s1, s3, None, None
    b = jnp.arange(n2)
    f2r, f2i = _cis((b[:, None] * b[None, :]) % n2, n2, -1.0)
    twr, twi = _cis((jnp.arange(n1)[:, None] * b[None, :]) % n_fft, n_fft, -1.0)
    fr, fi = _cmul(f2r[None], f2i[None], twr[:, None, :], twi[:, None, :])
    ft = jnp.concatenate([fr, fi], axis=1).astype(BF16)
    frt, fit = _cmul(f2r[None], f2i[None], twr[:, :, None], twi[:, :, None])
    it = jnp.concatenate([frt, -fit], axis=1).astype(BF16)
    return s1, s3, ft, it


def _stacked_cdot(s, xr, xi, m):
    p = _dot(s, xr.astype(BF16))
    if xi is None:
        return p[:m], p[m:]
    q = _dot(s, xi.astype(BF16))
    return p[:m] - q[m:], q[:m] + p[m:]


def _fft_a_kernel(*refs, parts, n1, has_g):
    if has_g:
        x_ref, s_ref, g_ref, y_ref = refs
    else:
        x_ref, s_ref, y_ref = refs
    yr, yi = _stacked_cdot(s_ref[...], x_ref[0], x_ref[1] if parts == 2 else None, n1)
    if has_g:
        yr, yi = _cmul(yr, yi, g_ref[0], g_ref[1])
    y_ref[0] = yr
    y_ref[1] = yi


def _fft_a(x, s1, g=None):
    parts, a_rows, cols = x.shape
    n1 = s1.shape[0] // 2
    tc = min(cols, 2048)
    in_specs = [
        pl.BlockSpec((parts, a_rows, tc), lambda j: (0, 0, j)),
        pl.BlockSpec((2 * n1, a_rows), lambda j: (0, 0)),
    ]
    args = [x, s1[:, :a_rows]]
    if g is not None:
        in_specs.append(pl.BlockSpec((2, n1, tc), lambda j: (0, 0, j)))
        args.append(g)
    return pl.pallas_call(
        functools.partial(_fft_a_kernel, parts=parts, n1=n1, has_g=g is not None),
        grid=(cols // tc,),
        in_specs=in_specs,
        out_specs=pl.BlockSpec((2, n1, tc), lambda j: (0, 0, j)),
        out_shape=jax.ShapeDtypeStruct((2, n1, cols), F32),
        compiler_params=_cparams(("parallel",)),
        name="fft_stage_a",
    )(*args)


def _fft_b_kernel(*refs, n2, inverse):
    if inverse:
        y_ref, ft_ref, it_ref, g_ref, o_ref = refs
    else:
        y_ref, ft_ref, o_ref = refs
    xr, xi = _stacked_cdot(ft_ref[0], y_ref[0, 0], y_ref[1, 0], n2)
    if inverse:
        zr, zi = _cmul(xr, xi, g_ref[0, 0], g_ref[1, 0])
        xr, xi = _stacked_cdot(it_ref[0], zr, zi, n2)
    o_ref[0, 0] = xr.astype(o_ref.dtype)
    o_ref[1, 0] = xi.astype(o_ref.dtype)


def _fft_b(y, ft, it=None, g=None):
    _, n1, n2, ch = y.shape
    inverse = it is not None
    blk = pl.BlockSpec((2, 1, n2, ch), lambda c: (0, c, 0, 0))
    mat = pl.BlockSpec((1, 2 * n2, n2), lambda c: (c, 0, 0))
    in_specs, args = [blk, mat], [y, ft]
    if inverse:
        in_specs += [mat, blk]
        args += [it, g]
    return pl.pallas_call(
        functools.partial(_fft_b_kernel, n2=n2, inverse=inverse),
        grid=(n1,),
        in_specs=in_specs,
        out_specs=blk,
        out_shape=jax.ShapeDtypeStruct(y.shape, BF16 if inverse else F32),
        compiler_params=_cparams(("parallel",)),
        name="fft_stage_b",
    )(*args)


def _fft_c_kernel(y_ref, s_ref, x_ref, g_ref, b_ref, o_ref, *, half):
    cr, ci = _stacked_cdot(s_ref[...], y_ref[0], y_ref[1], half)
    bias = b_ref[...]
    o_ref[0] = g_ref[0] * (cr + bias * x_ref[0])
    o_ref[1] = g_ref[1] * (ci + bias * x_ref[1])


def _fft_c(y, s3, x, gate, bias_cols):
    _, n1, cols = y.shape
    half = n1 // 2
    tc = min(cols, 2048)
    io = pl.BlockSpec((2, half, tc), lambda j: (0, 0, j))
    return pl.pallas_call(
        functools.partial(_fft_c_kernel, half=half),
        grid=(cols // tc,),
        in_specs=[
            pl.BlockSpec((2, n1, tc), lambda j: (0, 0, j)),
            pl.BlockSpec((n1, n1), lambda j: (0, 0)),
            io, io,
            pl.BlockSpec((1, tc), lambda j: (0, j)),
        ],
        out_specs=io,
        out_shape=jax.ShapeDtypeStruct((2, half, cols), F32),
        compiler_params=_cparams(("parallel",)),
        name="fft_stage_c",
    )(y, s3, x, gate, bias_cols)


FFT_BT = 16


def _to_fine_major(x):
    return pltpu.einshape("abc->bac", x)


def _fft_a4_kernel(x_ref, s_ref, y_ref, *, parts, n1):
    s = s_ref[...]
    xr = _to_fine_major(x_ref[0].astype(BF16))
    xi = _to_fine_major(x_ref[1].astype(BF16)) if parts == 2 else None
    out = [_stacked_cdot(s, xr[b], None if xi is None else xi[b], n1) for b in range(FFT_BT)]
    y_ref[0] = pltpu.einshape("bac->abc", jnp.stack([o[0] for o in out])).astype(BF16)
    y_ref[1] = pltpu.einshape("bac->abc", jnp.stack([o[1] for o in out])).astype(BF16)


def _fft_a4(x, s1):
    parts, a_rows, n2, ch = x.shape
    n1 = s1.shape[0] // 2
    return pl.pallas_call(
        functools.partial(_fft_a4_kernel, parts=parts, n1=n1),
        grid=(n2 // FFT_BT,),
        in_specs=[
            pl.BlockSpec((parts, a_rows, FFT_BT, ch), lambda j: (0, 0, j, 0)),
            pl.BlockSpec((2 * n1, a_rows), lambda j: (0, 0)),
        ],
        out_specs=pl.BlockSpec((2, n1, FFT_BT, ch), lambda j: (0, 0, j, 0)),
        out_shape=jax.ShapeDtypeStruct((2, n1, n2, ch), BF16),
        compiler_params=_cparams(("parallel",)),
        name="fft_stage_a",
    )(x, s1[:, :a_rows])


def _fft_c4_kernel(y_ref, s_ref, x_ref, g_ref, b_ref, o_ref, c_ref, *, half):
    s = s_ref[...]
    yr = _to_fine_major(y_ref[0].astype(BF16))
    yi = _to_fine_major(y_ref[1].astype(BF16))
    out = [_stacked_cdot(s, yr[b], yi[b], half) for b in range(FFT_BT)]
    c_ref[0] = pltpu.einshape("bac->abc", jnp.stack([o[0] for o in out]))
    c_ref[1] = pltpu.einshape("bac->abc", jnp.stack([o[1] for o in out]))
    o_ref[...] = g_ref[...] * (c_ref[...] + b_ref[...][None, None] * x_ref[...])


def _fft_c4(y, s3, x, gate, bias):
    _, n1, n2, ch = y.shape
    half = n1 // 2
    io = pl.BlockSpec((2, half, FFT_BT, ch), lambda j: (0, 0, j, 0))
    return pl.pallas_call(
        functools.partial(_fft_c4_kernel, half=half),
        grid=(n2 // FFT_BT,),
        in_specs=[
            pl.BlockSpec((2, n1, FFT_BT, ch), lambda j: (0, 0, j, 0)),
            pl.BlockSpec((n1, n1), lambda j: (0, 0)),
            io, io,
            pl.BlockSpec((1, ch), lambda j: (0, 0)),
        ],
        out_specs=io,
        out_shape=jax.ShapeDtypeStruct((2, half, n2, ch), F32),
        scratch_shapes=[pltpu.VMEM((2, half, FFT_BT, ch), F32)],
        compiler_params=_cparams(("parallel",)),
        name="fft_stage_c",
    )(y, s3, x, gate, bias)


def _hyena_mixer(v, gates, gfilt, bias):
    bsz, n, ch = v.shape
    assert bsz == 2, "the complex packing of the long convolution pairs exactly two batch rows"
    n_fft = 2 * n
    n1, n2 = _fft_split(n_fft)
    s1, s3, ft, it = _fft_tables(n_fft)
    if n2 == 1:
        y = v
        for o in range(HY_ORDER):
            gspec = _fft_a(gfilt[o][None], s1)
            y = _fft_c(_fft_a(y, s1, gspec), s3, y, gates[o], bias[o][None])
        return y
    half = n1 // 2
    y = v.reshape(2, half, n2, ch)
    for o in range(HY_ORDER):
        gspec = _fft_b(_fft_a4(gfilt[o].reshape(1, n1, n2, ch), s1), ft)
        spec = _fft_b(_fft_a4(y, s1), ft, it, gspec)
        y = _fft_c4(spec, s3, y, gates[o].reshape(2, half, n2, ch), bias[o][None])
    return y.reshape(2, n, ch)


def _gelu_tanh(x):
    return 0.5 * x * (1.0 + jnp.tanh(math.sqrt(2.0 / math.pi) * (x + 0.044715 * (x * x * x))))


def _merge_kernel(x_ref, g0_ref, g1_ref, g2_ref, a_ref, yf_ref, yb_ref, u_ref, e_ref, d_ref,
                  wglu_ref, wm_ref, ws_ref, wh_ref, wo_ref, gt_ref, o_ref):
    att = _dot(jnp.concatenate([a_ref[0, h] for h in range(MLA_HEADS)], axis=-1), wm_ref[...])
    y = yf_ref[0] + yb_ref[0] + d_ref[...] * u_ref[0].astype(F32)
    gy = _gelu_tanh(y)
    s5 = gy * jax.nn.sigmoid(_dot(gy.astype(BF16), wglu_ref[...]))
    s5 = _dot(s5.astype(BF16), ws_ref[...])
    hy = _dot(e_ref[0].astype(BF16), wh_ref[...])
    merged = jax.nn.sigmoid(g0_ref[0].astype(F32)) * att
    merged = merged + jax.nn.sigmoid(g1_ref[0].astype(F32)) * s5
    merged = merged + jax.nn.sigmoid(g2_ref[0].astype(F32)) * hy
    o_ref[0] = x_ref[0] + gt_ref[0] * _dot(merged.astype(BF16), wo_ref[...])


def _merge(x, z, att, y5, e, s5_d, w_glu, w_mla, w_s5, w_hy, w_out, gate):
    bsz, n, d = x.shape
    tm = min(n, 512)
    full = lambda shape: pl.BlockSpec(shape, lambda b, i: (0,) * len(shape))
    zs5 = Z_S5 // S5_WIDTH
    return pl.pallas_call(
        _merge_kernel,
        grid=(bsz, n // tm),
        in_specs=[
            pl.BlockSpec((1, tm, d), lambda b, i: (b, i, 0)),
            pl.BlockSpec((1, tm, d), lambda b, i: (b, i, 0)),
            pl.BlockSpec((1, tm, d), lambda b, i: (b, i, 1)),
            pl.BlockSpec((1, tm, d), lambda b, i: (b, i, 2)),
            pl.BlockSpec((1, MLA_HEADS, tm, MLA_V), lambda b, i: (b, 0, i, 0)),
            pl.BlockSpec((1, tm, S5_WIDTH), lambda b, i: (b, i, 0)),
            pl.BlockSpec((1, tm, S5_WIDTH), lambda b, i: (b, i, 0)),
            pl.BlockSpec((1, tm, S5_WIDTH), lambda b, i: (b, i, zs5)),
            pl.BlockSpec((1, tm, HY_WIDTH), lambda b, i: (b, i, 0)),
            full((1, S5_WIDTH)),
            full((S5_WIDTH, S5_WIDTH)),
            full((MLA_HEADS * MLA_V, d)),
            full((S5_WIDTH, d)),
            full((HY_WIDTH, d)),
            full((d, d)),
            pl.BlockSpec((1, 1, d), lambda b, i: (b, 0, 0)),
        ],
        out_specs=pl.BlockSpec((1, tm, d), lambda b, i: (b, i, 0)),
        out_shape=jax.ShapeDtypeStruct((bsz, n, d), F32),
        compiler_params=_cparams(("parallel", "parallel")),
        name="merge",
    )(x, z, z, z, att, y5[0], y5[1], z, e, s5_d, w_glu, w_mla, w_s5, w_hy, w_out, gate)


def _ffn_kernel(x_ref, g_ref, sh_ref, sc_ref, gt_ref, wg_ref, wu_ref, wd_ref, fg_ref, o_ref, h_ref, acc_ref,
                *, nk, final):
    k = pl.program_id(2)

    @pl.when(k == 0)
    def _():
        h_ref[...] = _norm_mod(x_ref[0], g_ref[...], sh_ref[0], sc_ref[0]).astype(BF16)
        acc_ref[...] = jnp.zeros_like(acc_ref)

    h = h_ref[...]
    act = jax.nn.silu(_dot(h, wg_ref[...])) * _dot(h, wu_ref[...])
    acc_ref[...] += _dot(act.astype(BF16), wd_ref[...])

    @pl.when(k == nk - 1)
    def _():
        r = x_ref[0] + gt_ref[0] * acc_ref[...]
        o_ref[0] = _rms(r, fg_ref[...]) if final else r


def _ffn(x, g, shift, scale, gate, w_g, w_u, w_d, final_g, final):
    bsz, n, d = x.shape
    dff = w_g.shape[1]
    tm = min(n, 1024)
    tf = 256
    nk = dff // tf
    vec = pl.BlockSpec((1, 1, d), lambda b, i, k: (b, 0, 0))
    row = pl.BlockSpec((1, d), lambda b, i, k: (0, 0))
    return pl.pallas_call(
        functools.partial(_ffn_kernel, nk=nk, final=final),
        grid=(bsz, n // tm, nk),
        in_specs=[
            pl.BlockSpec((1, tm, d), lambda b, i, k: (b, i, 0)),
            row, vec, vec, vec,
            pl.BlockSpec((d, tf), lambda b, i, k: (0, k)),
            pl.BlockSpec((d, tf), lambda b, i, k: (0, k)),
            pl.BlockSpec((tf, d), lambda b, i, k: (k, 0)),
            row,
        ],
        out_specs=pl.BlockSpec((1, tm, d), lambda b, i, k: (b, i, 0)),
        out_shape=jax.ShapeDtypeStruct((bsz, n, d), F32),
        scratch_shapes=[pltpu.VMEM((tm, d), BF16), pltpu.VMEM((tm, d), F32)],
        compiler_params=_cparams(("parallel", "parallel", "arbitrary")),
        name="ffn",
    )(x, g, shift, scale, gate, w_g, w_u, w_d, final_g)


def _rope_tables(n_tok):
    rows = n_tok // GRID_W
    row = jnp.broadcast_to(jnp.arange(rows, dtype=F32)[:, None], (rows, GRID_W)).reshape(-1)
    col = jnp.broadcast_to(jnp.arange(GRID_W, dtype=F32)[None, :], (rows, GRID_W)).reshape(-1)
    n_freq = MLA_ROPE // 4
    inv = ROPE_BASE ** (-jnp.arange(n_freq, dtype=F32) / n_freq)
    ang = jnp.concatenate([row[:, None] * inv, col[:, None] * inv], axis=-1)
    cos, sin = jnp.cos(ang), jnp.sin(ang)
    pad = jnp.zeros((n_tok, HEAD_W - MLA_ROPE), F32)
    return jnp.concatenate([cos, cos, pad], axis=-1), jnp.concatenate([-sin, sin, pad], axis=-1)


def _identity_rope_tables(n_tok):
    one = jnp.ones((n_tok, MLA_ROPE), F32)
    pad = jnp.zeros((n_tok, HEAD_W - MLA_ROPE), F32)
    return jnp.concatenate([one, pad], axis=-1), jnp.zeros((n_tok, HEAD_W), F32)


def _layout_w_in(w):
    kr = w[:, OFF_KR:OFF_S5]
    x1, x2 = kr[:, 0::2], kr[:, 1::2]
    pad = jnp.zeros((w.shape[0], HEAD_W - 3 * (MLA_ROPE // 2)), w.dtype)
    return jnp.concatenate([w[:, OFF_GATE:], w[:, OFF_HY:OFF_GATE], w[:, OFF_S5:OFF_HY],
                            w[:, OFF_CQ:OFF_CKV], w[:, OFF_CKV:OFF_KR], x1, x2, x1, pad], axis=1).astype(BF16)


def _layout_w_uq(w):
    w = w.reshape(MLA_Q_LORA, MLA_HEADS, MLA_NOPE + MLA_ROPE)
    nope, rope = w[..., :MLA_NOPE], w[..., MLA_NOPE:]
    x1, x2 = rope[..., 0::2], rope[..., 1::2]
    z32 = jnp.zeros_like(rope)
    wq = jnp.concatenate([x1, x2, z32, nope], axis=-1)
    wqs = jnp.concatenate([x2, x1, z32, jnp.zeros_like(nope)], axis=-1)
    shape = (MLA_Q_LORA, MLA_HEADS * HEAD_W)
    return wq.reshape(shape).astype(BF16), wqs.reshape(shape).astype(BF16)


def _layout_w_ukv(w):
    w = w.reshape(MLA_KV_LORA, MLA_HEADS, MLA_NOPE + MLA_V)
    nope, val = w[..., :MLA_NOPE], w[..., MLA_NOPE:]
    wk = jnp.concatenate([jnp.zeros_like(nope), nope], axis=-1).reshape(MLA_KV_LORA, MLA_HEADS * HEAD_W)
    wvt = jnp.concatenate([val, jnp.zeros_like(val[..., :VT_ROWS - MLA_V])], axis=-1).reshape(
        MLA_KV_LORA, MLA_HEADS * VT_ROWS).T
    return wk.astype(BF16), wvt.astype(BF16)


def kernel(x, c, ctx, c_ctx, ada_w, ada_b, norm_mix, w_in, mla_q_norm, mla_w_uq, mla_kv_norm, mla_w_ukv,
           s5_lam_re, s5_lam_im, s5_log_dt, s5_b_re, s5_b_im, s5_c_re, s5_c_im, s5_d, s5_w_glu,
           hy_conv_w, hy_conv_b, hy_f_w1, hy_f_b1, hy_f_w2, hy_f_b2, hy_f_w3, hy_f_freq, hy_bias,
           w_branch_mla, w_branch_s5, w_branch_hy, w_out, norm_ffn, ffn_w_gu, ffn_w_down, final_norm):
    bsz, n_tok, d = x.shape
    n_ctx = ctx.shape[1]
    depth = ada_w.shape[0]
    rope_x = _rope_tables(n_tok)
    rope_c = _identity_rope_tables(n_ctx)
    h_zero = jnp.zeros((bsz, 2, 4, S5_HSTATE), F32)

    for i in range(depth):
        last = i == depth - 1
        mx = (jax.nn.silu(c) @ ada_w[i] + ada_b[i]).reshape(bsz, N_MOD, 1, d)
        mc = jnp.broadcast_to((jax.nn.silu(c_ctx) @ ada_w[i] + ada_b[i]).reshape(1, N_MOD, 1, d),
                              (bsz, N_MOD, 1, d))
        w_z = _layout_w_in(w_in[i])
        wq, wqs = _layout_w_uq(mla_w_uq[i])
        wk, wvt = _layout_w_ukv(mla_w_ukv[i])
        qn, kvn = mla_q_norm[i][None], mla_kv_norm[i][None]
        s5_tabs = _s5_tables(s5_lam_re[i], s5_lam_im[i], s5_log_dt[i], s5_b_re[i], s5_b_im[i],
                             s5_c_re[i], s5_c_im[i])
        fparams = (hy_f_w1[i], hy_f_b1[i], hy_f_w2[i], hy_f_b2[i], hy_f_w3[i], hy_f_freq[i])
        merge_w = (s5_d[i][None], s5_w_glu[i].astype(BF16),
                   w_branch_mla[i].astype(BF16),
                   w_branch_s5[i].astype(BF16), w_branch_hy[i].astype(BF16), w_out[i].astype(BF16))
        w_g = ffn_w_gu[i][:, :D_FF].astype(BF16)
        w_u = ffn_w_gu[i][:, D_FF:].astype(BF16)
        w_d = ffn_w_down[i].astype(BF16)
        g_mix, g_ffn = norm_mix[i][None], norm_ffn[i][None]

        zx = _proj_in(x, g_mix, mx[:, 0], mx[:, 1], w_z)
        zc = _proj_in(ctx, g_mix, mc[:, 0], mc[:, 1], w_z)

        q_c, k_c, vt_c = _mla_prep(zc, qn, kvn, wq, wqs, wk, wvt, *rope_c)
        q_x, k_x, vt_x = _mla_prep(zx, qn, kvn, wq, wqs, wk, wvt, *rope_x)
        a_x = _attention(q_x, jnp.concatenate([k_c, k_x], axis=2), jnp.concatenate([vt_c, vt_x], axis=3))

        *y5_c, finals = _s5_scan(zc, h_zero, s5_tabs)
        *y5_x, _ = _s5_scan(zx, finals, s5_tabs)

        v_x, g1_x, g2_x = _short_conv(zx, hy_conv_w[i], hy_conv_b[i][None])
        e_x = _hyena_mixer(v_x, (g1_x, g2_x), _hyena_filters(n_tok, *fparams), hy_bias[i])

        x = _merge(x, zx, a_x, y5_x, e_x, *merge_w, mx[:, 2])
        x = _ffn(x, g_ffn, mx[:, 3], mx[:, 4], mx[:, 5], w_g, w_u, w_d, final_norm[None], last)

        if not last:
            a_c = _attention(q_c, k_c, vt_c)
            v_c, g1_c, g2_c = _short_conv(zc, hy_conv_w[i], hy_conv_b[i][None])
            e_c = _hyena_mixer(v_c, (g1_c, g2_c), _hyena_filters(n_ctx, *fparams), hy_bias[i])
            ctx = _merge(ctx, zc, a_c, y5_c, e_c, *merge_w, mc[:, 2])
            ctx = _ffn(ctx, g_ffn, mc[:, 3], mc[:, 4], mc[:, 5], w_g, w_u, w_d, final_norm[None], False)
    return x
```

```python
import functools
import math

import jax
import jax.numpy as jnp
import numpy as np
from jax import lax
from jax.experimental import pallas as pl
from jax.experimental.pallas import tpu as pltpu

F32 = jnp.float32
BF16 = jnp.bfloat16
HIGHEST = lax.Precision.HIGHEST

D_MODEL = 1024
GRID_W = 64
NORM_EPS = 1e-6
N_MOD = 6

MLA_HEADS = 8
MLA_NOPE = 64
MLA_ROPE = 32
MLA_V = 64
MLA_Q_LORA = 256
MLA_KV_LORA = 128
ROPE_BASE = 10000.0
HEAD_W = 128
VT_ROWS = 80
QK_SCALE_LOG2 = (MLA_NOPE + MLA_ROPE) ** -0.5 * math.log2(math.e)
ATTN_SUB = 128
ATTN_LAG = 2
ATTN_UNROLL = 12

S5_WIDTH = 512
S5_GROUP = 16
S5_GROUPS = 32
S5_STATE = 64
S5_HALF = 256
S5_HSTATE = 1024
S5_CHUNK = 128
S5_CHUNKS_PER_STEP = 4

HY_WIDTH = 512
HY_ORDER = 2
HY_BANDS = 16
HY_POS_DIM = 1 + 2 * HY_BANDS
HY_POS_PAD = 64
HY_FILTER_HIDDEN = 64
HY_FILTER_OUT = HY_ORDER * 2 * HY_WIDTH
HY_DECAY_TARGET = 1e-2
HY_DECAY_SHORT = 0.3
HY_DECAY_LONG = 1.5
HY_DECAY_SHIFT = 0.05

D_FF = 2816

OFF_CQ = 0
OFF_CKV = OFF_CQ + MLA_Q_LORA
OFF_KR = OFF_CKV + MLA_KV_LORA
OFF_S5 = OFF_KR + MLA_ROPE
OFF_HY = OFF_S5 + S5_WIDTH
OFF_GATE = OFF_HY + 3 * HY_WIDTH

Z_GATE = 0
Z_HY = 3072
Z_S5 = 4608
Z_MLA = 5120
Z_WIDTH = 5632

VMEM_LIMIT_V7X = 52 * 1024 * 1024


def _cparams(sem, flags=None):
    return pltpu.CompilerParams(dimension_semantics=sem, vmem_limit_bytes=VMEM_LIMIT_V7X, flags=flags)


def _dot(a, b):
    return jnp.dot(a, b, preferred_element_type=F32)


def _rms(x, g):
    return x * lax.rsqrt(jnp.mean(x * x, axis=-1, keepdims=True) + NORM_EPS) * g


def _norm_mod(x, g, shift, scale):
    return _rms(x, g) * (1.0 + scale) + shift


def _proj_kernel(x_ref, g_ref, sh_ref, sc_ref, w_ref, o_ref, h_ref):
    @pl.when(pl.program_id(2) == 0)
    def _():
        h_ref[...] = _norm_mod(x_ref[0], g_ref[...], sh_ref[0], sc_ref[0]).astype(BF16)

    o_ref[0] = _dot(h_ref[...], w_ref[...]).astype(BF16)


def _proj_in(x, g, shift, scale, w):
    bsz, n, d = x.shape
    nz = w.shape[1]
    tm = min(n, 2048)
    tn = 512
    return pl.pallas_call(
        _proj_kernel,
        grid=(bsz, n // tm, nz // tn),
        in_specs=[
            pl.BlockSpec((1, tm, d), lambda b, i, j: (b, i, 0)),
            pl.BlockSpec((1, d), lambda b, i, j: (0, 0)),
            pl.BlockSpec((1, 1, d), lambda b, i, j: (b, 0, 0)),
            pl.BlockSpec((1, 1, d), lambda b, i, j: (b, 0, 0)),
            pl.BlockSpec((d, tn), lambda b, i, j: (0, j)),
        ],
        out_specs=pl.BlockSpec((1, tm, tn), lambda b, i, j: (b, i, j)),
        out_shape=jax.ShapeDtypeStruct((bsz, n, nz), BF16),
        scratch_shapes=[pltpu.VMEM((tm, d), BF16)],
        compiler_params=_cparams(("parallel", "parallel", "arbitrary")),
        name="proj_in",
    )(x, g, shift, scale, w)


def _mla_prep_kernel(z_ref, qn_ref, kvn_ref, wq_ref, wqs_ref, wk_ref, wvt_ref, c_ref, s_ref, q_ref, k_ref, vt_ref):
    z = z_ref[0].astype(F32)
    hq = _rms(z[:, :MLA_Q_LORA], qn_ref[...]).astype(BF16)
    hkv = _rms(z[:, MLA_Q_LORA:MLA_Q_LORA + MLA_KV_LORA], kvn_ref[...]).astype(BF16)
    krb = z[:, MLA_Q_LORA + MLA_KV_LORA:]
    qa = _dot(hq, wq_ref[...])
    qb = _dot(hq, wqs_ref[...])
    kn = _dot(hkv, wk_ref[...])
    vt = lax.dot_general(wvt_ref[...], hkv, (((1,), (1,)), ((), ())), preferred_element_type=F32)
    ck = c_ref[...]
    sn = s_ref[...]
    lane = lax.broadcasted_iota(jnp.int32, ck.shape, 1)
    cq = jnp.where(lane < MLA_ROPE, ck, 1.0)
    kr = krb * ck + pltpu.roll(krb, HEAD_W - MLA_ROPE // 2, 1) * sn
    row = lax.broadcasted_iota(jnp.int32, (VT_ROWS, z.shape[0]), 0)
    for h in range(MLA_HEADS):
        sl = slice(h * HEAD_W, (h + 1) * HEAD_W)
        q_ref[0, h] = ((qa[:, sl] * cq + qb[:, sl] * sn) * QK_SCALE_LOG2).astype(BF16)
        k_ref[0, h] = (kr + kn[:, sl]).astype(BF16)
        vt_ref[0, h] = jnp.where(row == MLA_V, 1.0, vt[h * VT_ROWS:(h + 1) * VT_ROWS, :]).astype(BF16)


def _mla_prep(z, qn, kvn, wq, wqs, wk, wvt, ctab, stab):
    bsz, n, _ = z.shape
    tm = min(n, 512)
    hw = MLA_HEADS * HEAD_W
    zb = Z_MLA // 512
    full = lambda shape: pl.BlockSpec(shape, lambda b, i: (0,) * len(shape))
    return pl.pallas_call(
        _mla_prep_kernel,
        grid=(bsz, n // tm),
        in_specs=[
            pl.BlockSpec((1, tm, 512), lambda b, i: (b, i, zb)),
            full((1, MLA_Q_LORA)),
            full((1, MLA_KV_LORA)),
            full((MLA_Q_LORA, hw)),
            full((MLA_Q_LORA, hw)),
            full((MLA_KV_LORA, hw)),
            full((MLA_HEADS * VT_ROWS, MLA_KV_LORA)),
            pl.BlockSpec((tm, HEAD_W), lambda b, i: (i, 0)),
            pl.BlockSpec((tm, HEAD_W), lambda b, i: (i, 0)),
        ],
        out_specs=[
            pl.BlockSpec((1, MLA_HEADS, tm, HEAD_W), lambda b, i: (b, 0, i, 0)),
            pl.BlockSpec((1, MLA_HEADS, tm, HEAD_W), lambda b, i: (b, 0, i, 0)),
            pl.BlockSpec((1, MLA_HEADS, VT_ROWS, tm), lambda b, i: (b, 0, 0, i)),
        ],
        out_shape=[
            jax.ShapeDtypeStruct((bsz, MLA_HEADS, n, HEAD_W), BF16),
            jax.ShapeDtypeStruct((bsz, MLA_HEADS, n, HEAD_W), BF16),
            jax.ShapeDtypeStruct((bsz, MLA_HEADS, VT_ROWS, n), BF16),
        ],
        compiler_params=_cparams(("parallel", "parallel")),
        name="mla_prep",
    )(z, qn, kvn, wq, wqs, wk, wvt, ctab, stab)


def _attn_kernel(q_ref, k_ref, vt_ref, o_ref, s0_ref, s1_ref, s2_ref, p0_ref, p1_ref, p2_ref, acc_ref, *, tk, nkc):
    q = q_ref[0, 0]
    tq = q.shape[0]
    s_refs = (s0_ref, s1_ref, s2_ref)
    p_refs = (p0_ref, p1_ref, p2_ref)

    ts = min(tk, ATTN_SUB)
    subs = [slice(r, r + ts) for r in range(0, tk, ts)]

    def scores_sub(j, s_ref, sub, cmax):
        off = pl.multiple_of(j * tk + sub.start, ts)
        s = lax.dot_general(k_ref[0, 0, pl.ds(off, ts), :], q, (((1,), (1,)), ((), ())),
                            preferred_element_type=F32)
        s_ref[sub, :] = s
        cm = jnp.max(s, axis=0, keepdims=True)
        return cm if cmax is None else jnp.maximum(cmax, cm)

    def softmax_sub(slot, sub, mn):
        p_refs[slot][sub, :] = jnp.exp2(s_refs[slot][sub, :] - mn).astype(BF16)

    def step(i, slot, carry, do_softmax, do_scores):
        cm, m, alpha = carry
        mn = jnp.maximum(m, cm)
        off = pl.multiple_of(i * tk, tk)
        acc_ref[...] = alpha * acc_ref[...] + _dot(vt_ref[0, 0, :, pl.ds(off, tk)], p_refs[slot][...])
        cm_new, cms = None, []
        for r, sub in enumerate(subs):
            mn_sub = mn
            if do_scores:
                cm_new = scores_sub(i + 2, s_refs[(slot + 2) % 3], sub, cm_new)
                cms.append(cm_new)
                if r >= ATTN_LAG:
                    mn_sub = jnp.maximum(mn, cms[r - ATTN_LAG] - 1e30)
            if do_softmax:
                softmax_sub((slot + 1) % 3, sub, mn_sub)
        if do_softmax:
            alpha, m = jnp.exp2(m - mn), mn
        return (cm_new if do_scores else cm), m, alpha

    m = jnp.full((1, tq), -1e30, F32)
    acc_ref[...] = jnp.zeros_like(acc_ref)
    cm, cm_next = None, None
    for sub in subs:
        cm = scores_sub(0, s0_ref, sub, cm)
        if nkc > 1:
            cm_next = scores_sub(1, s1_ref, sub, cm_next)
    mn = jnp.maximum(m, cm)
    for sub in subs:
        softmax_sub(0, sub, mn)
    carry = (cm_next if nkc > 1 else cm, mn, jnp.exp2(m - mn))

    def body(t, carry):
        for r in range(ATTN_UNROLL):
            carry = step(ATTN_UNROLL * t + r, r % 3, carry, True, True)
        return carry

    nloop = max(nkc - 2, 0) // ATTN_UNROLL
    carry = lax.fori_loop(0, nloop, body, carry)
    for i in range(ATTN_UNROLL * nloop, nkc):
        carry = step(i, i % 3, carry, i + 1 < nkc, i + 2 < nkc)
    acc = acc_ref[...]
    out = acc * (1.0 / acc[MLA_V:MLA_V + 1, :])
    out = jnp.concatenate([out, jnp.zeros((HEAD_W - VT_ROWS, tq), F32)], axis=0).T
    o_ref[0, 0] = out[:, :MLA_V].astype(BF16)


def _kv_chunk(nk):
    for tk in (640, 512, 256, 128):
        if nk % tk == 0:
            return tk
    raise ValueError(f"unsupported key count {nk}")


def _attention(q, k, vt):
    bsz, nh, nq, _ = q.shape
    nk = k.shape[2]
    tq = min(nq, 1024)
    tk = _kv_chunk(nk)
    return pl.pallas_call(
        functools.partial(_attn_kernel, tk=tk, nkc=nk // tk),
        grid=(bsz, nh, nq // tq),
        in_specs=[
            pl.BlockSpec((1, 1, tq, HEAD_W), lambda b, h, i: (b, h, i, 0)),
            pl.BlockSpec((1, 1, nk, HEAD_W), lambda b, h, i: (b, h, 0, 0)),
            pl.BlockSpec((1, 1, VT_ROWS, nk), lambda b, h, i: (b, h, 0, 0)),
        ],
        out_specs=pl.BlockSpec((1, 1, tq, MLA_V), lambda b, h, i: (b, h, i, 0)),
        out_shape=jax.ShapeDtypeStruct((bsz, nh, nq, MLA_V), BF16),
        scratch_shapes=[pltpu.VMEM((tk, tq), F32)] * 3 + [pltpu.VMEM((tk, tq), BF16)] * 3
        + [pltpu.VMEM((VT_ROWS, tq), F32)],
        compiler_params=_cparams(("parallel", "parallel", "arbitrary")),
        name="attention",
    )(q, k, vt)


def _cmul(ar, ai, br, bi):
    return ar * br - ai * bi, ar * bi + ai * br


def _s5_kernel(u_ref, h0_ref, bm_ref, cm_ref, wn_ref, wp_ref, l1_ref, tri_ref, y_ref, hf_ref,
               carry_ref, h0cat_ref, h1cat_ref, *, tc, nr, ns, reverse):
    i = pl.program_id(1)

    @pl.when(i == 0)
    def _():
        carry_ref[...] = h0_ref[0]

    u = u_ref[0]
    tri = tri_ref[...]
    order = range(nr - 1, -1, -1) if reverse else range(nr)
    edge = 0 if reverse else tc - 1
    ys = []
    for hf, hcat_ref in enumerate((h0cat_ref, h1cat_ref)):
        bu = _dot(u[:, hf * S5_HALF:(hf + 1) * S5_HALF], bm_ref[hf])
        hr_in, hi_in = carry_ref[2 * hf:2 * hf + 1, :], carry_ref[2 * hf + 1:2 * hf + 2, :]
        for c in order:
            rows = slice(c * tc, (c + 1) * tc)
            xr, xi = _cmul(wn_ref[hf, 0], wn_ref[hf, 1], bu[rows, :S5_HSTATE], bu[rows, S5_HSTATE:])
            s = _dot(tri, jnp.concatenate([xr, xi], axis=1).astype(BF16))
            cr, ci = _cmul(l1_ref[hf, 0], l1_ref[hf, 1], hr_in, hi_in)
            hr, hi = _cmul(wp_ref[hf, 0], wp_ref[hf, 1], s[:, :S5_HSTATE] + cr, s[:, S5_HSTATE:] + ci)
            hcat_ref[rows, :S5_HSTATE] = hr.astype(BF16)
            hcat_ref[rows, S5_HSTATE:] = hi.astype(BF16)
            hr_in, hi_in = hr[edge:edge + 1], hi[edge:edge + 1]
        carry_ref[2 * hf:2 * hf + 1, :] = hr_in
        carry_ref[2 * hf + 1:2 * hf + 2, :] = hi_in
        ys.append(_dot(hcat_ref[...], cm_ref[hf]))
    y_ref[0] = jnp.concatenate(ys, axis=1)

    @pl.when(i == ns - 1)
    def _():
        hf_ref[0] = carry_ref[...]


def _s5_scan_dir(z, h0, tabs, reverse):
    bm, cm, wn, wp, l1, tri = tabs
    bsz, n, _ = z.shape
    tc = S5_CHUNK
    nr = min(S5_CHUNKS_PER_STEP, n // tc)
    rows = nr * tc
    ns = n // rows
    zb = Z_S5 // S5_WIDTH
    blk = (lambda i: ns - 1 - i) if reverse else (lambda i: i)
    full = lambda shape: pl.BlockSpec(shape, lambda b, i: (0,) * len(shape))
    return pl.pallas_call(
        functools.partial(_s5_kernel, tc=tc, nr=nr, ns=ns, reverse=reverse),
        grid=(bsz, ns),
        in_specs=[
            pl.BlockSpec((1, rows, S5_WIDTH), lambda b, i: (b, blk(i), zb)),
            pl.BlockSpec((1, 4, S5_HSTATE), lambda b, i: (b, 0, 0)),
            full((2, S5_HALF, 2 * S5_HSTATE)),
            full((2, 2 * S5_HSTATE, S5_HALF)),
            full((2, 2, tc, S5_HSTATE)),
            full((2, 2, tc, S5_HSTATE)),
            full((2, 2, 1, S5_HSTATE)),
            full((tc, tc)),
        ],
        out_specs=[
            pl.BlockSpec((1, rows, S5_WIDTH), lambda b, i: (b, blk(i), 0)),
            pl.BlockSpec((1, 4, S5_HSTATE), lambda b, i: (b, 0, 0)),
        ],
        out_shape=[
            jax.ShapeDtypeStruct((bsz, n, S5_WIDTH), F32),
            jax.ShapeDtypeStruct((bsz, 4, S5_HSTATE), F32),
        ],
        scratch_shapes=[pltpu.VMEM((4, S5_HSTATE), F32),
                        pltpu.VMEM((rows, 2 * S5_HSTATE), BF16), pltpu.VMEM((rows, 2 * S5_HSTATE), BF16)],
        compiler_params=_cparams(("parallel", "arbitrary")),
        name="s5_scan",
    )(z, h0, bm, cm, wn, wp, l1, tri)


def _s5_scan(z, h0, tabs):
    outs = [_s5_scan_dir(z, h0[:, d], [t[d] for t in tabs], reverse=bool(d)) for d in range(2)]
    return outs[0][0], outs[1][0], jnp.stack([outs[0][1], outs[1][1]], axis=1)


def _s5_tables(lam_re, lam_im, log_dt, b_re, b_im, c_re, c_im):
    tc = S5_CHUNK
    dt = jnp.exp(log_dt)[..., None]
    zr, zi = lam_re * dt, lam_im * dt
    mag = jnp.exp(zr)
    lbr, lbi = mag * jnp.cos(zi), mag * jnp.sin(zi)
    den = lam_re * lam_re + lam_im * lam_im
    nr, ni = lbr - 1.0, lbi
    cfr = (nr * lam_re + ni * lam_im) / den
    cfi = (ni * lam_re - nr * lam_im) / den
    bbr = cfr[..., None] * b_re - cfi[..., None] * b_im
    bbi = cfr[..., None] * b_im + cfi[..., None] * b_re
    eye = jnp.eye(S5_GROUP, dtype=F32)

    def blockdiag_in(b):
        b = b.reshape(2, 2, S5_GROUP, S5_STATE, S5_GROUP)
        return jnp.einsum('gk,dhgpn->dhgnkp', eye, b).reshape(2, 2, S5_HALF, S5_HSTATE)

    def blockdiag_out(c):
        c = c.reshape(2, 2, S5_GROUP, S5_GROUP, S5_STATE)
        return jnp.einsum('gk,dhgnp->dhgpkn', eye, c).reshape(2, 2, S5_HSTATE, S5_HALF)

    bm = jnp.concatenate([blockdiag_in(bbr), blockdiag_in(bbi)], axis=-1).astype(BF16)
    cm = jnp.concatenate([blockdiag_out(c_re), blockdiag_out(-c_im)], axis=-2).astype(BF16)

    def powers(k):
        zr_ = zr.reshape(2, 2, 1, S5_HSTATE)
        zi_ = zi.reshape(2, 2, 1, S5_HSTATE)
        kk = k[:, None, :, None]
        m = jnp.exp(kk * zr_)
        return jnp.stack([m * jnp.cos(kk * zi_), m * jnp.sin(kk * zi_)], axis=2)

    t = jnp.arange(tc, dtype=F32)
    wn = powers(jnp.stack([-t, -(tc - 1 - t)]))
    wp = powers(jnp.stack([t, tc - 1 - t]))
    l1 = powers(jnp.ones((2, 1), F32))
    r = jnp.arange(tc)
    tri = jnp.stack([r[:, None] >= r[None, :], r[:, None] <= r[None, :]]).astype(BF16)
    return bm, cm, wn, wp, l1, tri


HALO = 16


def _sconv_kernel(z_ref, zp_ref, zn_ref, w_ref, b_ref, v_ref, g1_ref, g2_ref, *, nt):
    i = pl.program_id(1)
    z = z_ref[0].astype(F32)
    tm = z.shape[0]
    row = lax.broadcasted_iota(jnp.int32, z.shape, 0)
    prev = jnp.where(i > 0, zp_ref[0, HALO - 1:HALO, :].astype(F32), 0.0)
    nxt = jnp.where(i < nt - 1, zn_ref[0, 0:1, :].astype(F32), 0.0)
    up = jnp.where(row == 0, prev, pltpu.roll(z, 1, 0))
    dn = jnp.where(row == tm - 1, nxt, pltpu.roll(z, tm - 1, 0))
    u = up * w_ref[0:1, :] + z * w_ref[1:2, :] + dn * w_ref[2:3, :] + b_ref[...]
    for k, o_ref in enumerate((v_ref, g1_ref, g2_ref)):
        o_ref[0] = u[:, k * HY_WIDTH:(k + 1) * HY_WIDTH]


def _short_conv(z, w, b):
    bsz, n, _ = z.shape
    cw = 3 * HY_WIDTH
    tm = min(n, 512)
    nt = n // tm
    zb = Z_HY // cw
    rh = tm // HALO
    out = pl.BlockSpec((1, tm, HY_WIDTH), lambda b_, i: (b_, i, 0))
    return pl.pallas_call(
        functools.partial(_sconv_kernel, nt=nt),
        grid=(bsz, nt),
        in_specs=[
            pl.BlockSpec((1, tm, cw), lambda b_, i: (b_, i, zb)),
            pl.BlockSpec((1, HALO, cw), lambda b_, i: (b_, jnp.maximum(i * rh - 1, 0), zb)),
            pl.BlockSpec((1, HALO, cw), lambda b_, i: (b_, jnp.minimum((i + 1) * rh, n // HALO - 1), zb)),
            pl.BlockSpec((3, cw), lambda b_, i: (0, 0)),
            pl.BlockSpec((1, cw), lambda b_, i: (0, 0)),
        ],
        out_specs=[out] * 3,
        out_shape=[jax.ShapeDtypeStruct((bsz, n, HY_WIDTH), F32)] * 3,
        compiler_params=_cparams(("parallel", "parallel")),
        name="short_conv",
    )(z, z, z, w, b)


def _filt_kernel(f_ref, w1_ref, b1_ref, w2_ref, b2_ref, w3h_ref, w3l_ref, fq_ref, dl_ref, o_ref, *, n_tok):
    z = f_ref[...]
    tm = z.shape[0]
    fq = fq_ref[...]
    hid = jnp.sin(fq * (jnp.dot(z, w1_ref[...], precision=HIGHEST, preferred_element_type=F32) + b1_ref[...]))
    hid = jnp.sin(fq * (jnp.dot(hid, w2_ref[...], precision=HIGHEST, preferred_element_type=F32) + b2_ref[...]))
    hid_hi = hid.astype(BF16)
    hid_lo = (hid - hid_hi.astype(F32)).astype(BF16)

    def filters(rows, d):
        wh, wl = w3h_ref[d], w3l_ref[d]
        f = _dot(hid_hi[rows], wh) + (_dot(hid_hi[rows], wl) + _dot(hid_lo[rows], wh))
        return f * (jnp.exp(-z[rows, 0:1] * dl_ref[d]) + HY_DECAY_SHIFT)

    i = pl.program_id(0)
    filt = filters(slice(None), jnp.where(i * tm >= n_tok, 1, 0))
    m = i * tm + lax.broadcasted_iota(jnp.int32, filt.shape, 0)
    filt = jnp.where(m == n_tok, 0.0, filt)
    for o in range(HY_ORDER):
        o_ref[o] = filt[:, o * HY_WIDTH:(o + 1) * HY_WIDTH]

    @pl.when(i == 0)
    def _():
        head = slice(0, HALO)
        lag0 = filters(head, 1)
        lag0 = jnp.where(lax.broadcasted_iota(jnp.int32, lag0.shape, 0) == 0, lag0, 0.0)
        for o in range(HY_ORDER):
            o_ref[o, head, :] += lag0[:, o * HY_WIDTH:(o + 1) * HY_WIDTH]


def _hyena_filter_feats(n_tok):
    m = jnp.arange(2 * n_tok)
    lag = jnp.where(m < n_tok, m, jnp.where(m > n_tok, 2 * n_tok - m, 0))
    t = (lag.astype(F32) / (n_tok - 1))[:, None]
    w = (2.0 * math.pi * lag.astype(F32) / n_tok)[:, None]
    bands = jnp.linspace(1e-4, HY_BANDS - 1, HY_BANDS, dtype=F32)[None, :]
    feats = jnp.concatenate([t, jnp.cos(bands * w), -jnp.sin(bands * w)], axis=-1)
    return jnp.pad(feats, ((0, 0), (0, HY_POS_PAD - HY_POS_DIM)))


def _hyena_filters(n_tok, w1, b1, w2, b2, w3, freq):
    feats = _hyena_filter_feats(n_tok)
    deltas = jnp.abs(jnp.linspace(math.log(HY_DECAY_TARGET) / HY_DECAY_SHORT,
                                  math.log(HY_DECAY_TARGET) / HY_DECAY_LONG, HY_FILTER_OUT, dtype=F32))[None, :]
    w1p = jnp.pad(w1, ((0, HY_POS_PAD - HY_POS_DIM), (0, 0)))

    def by_direction(a):
        a = a.reshape(a.shape[:-1] + (HY_ORDER, 2, HY_WIDTH))
        return jnp.moveaxis(a, -2, 0).reshape((2,) + a.shape[:-3] + (HY_ORDER * HY_WIDTH,))

    w3, deltas = by_direction(w3), by_direction(deltas)
    w3_hi = w3.astype(BF16)
    w3_lo = (w3 - w3_hi.astype(F32)).astype(BF16)
    n2 = 2 * n_tok
    tm = min(n_tok, 512)
    full = lambda shape: pl.BlockSpec(shape, lambda i: (0,) * len(shape))
    hh = HY_FILTER_HIDDEN
    wd = HY_ORDER * HY_WIDTH
    return pl.pallas_call(
        functools.partial(_filt_kernel, n_tok=n_tok),
        grid=(n2 // tm,),
        in_specs=[
            pl.BlockSpec((tm, HY_POS_PAD), lambda i: (i, 0)),
            full((HY_POS_PAD, hh)), full((1, hh)), full((hh, hh)), full((1, hh)),
            full((2, hh, wd)), full((2, hh, wd)), full((1, hh)), full((2, 1, wd)),
        ],
        out_specs=pl.BlockSpec((HY_ORDER, tm, HY_WIDTH), lambda i: (0, i, 0)),
        out_shape=jax.ShapeDtypeStruct((HY_ORDER, n2, HY_WIDTH), F32),
        compiler_params=_cparams(("parallel",)),
        name="hyena_filter",
    )(feats, w1p, b1[None], w2, b2[None], w3_hi, w3_lo, freq[None], deltas)


def _fft_split(n_fft):
    if n_fft <= 1024:
        return n_fft, 1
    n1 = 1 << (int(math.log2(n_fft)) // 2)
    return n1, n_fft // n1


def _cis(num, den, sign):
    ang = (2.0 * math.pi / den) * num.astype(F32)
    return jnp.cos(ang), sign * jnp.sin(ang)


def _fft_tables(n_fft):
    n1, n2 = _fft_split(n_fft)
    a = jnp.arange(n1)
    f1r, f1i = _cis((a[:, None] * a[None, :]) % n1, n1, -1.0)
    s1 = jnp.concatenate([f1r, f1i], axis=0).astype(BF16)
    half = n1 // 2
    s3 = (jnp.concatenate([f1r[:half], -f1i[:half]], axis=0) / n_fft).astype(BF16)
    if n2 == 1:
        return s1, s3, None, None
    b = jnp.arange(n2)
    f2r, f2i = _cis((b[:, None] * b[None, :]) % n2, n2, -1.0)
    twr, twi = _cis((jnp.arange(n1)[:, None] * b[None, :]) % n_fft, n_fft, -1.0)
    fr, fi = _cmul(f2r[None], f2i[None], twr[:, None, :], twi[:, None, :])
    ft = jnp.concatenate([fr, fi], axis=1).astype(BF16)
    frt, fit = _cmul(f2r[None], f2i[None], twr[:, :, None], twi[:, :, None])
    it = jnp.concatenate([frt, -fit], axis=1).astype(BF16)
    return s1, s3, ft, it


def _stacked_cdot(s, xr, xi, m):
    p = _dot(s, xr.astype(BF16))
    if xi is None:
        return p[:m], p[m:]
    q = _dot(s, xi.astype(BF16))
    return p[:m] - q[m:], q[:m] + p[m:]


def _fft_a_kernel(*refs, parts, n1, has_g):
    if has_g:
        x_ref, s_ref, g_ref, y_ref = refs
    else:
        x_ref, s_ref, y_ref = refs
    yr, yi = _stacked_cdot(s_ref[...], x_ref[0], x_ref[1] if parts == 2 else None, n1)
    if has_g:
        yr, yi = _cmul(yr, yi, g_ref[0], g_ref[1])
    y_ref[0] = yr
    y_ref[1] = yi


def _fft_a(x, s1, g=None):
    parts, a_rows, cols = x.shape
    n1 = s1.shape[0] // 2
    tc = min(cols, 2048)
    in_specs = [
        pl.BlockSpec((parts, a_rows, tc), lambda j: (0, 0, j)),
        pl.BlockSpec((2 * n1, a_rows), lambda j: (0, 0)),
    ]
    args = [x, s1[:, :a_rows]]
    if g is not None:
        in_specs.append(pl.BlockSpec((2, n1, tc), lambda j: (0, 0, j)))
        args.append(g)
    return pl.pallas_call(
        functools.partial(_fft_a_kernel, parts=parts, n1=n1, has_g=g is not None),
        grid=(cols // tc,),
        in_specs=in_specs,
        out_specs=pl.BlockSpec((2, n1, tc), lambda j: (0, 0, j)),
        out_shape=jax.ShapeDtypeStruct((2, n1, cols), F32),
        compiler_params=_cparams(("parallel",)),
        name="fft_stage_a",
    )(*args)


def _fft_b_kernel(*refs, n2, inverse):
    if inverse:
        y_ref, ft_ref, it_ref, g_ref, o_ref = refs
    else:
        y_ref, ft_ref, o_ref = refs
    xr, xi = _stacked_cdot(ft_ref[0], y_ref[0, 0], y_ref[1, 0], n2)
    if inverse:
        zr, zi = _cmul(xr, xi, g_ref[0, 0], g_ref[1, 0])
        xr, xi = _stacked_cdot(it_ref[0], zr, zi, n2)
    o_ref[0, 0] = xr.astype(o_ref.dtype)
    o_ref[1, 0] = xi.astype(o_ref.dtype)


def _fft_b(y, ft, it=None, g=None):
    _, n1, n2, ch = y.shape
    inverse = it is not None
    blk = pl.BlockSpec((2, 1, n2, ch), lambda c: (0, c, 0, 0))
    mat = pl.BlockSpec((1, 2 * n2, n2), lambda c: (c, 0, 0))
    in_specs, args = [blk, mat], [y, ft]
    if inverse:
        in_specs += [mat, blk]
        args += [it, g]
    return pl.pallas_call(
        functools.partial(_fft_b_kernel, n2=n2, inverse=inverse),
        grid=(n1,),
        in_specs=in_specs,
        out_specs=blk,
        out_shape=jax.ShapeDtypeStruct(y.shape, BF16 if inverse else F32),
        compiler_params=_cparams(("parallel",)),
        name="fft_stage_b",
    )(*args)


def _fft_c_kernel(y_ref, s_ref, x_ref, g_ref, b_ref, o_ref, *, half):
    cr, ci = _stacked_cdot(s_ref[...], y_ref[0], y_ref[1], half)
    bias = b_ref[...]
    o_ref[0] = g_ref[0] * (cr + bias * x_ref[0])
    o_ref[1] = g_ref[1] * (ci + bias * x_ref[1])


def _fft_c(y, s3, x, gate, bias_cols):
    _, n1, cols = y.shape
    half = n1 // 2
    tc = min(cols, 2048)
    io = pl.BlockSpec((2, half, tc), lambda j: (0, 0, j))
    return pl.pallas_call(
        functools.partial(_fft_c_kernel, half=half),
        grid=(cols // tc,),
        in_specs=[
            pl.BlockSpec((2, n1, tc), lambda j: (0, 0, j)),
            pl.BlockSpec((n1, n1), lambda j: (0, 0)),
            io, io,
            pl.BlockSpec((1, tc), lambda j: (0, j)),
        ],
        out_specs=io,
        out_shape=jax.ShapeDtypeStruct((2, half, cols), F32),
        compiler_params=_cparams(("parallel",)),
        name="fft_stage_c",
    )(y, s3, x, gate, bias_cols)


FFT_BT = 16


def _to_fine_major(x):
    return pltpu.einshape("abc->bac", x)


def _fft_a4_kernel(x_ref, s_ref, y_ref, *, parts, n1):
    s = s_ref[...]
    xr = _to_fine_major(x_ref[0].astype(BF16))
    xi = _to_fine_major(x_ref[1].astype(BF16)) if parts == 2 else None
    out = [_stacked_cdot(s, xr[b], None if xi is None else xi[b], n1) for b in range(FFT_BT)]
    y_ref[0] = pltpu.einshape("bac->abc", jnp.stack([o[0] for o in out])).astype(BF16)
    y_ref[1] = pltpu.einshape("bac->abc", jnp.stack([o[1] for o in out])).astype(BF16)


def _fft_a4(x, s1):
    parts, a_rows, n2, ch = x.shape
    n1 = s1.shape[0] // 2
    return pl.pallas_call(
        functools.partial(_fft_a4_kernel, parts=parts, n1=n1),
        grid=(n2 // FFT_BT,),
        in_specs=[
            pl.BlockSpec((parts, a_rows, FFT_BT, ch), lambda j: (0, 0, j, 0)),
            pl.BlockSpec((2 * n1, a_rows), lambda j: (0, 0)),
        ],
        out_specs=pl.BlockSpec((2, n1, FFT_BT, ch), lambda j: (0, 0, j, 0)),
        out_shape=jax.ShapeDtypeStruct((2, n1, n2, ch), BF16),
        compiler_params=_cparams(("parallel",)),
        name="fft_stage_a",
    )(x, s1[:, :a_rows])


def _fft_c4_kernel(y_ref, s_ref, x_ref, g_ref, b_ref, o_ref, c_ref, *, half):
    s = s_ref[...]
    yr = _to_fine_major(y_ref[0].astype(BF16))
    yi = _to_fine_major(y_ref[1].astype(BF16))
    out = [_stacked_cdot(s, yr[b], yi[b], half) for b in range(FFT_BT)]
    c_ref[0] = pltpu.einshape("bac->abc", jnp.stack([o[0] for o in out]))
    c_ref[1] = pltpu.einshape("bac->abc", jnp.stack([o[1] for o in out]))
    o_ref[...] = g_ref[...] * (c_ref[...] + b_ref[...][None, None] * x_ref[...])


def _fft_c4(y, s3, x, gate, bias):
    _, n1, n2, ch = y.shape
    half = n1 // 2
    io = pl.BlockSpec((2, half, FFT_BT, ch), lambda j: (0, 0, j, 0))
    return pl.pallas_call(
        functools.partial(_fft_c4_kernel, half=half),
        grid=(n2 // FFT_BT,),
        in_specs=[
            pl.BlockSpec((2, n1, FFT_BT, ch), lambda j: (0, 0, j, 0)),
            pl.BlockSpec((n1, n1), lambda j: (0, 0)),
            io, io,
            pl.BlockSpec((1, ch), lambda j: (0, 0)),
        ],
        out_specs=io,
        out_shape=jax.ShapeDtypeStruct((2, half, n2, ch), F32),
        scratch_shapes=[pltpu.VMEM((2, half, FFT_BT, ch), F32)],
        compiler_params=_cparams(("parallel",)),
        name="fft_stage_c",
    )(y, s3, x, gate, bias)


def _hyena_mixer(v, gates, gfilt, bias):
    bsz, n, ch = v.shape
    assert bsz == 2, "the complex packing of the long convolution pairs exactly two batch rows"
    n_fft = 2 * n
    n1, n2 = _fft_split(n_fft)
    s1, s3, ft, it = _fft_tables(n_fft)
    if n2 == 1:
        y = v
        for o in range(HY_ORDER):
            gspec = _fft_a(gfilt[o][None], s1)
            y = _fft_c(_fft_a(y, s1, gspec), s3, y, gates[o], bias[o][None])
        return y
    half = n1 // 2
    y = v.reshape(2, half, n2, ch)
    for o in range(HY_ORDER):
        gspec = _fft_b(_fft_a4(gfilt[o].reshape(1, n1, n2, ch), s1), ft)
        spec = _fft_b(_fft_a4(y, s1), ft, it, gspec)
        y = _fft_c4(spec, s3, y, gates[o].reshape(2, half, n2, ch), bias[o][None])
    return y.reshape(2, n, ch)


def _gelu_tanh(x):
    return 0.5 * x * (1.0 + jnp.tanh(math.sqrt(2.0 / math.pi) * (x + 0.044715 * (x * x * x))))


def _merge_kernel(x_ref, g0_ref, g1_ref, g2_ref, a_ref, yf_ref, yb_ref, u_ref, e_ref, d_ref,
                  wglu_ref, wm_ref, ws_ref, wh_ref, wo_ref, gt_ref, o_ref):
    att = _dot(jnp.concatenate([a_ref[0, h] for h in range(MLA_HEADS)], axis=-1), wm_ref[...])
    y = yf_ref[0] + yb_ref[0] + d_ref[...] * u_ref[0].astype(F32)
    gy = _gelu_tanh(y)
    s5 = gy * jax.nn.sigmoid(_dot(gy.astype(BF16), wglu_ref[...]))
    s5 = _dot(s5.astype(BF16), ws_ref[...])
    hy = _dot(e_ref[0].astype(BF16), wh_ref[...])
    merged = jax.nn.sigmoid(g0_ref[0].astype(F32)) * att
    merged = merged + jax.nn.sigmoid(g1_ref[0].astype(F32)) * s5
    merged = merged + jax.nn.sigmoid(g2_ref[0].astype(F32)) * hy
    o_ref[0] = x_ref[0] + gt_ref[0] * _dot(merged.astype(BF16), wo_ref[...])


def _merge(x, z, att, y5, e, s5_d, w_glu, w_mla, w_s5, w_hy, w_out, gate):
    bsz, n, d = x.shape
    tm = min(n, 512)
    full = lambda shape: pl.BlockSpec(shape, lambda b, i: (0,) * len(shape))
    zs5 = Z_S5 // S5_WIDTH
    return pl.pallas_call(
        _merge_kernel,
        grid=(bsz, n // tm),
        in_specs=[
            pl.BlockSpec((1, tm, d), lambda b, i: (b, i, 0)),
            pl.BlockSpec((1, tm, d), lambda b, i: (b, i, 0)),
            pl.BlockSpec((1, tm, d), lambda b, i: (b, i, 1)),
            pl.BlockSpec((1, tm, d), lambda b, i: (b, i, 2)),
            pl.BlockSpec((1, MLA_HEADS, tm, MLA_V), lambda b, i: (b, 0, i, 0)),
            pl.BlockSpec((1, tm, S5_WIDTH), lambda b, i: (b, i, 0)),
            pl.BlockSpec((1, tm, S5_WIDTH), lambda b, i: (b, i, 0)),
            pl.BlockSpec((1, tm, S5_WIDTH), lambda b, i: (b, i, zs5)),
            pl.BlockSpec((1, tm, HY_WIDTH), lambda b, i: (b, i, 0)),
            full((1, S5_WIDTH)),
            full((S5_WIDTH, S5_WIDTH)),
            full((MLA_HEADS * MLA_V, d)),
            full((S5_WIDTH, d)),
            full((HY_WIDTH, d)),
            full((d, d)),
            pl.BlockSpec((1, 1, d), lambda b, i: (b, 0, 0)),
        ],
        out_specs=pl.BlockSpec((1, tm, d), lambda b, i: (b, i, 0)),
        out_shape=jax.ShapeDtypeStruct((bsz, n, d), F32),
        compiler_params=_cparams(("parallel", "parallel")),
        name="merge",
    )(x, z, z, z, att, y5[0], y5[1], z, e, s5_d, w_glu, w_mla, w_s5, w_hy, w_out, gate)


def _ffn_kernel(x_ref, g_ref, sh_ref, sc_ref, gt_ref, wg_ref, wu_ref, wd_ref, fg_ref, o_ref, h_ref, acc_ref,
                *, nk, final):
    k = pl.program_id(2)

    @pl.when(k == 0)
    def _():
        h_ref[...] = _norm_mod(x_ref[0], g_ref[...], sh_ref[0], sc_ref[0]).astype(BF16)
        acc_ref[...] = jnp.zeros_like(acc_ref)

    h = h_ref[...]
    act = jax.nn.silu(_dot(h, wg_ref[...])) * _dot(h, wu_ref[...])
    acc_ref[...] += _dot(act.astype(BF16), wd_ref[...])

    @pl.when(k == nk - 1)
    def _():
        r = x_ref[0] + gt_ref[0] * acc_ref[...]
        o_ref[0] = _rms(r, fg_ref[...]) if final else r


def _ffn(x, g, shift, scale, gate, w_g, w_u, w_d, final_g, final):
    bsz, n, d = x.shape
    dff = w_g.shape[1]
    tm = min(n, 1024)
    tf = 256
    nk = dff // tf
    vec = pl.BlockSpec((1, 1, d), lambda b, i, k: (b, 0, 0))
    row = pl.BlockSpec((1, d), lambda b, i, k: (0, 0))
    return pl.pallas_call(
        functools.partial(_ffn_kernel, nk=nk, final=final),
        grid=(bsz, n // tm, nk),
        in_specs=[
            pl.BlockSpec((1, tm, d), lambda b, i, k: (b, i, 0)),
            row, vec, vec, vec,
            pl.BlockSpec((d, tf), lambda b, i, k: (0, k)),
            pl.BlockSpec((d, tf), lambda b, i, k: (0, k)),
            pl.BlockSpec((tf, d), lambda b, i, k: (k, 0)),
            row,
        ],
        out_specs=pl.BlockSpec((1, tm, d), lambda b, i, k: (b, i, 0)),
        out_shape=jax.ShapeDtypeStruct((bsz, n, d), F32),
        scratch_shapes=[pltpu.VMEM((tm, d), BF16), pltpu.VMEM((tm, d), F32)],
        compiler_params=_cparams(("parallel", "parallel", "arbitrary")),
        name="ffn",
    )(x, g, shift, scale, gate, w_g, w_u, w_d, final_g)


def _rope_tables(n_tok):
    rows = n_tok // GRID_W
    row = jnp.broadcast_to(jnp.arange(rows, dtype=F32)[:, None], (rows, GRID_W)).reshape(-1)
    col = jnp.broadcast_to(jnp.arange(GRID_W, dtype=F32)[None, :], (rows, GRID_W)).reshape(-1)
    n_freq = MLA_ROPE // 4
    inv = ROPE_BASE ** (-jnp.arange(n_freq, dtype=F32) / n_freq)
    ang = jnp.concatenate([row[:, None] * inv, col[:, None] * inv], axis=-1)
    cos, sin = jnp.cos(ang), jnp.sin(ang)
    pad = jnp.zeros((n_tok, HEAD_W - MLA_ROPE), F32)
    return jnp.concatenate([cos, cos, pad], axis=-1), jnp.concatenate([-sin, sin, pad], axis=-1)


def _identity_rope_tables(n_tok):
    one = jnp.ones((n_tok, MLA_ROPE), F32)
    pad = jnp.zeros((n_tok, HEAD_W - MLA_ROPE), F32)
    return jnp.concatenate([one, pad], axis=-1), jnp.zeros((n_tok, HEAD_W), F32)


def _layout_w_in(w):
    kr = w[:, OFF_KR:OFF_S5]
    x1, x2 = kr[:, 0::2], kr[:, 1::2]
    pad = jnp.zeros((w.shape[0], HEAD_W - 3 * (MLA_ROPE // 2)), w.dtype)
    return jnp.concatenate([w[:, OFF_GATE:], w[:, OFF_HY:OFF_GATE], w[:, OFF_S5:OFF_HY],
                            w[:, OFF_CQ:OFF_CKV], w[:, OFF_CKV:OFF_KR], x1, x2, x1, pad], axis=1).astype(BF16)


def _layout_w_uq(w):
    w = w.reshape(MLA_Q_LORA, MLA_HEADS, MLA_NOPE + MLA_ROPE)
    nope, rope = w[..., :MLA_NOPE], w[..., MLA_NOPE:]
    x1, x2 = rope[..., 0::2], rope[..., 1::2]
    z32 = jnp.zeros_like(rope)
    wq = jnp.concatenate([x1, x2, z32, nope], axis=-1)
    wqs = jnp.concatenate([x2, x1, z32, jnp.zeros_like(nope)], axis=-1)
    shape = (MLA_Q_LORA, MLA_HEADS * HEAD_W)
    return wq.reshape(shape).astype(BF16), wqs.reshape(shape).astype(BF16)


def _layout_w_ukv(w):
    w = w.reshape(MLA_KV_LORA, MLA_HEADS, MLA_NOPE + MLA_V)
    nope, val = w[..., :MLA_NOPE], w[..., MLA_NOPE:]
    wk = jnp.concatenate([jnp.zeros_like(nope), nope], axis=-1).reshape(MLA_KV_LORA, MLA_HEADS * HEAD_W)
    wvt = jnp.concatenate([val, jnp.zeros_like(val[..., :VT_ROWS - MLA_V])], axis=-1).reshape(
        MLA_KV_LORA, MLA_HEADS * VT_ROWS).T
    return wk.astype(BF16), wvt.astype(BF16)


def kernel(x, c, ctx, c_ctx, ada_w, ada_b, norm_mix, w_in, mla_q_norm, mla_w_uq, mla_kv_norm, mla_w_ukv,
           s5_lam_re, s5_lam_im, s5_log_dt, s5_b_re, s5_b_im, s5_c_re, s5_c_im, s5_d, s5_w_glu,
           hy_conv_w, hy_conv_b, hy_f_w1, hy_f_b1, hy_f_w2, hy_f_b2, hy_f_w3, hy_f_freq, hy_bias,
           w_branch_mla, w_branch_s5, w_branch_hy, w_out, norm_ffn, ffn_w_gu, ffn_w_down, final_norm):
    bsz, n_tok, d = x.shape
    n_ctx = ctx.shape[1]
    depth = ada_w.shape[0]
    rope_x = _rope_tables(n_tok)
    rope_c = _identity_rope_tables(n_ctx)
    h_zero = jnp.zeros((bsz, 2, 4, S5_HSTATE), F32)

    for i in range(depth):
        last = i == depth - 1
        mx = (jax.nn.silu(c) @ ada_w[i] + ada_b[i]).reshape(bsz, N_MOD, 1, d)
        mc = jnp.broadcast_to((jax.nn.silu(c_ctx) @ ada_w[i] + ada_b[i]).reshape(1, N_MOD, 1, d),
                              (bsz, N_MOD, 1, d))
        w_z = _layout_w_in(w_in[i])
        wq, wqs = _layout_w_uq(mla_w_uq[i])
        wk, wvt = _layout_w_ukv(mla_w_ukv[i])
        qn, kvn = mla_q_norm[i][None], mla_kv_norm[i][None]
        s5_tabs = _s5_tables(s5_lam_re[i], s5_lam_im[i], s5_log_dt[i], s5_b_re[i], s5_b_im[i],
                             s5_c_re[i], s5_c_im[i])
        fparams = (hy_f_w1[i], hy_f_b1[i], hy_f_w2[i], hy_f_b2[i], hy_f_w3[i], hy_f_freq[i])
        merge_w = (s5_d[i][None], s5_w_glu[i].astype(BF16),
                   w_branch_mla[i].astype(BF16),
                   w_branch_s5[i].astype(BF16), w_branch_hy[i].astype(BF16), w_out[i].astype(BF16))
        w_g = ffn_w_gu[i][:, :D_FF].astype(BF16)
        w_u = ffn_w_gu[i][:, D_FF:].astype(BF16)
        w_d = ffn_w_down[i].astype(BF16)
        g_mix, g_ffn = norm_mix[i][None], norm_ffn[i][None]

        zx = _proj_in(x, g_mix, mx[:, 0], mx[:, 1], w_z)
        zc = _proj_in(ctx, g_mix, mc[:, 0], mc[:, 1], w_z)

        q_c, k_c, vt_c = _mla_prep(zc, qn, kvn, wq, wqs, wk, wvt, *rope_c)
        q_x, k_x, vt_x = _mla_prep(zx, qn, kvn, wq, wqs, wk, wvt, *rope_x)
        a_x = _attention(q_x, jnp.concatenate([k_c, k_x], axis=2), jnp.concatenate([vt_c, vt_x], axis=3))

        *y5_c, finals = _s5_scan(zc, h_zero, s5_tabs)
        *y5_x, _ = _s5_scan(zx, finals, s5_tabs)

        v_x, g1_x, g2_x = _short_conv(zx, hy_conv_w[i], hy_conv_b[i][None])
        e_x = _hyena_mixer(v_x, (g1_x, g2_x), _hyena_filters(n_tok, *fparams), hy_bias[i])

        x = _merge(x, zx, a_x, y5_x, e_x, *merge_w, mx[:, 2])
        x = _ffn(x, g_ffn, mx[:, 3], mx[:, 4], mx[:, 5], w_g, w_u, w_d, final_norm[None], last)

        if not last:
            a_c = _attention(q_c, k_c, vt_c)
            v_c, g1_c, g2_c = _short_conv(zc, hy_conv_w[i], hy_conv_b[i][None])
            e_c = _hyena_mixer(v_c, (g1_c, g2_c), _hyena_filters(n_ctx, *fparams), hy_bias[i])
            ctx = _merge(ctx, zc, a_c, y5_c, e_c, *merge_w, mc[:, 2])
            ctx = _ffn(ctx, g_ffn, mc[:, 3], mc[:, 4], mc[:, 5], w_g, w_u, w_d, final_norm[None], False)
    return x
```

```python
import functools
import math

import jax
import jax.numpy as jnp
import numpy as np
from jax import lax
from jax.experimental import pallas as pl
from jax.experimental.pallas import tpu as pltpu

F32 = jnp.float32
BF16 = jnp.bfloat16
HIGHEST = lax.Precision.HIGHEST

D_MODEL = 1024
GRID_W = 64
NORM_EPS = 1e-6
N_MOD = 6

MLA_HEADS = 8
MLA_NOPE = 64
MLA_ROPE = 32
MLA_V = 64
MLA_Q_LORA = 256
MLA_KV_LORA = 128
ROPE_BASE = 10000.0
HEAD_W = 128
MLA_PREP_ROWS = 256
VT_ROWS = 80
QK_SCALE_LOG2 = (MLA_NOPE + MLA_ROPE) ** -0.5 * math.log2(math.e)
ATTN_SUB = 128
ATTN_LAG = 2
ATTN_UNROLL = 12

S5_WIDTH = 512
S5_GROUP = 16
S5_GROUPS = 32
S5_STATE = 64
S5_HALF = 256
S5_HSTATE = 1024
S5_CHUNK = 128
S5_CHUNKS_PER_STEP = 4

HY_WIDTH = 512
HY_ORDER = 2
HY_BANDS = 16
HY_POS_DIM = 1 + 2 * HY_BANDS
HY_POS_PAD = 64
HY_FILTER_HIDDEN = 64
HY_FILTER_OUT = HY_ORDER * 2 * HY_WIDTH
HY_DECAY_TARGET = 1e-2
HY_DECAY_SHORT = 0.3
HY_DECAY_LONG = 1.5
HY_DECAY_SHIFT = 0.05

D_FF = 2816

OFF_CQ = 0
OFF_CKV = OFF_CQ + MLA_Q_LORA
OFF_KR = OFF_CKV + MLA_KV_LORA
OFF_S5 = OFF_KR + MLA_ROPE
OFF_HY = OFF_S5 + S5_WIDTH
OFF_GATE = OFF_HY + 3 * HY_WIDTH

Z_GATE = 0
Z_HY = 3072
Z_S5 = 4608
Z_MLA = 5120
Z_WIDTH = 5632

VMEM_LIMIT_V7X = 52 * 1024 * 1024


def _cparams(sem, flags=None):
    return pltpu.CompilerParams(dimension_semantics=sem, vmem_limit_bytes=VMEM_LIMIT_V7X, flags=flags)


def _dot(a, b):
    return jnp.dot(a, b, preferred_element_type=F32)


def _rms(x, g):
    return x * lax.rsqrt(jnp.mean(x * x, axis=-1, keepdims=True) + NORM_EPS) * g


def _norm_mod(x, g, shift, scale):
    return _rms(x, g) * (1.0 + scale) + shift


def _proj_kernel(x_ref, g_ref, sh_ref, sc_ref, w_ref, o_ref, h_ref):
    @pl.when(pl.program_id(2) == 0)
    def _():
        h_ref[...] = _norm_mod(x_ref[0], g_ref[...], sh_ref[0], sc_ref[0]).astype(BF16)

    o_ref[0] = _dot(h_ref[...], w_ref[...]).astype(BF16)


def _proj_in(x, g, shift, scale, w):
    bsz, n, d = x.shape
    nz = w.shape[1]
    tm = min(n, 2048)
    tn = 512
    return pl.pallas_call(
        _proj_kernel,
        grid=(bsz, n // tm, nz // tn),
        in_specs=[
            pl.BlockSpec((1, tm, d), lambda b, i, j: (b, i, 0)),
            pl.BlockSpec((1, d), lambda b, i, j: (0, 0)),
            pl.BlockSpec((1, 1, d), lambda b, i, j: (b, 0, 0)),
            pl.BlockSpec((1, 1, d), lambda b, i, j: (b, 0, 0)),
            pl.BlockSpec((d, tn), lambda b, i, j: (0, j)),
        ],
        out_specs=pl.BlockSpec((1, tm, tn), lambda b, i, j: (b, i, j)),
        out_shape=jax.ShapeDtypeStruct((bsz, n, nz), BF16),
        scratch_shapes=[pltpu.VMEM((tm, d), BF16)],
        compiler_params=_cparams(("parallel", "parallel", "arbitrary")),
        name="proj_in",
    )(x, g, shift, scale, w)


def _mla_prep_kernel(z_ref, qn_ref, kvn_ref, wq_ref, wqs_ref, wk_ref, wvt_ref, c_ref, s_ref, *rest):
    q_ref, k_ref, vt_ref = rest[-3:]
    z = z_ref[0].astype(F32)
    hq = _rms(z[:, :MLA_Q_LORA], qn_ref[...]).astype(BF16)
    hkv = _rms(z[:, MLA_Q_LORA:MLA_Q_LORA + MLA_KV_LORA], kvn_ref[...]).astype(BF16)
    krb = z[:, MLA_Q_LORA + MLA_KV_LORA:]
    qa = _dot(hq, wq_ref[...])
    qb = _dot(hq, wqs_ref[...])
    kn = _dot(hkv, wk_ref[...])
    vt = lax.dot_general(wvt_ref[...], hkv, (((1,), (1,)), ((), ())), preferred_element_type=F32)
    ck = c_ref[...]
    sn = s_ref[...]
    lane = lax.broadcasted_iota(jnp.int32, ck.shape, 1)
    cq = jnp.where(lane < MLA_ROPE, ck, 1.0)
    kr = krb * ck + pltpu.roll(krb, HEAD_W - MLA_ROPE // 2, 1) * sn
    row = lax.broadcasted_iota(jnp.int32, (VT_ROWS, z.shape[0]), 0)
    for h in range(MLA_HEADS):
        sl = slice(h * HEAD_W, (h + 1) * HEAD_W)
        q_ref[0, h] = ((qa[:, sl] * cq + qb[:, sl] * sn) * QK_SCALE_LOG2).astype(BF16)
        k_ref[0, h] = (kr + kn[:, sl]).astype(BF16)
        vt_ref[0, h] = jnp.where(row == MLA_V, 1.0, vt[h * VT_ROWS:(h + 1) * VT_ROWS, :]).astype(BF16)


def _mla_prep(z, qn, kvn, wq, wqs, wk, wvt, ctab, stab, n_keys, key_off, kv_bufs=None):
    bsz, n, _ = z.shape
    tm = min(n, MLA_PREP_ROWS)
    assert key_off % tm == 0, "the key offset must be a whole number of row tiles"
    ob = key_off // tm
    hw = MLA_HEADS * HEAD_W
    zb = Z_MLA // 512
    full = lambda shape: pl.BlockSpec(shape, lambda b, i: (0,) * len(shape))
    in_specs = [
        pl.BlockSpec((1, tm, 512), lambda b, i: (b, i, zb)),
        full((1, MLA_Q_LORA)),
        full((1, MLA_KV_LORA)),
        full((MLA_Q_LORA, hw)),
        full((MLA_Q_LORA, hw)),
        full((MLA_KV_LORA, hw)),
        full((MLA_HEADS * VT_ROWS, MLA_KV_LORA)),
        pl.BlockSpec((tm, HEAD_W), lambda b, i: (i, 0)),
        pl.BlockSpec((tm, HEAD_W), lambda b, i: (i, 0)),
    ]
    args = [z, qn, kvn, wq, wqs, wk, wvt, ctab, stab]
    aliases = {}
    if kv_bufs is not None:
        aliases = {len(args): 1, len(args) + 1: 2}
        in_specs += [pl.BlockSpec(memory_space=pl.ANY)] * 2
        args += list(kv_bufs)
    return pl.pallas_call(
        _mla_prep_kernel,
        grid=(bsz, n // tm),
        in_specs=in_specs,
        out_specs=[
            pl.BlockSpec((1, MLA_HEADS, tm, HEAD_W), lambda b, i: (b, 0, i, 0)),
            pl.BlockSpec((1, MLA_HEADS, tm, HEAD_W), lambda b, i: (b, 0, i + ob, 0)),
            pl.BlockSpec((1, MLA_HEADS, VT_ROWS, tm), lambda b, i: (b, 0, 0, i + ob)),
        ],
        out_shape=[
            jax.ShapeDtypeStruct((bsz, MLA_HEADS, n, HEAD_W), BF16),
            jax.ShapeDtypeStruct((bsz, MLA_HEADS, n_keys, HEAD_W), BF16),
            jax.ShapeDtypeStruct((bsz, MLA_HEADS, VT_ROWS, n_keys), BF16),
        ],
        input_output_aliases=aliases,
        compiler_params=_cparams(("parallel", "parallel")),
        name="mla_prep",
    )(*args)


def _attn_kernel(q_ref, k_ref, vt_ref, o_ref, s0_ref, s1_ref, s2_ref, p0_ref, p1_ref, p2_ref, acc_ref, *, tk, nkc):
    q = q_ref[0, 0]
    tq = q.shape[0]
    s_refs = (s0_ref, s1_ref, s2_ref)
    p_refs = (p0_ref, p1_ref, p2_ref)

    ts = min(tk, ATTN_SUB)
    subs = [slice(r, r + ts) for r in range(0, tk, ts)]

    def scores_sub(j, s_ref, sub, cmax):
        off = pl.multiple_of(j * tk + sub.start, ts)
        s = lax.dot_general(k_ref[0, 0, pl.ds(off, ts), :], q, (((1,), (1,)), ((), ())),
                            preferred_element_type=F32)
        s_ref[sub, :] = s
        cm = jnp.max(s, axis=0, keepdims=True)
        return cm if cmax is None else jnp.maximum(cmax, cm)

    def softmax_sub(slot, sub, mn):
        p_refs[slot][sub, :] = jnp.exp2(s_refs[slot][sub, :] - mn).astype(BF16)

    def step(i, slot, carry, do_softmax, do_scores):
        cm, m, alpha = carry
        mn = jnp.maximum(m, cm)
        off = pl.multiple_of(i * tk, tk)
        acc_ref[...] = alpha * acc_ref[...] + _dot(vt_ref[0, 0, :, pl.ds(off, tk)], p_refs[slot][...])
        cm_new, cms = None, []
        for r, sub in enumerate(subs):
            mn_sub = mn
            if do_scores:
                cm_new = scores_sub(i + 2, s_refs[(slot + 2) % 3], sub, cm_new)
                cms.append(cm_new)
                if r >= ATTN_LAG:
                    mn_sub = jnp.maximum(mn, cms[r - ATTN_LAG] - 1e30)
            if do_softmax:
                softmax_sub((slot + 1) % 3, sub, mn_sub)
        if do_softmax:
            alpha, m = jnp.exp2(m - mn), mn
        return (cm_new if do_scores else cm), m, alpha

    m = jnp.full((1, tq), -1e30, F32)
    acc_ref[...] = jnp.zeros_like(acc_ref)
    cm, cm_next = None, None
    for sub in subs:
        cm = scores_sub(0, s0_ref, sub, cm)
        if nkc > 1:
            cm_next = scores_sub(1, s1_ref, sub, cm_next)
    mn = jnp.maximum(m, cm)
    for sub in subs:
        softmax_sub(0, sub, mn)
    carry = (cm_next if nkc > 1 else cm, mn, jnp.exp2(m - mn))

    def body(t, carry):
        for r in range(ATTN_UNROLL):
            carry = step(ATTN_UNROLL * t + r, r % 3, carry, True, True)
        return carry

    nloop = max(nkc - 2, 0) // ATTN_UNROLL
    carry = lax.fori_loop(0, nloop, body, carry)
    for i in range(ATTN_UNROLL * nloop, nkc):
        carry = step(i, i % 3, carry, i + 1 < nkc, i + 2 < nkc)
    acc = acc_ref[...]
    out = acc * (1.0 / acc[MLA_V:MLA_V + 1, :])
    out = jnp.concatenate([out, jnp.zeros((HEAD_W - VT_ROWS, tq), F32)], axis=0).T
    o_ref[0, 0] = out[:, :MLA_V].astype(BF16)


def _kv_chunk(nk):
    for tk in (640, 512, 256, 128):
        if nk % tk == 0:
            return tk
    raise ValueError(f"unsupported key count {nk}")


def _attention(q, k, vt, nk):
    bsz, nh, nq, _ = q.shape
    tq = min(nq, 1024)
    tk = _kv_chunk(nk)
    return pl.pallas_call(
        functools.partial(_attn_kernel, tk=tk, nkc=nk // tk),
        grid=(bsz, nh, nq // tq),
        in_specs=[
            pl.BlockSpec((1, 1, tq, HEAD_W), lambda b, h, i: (b, h, i, 0)),
            pl.BlockSpec((1, 1, nk, HEAD_W), lambda b, h, i: (b, h, 0, 0)),
            pl.BlockSpec((1, 1, VT_ROWS, nk), lambda b, h, i: (b, h, 0, 0)),
        ],
        out_specs=pl.BlockSpec((1, 1, tq, MLA_V), lambda b, h, i: (b, h, i, 0)),
        out_shape=jax.ShapeDtypeStruct((bsz, nh, nq, MLA_V), BF16),
        scratch_shapes=[pltpu.VMEM((tk, tq), F32)] * 3 + [pltpu.VMEM((tk, tq), BF16)] * 3
        + [pltpu.VMEM((VT_ROWS, tq), F32)],
        compiler_params=_cparams(("parallel", "parallel", "arbitrary")),
        name="attention",
    )(q, k, vt)


def _cmul(ar, ai, br, bi):
    return ar * br - ai * bi, ar * bi + ai * br


def _s5_kernel(u_ref, h0_ref, bm_ref, cm_ref, wn_ref, wp_ref, l1_ref, tri_ref, y_ref, hf_ref,
               carry_ref, h0cat_ref, h1cat_ref, *, tc, nr, ns, reverse):
    i = pl.program_id(1)

    @pl.when(i == 0)
    def _():
        carry_ref[...] = h0_ref[0]

    u = u_ref[0]
    tri = tri_ref[...]
    order = range(nr - 1, -1, -1) if reverse else range(nr)
    edge = 0 if reverse else tc - 1
    ys = []
    for hf, hcat_ref in enumerate((h0cat_ref, h1cat_ref)):
        bu = _dot(u[:, hf * S5_HALF:(hf + 1) * S5_HALF], bm_ref[hf])
        hr_in, hi_in = carry_ref[2 * hf:2 * hf + 1, :], carry_ref[2 * hf + 1:2 * hf + 2, :]
        for c in order:
            rows = slice(c * tc, (c + 1) * tc)
            xr, xi = _cmul(wn_ref[hf, 0], wn_ref[hf, 1], bu[rows, :S5_HSTATE], bu[rows, S5_HSTATE:])
            s = _dot(tri, jnp.concatenate([xr, xi], axis=1).astype(BF16))
            cr, ci = _cmul(l1_ref[hf, 0], l1_ref[hf, 1], hr_in, hi_in)
            hr, hi = _cmul(wp_ref[hf, 0], wp_ref[hf, 1], s[:, :S5_HSTATE] + cr, s[:, S5_HSTATE:] + ci)
            hcat_ref[rows, :S5_HSTATE] = hr.astype(BF16)
            hcat_ref[rows, S5_HSTATE:] = hi.astype(BF16)
            hr_in, hi_in = hr[edge:edge + 1], hi[edge:edge + 1]
        carry_ref[2 * hf:2 * hf + 1, :] = hr_in
        carry_ref[2 * hf + 1:2 * hf + 2, :] = hi_in
        ys.append(_dot(hcat_ref[...], cm_ref[hf]))
    y_ref[0] = jnp.concatenate(ys, axis=1)

    @pl.when(i == ns - 1)
    def _():
        hf_ref[0] = carry_ref[...]


def _s5_scan_dir(z, h0, tabs, reverse):
    bm, cm, wn, wp, l1, tri = tabs
    bsz, n, _ = z.shape
    tc = S5_CHUNK
    nr = min(S5_CHUNKS_PER_STEP, n // tc)
    rows = nr * tc
    ns = n // rows
    zb = Z_S5 // S5_WIDTH
    blk = (lambda i: ns - 1 - i) if reverse else (lambda i: i)
    full = lambda shape: pl.BlockSpec(shape, lambda b, i: (0,) * len(shape))
    return pl.pallas_call(
        functools.partial(_s5_kernel, tc=tc, nr=nr, ns=ns, reverse=reverse),
        grid=(bsz, ns),
        in_specs=[
            pl.BlockSpec((1, rows, S5_WIDTH), lambda b, i: (b, blk(i), zb)),
            pl.BlockSpec((1, 4, S5_HSTATE), lambda b, i: (b, 0, 0)),
            full((2, S5_HALF, 2 * S5_HSTATE)),
            full((2, 2 * S5_HSTATE, S5_HALF)),
            full((2, 2, tc, S5_HSTATE)),
            full((2, 2, tc, S5_HSTATE)),
            full((2, 2, 1, S5_HSTATE)),
            full((tc, tc)),
        ],
        out_specs=[
            pl.BlockSpec((1, rows, S5_WIDTH), lambda b, i: (b, blk(i), 0)),
            pl.BlockSpec((1, 4, S5_HSTATE), lambda b, i: (b, 0, 0)),
        ],
        out_shape=[
            jax.ShapeDtypeStruct((bsz, n, S5_WIDTH), F32),
            jax.ShapeDtypeStruct((bsz, 4, S5_HSTATE), F32),
        ],
        scratch_shapes=[pltpu.VMEM((4, S5_HSTATE), F32),
                        pltpu.VMEM((rows, 2 * S5_HSTATE), BF16), pltpu.VMEM((rows, 2 * S5_HSTATE), BF16)],
        compiler_params=_cparams(("parallel", "arbitrary")),
        name="s5_scan",
    )(z, h0, bm, cm, wn, wp, l1, tri)


def _s5_scan(z, h0, tabs):
    outs = [_s5_scan_dir(z, h0[:, d], [t[d] for t in tabs], reverse=bool(d)) for d in range(2)]
    return outs[0][0], outs[1][0], jnp.stack([outs[0][1], outs[1][1]], axis=1)


def _s5_tables(lam_re, lam_im, log_dt, b_re, b_im, c_re, c_im):
    tc = S5_CHUNK
    dt = jnp.exp(log_dt)[..., None]
    zr, zi = lam_re * dt, lam_im * dt
    mag = jnp.exp(zr)
    lbr, lbi = mag * jnp.cos(zi), mag * jnp.sin(zi)
    den = lam_re * lam_re + lam_im * lam_im
    nr, ni = lbr - 1.0, lbi
    cfr = (nr * lam_re + ni * lam_im) / den
    cfi = (ni * lam_re - nr * lam_im) / den
    bbr = cfr[..., None] * b_re - cfi[..., None] * b_im
    bbi = cfr[..., None] * b_im + cfi[..., None] * b_re
    eye = jnp.eye(S5_GROUP, dtype=F32)

    def blockdiag_in(b):
        b = b.reshape(2, 2, S5_GROUP, S5_STATE, S5_GROUP)
        return jnp.einsum('gk,dhgpn->dhgnkp', eye, b).reshape(2, 2, S5_HALF, S5_HSTATE)

    def blockdiag_out(c):
        c = c.reshape(2, 2, S5_GROUP, S5_GROUP, S5_STATE)
        return jnp.einsum('gk,dhgnp->dhgpkn', eye, c).reshape(2, 2, S5_HSTATE, S5_HALF)

    bm = jnp.concatenate([blockdiag_in(bbr), blockdiag_in(bbi)], axis=-1).astype(BF16)
    cm = jnp.concatenate([blockdiag_out(c_re), blockdiag_out(-c_im)], axis=-2).astype(BF16)

    def powers(k):
        zr_ = zr.reshape(2, 2, 1, S5_HSTATE)
        zi_ = zi.reshape(2, 2, 1, S5_HSTATE)
        kk = k[:, None, :, None]
        m = jnp.exp(kk * zr_)
        return jnp.stack([m * jnp.cos(kk * zi_), m * jnp.sin(kk * zi_)], axis=2)

    t = jnp.arange(tc, dtype=F32)
    wn = powers(jnp.stack([-t, -(tc - 1 - t)]))
    wp = powers(jnp.stack([t, tc - 1 - t]))
    l1 = powers(jnp.ones((2, 1), F32))
    r = jnp.arange(tc)
    tri = jnp.stack([r[:, None] >= r[None, :], r[:, None] <= r[None, :]]).astype(BF16)
    return bm, cm, wn, wp, l1, tri


HALO = 16


def _sconv_kernel(z_ref, zp_ref, zn_ref, w_ref, b_ref, v_ref, g1_ref, g2_ref, *, nt):
    i = pl.program_id(1)
    z = z_ref[0].astype(F32)
    tm = z.shape[0]
    row = lax.broadcasted_iota(jnp.int32, z.shape, 0)
    prev = jnp.where(i > 0, zp_ref[0, HALO - 1:HALO, :].astype(F32), 0.0)
    nxt = jnp.where(i < nt - 1, zn_ref[0, 0:1, :].astype(F32), 0.0)
    up = jnp.where(row == 0, prev, pltpu.roll(z, 1, 0))
    dn = jnp.where(row == tm - 1, nxt, pltpu.roll(z, tm - 1, 0))
    u = up * w_ref[0:1, :] + z * w_ref[1:2, :] + dn * w_ref[2:3, :] + b_ref[...]
    for k, o_ref in enumerate((v_ref, g1_ref, g2_ref)):
        o_ref[0] = u[:, k * HY_WIDTH:(k + 1) * HY_WIDTH]


def _short_conv(z, w, b):
    bsz, n, _ = z.shape
    cw = 3 * HY_WIDTH
    tm = min(n, 512)
    nt = n // tm
    zb = Z_HY // cw
    rh = tm // HALO
    out = pl.BlockSpec((1, tm, HY_WIDTH), lambda b_, i: (b_, i, 0))
    return pl.pallas_call(
        functools.partial(_sconv_kernel, nt=nt),
        grid=(bsz, nt),
        in_specs=[
            pl.BlockSpec((1, tm, cw), lambda b_, i: (b_, i, zb)),
            pl.BlockSpec((1, HALO, cw), lambda b_, i: (b_, jnp.maximum(i * rh - 1, 0), zb)),
            pl.BlockSpec((1, HALO, cw), lambda b_, i: (b_, jnp.minimum((i + 1) * rh, n // HALO - 1), zb)),
            pl.BlockSpec((3, cw), lambda b_, i: (0, 0)),
            pl.BlockSpec((1, cw), lambda b_, i: (0, 0)),
        ],
        out_specs=[out] * 3,
        out_shape=[jax.ShapeDtypeStruct((bsz, n, HY_WIDTH), F32)] * 3,
        compiler_params=_cparams(("parallel", "parallel")),
        name="short_conv",
    )(z, z, z, w, b)


def _filt_kernel(f_ref, w1_ref, b1_ref, w2_ref, b2_ref, w3h_ref, w3l_ref, fq_ref, dl_ref, o_ref, *, n_tok):
    z = f_ref[...]
    tm = z.shape[0]
    fq = fq_ref[...]
    hid = jnp.sin(fq * (jnp.dot(z, w1_ref[...], precision=HIGHEST, preferred_element_type=F32) + b1_ref[...]))
    hid = jnp.sin(fq * (jnp.dot(hid, w2_ref[...], precision=HIGHEST, preferred_element_type=F32) + b2_ref[...]))
    hid_hi = hid.astype(BF16)
    hid_lo = (hid - hid_hi.astype(F32)).astype(BF16)

    def filters(rows, d):
        wh, wl = w3h_ref[d], w3l_ref[d]
        f = _dot(hid_hi[rows], wh) + (_dot(hid_hi[rows], wl) + _dot(hid_lo[rows], wh))
        return f * (jnp.exp(-z[rows, 0:1] * dl_ref[d]) + HY_DECAY_SHIFT)

    i = pl.program_id(0)
    filt = filters(slice(None), jnp.where(i * tm >= n_tok, 1, 0))
    m = i * tm + lax.broadcasted_iota(jnp.int32, filt.shape, 0)
    filt = jnp.where(m == n_tok, 0.0, filt)
    for o in range(HY_ORDER):
        o_ref[o] = filt[:, o * HY_WIDTH:(o + 1) * HY_WIDTH]

    @pl.when(i == 0)
    def _():
        head = slice(0, HALO)
        lag0 = filters(head, 1)
        lag0 = jnp.where(lax.broadcasted_iota(jnp.int32, lag0.shape, 0) == 0, lag0, 0.0)
        for o in range(HY_ORDER):
            o_ref[o, head, :] += lag0[:, o * HY_WIDTH:(o + 1) * HY_WIDTH]


def _hyena_filter_feats(n_tok):
    m = jnp.arange(2 * n_tok)
    lag = jnp.where(m < n_tok, m, jnp.where(m > n_tok, 2 * n_tok - m, 0))
    t = (lag.astype(F32) / (n_tok - 1))[:, None]
    w = (2.0 * math.pi * lag.astype(F32) / n_tok)[:, None]
    bands = jnp.linspace(1e-4, HY_BANDS - 1, HY_BANDS, dtype=F32)[None, :]
    feats = jnp.concatenate([t, jnp.cos(bands * w), -jnp.sin(bands * w)], axis=-1)
    return jnp.pad(feats, ((0, 0), (0, HY_POS_PAD - HY_POS_DIM)))


def _hyena_filters(n_tok, w1, b1, w2, b2, w3, freq):
    feats = _hyena_filter_feats(n_tok)
    deltas = jnp.abs(jnp.linspace(math.log(HY_DECAY_TARGET) / HY_DECAY_SHORT,
                                  math.log(HY_DECAY_TARGET) / HY_DECAY_LONG, HY_FILTER_OUT, dtype=F32))[None, :]
    w1p = jnp.pad(w1, ((0, HY_POS_PAD - HY_POS_DIM), (0, 0)))

    def by_direction(a):
        a = a.reshape(a.shape[:-1] + (HY_ORDER, 2, HY_WIDTH))
        return jnp.moveaxis(a, -2, 0).reshape((2,) + a.shape[:-3] + (HY_ORDER * HY_WIDTH,))

    w3, deltas = by_direction(w3), by_direction(deltas)
    w3_hi = w3.astype(BF16)
    w3_lo = (w3 - w3_hi.astype(F32)).astype(BF16)
    n2 = 2 * n_tok
    tm = min(n_tok, 512)
    full = lambda shape: pl.BlockSpec(shape, lambda i: (0,) * len(shape))
    hh = HY_FILTER_HIDDEN
    wd = HY_ORDER * HY_WIDTH
    return pl.pallas_call(
        functools.partial(_filt_kernel, n_tok=n_tok),
        grid=(n2 // tm,),
        in_specs=[
            pl.BlockSpec((tm, HY_POS_PAD), lambda i: (i, 0)),
            full((HY_POS_PAD, hh)), full((1, hh)), full((hh, hh)), full((1, hh)),
            full((2, hh, wd)), full((2, hh, wd)), full((1, hh)), full((2, 1, wd)),
        ],
        out_specs=pl.BlockSpec((HY_ORDER, tm, HY_WIDTH), lambda i: (0, i, 0)),
        out_shape=jax.ShapeDtypeStruct((HY_ORDER, n2, HY_WIDTH), F32),
        compiler_params=_cparams(("parallel",)),
        name="hyena_filter",
    )(feats, w1p, b1[None], w2, b2[None], w3_hi, w3_lo, freq[None], deltas)


def _fft_split(n_fft):
    if n_fft <= 1024:
        return n_fft, 1
    n1 = 1 << (int(math.log2(n_fft)) // 2)
    return n1, n_fft // n1


def _cis(num, den, sign):
    ang = (2.0 * math.pi / den) * num.astype(F32)
    return jnp.cos(ang), sign * jnp.sin(ang)


def _fft_tables(n_fft):
    n1, n2 = _fft_split(n_fft)
    a = jnp.arange(n1)
    f1r, f1i = _cis((a[:, None] * a[None, :]) % n1, n1, -1.0)
    s1 = jnp.concatenate([f1r, f1i], axis=0).astype(BF16)
    half = n1 // 2
    s3 = (jnp.concatenate([f1r[:half], -f1i[:half]], axis=0) / n_fft).astype(BF16)
    if n2 == 1:
        return s1, s3, None, None
    b = jnp.arange(n2)
    f2r, f2i = _cis((b[:, None] * b[None, :]) % n2, n2, -1.0)
    twr, twi = _cis((jnp.arange(n1)[:, None] * b[None, :]) % n_fft, n_fft, -1.0)
    fr, fi = _cmul(f2r[None], f2i[None], twr[:, None, :], twi[:, None, :])
    ft = jnp.concatenate([fr, fi], axis=1).astype(BF16)
    frt, fit = _cmul(f2r[None], f2i[None], twr[:, :, None], twi[:, :, None])
    it = jnp.concatenate([frt, -fit], axis=1).astype(BF16)
    return s1, s3, ft, it


def _stacked_cdot(s, xr, xi, m):
    p = _dot(s, xr.astype(BF16))
    if xi is None:
        return p[:m], p[m:]
    q = _dot(s, xi.astype(BF16))
    return p[:m] - q[m:], q[:m] + p[m:]


def _fft_a_kernel(*refs, parts, n1, has_g):
    if has_g:
        x_ref, s_ref, g_ref, y_ref = refs
    else:
        x_ref, s_ref, y_ref = refs
    yr, yi = _stacked_cdot(s_ref[...], x_ref[0], x_ref[1] if parts == 2 else None, n1)
    if has_g:
        yr, yi = _cmul(yr, yi, g_ref[0], g_ref[1])
    y_ref[0] = yr
    y_ref[1] = yi


def _fft_a(x, s1, g=None):
    parts, a_rows, cols = x.shape
    n1 = s1.shape[0] // 2
    tc = min(cols, 2048)
    in_specs = [
        pl.BlockSpec((parts, a_rows, tc), lambda j: (0, 0, j)),
        pl.BlockSpec((2 * n1, a_rows), lambda j: (0, 0)),
    ]
    args = [x, s1[:, :a_rows]]
    if g is not None:
        in_specs.append(pl.BlockSpec((2, n1, tc), lambda j: (0, 0, j)))
        args.append(g)
    return pl.pallas_call(
        functools.partial(_fft_a_kernel, parts=parts, n1=n1, has_g=g is not None),
        grid=(cols // tc,),
        in_specs=in_specs,
        out_specs=pl.BlockSpec((2, n1, tc), lambda j: (0, 0, j)),
        out_shape=jax.ShapeDtypeStruct((2, n1, cols), F32),
        compiler_params=_cparams(("parallel",)),
        name="fft_stage_a",
    )(*args)


def _fft_b_kernel(*refs, n2, inverse):
    if inverse:
        y_ref, ft_ref, it_ref, g_ref, o_ref = refs
    else:
        y_ref, ft_ref, o_ref = refs
    xr, xi = _stacked_cdot(ft_ref[0], y_ref[0, 0], y_ref[1, 0], n2)
    if inverse:
        zr, zi = _cmul(xr, xi, g_ref[0, 0], g_ref[1, 0])
        xr, xi = _stacked_cdot(it_ref[0], zr, zi, n2)
    o_ref[0, 0] = xr.astype(o_ref.dtype)
    o_ref[1, 0] = xi.astype(o_ref.dtype)


def _fft_b(y, ft, it=None, g=None):
    _, n1, n2, ch = y.shape
    inverse = it is not None
    blk = pl.BlockSpec((2, 1, n2, ch), lambda c: (0, c, 0, 0))
    mat = pl.BlockSpec((1, 2 * n2, n2), lambda c: (c, 0, 0))
    in_specs, args = [blk, mat], [y, ft]
    if inverse:
        in_specs += [mat, blk]
        args += [it, g]
    return pl.pallas_call(
        functools.partial(_fft_b_kernel, n2=n2, inverse=inverse),
        grid=(n1,),
        in_specs=in_specs,
        out_specs=blk,
        out_shape=jax.ShapeDtypeStruct(y.shape, BF16 if inverse else F32),
        compiler_params=_cparams(("parallel",)),
        name="fft_stage_b",
    )(*args)


def _fft_c_kernel(y_ref, s_ref, x_ref, g_ref, b_ref, o_ref, *, half):
    cr, ci = _stacked_cdot(s_ref[...], y_ref[0], y_ref[1], half)
    bias = b_ref[...]
    o_ref[0] = g_ref[0] * (cr + bias * x_ref[0])
    o_ref[1] = g_ref[1] * (ci + bias * x_ref[1])


def _fft_c(y, s3, x, gate, bias_cols):
    _, n1, cols = y.shape
    half = n1 // 2
    tc = min(cols, 2048)
    io = pl.BlockSpec((2, half, tc), lambda j: (0, 0, j))
    return pl.pallas_call(
        functools.partial(_fft_c_kernel, half=half),
        grid=(cols // tc,),
        in_specs=[
            pl.BlockSpec((2, n1, tc), lambda j: (0, 0, j)),
            pl.BlockSpec((n1, n1), lambda j: (0, 0)),
            io, io,
            pl.BlockSpec((1, tc), lambda j: (0, j)),
        ],
        out_specs=io,
        out_shape=jax.ShapeDtypeStruct((2, half, cols), F32),
        compiler_params=_cparams(("parallel",)),
        name="fft_stage_c",
    )(y, s3, x, gate, bias_cols)


FFT_BT = 16


def _to_fine_major(x):
    return pltpu.einshape("abc->bac", x)


def _fft_a4_kernel(x_ref, s_ref, y_ref, *, parts, n1):
    s = s_ref[...]
    xr = _to_fine_major(x_ref[0].astype(BF16))
    xi = _to_fine_major(x_ref[1].astype(BF16)) if parts == 2 else None
    out = [_stacked_cdot(s, xr[b], None if xi is None else xi[b], n1) for b in range(FFT_BT)]
    y_ref[0] = pltpu.einshape("bac->abc", jnp.stack([o[0] for o in out])).astype(BF16)
    y_ref[1] = pltpu.einshape("bac->abc", jnp.stack([o[1] for o in out])).astype(BF16)


def _fft_a4(x, s1):
    parts, a_rows, n2, ch = x.shape
    n1 = s1.shape[0] // 2
    return pl.pallas_call(
        functools.partial(_fft_a4_kernel, parts=parts, n1=n1),
        grid=(n2 // FFT_BT,),
        in_specs=[
            pl.BlockSpec((parts, a_rows, FFT_BT, ch), lambda j: (0, 0, j, 0)),
            pl.BlockSpec((2 * n1, a_rows), lambda j: (0, 0)),
        ],
        out_specs=pl.BlockSpec((2, n1, FFT_BT, ch), lambda j: (0, 0, j, 0)),
        out_shape=jax.ShapeDtypeStruct((2, n1, n2, ch), BF16),
        compiler_params=_cparams(("parallel",)),
        name="fft_stage_a",
    )(x, s1[:, :a_rows])


def _fft_c4_kernel(y_ref, s_ref, x_ref, g_ref, b_ref, o_ref, c_ref, *, half):
    s = s_ref[...]
    yr = _to_fine_major(y_ref[0].astype(BF16))
    yi = _to_fine_major(y_ref[1].astype(BF16))
    out = [_stacked_cdot(s, yr[b], yi[b], half) for b in range(FFT_BT)]
    c_ref[0] = pltpu.einshape("bac->abc", jnp.stack([o[0] for o in out]))
    c_ref[1] = pltpu.einshape("bac->abc", jnp.stack([o[1] for o in out]))
    o_ref[...] = g_ref[...] * (c_ref[...] + b_ref[...][None, None] * x_ref[...])


def _fft_c4(y, s3, x, gate, bias):
    _, n1, n2, ch = y.shape
    half = n1 // 2
    io = pl.BlockSpec((2, half, FFT_BT, ch), lambda j: (0, 0, j, 0))
    return pl.pallas_call(
        functools.partial(_fft_c4_kernel, half=half),
        grid=(n2 // FFT_BT,),
        in_specs=[
            pl.BlockSpec((2, n1, FFT_BT, ch), lambda j: (0, 0, j, 0)),
            pl.BlockSpec((n1, n1), lambda j: (0, 0)),
            io, io,
            pl.BlockSpec((1, ch), lambda j: (0, 0)),
        ],
        out_specs=io,
        out_shape=jax.ShapeDtypeStruct((2, half, n2, ch), F32),
        scratch_shapes=[pltpu.VMEM((2, half, FFT_BT, ch), F32)],
        compiler_params=_cparams(("parallel",)),
        name="fft_stage_c",
    )(y, s3, x, gate, bias)


def _hyena_mixer(v, gates, gfilt, bias):
    bsz, n, ch = v.shape
    assert bsz == 2, "the complex packing of the long convolution pairs exactly two batch rows"
    n_fft = 2 * n
    n1, n2 = _fft_split(n_fft)
    s1, s3, ft, it = _fft_tables(n_fft)
    if n2 == 1:
        y = v
        for o in range(HY_ORDER):
            gspec = _fft_a(gfilt[o][None], s1)
            y = _fft_c(_fft_a(y, s1, gspec), s3, y, gates[o], bias[o][None])
        return y
    half = n1 // 2
    y = v.reshape(2, half, n2, ch)
    for o in range(HY_ORDER):
        gspec = _fft_b(_fft_a4(gfilt[o].reshape(1, n1, n2, ch), s1), ft)
        spec = _fft_b(_fft_a4(y, s1), ft, it, gspec)
        y = _fft_c4(spec, s3, y, gates[o].reshape(2, half, n2, ch), bias[o][None])
    return y.reshape(2, n, ch)


def _gelu_tanh(x):
    return 0.5 * x * (1.0 + jnp.tanh(math.sqrt(2.0 / math.pi) * (x + 0.044715 * (x * x * x))))


def _merge_kernel(x_ref, g0_ref, g1_ref, g2_ref, a_ref, yf_ref, yb_ref, u_ref, e_ref, d_ref,
                  wglu_ref, wm_ref, ws_ref, wh_ref, wo_ref, gt_ref, o_ref):
    att = _dot(jnp.concatenate([a_ref[0, h] for h in range(MLA_HEADS)], axis=-1), wm_ref[...])
    y = yf_ref[0] + yb_ref[0] + d_ref[...] * u_ref[0].astype(F32)
    gy = _gelu_tanh(y)
    s5 = gy * jax.nn.sigmoid(_dot(gy.astype(BF16), wglu_ref[...]))
    s5 = _dot(s5.astype(BF16), ws_ref[...])
    hy = _dot(e_ref[0].astype(BF16), wh_ref[...])
    merged = jax.nn.sigmoid(g0_ref[0].astype(F32)) * att
    merged = merged + jax.nn.sigmoid(g1_ref[0].astype(F32)) * s5
    merged = merged + jax.nn.sigmoid(g2_ref[0].astype(F32)) * hy
    o_ref[0] = x_ref[0] + gt_ref[0] * _dot(merged.astype(BF16), wo_ref[...])


def _merge(x, z, att, y5, e, s5_d, w_glu, w_mla, w_s5, w_hy, w_out, gate):
    bsz, n, d = x.shape
    tm = min(n, 512)
    full = lambda shape: pl.BlockSpec(shape, lambda b, i: (0,) * len(shape))
    zs5 = Z_S5 // S5_WIDTH
    return pl.pallas_call(
        _merge_kernel,
        grid=(bsz, n // tm),
        in_specs=[
            pl.BlockSpec((1, tm, d), lambda b, i: (b, i, 0)),
            pl.BlockSpec((1, tm, d), lambda b, i: (b, i, 0)),
            pl.BlockSpec((1, tm, d), lambda b, i: (b, i, 1)),
            pl.BlockSpec((1, tm, d), lambda b, i: (b, i, 2)),
            pl.BlockSpec((1, MLA_HEADS, tm, MLA_V), lambda b, i: (b, 0, i, 0)),
            pl.BlockSpec((1, tm, S5_WIDTH), lambda b, i: (b, i, 0)),
            pl.BlockSpec((1, tm, S5_WIDTH), lambda b, i: (b, i, 0)),
            pl.BlockSpec((1, tm, S5_WIDTH), lambda b, i: (b, i, zs5)),
            pl.BlockSpec((1, tm, HY_WIDTH), lambda b, i: (b, i, 0)),
            full((1, S5_WIDTH)),
            full((S5_WIDTH, S5_WIDTH)),
            full((MLA_HEADS * MLA_V, d)),
            full((S5_WIDTH, d)),
            full((HY_WIDTH, d)),
            full((d, d)),
            pl.BlockSpec((1, 1, d), lambda b, i: (b, 0, 0)),
        ],
        out_specs=pl.BlockSpec((1, tm, d), lambda b, i: (b, i, 0)),
        out_shape=jax.ShapeDtypeStruct((bsz, n, d), F32),
        compiler_params=_cparams(("parallel", "parallel")),
        name="merge",
    )(x, z, z, z, att, y5[0], y5[1], z, e, s5_d, w_glu, w_mla, w_s5, w_hy, w_out, gate)


def _ffn_kernel(x_ref, g_ref, sh_ref, sc_ref, gt_ref, wg_ref, wu_ref, wd_ref, fg_ref, o_ref, h_ref, acc_ref,
                *, nk, final):
    k = pl.program_id(2)

    @pl.when(k == 0)
    def _():
        h_ref[...] = _norm_mod(x_ref[0], g_ref[...], sh_ref[0], sc_ref[0]).astype(BF16)
        acc_ref[...] = jnp.zeros_like(acc_ref)

    h = h_ref[...]
    act = jax.nn.silu(_dot(h, wg_ref[...])) * _dot(h, wu_ref[...])
    acc_ref[...] += _dot(act.astype(BF16), wd_ref[...])

    @pl.when(k == nk - 1)
    def _():
        r = x_ref[0] + gt_ref[0] * acc_ref[...]
        o_ref[0] = _rms(r, fg_ref[...]) if final else r


def _ffn(x, g, shift, scale, gate, w_g, w_u, w_d, final_g, final):
    bsz, n, d = x.shape
    dff = w_g.shape[1]
    tm = min(n, 1024)
    tf = 256
    nk = dff // tf
    vec = pl.BlockSpec((1, 1, d), lambda b, i, k: (b, 0, 0))
    row = pl.BlockSpec((1, d), lambda b, i, k: (0, 0))
    return pl.pallas_call(
        functools.partial(_ffn_kernel, nk=nk, final=final),
        grid=(bsz, n // tm, nk),
        in_specs=[
            pl.BlockSpec((1, tm, d), lambda b, i, k: (b, i, 0)),
            row, vec, vec, vec,
            pl.BlockSpec((d, tf), lambda b, i, k: (0, k)),
            pl.BlockSpec((d, tf), lambda b, i, k: (0, k)),
            pl.BlockSpec((tf, d), lambda b, i, k: (k, 0)),
            row,
        ],
        out_specs=pl.BlockSpec((1, tm, d), lambda b, i, k: (b, i, 0)),
        out_shape=jax.ShapeDtypeStruct((bsz, n, d), F32),
        scratch_shapes=[pltpu.VMEM((tm, d), BF16), pltpu.VMEM((tm, d), F32)],
        compiler_params=_cparams(("parallel", "parallel", "arbitrary")),
        name="ffn",
    )(x, g, shift, scale, gate, w_g, w_u, w_d, final_g)


def _rope_tables(n_tok):
    rows = n_tok // GRID_W
    row = jnp.broadcast_to(jnp.arange(rows, dtype=F32)[:, None], (rows, GRID_W)).reshape(-1)
    col = jnp.broadcast_to(jnp.arange(GRID_W, dtype=F32)[None, :], (rows, GRID_W)).reshape(-1)
    n_freq = MLA_ROPE // 4
    inv = ROPE_BASE ** (-jnp.arange(n_freq, dtype=F32) / n_freq)
    ang = jnp.concatenate([row[:, None] * inv, col[:, None] * inv], axis=-1)
    cos, sin = jnp.cos(ang), jnp.sin(ang)
    pad = jnp.zeros((n_tok, HEAD_W - MLA_ROPE), F32)
    return jnp.concatenate([cos, cos, pad], axis=-1), jnp.concatenate([-sin, sin, pad], axis=-1)


def _identity_rope_tables(n_tok):
    one = jnp.ones((n_tok, MLA_ROPE), F32)
    pad = jnp.zeros((n_tok, HEAD_W - MLA_ROPE), F32)
    return jnp.concatenate([one, pad], axis=-1), jnp.zeros((n_tok, HEAD_W), F32)


def _layout_w_in(w):
    kr = w[:, OFF_KR:OFF_S5]
    x1, x2 = kr[:, 0::2], kr[:, 1::2]
    pad = jnp.zeros((w.shape[0], HEAD_W - 3 * (MLA_ROPE // 2)), w.dtype)
    return jnp.concatenate([w[:, OFF_GATE:], w[:, OFF_HY:OFF_GATE], w[:, OFF_S5:OFF_HY],
                            w[:, OFF_CQ:OFF_CKV], w[:, OFF_CKV:OFF_KR], x1, x2, x1, pad], axis=1).astype(BF16)


def _layout_w_uq(w):
    w = w.reshape(MLA_Q_LORA, MLA_HEADS, MLA_NOPE + MLA_ROPE)
    nope, rope = w[..., :MLA_NOPE], w[..., MLA_NOPE:]
    x1, x2 = rope[..., 0::2], rope[..., 1::2]
    z32 = jnp.zeros_like(rope)
    wq = jnp.concatenate([x1, x2, z32, nope], axis=-1)
    wqs = jnp.concatenate([x2, x1, z32, jnp.zeros_like(nope)], axis=-1)
    shape = (MLA_Q_LORA, MLA_HEADS * HEAD_W)
    return wq.reshape(shape).astype(BF16), wqs.reshape(shape).astype(BF16)


def _layout_w_ukv(w):
    w = w.reshape(MLA_KV_LORA, MLA_HEADS, MLA_NOPE + MLA_V)
    nope, val = w[..., :MLA_NOPE], w[..., MLA_NOPE:]
    wk = jnp.concatenate([jnp.zeros_like(nope), nope], axis=-1).reshape(MLA_KV_LORA, MLA_HEADS * HEAD_W)
    wvt = jnp.concatenate([val, jnp.zeros_like(val[..., :VT_ROWS - MLA_V])], axis=-1).reshape(
        MLA_KV_LORA, MLA_HEADS * VT_ROWS).T
    return wk.astype(BF16), wvt.astype(BF16)


def kernel(x, c, ctx, c_ctx, ada_w, ada_b, norm_mix, w_in, mla_q_norm, mla_w_uq, mla_kv_norm, mla_w_ukv,
           s5_lam_re, s5_lam_im, s5_log_dt, s5_b_re, s5_b_im, s5_c_re, s5_c_im, s5_d, s5_w_glu,
           hy_conv_w, hy_conv_b, hy_f_w1, hy_f_b1, hy_f_w2, hy_f_b2, hy_f_w3, hy_f_freq, hy_bias,
           w_branch_mla, w_branch_s5, w_branch_hy, w_out, norm_ffn, ffn_w_gu, ffn_w_down, final_norm):
    bsz, n_tok, d = x.shape
    n_ctx = ctx.shape[1]
    depth = ada_w.shape[0]
    rope_x = _rope_tables(n_tok)
    rope_c = _identity_rope_tables(n_ctx)
    h_zero = jnp.zeros((bsz, 2, 4, S5_HSTATE), F32)

    for i in range(depth):
        last = i == depth - 1
        mx = (jax.nn.silu(c) @ ada_w[i] + ada_b[i]).reshape(bsz, N_MOD, 1, d)
        mc = jnp.broadcast_to((jax.nn.silu(c_ctx) @ ada_w[i] + ada_b[i]).reshape(1, N_MOD, 1, d),
                              (bsz, N_MOD, 1, d))
        w_z = _layout_w_in(w_in[i])
        wq, wqs = _layout_w_uq(mla_w_uq[i])
        wk, wvt = _layout_w_ukv(mla_w_ukv[i])
        qn, kvn = mla_q_norm[i][None], mla_kv_norm[i][None]
        s5_tabs = _s5_tables(s5_lam_re[i], s5_lam_im[i], s5_log_dt[i], s5_b_re[i], s5_b_im[i],
                             s5_c_re[i], s5_c_im[i])
        fparams = (hy_f_w1[i], hy_f_b1[i], hy_f_w2[i], hy_f_b2[i], hy_f_w3[i], hy_f_freq[i])
        merge_w = (s5_d[i][None], s5_w_glu[i].astype(BF16),
                   w_branch_mla[i].astype(BF16),
                   w_branch_s5[i].astype(BF16), w_branch_hy[i].astype(BF16), w_out[i].astype(BF16))
        w_g = ffn_w_gu[i][:, :D_FF].astype(BF16)
        w_u = ffn_w_gu[i][:, D_FF:].astype(BF16)
        w_d = ffn_w_down[i].astype(BF16)
        g_mix, g_ffn = norm_mix[i][None], norm_ffn[i][None]

        zx = _proj_in(x, g_mix, mx[:, 0], mx[:, 1], w_z)
        zc = _proj_in(ctx, g_mix, mc[:, 0], mc[:, 1], w_z)

        n_keys = n_ctx + n_tok
        q_x, k_all, vt_all = _mla_prep(zx, qn, kvn, wq, wqs, wk, wvt, *rope_x, n_keys, n_ctx)
        q_c, k_all, vt_all = _mla_prep(zc, qn, kvn, wq, wqs, wk, wvt, *rope_c, n_keys, 0, (k_all, vt_all))
        a_x = _attention(q_x, k_all, vt_all, n_keys)

        *y5_c, finals = _s5_scan(zc, h_zero, s5_tabs)
        *y5_x, _ = _s5_scan(zx, finals, s5_tabs)

        v_x, g1_x, g2_x = _short_conv(zx, hy_conv_w[i], hy_conv_b[i][None])
        e_x = _hyena_mixer(v_x, (g1_x, g2_x), _hyena_filters(n_tok, *fparams), hy_bias[i])

        x = _merge(x, zx, a_x, y5_x, e_x, *merge_w, mx[:, 2])
        x = _ffn(x, g_ffn, mx[:, 3], mx[:, 4], mx[:, 5], w_g, w_u, w_d, final_norm[None], last)

        if not last:
            a_c = _attention(q_c, k_all, vt_all, n_ctx)
            v_c, g1_c, g2_c = _short_conv(zc, hy_conv_w[i], hy_conv_b[i][None])
            e_c = _hyena_mixer(v_c, (g1_c, g2_c), _hyena_filters(n_ctx, *fparams), hy_bias[i])
            ctx = _merge(ctx, zc, a_c, y5_c, e_c, *merge_w, mc[:, 2])
            ctx = _ffn(ctx, g_ffn, mc[:, 3], mc[:, 4], mc[:, 5], w_g, w_u, w_d, final_norm[None], False)
    return x
```

```python
import functools
import math

import jax
import jax.numpy as jnp
import numpy as np
from jax import lax
from jax.experimental import pallas as pl
from jax.experimental.pallas import tpu as pltpu

F32 = jnp.float32
BF16 = jnp.bfloat16
HIGHEST = lax.Precision.HIGHEST

D_MODEL = 1024
GRID_W = 64
NORM_EPS = 1e-6
N_MOD = 6

MLA_HEADS = 8
MLA_NOPE = 64
MLA_ROPE = 32
MLA_V = 64
MLA_Q_LORA = 256
MLA_KV_LORA = 128
ROPE_BASE = 10000.0
HEAD_W = 128
MLA_PREP_ROWS = 512
VT_ROWS = 80
QK_SCALE_LOG2 = (MLA_NOPE + MLA_ROPE) ** -0.5 * math.log2(math.e)
ATTN_SUB = 128
ATTN_LAG = 2
ATTN_UNROLL = 12

S5_WIDTH = 512
S5_GROUP = 16
S5_GROUPS = 32
S5_STATE = 64
S5_HALF = 256
S5_HSTATE = 1024
S5_CHUNK = 128
S5_CHUNKS_PER_STEP = 4

HY_WIDTH = 512
HY_ORDER = 2
HY_BANDS = 16
HY_POS_DIM = 1 + 2 * HY_BANDS
HY_POS_PAD = 64
HY_FILTER_HIDDEN = 64
HY_FILTER_OUT = HY_ORDER * 2 * HY_WIDTH
HY_DECAY_TARGET = 1e-2
HY_DECAY_SHORT = 0.3
HY_DECAY_LONG = 1.5
HY_DECAY_SHIFT = 0.05

D_FF = 2816

OFF_CQ = 0
OFF_CKV = OFF_CQ + MLA_Q_LORA
OFF_KR = OFF_CKV + MLA_KV_LORA
OFF_S5 = OFF_KR + MLA_ROPE
OFF_HY = OFF_S5 + S5_WIDTH
OFF_GATE = OFF_HY + 3 * HY_WIDTH

Z_GATE = 0
Z_HY = 3072
Z_S5 = 4608
Z_MLA = 5120
Z_WIDTH = 5632

VMEM_LIMIT_V7X = 52 * 1024 * 1024


def _cparams(sem, flags=None):
    return pltpu.CompilerParams(dimension_semantics=sem, vmem_limit_bytes=VMEM_LIMIT_V7X, flags=flags)


def _dot(a, b):
    return jnp.dot(a, b, preferred_element_type=F32)


def _rms(x, g):
    return x * lax.rsqrt(jnp.mean(x * x, axis=-1, keepdims=True) + NORM_EPS) * g


def _norm_mod(x, g, shift, scale):
    return _rms(x, g) * (1.0 + scale) + shift


def _proj_kernel(x_ref, g_ref, sh_ref, sc_ref, w_ref, o_ref, h_ref):
    @pl.when(pl.program_id(2) == 0)
    def _():
        h_ref[...] = _norm_mod(x_ref[0], g_ref[...], sh_ref[0], sc_ref[0]).astype(BF16)

    o_ref[0] = _dot(h_ref[...], w_ref[...]).astype(BF16)


def _proj_in(x, g, shift, scale, w):
    bsz, n, d = x.shape
    nz = w.shape[1]
    tm = min(n, 2048)
    tn = 512
    return pl.pallas_call(
        _proj_kernel,
        grid=(bsz, n // tm, nz // tn),
        in_specs=[
            pl.BlockSpec((1, tm, d), lambda b, i, j: (b, i, 0)),
            pl.BlockSpec((1, d), lambda b, i, j: (0, 0)),
            pl.BlockSpec((1, 1, d), lambda b, i, j: (b, 0, 0)),
            pl.BlockSpec((1, 1, d), lambda b, i, j: (b, 0, 0)),
            pl.BlockSpec((d, tn), lambda b, i, j: (0, j)),
        ],
        out_specs=pl.BlockSpec((1, tm, tn), lambda b, i, j: (b, i, j)),
        out_shape=jax.ShapeDtypeStruct((bsz, n, nz), BF16),
        scratch_shapes=[pltpu.VMEM((tm, d), BF16)],
        compiler_params=_cparams(("parallel", "parallel", "arbitrary")),
        name="proj_in",
    )(x, g, shift, scale, w)


def _mla_prep_kernel(z_ref, qn_ref, kvn_ref, wq_ref, wqs_ref, wk_ref, wvt_ref, c_ref, s_ref, *rest):
    q_ref, k_ref, vt_ref = rest[-3:]
    z = z_ref[0].astype(F32)
    hq = _rms(z[:, :MLA_Q_LORA], qn_ref[...]).astype(BF16)
    hkv = _rms(z[:, MLA_Q_LORA:MLA_Q_LORA + MLA_KV_LORA], kvn_ref[...]).astype(BF16)
    krb = z[:, MLA_Q_LORA + MLA_KV_LORA:]
    qa = _dot(hq, wq_ref[...])
    qb = _dot(hq, wqs_ref[...])
    kn = _dot(hkv, wk_ref[...])
    vt = lax.dot_general(wvt_ref[...], hkv, (((1,), (1,)), ((), ())), preferred_element_type=F32)
    ck = c_ref[...]
    sn = s_ref[...]
    lane = lax.broadcasted_iota(jnp.int32, ck.shape, 1)
    cq = jnp.where(lane < MLA_ROPE, ck, 1.0)
    kr = krb * ck + pltpu.roll(krb, HEAD_W - MLA_ROPE // 2, 1) * sn
    row = lax.broadcasted_iota(jnp.int32, (VT_ROWS, z.shape[0]), 0)
    for h in range(MLA_HEADS):
        sl = slice(h * HEAD_W, (h + 1) * HEAD_W)
        q_ref[0, h] = ((qa[:, sl] * cq + qb[:, sl] * sn) * QK_SCALE_LOG2).astype(BF16)
        k_ref[0, h] = (kr + kn[:, sl]).astype(BF16)
        vt_ref[0, h] = jnp.where(row == MLA_V, 1.0, vt[h * VT_ROWS:(h + 1) * VT_ROWS, :]).astype(BF16)


def _mla_prep(z, qn, kvn, wq, wqs, wk, wvt, ctab, stab, n_keys, key_off, kv_bufs=None):
    bsz, n, _ = z.shape
    tm = min(n, MLA_PREP_ROWS)
    assert key_off % tm == 0, "the key offset must be a whole number of row tiles"
    ob = key_off // tm
    hw = MLA_HEADS * HEAD_W
    zb = Z_MLA // 512
    full = lambda shape: pl.BlockSpec(shape, lambda b, i: (0,) * len(shape))
    in_specs = [
        pl.BlockSpec((1, tm, 512), lambda b, i: (b, i, zb)),
        full((1, MLA_Q_LORA)),
        full((1, MLA_KV_LORA)),
        full((MLA_Q_LORA, hw)),
        full((MLA_Q_LORA, hw)),
        full((MLA_KV_LORA, hw)),
        full((MLA_HEADS * VT_ROWS, MLA_KV_LORA)),
        pl.BlockSpec((tm, HEAD_W), lambda b, i: (i, 0)),
        pl.BlockSpec((tm, HEAD_W), lambda b, i: (i, 0)),
    ]
    args = [z, qn, kvn, wq, wqs, wk, wvt, ctab, stab]
    aliases = {}
    if kv_bufs is not None:
        aliases = {len(args): 1, len(args) + 1: 2}
        in_specs += [pl.BlockSpec(memory_space=pl.ANY)] * 2
        args += list(kv_bufs)
    return pl.pallas_call(
        _mla_prep_kernel,
        grid=(bsz, n // tm),
        in_specs=in_specs,
        out_specs=[
            pl.BlockSpec((1, MLA_HEADS, tm, HEAD_W), lambda b, i: (b, 0, i, 0)),
            pl.BlockSpec((1, MLA_HEADS, tm, HEAD_W), lambda b, i: (b, 0, i + ob, 0)),
            pl.BlockSpec((1, MLA_HEADS, VT_ROWS, tm), lambda b, i: (b, 0, 0, i + ob)),
        ],
        out_shape=[
            jax.ShapeDtypeStruct((bsz, MLA_HEADS, n, HEAD_W), BF16),
            jax.ShapeDtypeStruct((bsz, MLA_HEADS, n_keys, HEAD_W), BF16),
            jax.ShapeDtypeStruct((bsz, MLA_HEADS, VT_ROWS, n_keys), BF16),
        ],
        input_output_aliases=aliases,
        compiler_params=_cparams(("parallel", "parallel")),
        name="mla_prep",
    )(*args)


def _attn_kernel(q_ref, k_ref, vt_ref, o_ref, s0_ref, s1_ref, s2_ref, p0_ref, p1_ref, p2_ref, acc_ref, *, tk, nkc):
    q = q_ref[0, 0]
    tq = q.shape[0]
    s_refs = (s0_ref, s1_ref, s2_ref)
    p_refs = (p0_ref, p1_ref, p2_ref)

    ts = min(tk, ATTN_SUB)
    subs = [slice(r, r + ts) for r in range(0, tk, ts)]

    def scores_sub(j, s_ref, sub, cmax):
        off = pl.multiple_of(j * tk + sub.start, ts)
        s = lax.dot_general(k_ref[0, 0, pl.ds(off, ts), :], q, (((1,), (1,)), ((), ())),
                            preferred_element_type=F32)
        s_ref[sub, :] = s
        cm = jnp.max(s, axis=0, keepdims=True)
        return cm if cmax is None else jnp.maximum(cmax, cm)

    def softmax_sub(slot, sub, mn):
        p_refs[slot][sub, :] = jnp.exp2(s_refs[slot][sub, :] - mn).astype(BF16)

    def step(i, slot, carry, do_softmax, do_scores):
        cm, m, alpha = carry
        mn = jnp.maximum(m, cm)
        off = pl.multiple_of(i * tk, tk)
        acc_ref[...] = alpha * acc_ref[...] + _dot(vt_ref[0, 0, :, pl.ds(off, tk)], p_refs[slot][...])
        cm_new, cms = None, []
        for r, sub in enumerate(subs):
            mn_sub = mn
            if do_scores:
                cm_new = scores_sub(i + 2, s_refs[(slot + 2) % 3], sub, cm_new)
                cms.append(cm_new)
                if r >= ATTN_LAG:
                    mn_sub = jnp.maximum(mn, cms[r - ATTN_LAG] - 1e30)
            if do_softmax:
                softmax_sub((slot + 1) % 3, sub, mn_sub)
        if do_softmax:
            alpha, m = jnp.exp2(m - mn), mn
        return (cm_new if do_scores else cm), m, alpha

    m = jnp.full((1, tq), -1e30, F32)
    acc_ref[...] = jnp.zeros_like(acc_ref)
    cm, cm_next = None, None
    for sub in subs:
        cm = scores_sub(0, s0_ref, sub, cm)
        if nkc > 1:
            cm_next = scores_sub(1, s1_ref, sub, cm_next)
    mn = jnp.maximum(m, cm)
    for sub in subs:
        softmax_sub(0, sub, mn)
    carry = (cm_next if nkc > 1 else cm, mn, jnp.exp2(m - mn))

    def body(t, carry):
        for r in range(ATTN_UNROLL):
            carry = step(ATTN_UNROLL * t + r, r % 3, carry, True, True)
        return carry

    nloop = max(nkc - 2, 0) // ATTN_UNROLL
    carry = lax.fori_loop(0, nloop, body, carry)
    for i in range(ATTN_UNROLL * nloop, nkc):
        carry = step(i, i % 3, carry, i + 1 < nkc, i + 2 < nkc)
    acc = acc_ref[...]
    out = acc * (1.0 / acc[MLA_V:MLA_V + 1, :])
    out = jnp.concatenate([out, jnp.zeros((HEAD_W - VT_ROWS, tq), F32)], axis=0).T
    o_ref[0, 0] = out[:, :MLA_V].astype(BF16)


def _kv_chunk(nk):
    for tk in (640, 512, 256, 128):
        if nk % tk == 0:
            return tk
    raise ValueError(f"unsupported key count {nk}")


def _attention(q, k, vt, nk, key_block=0):
    bsz, nh, nq, _ = q.shape
    tq = min(nq, 1024)
    tk = _kv_chunk(nk)
    return pl.pallas_call(
        functools.partial(_attn_kernel, tk=tk, nkc=nk // tk),
        grid=(bsz, nh, nq // tq),
        in_specs=[
            pl.BlockSpec((1, 1, tq, HEAD_W), lambda b, h, i: (b, h, i, 0)),
            pl.BlockSpec((1, 1, nk, HEAD_W), lambda b, h, i: (b, h, key_block, 0)),
            pl.BlockSpec((1, 1, VT_ROWS, nk), lambda b, h, i: (b, h, 0, key_block)),
        ],
        out_specs=pl.BlockSpec((1, 1, tq, MLA_V), lambda b, h, i: (b, h, i, 0)),
        out_shape=jax.ShapeDtypeStruct((bsz, nh, nq, MLA_V), BF16),
        scratch_shapes=[pltpu.VMEM((tk, tq), F32)] * 3 + [pltpu.VMEM((tk, tq), BF16)] * 3
        + [pltpu.VMEM((VT_ROWS, tq), F32)],
        compiler_params=_cparams(("parallel", "parallel", "arbitrary")),
        name="attention",
    )(q, k, vt)


def _cmul(ar, ai, br, bi):
    return ar * br - ai * bi, ar * bi + ai * br


def _s5_kernel(u_ref, h0_ref, bm_ref, cm_ref, wn_ref, wp_ref, l1_ref, tri_ref, y_ref, hf_ref,
               carry_ref, h0cat_ref, h1cat_ref, *, tc, nr, ns, reverse):
    i = pl.program_id(1)

    @pl.when(i == 0)
    def _():
        carry_ref[...] = h0_ref[0]

    u = u_ref[0]
    tri = tri_ref[...]
    order = range(nr - 1, -1, -1) if reverse else range(nr)
    edge = 0 if reverse else tc - 1
    ys = []
    for hf, hcat_ref in enumerate((h0cat_ref, h1cat_ref)):
        bu = _dot(u[:, hf * S5_HALF:(hf + 1) * S5_HALF], bm_ref[hf])
        hr_in, hi_in = carry_ref[2 * hf:2 * hf + 1, :], carry_ref[2 * hf + 1:2 * hf + 2, :]
        for c in order:
            rows = slice(c * tc, (c + 1) * tc)
            xr, xi = _cmul(wn_ref[hf, 0], wn_ref[hf, 1], bu[rows, :S5_HSTATE], bu[rows, S5_HSTATE:])
            s = _dot(tri, jnp.concatenate([xr, xi], axis=1).astype(BF16))
            cr, ci = _cmul(l1_ref[hf, 0], l1_ref[hf, 1], hr_in, hi_in)
            hr, hi = _cmul(wp_ref[hf, 0], wp_ref[hf, 1], s[:, :S5_HSTATE] + cr, s[:, S5_HSTATE:] + ci)
            hcat_ref[rows, :S5_HSTATE] = hr.astype(BF16)
            hcat_ref[rows, S5_HSTATE:] = hi.astype(BF16)
            hr_in, hi_in = hr[edge:edge + 1], hi[edge:edge + 1]
        carry_ref[2 * hf:2 * hf + 1, :] = hr_in
        carry_ref[2 * hf + 1:2 * hf + 2, :] = hi_in
        ys.append(_dot(hcat_ref[...], cm_ref[hf]))
    y_ref[0] = jnp.concatenate(ys, axis=1)

    @pl.when(i == ns - 1)
    def _():
        hf_ref[0] = carry_ref[...]


def _s5_scan_dir(z, h0, tabs, reverse):
    bm, cm, wn, wp, l1, tri = tabs
    bsz, n, _ = z.shape
    tc = S5_CHUNK
    nr = min(S5_CHUNKS_PER_STEP, n // tc)
    rows = nr * tc
    ns = n // rows
    zb = Z_S5 // S5_WIDTH
    blk = (lambda i: ns - 1 - i) if reverse else (lambda i: i)
    full = lambda shape: pl.BlockSpec(shape, lambda b, i: (0,) * len(shape))
    return pl.pallas_call(
        functools.partial(_s5_kernel, tc=tc, nr=nr, ns=ns, reverse=reverse),
        grid=(bsz, ns),
        in_specs=[
            pl.BlockSpec((1, rows, S5_WIDTH), lambda b, i: (b, blk(i), zb)),
            pl.BlockSpec((1, 4, S5_HSTATE), lambda b, i: (b, 0, 0)),
            full((2, S5_HALF, 2 * S5_HSTATE)),
            full((2, 2 * S5_HSTATE, S5_HALF)),
            full((2, 2, tc, S5_HSTATE)),
            full((2, 2, tc, S5_HSTATE)),
            full((2, 2, 1, S5_HSTATE)),
            full((tc, tc)),
        ],
        out_specs=[
            pl.BlockSpec((1, rows, S5_WIDTH), lambda b, i: (b, blk(i), 0)),
            pl.BlockSpec((1, 4, S5_HSTATE), lambda b, i: (b, 0, 0)),
        ],
        out_shape=[
            jax.ShapeDtypeStruct((bsz, n, S5_WIDTH), F32),
            jax.ShapeDtypeStruct((bsz, 4, S5_HSTATE), F32),
        ],
        scratch_shapes=[pltpu.VMEM((4, S5_HSTATE), F32),
                        pltpu.VMEM((rows, 2 * S5_HSTATE), BF16), pltpu.VMEM((rows, 2 * S5_HSTATE), BF16)],
        compiler_params=_cparams(("parallel", "arbitrary")),
        name="s5_scan",
    )(z, h0, bm, cm, wn, wp, l1, tri)


def _s5_scan(z, h0, tabs):
    outs = [_s5_scan_dir(z, h0[:, d], [t[d] for t in tabs], reverse=bool(d)) for d in range(2)]
    return outs[0][0], outs[1][0], jnp.stack([outs[0][1], outs[1][1]], axis=1)


def _s5_tables(lam_re, lam_im, log_dt, b_re, b_im, c_re, c_im):
    tc = S5_CHUNK
    dt = jnp.exp(log_dt)[..., None]
    zr, zi = lam_re * dt, lam_im * dt
    mag = jnp.exp(zr)
    lbr, lbi = mag * jnp.cos(zi), mag * jnp.sin(zi)
    den = lam_re * lam_re + lam_im * lam_im
    nr, ni = lbr - 1.0, lbi
    cfr = (nr * lam_re + ni * lam_im) / den
    cfi = (ni * lam_re - nr * lam_im) / den
    bbr = cfr[..., None] * b_re - cfi[..., None] * b_im
    bbi = cfr[..., None] * b_im + cfi[..., None] * b_re
    eye = jnp.eye(S5_GROUP, dtype=F32)

    def blockdiag_in(b):
        b = b.reshape(2, 2, S5_GROUP, S5_STATE, S5_GROUP)
        return jnp.einsum('gk,dhgpn->dhgnkp', eye, b).reshape(2, 2, S5_HALF, S5_HSTATE)

    def blockdiag_out(c):
        c = c.reshape(2, 2, S5_GROUP, S5_GROUP, S5_STATE)
        return jnp.einsum('gk,dhgnp->dhgpkn', eye, c).reshape(2, 2, S5_HSTATE, S5_HALF)

    bm = jnp.concatenate([blockdiag_in(bbr), blockdiag_in(bbi)], axis=-1).astype(BF16)
    cm = jnp.concatenate([blockdiag_out(c_re), blockdiag_out(-c_im)], axis=-2).astype(BF16)

    def powers(k):
        zr_ = zr.reshape(2, 2, 1, S5_HSTATE)
        zi_ = zi.reshape(2, 2, 1, S5_HSTATE)
        kk = k[:, None, :, None]
        m = jnp.exp(kk * zr_)
        return jnp.stack([m * jnp.cos(kk * zi_), m * jnp.sin(kk * zi_)], axis=2)

    mid = tc // 2
    t = jnp.arange(tc, dtype=F32) - mid
    wn = powers(jnp.stack([-t, -t[::-1]]))
    wp = powers(jnp.stack([t, t[::-1]]))
    l1 = powers(jnp.full((2, 1), mid + 1, F32))
    r = jnp.arange(tc)
    tri = jnp.stack([r[:, None] >= r[None, :], r[:, None] <= r[None, :]]).astype(BF16)
    return bm, cm, wn, wp, l1, tri


HALO = 16


def _sconv_kernel(z_ref, zp_ref, zn_ref, w_ref, b_ref, v_ref, g1_ref, g2_ref, *, nt):
    i = pl.program_id(1)
    z = z_ref[0].astype(F32)
    tm = z.shape[0]
    row = lax.broadcasted_iota(jnp.int32, z.shape, 0)
    prev = jnp.where(i > 0, zp_ref[0, HALO - 1:HALO, :].astype(F32), 0.0)
    nxt = jnp.where(i < nt - 1, zn_ref[0, 0:1, :].astype(F32), 0.0)
    up = jnp.where(row == 0, prev, pltpu.roll(z, 1, 0))
    dn = jnp.where(row == tm - 1, nxt, pltpu.roll(z, tm - 1, 0))
    u = up * w_ref[0:1, :] + z * w_ref[1:2, :] + dn * w_ref[2:3, :] + b_ref[...]
    for k, o_ref in enumerate((v_ref, g1_ref, g2_ref)):
        o_ref[0] = u[:, k * HY_WIDTH:(k + 1) * HY_WIDTH]


def _short_conv(z, w, b):
    bsz, n, _ = z.shape
    cw = 3 * HY_WIDTH
    tm = min(n, 512)
    nt = n // tm
    zb = Z_HY // cw
    rh = tm // HALO
    out = pl.BlockSpec((1, tm, HY_WIDTH), lambda b_, i: (b_, i, 0))
    return pl.pallas_call(
        functools.partial(_sconv_kernel, nt=nt),
        grid=(bsz, nt),
        in_specs=[
            pl.BlockSpec((1, tm, cw), lambda b_, i: (b_, i, zb)),
            pl.BlockSpec((1, HALO, cw), lambda b_, i: (b_, jnp.maximum(i * rh - 1, 0), zb)),
            pl.BlockSpec((1, HALO, cw), lambda b_, i: (b_, jnp.minimum((i + 1) * rh, n // HALO - 1), zb)),
            pl.BlockSpec((3, cw), lambda b_, i: (0, 0)),
            pl.BlockSpec((1, cw), lambda b_, i: (0, 0)),
        ],
        out_specs=[out] * 3,
        out_shape=[jax.ShapeDtypeStruct((bsz, n, HY_WIDTH), F32)] * 3,
        compiler_params=_cparams(("parallel", "parallel")),
        name="short_conv",
    )(z, z, z, w, b)


def _filt_kernel(f_ref, w1_ref, b1_ref, w2_ref, b2_ref, w3h_ref, w3l_ref, fq_ref, dl_ref, o_ref, *, n_tok):
    z = f_ref[...]
    tm = z.shape[0]
    fq = fq_ref[...]
    hid = jnp.sin(fq * (jnp.dot(z, w1_ref[...], precision=HIGHEST, preferred_element_type=F32) + b1_ref[...]))
    hid = jnp.sin(fq * (jnp.dot(hid, w2_ref[...], precision=HIGHEST, preferred_element_type=F32) + b2_ref[...]))
    hid_hi = hid.astype(BF16)
    hid_lo = (hid - hid_hi.astype(F32)).astype(BF16)

    def filters(rows, d):
        wh, wl = w3h_ref[d], w3l_ref[d]
        f = _dot(hid_hi[rows], wh) + (_dot(hid_hi[rows], wl) + _dot(hid_lo[rows], wh))
        return f * (jnp.exp(-z[rows, 0:1] * dl_ref[d]) + HY_DECAY_SHIFT)

    i = pl.program_id(0)
    filt = filters(slice(None), jnp.where(i * tm >= n_tok, 1, 0))
    m = i * tm + lax.broadcasted_iota(jnp.int32, filt.shape, 0)
    filt = jnp.where(m == n_tok, 0.0, filt)
    for o in range(HY_ORDER):
        o_ref[o] = filt[:, o * HY_WIDTH:(o + 1) * HY_WIDTH]

    @pl.when(i == 0)
    def _():
        head = slice(0, HALO)
        lag0 = filters(head, 1)
        lag0 = jnp.where(lax.broadcasted_iota(jnp.int32, lag0.shape, 0) == 0, lag0, 0.0)
        for o in range(HY_ORDER):
            o_ref[o, head, :] += lag0[:, o * HY_WIDTH:(o + 1) * HY_WIDTH]


def _hyena_filter_feats(n_tok):
    m = jnp.arange(2 * n_tok)
    lag = jnp.where(m < n_tok, m, jnp.where(m > n_tok, 2 * n_tok - m, 0))
    t = (lag.astype(F32) / (n_tok - 1))[:, None]
    w = (2.0 * math.pi * lag.astype(F32) / n_tok)[:, None]
    bands = jnp.linspace(1e-4, HY_BANDS - 1, HY_BANDS, dtype=F32)[None, :]
    feats = jnp.concatenate([t, jnp.cos(bands * w), -jnp.sin(bands * w)], axis=-1)
    return jnp.pad(feats, ((0, 0), (0, HY_POS_PAD - HY_POS_DIM)))


def _hyena_filters(n_tok, w1, b1, w2, b2, w3, freq):
    feats = _hyena_filter_feats(n_tok)
    deltas = jnp.abs(jnp.linspace(math.log(HY_DECAY_TARGET) / HY_DECAY_SHORT,
                                  math.log(HY_DECAY_TARGET) / HY_DECAY_LONG, HY_FILTER_OUT, dtype=F32))[None, :]
    w1p = jnp.pad(w1, ((0, HY_POS_PAD - HY_POS_DIM), (0, 0)))

    def by_direction(a):
        a = a.reshape(a.shape[:-1] + (HY_ORDER, 2, HY_WIDTH))
        return jnp.moveaxis(a, -2, 0).reshape((2,) + a.shape[:-3] + (HY_ORDER * HY_WIDTH,))

    w3, deltas = by_direction(w3), by_direction(deltas)
    w3_hi = w3.astype(BF16)
    w3_lo = (w3 - w3_hi.astype(F32)).astype(BF16)
    n2 = 2 * n_tok
    tm = min(n_tok, 512)
    full = lambda shape: pl.BlockSpec(shape, lambda i: (0,) * len(shape))
    hh = HY_FILTER_HIDDEN
    wd = HY_ORDER * HY_WIDTH
    return pl.pallas_call(
        functools.partial(_filt_kernel, n_tok=n_tok),
        grid=(n2 // tm,),
        in_specs=[
            pl.BlockSpec((tm, HY_POS_PAD), lambda i: (i, 0)),
            full((HY_POS_PAD, hh)), full((1, hh)), full((hh, hh)), full((1, hh)),
            full((2, hh, wd)), full((2, hh, wd)), full((1, hh)), full((2, 1, wd)),
        ],
        out_specs=pl.BlockSpec((HY_ORDER, tm, HY_WIDTH), lambda i: (0, i, 0)),
        out_shape=jax.ShapeDtypeStruct((HY_ORDER, n2, HY_WIDTH), F32),
        compiler_params=_cparams(("parallel",)),
        name="hyena_filter",
    )(feats, w1p, b1[None], w2, b2[None], w3_hi, w3_lo, freq[None], deltas)


def _fft_split(n_fft):
    if n_fft <= 1024:
        return n_fft, 1
    n1 = 1 << (int(math.log2(n_fft)) // 2)
    return n1, n_fft // n1


def _cis(num, den, sign):
    ang = (2.0 * math.pi / den) * num.astype(F32)
    return jnp.cos(ang), sign * jnp.sin(ang)


def _fft_tables(n_fft):
    n1, n2 = _fft_split(n_fft)
    a = jnp.arange(n1)
    f1r, f1i = _cis((a[:, None] * a[None, :]) % n1, n1, -1.0)
    s1 = jnp.concatenate([f1r, f1i], axis=0).astype(BF16)
    half = n1 // 2
    s3 = (jnp.concatenate([f1r[:half], -f1i[:half]], axis=0) / n_fft).astype(BF16)
    if n2 == 1:
        return s1, s3, None, None
    b = jnp.arange(n2)
    f2r, f2i = _cis((b[:, None] * b[None, :]) % n2, n2, -1.0)
    twr, twi = _cis((jnp.arange(n1)[:, None] * b[None, :]) % n_fft, n_fft, -1.0)
    fr, fi = _cmul(f2r[None], f2i[None], twr[:, None, :], twi[:, None, :])
    ft = jnp.concatenate([fr, fi], axis=1).astype(BF16)
    frt, fit = _cmul(f2r[None], f2i[None], twr[:, :, None], twi[:, :, None])
    it = jnp.concatenate([frt, -fit], axis=1).astype(BF16)
    return s1, s3, ft, it


def _stacked_cdot(s, xr, xi, m):
    p = _dot(s, xr.astype(BF16))
    if xi is None:
        return p[:m], p[m:]
    q = _dot(s, xi.astype(BF16))
    return p[:m] - q[m:], q[:m] + p[m:]


def _fft_a_kernel(*refs, parts, n1, has_g):
    if has_g:
        x_ref, s_ref, g_ref, y_ref = refs
    else:
        x_ref, s_ref, y_ref = refs
    yr, yi = _stacked_cdot(s_ref[...], x_ref[0], x_ref[1] if parts == 2 else None, n1)
    if has_g:
        yr, yi = _cmul(yr, yi, g_ref[0], g_ref[1])
    y_ref[0] = yr
    y_ref[1] = yi


def _fft_a(x, s1, g=None):
    parts, a_rows, cols = x.shape
    n1 = s1.shape[0] // 2
    tc = min(cols, 2048)
    in_specs = [
        pl.BlockSpec((parts, a_rows, tc), lambda j: (0, 0, j)),
        pl.BlockSpec((2 * n1, a_rows), lambda j: (0, 0)),
    ]
    args = [x, s1[:, :a_rows]]
    if g is not None:
        in_specs.append(pl.BlockSpec((2, n1, tc), lambda j: (0, 0, j)))
        args.append(g)
    return pl.pallas_call(
        functools.partial(_fft_a_kernel, parts=parts, n1=n1, has_g=g is not None),
        grid=(cols // tc,),
        in_specs=in_specs,
        out_specs=pl.BlockSpec((2, n1, tc), lambda j: (0, 0, j)),
        out_shape=jax.ShapeDtypeStruct((2, n1, cols), F32),
        compiler_params=_cparams(("parallel",)),
        name="fft_stage_a",
    )(*args)


def _fft_b_kernel(*refs, n2, inverse):
    if inverse:
        y_ref, ft_ref, it_ref, g_ref, o_ref = refs
    else:
        y_ref, ft_ref, o_ref = refs
    xr, xi = _stacked_cdot(ft_ref[0], y_ref[0, 0], y_ref[1, 0], n2)
    if inverse:
        zr, zi = _cmul(xr, xi, g_ref[0, 0], g_ref[1, 0])
        xr, xi = _stacked_cdot(it_ref[0], zr, zi, n2)
    o_ref[0, 0] = xr.astype(o_ref.dtype)
    o_ref[1, 0] = xi.astype(o_ref.dtype)


def _fft_b(y, ft, it=None, g=None):
    _, n1, n2, ch = y.shape
    inverse = it is not None
    blk = pl.BlockSpec((2, 1, n2, ch), lambda c: (0, c, 0, 0))
    mat = pl.BlockSpec((1, 2 * n2, n2), lambda c: (c, 0, 0))
    in_specs, args = [blk, mat], [y, ft]
    if inverse:
        in_specs += [mat, blk]
        args += [it, g]
    return pl.pallas_call(
        functools.partial(_fft_b_kernel, n2=n2, inverse=inverse),
        grid=(n1,),
        in_specs=in_specs,
        out_specs=blk,
        out_shape=jax.ShapeDtypeStruct(y.shape, BF16 if inverse else F32),
        compiler_params=_cparams(("parallel",)),
        name="fft_stage_b",
    )(*args)


def _fft_c_kernel(y_ref, s_ref, x_ref, g_ref, b_ref, o_ref, *, half):
    cr, ci = _stacked_cdot(s_ref[...], y_ref[0], y_ref[1], half)
    bias = b_ref[...]
    o_ref[0] = g_ref[0] * (cr + bias * x_ref[0])
    o_ref[1] = g_ref[1] * (ci + bias * x_ref[1])


def _fft_c(y, s3, x, gate, bias_cols):
    _, n1, cols = y.shape
    half = n1 // 2
    tc = min(cols, 2048)
    io = pl.BlockSpec((2, half, tc), lambda j: (0, 0, j))
    return pl.pallas_call(
        functools.partial(_fft_c_kernel, half=half),
        grid=(cols // tc,),
        in_specs=[
            pl.BlockSpec((2, n1, tc), lambda j: (0, 0, j)),
            pl.BlockSpec((n1, n1), lambda j: (0, 0)),
            io, io,
            pl.BlockSpec((1, tc), lambda j: (0, j)),
        ],
        out_specs=io,
        out_shape=jax.ShapeDtypeStruct((2, half, cols), F32),
        compiler_params=_cparams(("parallel",)),
        name="fft_stage_c",
    )(y, s3, x, gate, bias_cols)


FFT_BT = 16


def _to_fine_major(x):
    return pltpu.einshape("abc->bac", x)


def _fft_a4_kernel(x_ref, s_ref, y_ref, *, parts, n1):
    s = s_ref[...]
    xr = _to_fine_major(x_ref[0].astype(BF16))
    xi = _to_fine_major(x_ref[1].astype(BF16)) if parts == 2 else None
    out = [_stacked_cdot(s, xr[b], None if xi is None else xi[b], n1) for b in range(FFT_BT)]
    y_ref[0] = pltpu.einshape("bac->abc", jnp.stack([o[0] for o in out])).astype(BF16)
    y_ref[1] = pltpu.einshape("bac->abc", jnp.stack([o[1] for o in out])).astype(BF16)


def _fft_a4(x, s1):
    parts, a_rows, n2, ch = x.shape
    n1 = s1.shape[0] // 2
    return pl.pallas_call(
        functools.partial(_fft_a4_kernel, parts=parts, n1=n1),
        grid=(n2 // FFT_BT,),
        in_specs=[
            pl.BlockSpec((parts, a_rows, FFT_BT, ch), lambda j: (0, 0, j, 0)),
            pl.BlockSpec((2 * n1, a_rows), lambda j: (0, 0)),
        ],
        out_specs=pl.BlockSpec((2, n1, FFT_BT, ch), lambda j: (0, 0, j, 0)),
        out_shape=jax.ShapeDtypeStruct((2, n1, n2, ch), BF16),
        compiler_params=_cparams(("parallel",)),
        name="fft_stage_a",
    )(x, s1[:, :a_rows])


def _fft_c4_kernel(y_ref, s_ref, x_ref, g_ref, b_ref, o_ref, c_ref, *, half):
    s = s_ref[...]
    yr = _to_fine_major(y_ref[0].astype(BF16))
    yi = _to_fine_major(y_ref[1].astype(BF16))
    out = [_stacked_cdot(s, yr[b], yi[b], half) for b in range(FFT_BT)]
    c_ref[0] = pltpu.einshape("bac->abc", jnp.stack([o[0] for o in out]))
    c_ref[1] = pltpu.einshape("bac->abc", jnp.stack([o[1] for o in out]))
    o_ref[...] = g_ref[...] * (c_ref[...] + b_ref[...][None, None] * x_ref[...])


def _fft_c4(y, s3, x, gate, bias):
    _, n1, n2, ch = y.shape
    half = n1 // 2
    io = pl.BlockSpec((2, half, FFT_BT, ch), lambda j: (0, 0, j, 0))
    return pl.pallas_call(
        functools.partial(_fft_c4_kernel, half=half),
        grid=(n2 // FFT_BT,),
        in_specs=[
            pl.BlockSpec((2, n1, FFT_BT, ch), lambda j: (0, 0, j, 0)),
            pl.BlockSpec((n1, n1), lambda j: (0, 0)),
            io, io,
            pl.BlockSpec((1, ch), lambda j: (0, 0)),
        ],
        out_specs=io,
        out_shape=jax.ShapeDtypeStruct((2, half, n2, ch), F32),
        scratch_shapes=[pltpu.VMEM((2, half, FFT_BT, ch), F32)],
        compiler_params=_cparams(("parallel",)),
        name="fft_stage_c",
    )(y, s3, x, gate, bias)


def _hyena_mixer(v, gates, gfilt, bias):
    bsz, n, ch = v.shape
    assert bsz == 2, "the complex packing of the long convolution pairs exactly two batch rows"
    n_fft = 2 * n
    n1, n2 = _fft_split(n_fft)
    s1, s3, ft, it = _fft_tables(n_fft)
    if n2 == 1:
        y = v
        for o in range(HY_ORDER):
            gspec = _fft_a(gfilt[o][None], s1)
            y = _fft_c(_fft_a(y, s1, gspec), s3, y, gates[o], bias[o][None])
        return y
    half = n1 // 2
    y = v.reshape(2, half, n2, ch)
    for o in range(HY_ORDER):
        gspec = _fft_b(_fft_a4(gfilt[o].reshape(1, n1, n2, ch), s1), ft)
        spec = _fft_b(_fft_a4(y, s1), ft, it, gspec)
        y = _fft_c4(spec, s3, y, gates[o].reshape(2, half, n2, ch), bias[o][None])
    return y.reshape(2, n, ch)


def _gelu_tanh(x):
    return 0.5 * x * (1.0 + jnp.tanh(math.sqrt(2.0 / math.pi) * (x + 0.044715 * (x * x * x))))


def _merge_kernel(x_ref, g0_ref, g1_ref, g2_ref, a_ref, yf_ref, yb_ref, u_ref, e_ref, d_ref,
                  wglu_ref, wm_ref, ws_ref, wh_ref, wo_ref, gt_ref, o_ref):
    att = _dot(jnp.concatenate([a_ref[0, h] for h in range(MLA_HEADS)], axis=-1), wm_ref[...])
    y = yf_ref[0] + yb_ref[0] + d_ref[...] * u_ref[0].astype(F32)
    gy = _gelu_tanh(y)
    s5 = gy * jax.nn.sigmoid(_dot(gy.astype(BF16), wglu_ref[...]))
    s5 = _dot(s5.astype(BF16), ws_ref[...])
    hy = _dot(e_ref[0].astype(BF16), wh_ref[...])
    merged = jax.nn.sigmoid(g0_ref[0].astype(F32)) * att
    merged = merged + jax.nn.sigmoid(g1_ref[0].astype(F32)) * s5
    merged = merged + jax.nn.sigmoid(g2_ref[0].astype(F32)) * hy
    o_ref[0] = x_ref[0] + gt_ref[0] * _dot(merged.astype(BF16), wo_ref[...])


def _merge(x, z, att, y5, e, s5_d, w_glu, w_mla, w_s5, w_hy, w_out, gate):
    bsz, n, d = x.shape
    tm = min(n, 512)
    full = lambda shape: pl.BlockSpec(shape, lambda b, i: (0,) * len(shape))
    zs5 = Z_S5 // S5_WIDTH
    return pl.pallas_call(
        _merge_kernel,
        grid=(bsz, n // tm),
        in_specs=[
            pl.BlockSpec((1, tm, d), lambda b, i: (b, i, 0)),
            pl.BlockSpec((1, tm, d), lambda b, i: (b, i, 0)),
            pl.BlockSpec((1, tm, d), lambda b, i: (b, i, 1)),
            pl.BlockSpec((1, tm, d), lambda b, i: (b, i, 2)),
            pl.BlockSpec((1, MLA_HEADS, tm, MLA_V), lambda b, i: (b, 0, i, 0)),
            pl.BlockSpec((1, tm, S5_WIDTH), lambda b, i: (b, i, 0)),
            pl.BlockSpec((1, tm, S5_WIDTH), lambda b, i: (b, i, 0)),
            pl.BlockSpec((1, tm, S5_WIDTH), lambda b, i: (b, i, zs5)),
            pl.BlockSpec((1, tm, HY_WIDTH), lambda b, i: (b, i, 0)),
            full((1, S5_WIDTH)),
            full((S5_WIDTH, S5_WIDTH)),
            full((MLA_HEADS * MLA_V, d)),
            full((S5_WIDTH, d)),
            full((HY_WIDTH, d)),
            full((d, d)),
            pl.BlockSpec((1, 1, d), lambda b, i: (b, 0, 0)),
        ],
        out_specs=pl.BlockSpec((1, tm, d), lambda b, i: (b, i, 0)),
        out_shape=jax.ShapeDtypeStruct((bsz, n, d), F32),
        compiler_params=_cparams(("parallel", "parallel")),
        name="merge",
    )(x, z, z, z, att, y5[0], y5[1], z, e, s5_d, w_glu, w_mla, w_s5, w_hy, w_out, gate)


def _ffn_kernel(x_ref, g_ref, sh_ref, sc_ref, gt_ref, wg_ref, wu_ref, wd_ref, fg_ref, o_ref, h_ref, acc_ref,
                *, nk, final):
    k = pl.program_id(2)

    @pl.when(k == 0)
    def _():
        h_ref[...] = _norm_mod(x_ref[0], g_ref[...], sh_ref[0], sc_ref[0]).astype(BF16)
        acc_ref[...] = jnp.zeros_like(acc_ref)

    h = h_ref[...]
    act = jax.nn.silu(_dot(h, wg_ref[...])) * _dot(h, wu_ref[...])
    acc_ref[...] += _dot(act.astype(BF16), wd_ref[...])

    @pl.when(k == nk - 1)
    def _():
        r = x_ref[0] + gt_ref[0] * acc_ref[...]
        o_ref[0] = _rms(r, fg_ref[...]) if final else r


def _ffn(x, g, shift, scale, gate, w_g, w_u, w_d, final_g, final):
    bsz, n, d = x.shape
    dff = w_g.shape[1]
    tm = min(n, 1024)
    tf = 256
    nk = dff // tf
    vec = pl.BlockSpec((1, 1, d), lambda b, i, k: (b, 0, 0))
    row = pl.BlockSpec((1, d), lambda b, i, k: (0, 0))
    return pl.pallas_call(
        functools.partial(_ffn_kernel, nk=nk, final=final),
        grid=(bsz, n // tm, nk),
        in_specs=[
            pl.BlockSpec((1, tm, d), lambda b, i, k: (b, i, 0)),
            row, vec, vec, vec,
            pl.BlockSpec((d, tf), lambda b, i, k: (0, k)),
            pl.BlockSpec((d, tf), lambda b, i, k: (0, k)),
            pl.BlockSpec((tf, d), lambda b, i, k: (k, 0)),
            row,
        ],
        out_specs=pl.BlockSpec((1, tm, d), lambda b, i, k: (b, i, 0)),
        out_shape=jax.ShapeDtypeStruct((bsz, n, d), F32),
        scratch_shapes=[pltpu.VMEM((tm, d), BF16), pltpu.VMEM((tm, d), F32)],
        compiler_params=_cparams(("parallel", "parallel", "arbitrary")),
        name="ffn",
    )(x, g, shift, scale, gate, w_g, w_u, w_d, final_g)


def _rope_tables(n_tok):
    rows = n_tok // GRID_W
    row = jnp.broadcast_to(jnp.arange(rows, dtype=F32)[:, None], (rows, GRID_W)).reshape(-1)
    col = jnp.broadcast_to(jnp.arange(GRID_W, dtype=F32)[None, :], (rows, GRID_W)).reshape(-1)
    n_freq = MLA_ROPE // 4
    inv = ROPE_BASE ** (-jnp.arange(n_freq, dtype=F32) / n_freq)
    ang = jnp.concatenate([row[:, None] * inv, col[:, None] * inv], axis=-1)
    cos, sin = jnp.cos(ang), jnp.sin(ang)
    pad = jnp.zeros((n_tok, HEAD_W - MLA_ROPE), F32)
    return jnp.concatenate([cos, cos, pad], axis=-1), jnp.concatenate([-sin, sin, pad], axis=-1)


def _identity_rope_tables(n_tok):
    one = jnp.ones((n_tok, MLA_ROPE), F32)
    pad = jnp.zeros((n_tok, HEAD_W - MLA_ROPE), F32)
    return jnp.concatenate([one, pad], axis=-1), jnp.zeros((n_tok, HEAD_W), F32)


def _layout_w_in(w):
    kr = w[:, OFF_KR:OFF_S5]
    x1, x2 = kr[:, 0::2], kr[:, 1::2]
    pad = jnp.zeros((w.shape[0], HEAD_W - 3 * (MLA_ROPE // 2)), w.dtype)
    return jnp.concatenate([w[:, OFF_GATE:], w[:, OFF_HY:OFF_GATE], w[:, OFF_S5:OFF_HY],
                            w[:, OFF_CQ:OFF_CKV], w[:, OFF_CKV:OFF_KR], x1, x2, x1, pad], axis=1).astype(BF16)


def _layout_w_uq(w):
    w = w.reshape(MLA_Q_LORA, MLA_HEADS, MLA_NOPE + MLA_ROPE)
    nope, rope = w[..., :MLA_NOPE], w[..., MLA_NOPE:]
    x1, x2 = rope[..., 0::2], rope[..., 1::2]
    z32 = jnp.zeros_like(rope)
    wq = jnp.concatenate([x1, x2, z32, nope], axis=-1)
    wqs = jnp.concatenate([x2, x1, z32, jnp.zeros_like(nope)], axis=-1)
    shape = (MLA_Q_LORA, MLA_HEADS * HEAD_W)
    return wq.reshape(shape).astype(BF16), wqs.reshape(shape).astype(BF16)


def _layout_w_ukv(w):
    w = w.reshape(MLA_KV_LORA, MLA_HEADS, MLA_NOPE + MLA_V)
    nope, val = w[..., :MLA_NOPE], w[..., MLA_NOPE:]
    wk = jnp.concatenate([jnp.zeros_like(nope), nope], axis=-1).reshape(MLA_KV_LORA, MLA_HEADS * HEAD_W)
    wvt = jnp.concatenate([val, jnp.zeros_like(val[..., :VT_ROWS - MLA_V])], axis=-1).reshape(
        MLA_KV_LORA, MLA_HEADS * VT_ROWS).T
    return wk.astype(BF16), wvt.astype(BF16)


def kernel(x, c, ctx, c_ctx, ada_w, ada_b, norm_mix, w_in, mla_q_norm, mla_w_uq, mla_kv_norm, mla_w_ukv,
           s5_lam_re, s5_lam_im, s5_log_dt, s5_b_re, s5_b_im, s5_c_re, s5_c_im, s5_d, s5_w_glu,
           hy_conv_w, hy_conv_b, hy_f_w1, hy_f_b1, hy_f_w2, hy_f_b2, hy_f_w3, hy_f_freq, hy_bias,
           w_branch_mla, w_branch_s5, w_branch_hy, w_out, norm_ffn, ffn_w_gu, ffn_w_down, final_norm):
    bsz, n_tok, d = x.shape
    n_ctx = ctx.shape[1]
    depth = ada_w.shape[0]
    rope_x = _rope_tables(n_tok)
    rope_c = _identity_rope_tables(n_ctx)
    h_zero = jnp.zeros((bsz, 2, 4, S5_HSTATE), F32)

    for i in range(depth):
        last = i == depth - 1
        mx = (jax.nn.silu(c) @ ada_w[i] + ada_b[i]).reshape(bsz, N_MOD, 1, d)
        mc = jnp.broadcast_to((jax.nn.silu(c_ctx) @ ada_w[i] + ada_b[i]).reshape(1, N_MOD, 1, d),
                              (bsz, N_MOD, 1, d))
        w_z = _layout_w_in(w_in[i])
        wq, wqs = _layout_w_uq(mla_w_uq[i])
        wk, wvt = _layout_w_ukv(mla_w_ukv[i])
        qn, kvn = mla_q_norm[i][None], mla_kv_norm[i][None]
        s5_tabs = _s5_tables(s5_lam_re[i], s5_lam_im[i], s5_log_dt[i], s5_b_re[i], s5_b_im[i],
                             s5_c_re[i], s5_c_im[i])
        fparams = (hy_f_w1[i], hy_f_b1[i], hy_f_w2[i], hy_f_b2[i], hy_f_w3[i], hy_f_freq[i])
        merge_w = (s5_d[i][None], s5_w_glu[i].astype(BF16),
                   w_branch_mla[i].astype(BF16),
                   w_branch_s5[i].astype(BF16), w_branch_hy[i].astype(BF16), w_out[i].astype(BF16))
        w_g = ffn_w_gu[i][:, :D_FF].astype(BF16)
        w_u = ffn_w_gu[i][:, D_FF:].astype(BF16)
        w_d = ffn_w_down[i].astype(BF16)
        g_mix, g_ffn = norm_mix[i][None], norm_ffn[i][None]

        zx = _proj_in(x, g_mix, mx[:, 0], mx[:, 1], w_z)
        zc = _proj_in(ctx, g_mix, mc[:, 0], mc[:, 1], w_z)

        n_keys = n_tok + n_ctx
        q_x, k_all, vt_all = _mla_prep(zx, qn, kvn, wq, wqs, wk, wvt, *rope_x, n_keys, 0)
        q_c, k_all, vt_all = _mla_prep(zc, qn, kvn, wq, wqs, wk, wvt, *rope_c, n_keys, n_tok, (k_all, vt_all))
        a_x = _attention(q_x, k_all, vt_all, n_keys)

        *y5_c, finals = _s5_scan(zc, h_zero, s5_tabs)
        *y5_x, _ = _s5_scan(zx, finals, s5_tabs)

        v_x, g1_x, g2_x = _short_conv(zx, hy_conv_w[i], hy_conv_b[i][None])
        e_x = _hyena_mixer(v_x, (g1_x, g2_x), _hyena_filters(n_tok, *fparams), hy_bias[i])

        x = _merge(x, zx, a_x, y5_x, e_x, *merge_w, mx[:, 2])
        x = _ffn(x, g_ffn, mx[:, 3], mx[:, 4], mx[:, 5], w_g, w_u, w_d, final_norm[None], last)

        if not last:
            a_c = _attention(q_c, k_all, vt_all, n_ctx, n_tok // n_ctx)
            v_c, g1_c, g2_c = _short_conv(zc, hy_conv_w[i], hy_conv_b[i][None])
            e_c = _hyena_mixer(v_c, (g1_c, g2_c), _hyena_filters(n_ctx, *fparams), hy_bias[i])
            ctx = _merge(ctx, zc, a_c, y5_c, e_c, *merge_w, mc[:, 2])
            ctx = _ffn(ctx, g_ffn, mc[:, 3], mc[:, 4], mc[:, 5], w_g, w_u, w_d, final_norm[None], False)
    return x
```

```python
import functools
import math

import jax
import jax.numpy as jnp
import numpy as np
from jax import lax
from jax.experimental import pallas as pl
from jax.experimental.pallas import tpu as pltpu

F32 = jnp.float32
BF16 = jnp.bfloat16
HIGHEST = lax.Precision.HIGHEST

D_MODEL = 1024
GRID_W = 64
NORM_EPS = 1e-6
N_MOD = 6

MLA_HEADS = 8
MLA_NOPE = 64
MLA_ROPE = 32
MLA_V = 64
MLA_Q_LORA = 256
MLA_KV_LORA = 128
ROPE_BASE = 10000.0
HEAD_W = 128
MLA_PREP_ROWS = 512
VT_ROWS = 80
QK_SCALE_LOG2 = (MLA_NOPE + MLA_ROPE) ** -0.5 * math.log2(math.e)
ATTN_SUB = 128
ATTN_LAG = 2
ATTN_UNROLL = 12

S5_WIDTH = 512
S5_GROUP = 16
S5_GROUPS = 32
S5_STATE = 64
S5_HALF = 256
S5_HSTATE = 1024
S5_CHUNK = 128
S5_CHUNKS_PER_STEP = 4

HY_WIDTH = 512
HY_ORDER = 2
HY_BANDS = 16
HY_POS_DIM = 1 + 2 * HY_BANDS
HY_POS_PAD = 64
HY_FILTER_HIDDEN = 64
HY_FILTER_OUT = HY_ORDER * 2 * HY_WIDTH
HY_DECAY_TARGET = 1e-2
HY_DECAY_SHORT = 0.3
HY_DECAY_LONG = 1.5
HY_DECAY_SHIFT = 0.05

D_FF = 2816

OFF_CQ = 0
OFF_CKV = OFF_CQ + MLA_Q_LORA
OFF_KR = OFF_CKV + MLA_KV_LORA
OFF_S5 = OFF_KR + MLA_ROPE
OFF_HY = OFF_S5 + S5_WIDTH
OFF_GATE = OFF_HY + 3 * HY_WIDTH

Z_GATE = 0
Z_HY = 3072
Z_S5 = 4608
Z_MLA = 5120
Z_WIDTH = 5632

VMEM_LIMIT_V7X = 52 * 1024 * 1024


def _cparams(sem, flags=None):
    return pltpu.CompilerParams(dimension_semantics=sem, vmem_limit_bytes=VMEM_LIMIT_V7X, flags=flags)


def _dot(a, b):
    return jnp.dot(a, b, preferred_element_type=F32)


def _rms(x, g):
    return x * lax.rsqrt(jnp.mean(x * x, axis=-1, keepdims=True) + NORM_EPS) * g


def _norm_mod(x, g, shift, scale):
    return _rms(x, g) * (1.0 + scale) + shift


def _proj_kernel(x_ref, g_ref, sh_ref, sc_ref, w_ref, o_ref, h_ref):
    @pl.when(pl.program_id(2) == 0)
    def _():
        h_ref[...] = _norm_mod(x_ref[0], g_ref[...], sh_ref[0], sc_ref[0]).astype(BF16)

    o_ref[0] = _dot(h_ref[...], w_ref[...]).astype(BF16)


def _proj_in(x, g, shift, scale, w):
    bsz, n, d = x.shape
    nz = w.shape[1]
    tm = min(n, 2048)
    tn = 512
    return pl.pallas_call(
        _proj_kernel,
        grid=(bsz, n // tm, nz // tn),
        in_specs=[
            pl.BlockSpec((1, tm, d), lambda b, i, j: (b, i, 0)),
            pl.BlockSpec((1, d), lambda b, i, j: (0, 0)),
            pl.BlockSpec((1, 1, d), lambda b, i, j: (b, 0, 0)),
            pl.BlockSpec((1, 1, d), lambda b, i, j: (b, 0, 0)),
            pl.BlockSpec((d, tn), lambda b, i, j: (0, j)),
        ],
        out_specs=pl.BlockSpec((1, tm, tn), lambda b, i, j: (b, i, j)),
        out_shape=jax.ShapeDtypeStruct((bsz, n, nz), BF16),
        scratch_shapes=[pltpu.VMEM((tm, d), BF16)],
        compiler_params=_cparams(("parallel", "parallel", "arbitrary")),
        name="proj_in",
    )(x, g, shift, scale, w)


def _mla_prep_kernel(z_ref, qn_ref, kvn_ref, wq_ref, wqs_ref, wk_ref, wvt_ref, c_ref, s_ref, *rest):
    q_ref, k_ref, vt_ref = rest[-3:]
    z = z_ref[0].astype(F32)
    hq = _rms(z[:, :MLA_Q_LORA], qn_ref[...]).astype(BF16)
    hkv = _rms(z[:, MLA_Q_LORA:MLA_Q_LORA + MLA_KV_LORA], kvn_ref[...]).astype(BF16)
    krb = z[:, MLA_Q_LORA + MLA_KV_LORA:]
    qa = _dot(hq, wq_ref[...])
    qb = _dot(hq, wqs_ref[...])
    kn = _dot(hkv, wk_ref[...])
    vt = lax.dot_general(wvt_ref[...], hkv, (((1,), (1,)), ((), ())), preferred_element_type=F32)
    ck = c_ref[...]
    sn = s_ref[...]
    lane = lax.broadcasted_iota(jnp.int32, ck.shape, 1)
    cq = jnp.where(lane < MLA_ROPE, ck, 1.0)
    kr = krb * ck + pltpu.roll(krb, HEAD_W - MLA_ROPE // 2, 1) * sn
    row = lax.broadcasted_iota(jnp.int32, (VT_ROWS, z.shape[0]), 0)
    for h in range(MLA_HEADS):
        sl = slice(h * HEAD_W, (h + 1) * HEAD_W)
        q_ref[0, h] = ((qa[:, sl] * cq + qb[:, sl] * sn) * QK_SCALE_LOG2).astype(BF16)
        k_ref[0, h] = (kr + kn[:, sl]).astype(BF16)
        vt_ref[0, h] = jnp.where(row == MLA_V, 1.0, vt[h * VT_ROWS:(h + 1) * VT_ROWS, :]).astype(BF16)


def _mla_prep(z, qn, kvn, wq, wqs, wk, wvt, ctab, stab, n_keys, key_off, kv_bufs=None):
    bsz, n, _ = z.shape
    tm = min(n, MLA_PREP_ROWS)
    assert key_off % tm == 0, "the key offset must be a whole number of row tiles"
    ob = key_off // tm
    hw = MLA_HEADS * HEAD_W
    zb = Z_MLA // 512
    full = lambda shape: pl.BlockSpec(shape, lambda b, i: (0,) * len(shape))
    in_specs = [
        pl.BlockSpec((1, tm, 512), lambda b, i: (b, i, zb)),
        full((1, MLA_Q_LORA)),
        full((1, MLA_KV_LORA)),
        full((MLA_Q_LORA, hw)),
        full((MLA_Q_LORA, hw)),
        full((MLA_KV_LORA, hw)),
        full((MLA_HEADS * VT_ROWS, MLA_KV_LORA)),
        pl.BlockSpec((tm, HEAD_W), lambda b, i: (i, 0)),
        pl.BlockSpec((tm, HEAD_W), lambda b, i: (i, 0)),
    ]
    args = [z, qn, kvn, wq, wqs, wk, wvt, ctab, stab]
    aliases = {}
    if kv_bufs is not None:
        aliases = {len(args): 1, len(args) + 1: 2}
        in_specs += [pl.BlockSpec(memory_space=pl.ANY)] * 2
        args += list(kv_bufs)
    return pl.pallas_call(
        _mla_prep_kernel,
        grid=(bsz, n // tm),
        in_specs=in_specs,
        out_specs=[
            pl.BlockSpec((1, MLA_HEADS, tm, HEAD_W), lambda b, i: (b, 0, i, 0)),
            pl.BlockSpec((1, MLA_HEADS, tm, HEAD_W), lambda b, i: (b, 0, i + ob, 0)),
            pl.BlockSpec((1, MLA_HEADS, VT_ROWS, tm), lambda b, i: (b, 0, 0, i + ob)),
        ],
        out_shape=[
            jax.ShapeDtypeStruct((bsz, MLA_HEADS, n, HEAD_W), BF16),
            jax.ShapeDtypeStruct((bsz, MLA_HEADS, n_keys, HEAD_W), BF16),
            jax.ShapeDtypeStruct((bsz, MLA_HEADS, VT_ROWS, n_keys), BF16),
        ],
        input_output_aliases=aliases,
        compiler_params=_cparams(("parallel", "parallel")),
        name="mla_prep",
    )(*args)


def _attn_kernel(q_ref, k_ref, vt_ref, o_ref, s0_ref, s1_ref, s2_ref, p0_ref, p1_ref, p2_ref, acc_ref, *, tk, nkc):
    q = q_ref[0, 0]
    tq = q.shape[0]
    s_refs = (s0_ref, s1_ref, s2_ref)
    p_refs = (p0_ref, p1_ref, p2_ref)

    ts = min(tk, ATTN_SUB)
    subs = [slice(r, r + ts) for r in range(0, tk, ts)]

    def scores_sub(j, s_ref, sub, cmax):
        off = pl.multiple_of(j * tk + sub.start, ts)
        s = lax.dot_general(k_ref[0, 0, pl.ds(off, ts), :], q, (((1,), (1,)), ((), ())),
                            preferred_element_type=F32)
        s_ref[sub, :] = s
        cm = jnp.max(s, axis=0, keepdims=True)
        return cm if cmax is None else jnp.maximum(cmax, cm)

    def softmax_sub(slot, sub, mn):
        p_refs[slot][sub, :] = jnp.exp2(s_refs[slot][sub, :] - mn).astype(BF16)

    def step(i, slot, carry, do_softmax, do_scores):
        cm, m, alpha = carry
        mn = jnp.maximum(m, cm)
        off = pl.multiple_of(i * tk, tk)
        acc_ref[...] = alpha * acc_ref[...] + _dot(vt_ref[0, 0, :, pl.ds(off, tk)], p_refs[slot][...])
        cm_new, cms = None, []
        for r, sub in enumerate(subs):
            mn_sub = mn
            if do_scores:
                cm_new = scores_sub(i + 2, s_refs[(slot + 2) % 3], sub, cm_new)
                cms.append(cm_new)
                if r >= ATTN_LAG:
                    mn_sub = jnp.maximum(mn, cms[r - ATTN_LAG] - 1e30)
            if do_softmax:
                softmax_sub((slot + 1) % 3, sub, mn_sub)
        if do_softmax:
            alpha, m = jnp.exp2(m - mn), mn
        return (cm_new if do_scores else cm), m, alpha

    m = jnp.full((1, tq), -1e30, F32)
    acc_ref[...] = jnp.zeros_like(acc_ref)
    cm, cm_next = None, None
    for sub in subs:
        cm = scores_sub(0, s0_ref, sub, cm)
        if nkc > 1:
            cm_next = scores_sub(1, s1_ref, sub, cm_next)
    mn = jnp.maximum(m, cm)
    for sub in subs:
        softmax_sub(0, sub, mn)
    carry = (cm_next if nkc > 1 else cm, mn, jnp.exp2(m - mn))

    def body(t, carry):
        for r in range(ATTN_UNROLL):
            carry = step(ATTN_UNROLL * t + r, r % 3, carry, True, True)
        return carry

    nloop = max(nkc - 2, 0) // ATTN_UNROLL
    carry = lax.fori_loop(0, nloop, body, carry)
    for i in range(ATTN_UNROLL * nloop, nkc):
        carry = step(i, i % 3, carry, i + 1 < nkc, i + 2 < nkc)
    acc = acc_ref[...]
    out = acc * (1.0 / acc[MLA_V:MLA_V + 1, :])
    out = jnp.concatenate([out, jnp.zeros((HEAD_W - VT_ROWS, tq), F32)], axis=0).T
    o_ref[0, 0] = out[:, :MLA_V].astype(BF16)


def _kv_chunk(nk):
    for tk in (640, 512, 256, 128):
        if nk % tk == 0:
            return tk
    raise ValueError(f"unsupported key count {nk}")


def _attention(q, k, vt, nk, key_block=0):
    bsz, nh, nq, _ = q.shape
    tq = min(nq, 1024)
    tk = _kv_chunk(nk)
    return pl.pallas_call(
        functools.partial(_attn_kernel, tk=tk, nkc=nk // tk),
        grid=(bsz, nh, nq // tq),
        in_specs=[
            pl.BlockSpec((1, 1, tq, HEAD_W), lambda b, h, i: (b, h, i, 0)),
            pl.BlockSpec((1, 1, nk, HEAD_W), lambda b, h, i: (b, h, key_block, 0)),
            pl.BlockSpec((1, 1, VT_ROWS, nk), lambda b, h, i: (b, h, 0, key_block)),
        ],
        out_specs=pl.BlockSpec((1, 1, tq, MLA_V), lambda b, h, i: (b, h, i, 0)),
        out_shape=jax.ShapeDtypeStruct((bsz, nh, nq, MLA_V), BF16),
        scratch_shapes=[pltpu.VMEM((tk, tq), F32)] * 3 + [pltpu.VMEM((tk, tq), BF16)] * 3
        + [pltpu.VMEM((VT_ROWS, tq), F32)],
        compiler_params=_cparams(("parallel", "parallel", "arbitrary")),
        name="attention",
    )(q, k, vt)


def _cmul(ar, ai, br, bi):
    return ar * br - ai * bi, ar * bi + ai * br


def _s5_kernel(u_ref, h0_ref, bm_ref, cm_ref, wn_ref, wp_ref, l1_ref, tri_ref, y_ref, hf_ref,
               carry_ref, h0cat_ref, h1cat_ref, *, tc, nr, ns, reverse):
    i = pl.program_id(1)

    @pl.when(i == 0)
    def _():
        carry_ref[...] = h0_ref[0]

    u = u_ref[0]
    tri = tri_ref[...]
    order = range(nr - 1, -1, -1) if reverse else range(nr)
    edge = 0 if reverse else tc - 1
    ys = []
    for hf, hcat_ref in enumerate((h0cat_ref, h1cat_ref)):
        bu = _dot(u[:, hf * S5_HALF:(hf + 1) * S5_HALF], bm_ref[hf])
        hr_in, hi_in = carry_ref[2 * hf:2 * hf + 1, :], carry_ref[2 * hf + 1:2 * hf + 2, :]
        for c in order:
            rows = slice(c * tc, (c + 1) * tc)
            xr, xi = _cmul(wn_ref[hf, 0], wn_ref[hf, 1], bu[rows, :S5_HSTATE], bu[rows, S5_HSTATE:])
            s = _dot(tri, jnp.concatenate([xr, xi], axis=1).astype(BF16))
            cr, ci = _cmul(l1_ref[hf, 0], l1_ref[hf, 1], hr_in, hi_in)
            hr, hi = _cmul(wp_ref[hf, 0], wp_ref[hf, 1], s[:, :S5_HSTATE] + cr, s[:, S5_HSTATE:] + ci)
            hcat_ref[rows, :S5_HSTATE] = hr.astype(BF16)
            hcat_ref[rows, S5_HSTATE:] = hi.astype(BF16)
            hr_in, hi_in = hr[edge:edge + 1], hi[edge:edge + 1]
        carry_ref[2 * hf:2 * hf + 1, :] = hr_in
        carry_ref[2 * hf + 1:2 * hf + 2, :] = hi_in
        ys.append(_dot(hcat_ref[...], cm_ref[hf]))
    y_ref[0] = jnp.concatenate(ys, axis=1)

    @pl.when(i == ns - 1)
    def _():
        hf_ref[0] = carry_ref[...]


def _s5_scan_dir(z, h0, tabs, reverse):
    bm, cm, wn, wp, l1, tri = tabs
    bsz, n, _ = z.shape
    tc = S5_CHUNK
    nr = min(S5_CHUNKS_PER_STEP, n // tc)
    rows = nr * tc
    ns = n // rows
    zb = Z_S5 // S5_WIDTH
    blk = (lambda i: ns - 1 - i) if reverse else (lambda i: i)
    full = lambda shape: pl.BlockSpec(shape, lambda b, i: (0,) * len(shape))
    return pl.pallas_call(
        functools.partial(_s5_kernel, tc=tc, nr=nr, ns=ns, reverse=reverse),
        grid=(bsz, ns),
        in_specs=[
            pl.BlockSpec((1, rows, S5_WIDTH), lambda b, i: (b, blk(i), zb)),
            pl.BlockSpec((1, 4, S5_HSTATE), lambda b, i: (b, 0, 0)),
            full((2, S5_HALF, 2 * S5_HSTATE)),
            full((2, 2 * S5_HSTATE, S5_HALF)),
            full((2, 2, tc, S5_HSTATE)),
            full((2, 2, tc, S5_HSTATE)),
            full((2, 2, 1, S5_HSTATE)),
            full((tc, tc)),
        ],
        out_specs=[
            pl.BlockSpec((1, rows, S5_WIDTH), lambda b, i: (b, blk(i), 0)),
            pl.BlockSpec((1, 4, S5_HSTATE), lambda b, i: (b, 0, 0)),
        ],
        out_shape=[
            jax.ShapeDtypeStruct((bsz, n, S5_WIDTH), F32),
            jax.ShapeDtypeStruct((bsz, 4, S5_HSTATE), F32),
        ],
        scratch_shapes=[pltpu.VMEM((4, S5_HSTATE), F32),
                        pltpu.VMEM((rows, 2 * S5_HSTATE), BF16), pltpu.VMEM((rows, 2 * S5_HSTATE), BF16)],
        compiler_params=_cparams(("parallel", "arbitrary")),
        name="s5_scan",
    )(z, h0, bm, cm, wn, wp, l1, tri)


def _s5_scan(z, h0, tabs):
    outs = [_s5_scan_dir(z, h0[:, d], [t[d] for t in tabs], reverse=bool(d)) for d in range(2)]
    return outs[0][0], outs[1][0], jnp.stack([outs[0][1], outs[1][1]], axis=1)


def _s5_tables(lam_re, lam_im, log_dt, b_re, b_im, c_re, c_im):
    tc = S5_CHUNK
    dt = jnp.exp(log_dt)[..., None]
    zr, zi = lam_re * dt, lam_im * dt
    mag = jnp.exp(zr)
    lbr, lbi = mag * jnp.cos(zi), mag * jnp.sin(zi)
    den = lam_re * lam_re + lam_im * lam_im
    nr, ni = lbr - 1.0, lbi
    cfr = (nr * lam_re + ni * lam_im) / den
    cfi = (ni * lam_re - nr * lam_im) / den
    bbr = cfr[..., None] * b_re - cfi[..., None] * b_im
    bbi = cfr[..., None] * b_im + cfi[..., None] * b_re
    eye = jnp.eye(S5_GROUP, dtype=F32)

    def blockdiag_in(b):
        b = b.reshape(2, 2, S5_GROUP, S5_STATE, S5_GROUP)
        return jnp.einsum('gk,dhgpn->dhgnkp', eye, b).reshape(2, 2, S5_HALF, S5_HSTATE)

    def blockdiag_out(c):
        c = c.reshape(2, 2, S5_GROUP, S5_GROUP, S5_STATE)
        return jnp.einsum('gk,dhgnp->dhgpkn', eye, c).reshape(2, 2, S5_HSTATE, S5_HALF)

    bm = jnp.concatenate([blockdiag_in(bbr), blockdiag_in(bbi)], axis=-1).astype(BF16)
    cm = jnp.concatenate([blockdiag_out(c_re), blockdiag_out(-c_im)], axis=-2).astype(BF16)

    def powers(k):
        zr_ = zr.reshape(2, 2, 1, S5_HSTATE)
        zi_ = zi.reshape(2, 2, 1, S5_HSTATE)
        kk = k[:, None, :, None]
        m = jnp.exp(kk * zr_)
        return jnp.stack([m * jnp.cos(kk * zi_), m * jnp.sin(kk * zi_)], axis=2)

    mid = tc // 2
    t = jnp.arange(tc, dtype=F32) - mid
    wn = powers(jnp.stack([-t, -t[::-1]]))
    wp = powers(jnp.stack([t, t[::-1]]))
    l1 = powers(jnp.full((2, 1), mid + 1, F32))
    r = jnp.arange(tc)
    tri = jnp.stack([r[:, None] >= r[None, :], r[:, None] <= r[None, :]]).astype(BF16)
    return bm, cm, wn, wp, l1, tri


HALO = 16


def _sconv_kernel(z_ref, zp_ref, zn_ref, w_ref, b_ref, v_ref, g1_ref, g2_ref, *, nt):
    i = pl.program_id(1)
    z = z_ref[0].astype(F32)
    tm = z.shape[0]
    row = lax.broadcasted_iota(jnp.int32, z.shape, 0)
    prev = jnp.where(i > 0, zp_ref[0, HALO - 1:HALO, :].astype(F32), 0.0)
    nxt = jnp.where(i < nt - 1, zn_ref[0, 0:1, :].astype(F32), 0.0)
    up = jnp.where(row == 0, prev, pltpu.roll(z, 1, 0))
    dn = jnp.where(row == tm - 1, nxt, pltpu.roll(z, tm - 1, 0))
    u = up * w_ref[0:1, :] + z * w_ref[1:2, :] + dn * w_ref[2:3, :] + b_ref[...]
    for k, o_ref in enumerate((v_ref, g1_ref, g2_ref)):
        o_ref[0] = u[:, k * HY_WIDTH:(k + 1) * HY_WIDTH]


def _short_conv(z, w, b):
    bsz, n, _ = z.shape
    cw = 3 * HY_WIDTH
    tm = min(n, 1024)
    nt = n // tm
    zb = Z_HY // cw
    rh = tm // HALO
    out = pl.BlockSpec((1, tm, HY_WIDTH), lambda b_, i: (b_, i, 0))
    return pl.pallas_call(
        functools.partial(_sconv_kernel, nt=nt),
        grid=(bsz, nt),
        in_specs=[
            pl.BlockSpec((1, tm, cw), lambda b_, i: (b_, i, zb)),
            pl.BlockSpec((1, HALO, cw), lambda b_, i: (b_, jnp.maximum(i * rh - 1, 0), zb)),
            pl.BlockSpec((1, HALO, cw), lambda b_, i: (b_, jnp.minimum((i + 1) * rh, n // HALO - 1), zb)),
            pl.BlockSpec((3, cw), lambda b_, i: (0, 0)),
            pl.BlockSpec((1, cw), lambda b_, i: (0, 0)),
        ],
        out_specs=[out] * 3,
        out_shape=[jax.ShapeDtypeStruct((bsz, n, HY_WIDTH), F32)] * 3,
        compiler_params=_cparams(("parallel", "parallel")),
        name="short_conv",
    )(z, z, z, w, b)


def _filt_kernel(f_ref, w1_ref, b1_ref, w2_ref, b2_ref, w3h_ref, w3l_ref, fq_ref, dl_ref, o_ref, *, n_tok):
    z = f_ref[...]
    tm = z.shape[0]
    fq = fq_ref[...]
    hid = jnp.sin(fq * (jnp.dot(z, w1_ref[...], precision=HIGHEST, preferred_element_type=F32) + b1_ref[...]))
    hid = jnp.sin(fq * (jnp.dot(hid, w2_ref[...], precision=HIGHEST, preferred_element_type=F32) + b2_ref[...]))
    hid_hi = hid.astype(BF16)
    hid_lo = (hid - hid_hi.astype(F32)).astype(BF16)

    def filters(rows, d):
        wh, wl = w3h_ref[d], w3l_ref[d]
        f = _dot(hid_hi[rows], wh) + (_dot(hid_hi[rows], wl) + _dot(hid_lo[rows], wh))
        return f * (jnp.exp(-z[rows, 0:1] * dl_ref[d]) + HY_DECAY_SHIFT)

    i = pl.program_id(0)
    filt = filters(slice(None), jnp.where(i * tm >= n_tok, 1, 0))
    m = i * tm + lax.broadcasted_iota(jnp.int32, filt.shape, 0)
    filt = jnp.where(m == n_tok, 0.0, filt)
    for o in range(HY_ORDER):
        o_ref[o] = filt[:, o * HY_WIDTH:(o + 1) * HY_WIDTH]

    @pl.when(i == 0)
    def _():
        head = slice(0, HALO)
        lag0 = filters(head, 1)
        lag0 = jnp.where(lax.broadcasted_iota(jnp.int32, lag0.shape, 0) == 0, lag0, 0.0)
        for o in range(HY_ORDER):
            o_ref[o, head, :] += lag0[:, o * HY_WIDTH:(o + 1) * HY_WIDTH]


def _hyena_filter_feats(n_tok):
    m = jnp.arange(2 * n_tok)
    lag = jnp.where(m < n_tok, m, jnp.where(m > n_tok, 2 * n_tok - m, 0))
    t = (lag.astype(F32) / (n_tok - 1))[:, None]
    w = (2.0 * math.pi * lag.astype(F32) / n_tok)[:, None]
    bands = jnp.linspace(1e-4, HY_BANDS - 1, HY_BANDS, dtype=F32)[None, :]
    feats = jnp.concatenate([t, jnp.cos(bands * w), -jnp.sin(bands * w)], axis=-1)
    return jnp.pad(feats, ((0, 0), (0, HY_POS_PAD - HY_POS_DIM)))


def _hyena_filters(n_tok, w1, b1, w2, b2, w3, freq):
    feats = _hyena_filter_feats(n_tok)
    deltas = jnp.abs(jnp.linspace(math.log(HY_DECAY_TARGET) / HY_DECAY_SHORT,
                                  math.log(HY_DECAY_TARGET) / HY_DECAY_LONG, HY_FILTER_OUT, dtype=F32))[None, :]
    w1p = jnp.pad(w1, ((0, HY_POS_PAD - HY_POS_DIM), (0, 0)))

    def by_direction(a):
        a = a.reshape(a.shape[:-1] + (HY_ORDER, 2, HY_WIDTH))
        return jnp.moveaxis(a, -2, 0).reshape((2,) + a.shape[:-3] + (HY_ORDER * HY_WIDTH,))

    w3, deltas = by_direction(w3), by_direction(deltas)
    w3_hi = w3.astype(BF16)
    w3_lo = (w3 - w3_hi.astype(F32)).astype(BF16)
    n2 = 2 * n_tok
    tm = min(n_tok, 1024)
    full = lambda shape: pl.BlockSpec(shape, lambda i: (0,) * len(shape))
    hh = HY_FILTER_HIDDEN
    wd = HY_ORDER * HY_WIDTH
    return pl.pallas_call(
        functools.partial(_filt_kernel, n_tok=n_tok),
        grid=(n2 // tm,),
        in_specs=[
            pl.BlockSpec((tm, HY_POS_PAD), lambda i: (i, 0)),
            full((HY_POS_PAD, hh)), full((1, hh)), full((hh, hh)), full((1, hh)),
            full((2, hh, wd)), full((2, hh, wd)), full((1, hh)), full((2, 1, wd)),
        ],
        out_specs=pl.BlockSpec((HY_ORDER, tm, HY_WIDTH), lambda i: (0, i, 0)),
        out_shape=jax.ShapeDtypeStruct((HY_ORDER, n2, HY_WIDTH), F32),
        compiler_params=_cparams(("parallel",)),
        name="hyena_filter",
    )(feats, w1p, b1[None], w2, b2[None], w3_hi, w3_lo, freq[None], deltas)


def _fft_split(n_fft):
    if n_fft <= 1024:
        return n_fft, 1
    n1 = 1 << (int(math.log2(n_fft)) // 2)
    return n1, n_fft // n1


def _cis(num, den, sign):
    ang = (2.0 * math.pi / den) * num.astype(F32)
    return jnp.cos(ang), sign * jnp.sin(ang)


def _fft_tables(n_fft):
    n1, n2 = _fft_split(n_fft)
    a = jnp.arange(n1)
    f1r, f1i = _cis((a[:, None] * a[None, :]) % n1, n1, -1.0)
    s1 = jnp.concatenate([f1r, f1i], axis=0).astype(BF16)
    half = n1 // 2
    s3 = (jnp.concatenate([f1r[:half], -f1i[:half]], axis=0) / n_fft).astype(BF16)
    if n2 == 1:
        return s1, s3, None, None
    b = jnp.arange(n2)
    f2r, f2i = _cis((b[:, None] * b[None, :]) % n2, n2, -1.0)
    twr, twi = _cis((jnp.arange(n1)[:, None] * b[None, :]) % n_fft, n_fft, -1.0)
    fr, fi = _cmul(f2r[None], f2i[None], twr[:, None, :], twi[:, None, :])
    ft = jnp.concatenate([fr, fi], axis=1).astype(BF16)
    frt, fit = _cmul(f2r[None], f2i[None], twr[:, :, None], twi[:, :, None])
    it = jnp.concatenate([frt, -fit], axis=1).astype(BF16)
    return s1, s3, ft, it


def _stacked_cdot(s, xr, xi, m):
    p = _dot(s, xr.astype(BF16))
    if xi is None:
        return p[:m], p[m:]
    q = _dot(s, xi.astype(BF16))
    return p[:m] - q[m:], q[:m] + p[m:]


def _fft_a_kernel(*refs, parts, n1, has_g):
    if has_g:
        x_ref, s_ref, g_ref, y_ref = refs
    else:
        x_ref, s_ref, y_ref = refs
    yr, yi = _stacked_cdot(s_ref[...], x_ref[0], x_ref[1] if parts == 2 else None, n1)
    if has_g:
        yr, yi = _cmul(yr, yi, g_ref[0], g_ref[1])
    y_ref[0] = yr
    y_ref[1] = yi


def _fft_a(x, s1, g=None):
    parts, a_rows, cols = x.shape
    n1 = s1.shape[0] // 2
    tc = min(cols, 2048)
    in_specs = [
        pl.BlockSpec((parts, a_rows, tc), lambda j: (0, 0, j)),
        pl.BlockSpec((2 * n1, a_rows), lambda j: (0, 0)),
    ]
    args = [x, s1[:, :a_rows]]
    if g is not None:
        in_specs.append(pl.BlockSpec((2, n1, tc), lambda j: (0, 0, j)))
        args.append(g)
    return pl.pallas_call(
        functools.partial(_fft_a_kernel, parts=parts, n1=n1, has_g=g is not None),
        grid=(cols // tc,),
        in_specs=in_specs,
        out_specs=pl.BlockSpec((2, n1, tc), lambda j: (0, 0, j)),
        out_shape=jax.ShapeDtypeStruct((2, n1, cols), F32),
        compiler_params=_cparams(("parallel",)),
        name="fft_stage_a",
    )(*args)


FFT_CB = 4


def _fft_b_kernel(*refs, n2, inverse):
    if inverse:
        y_ref, ft_ref, it_ref, g_ref, o_ref = refs
    else:
        y_ref, ft_ref, o_ref = refs
    for c in range(FFT_CB):
        xr, xi = _stacked_cdot(ft_ref[c], y_ref[0, c], y_ref[1, c], n2)
        if inverse:
            zr, zi = _cmul(xr, xi, g_ref[0, c], g_ref[1, c])
            xr, xi = _stacked_cdot(it_ref[c], zr, zi, n2)
        o_ref[0, c] = xr.astype(o_ref.dtype)
        o_ref[1, c] = xi.astype(o_ref.dtype)


def _fft_b(y, ft, it=None, g=None):
    _, n1, n2, ch = y.shape
    inverse = it is not None
    blk = pl.BlockSpec((2, FFT_CB, n2, ch), lambda c: (0, c, 0, 0))
    mat = pl.BlockSpec((FFT_CB, 2 * n2, n2), lambda c: (c, 0, 0))
    in_specs, args = [blk, mat], [y, ft]
    if inverse:
        in_specs += [mat, blk]
        args += [it, g]
    return pl.pallas_call(
        functools.partial(_fft_b_kernel, n2=n2, inverse=inverse),
        grid=(n1 // FFT_CB,),
        in_specs=in_specs,
        out_specs=blk,
        out_shape=jax.ShapeDtypeStruct(y.shape, BF16 if inverse else F32),
        compiler_params=_cparams(("parallel",)),
        name="fft_stage_b",
    )(*args)


def _fft_c_kernel(y_ref, s_ref, x_ref, g_ref, b_ref, o_ref, *, half):
    cr, ci = _stacked_cdot(s_ref[...], y_ref[0], y_ref[1], half)
    bias = b_ref[...]
    o_ref[0] = g_ref[0] * (cr + bias * x_ref[0])
    o_ref[1] = g_ref[1] * (ci + bias * x_ref[1])


def _fft_c(y, s3, x, gate, bias_cols):
    _, n1, cols = y.shape
    half = n1 // 2
    tc = min(cols, 2048)
    io = pl.BlockSpec((2, half, tc), lambda j: (0, 0, j))
    return pl.pallas_call(
        functools.partial(_fft_c_kernel, half=half),
        grid=(cols // tc,),
        in_specs=[
            pl.BlockSpec((2, n1, tc), lambda j: (0, 0, j)),
            pl.BlockSpec((n1, n1), lambda j: (0, 0)),
            io, io,
            pl.BlockSpec((1, tc), lambda j: (0, j)),
        ],
        out_specs=io,
        out_shape=jax.ShapeDtypeStruct((2, half, cols), F32),
        compiler_params=_cparams(("parallel",)),
        name="fft_stage_c",
    )(y, s3, x, gate, bias_cols)


FFT_BT = 16


def _to_fine_major(x):
    return pltpu.einshape("abc->bac", x)


def _fft_a4_kernel(x_ref, s_ref, y_ref, *, parts, n1):
    s = s_ref[...]
    xr = _to_fine_major(x_ref[0].astype(BF16))
    xi = _to_fine_major(x_ref[1].astype(BF16)) if parts == 2 else None
    out = [_stacked_cdot(s, xr[b], None if xi is None else xi[b], n1) for b in range(FFT_BT)]
    y_ref[0] = pltpu.einshape("bac->abc", jnp.stack([o[0] for o in out])).astype(BF16)
    y_ref[1] = pltpu.einshape("bac->abc", jnp.stack([o[1] for o in out])).astype(BF16)


def _fft_a4(x, s1):
    parts, a_rows, n2, ch = x.shape
    n1 = s1.shape[0] // 2
    return pl.pallas_call(
        functools.partial(_fft_a4_kernel, parts=parts, n1=n1),
        grid=(n2 // FFT_BT,),
        in_specs=[
            pl.BlockSpec((parts, a_rows, FFT_BT, ch), lambda j: (0, 0, j, 0)),
            pl.BlockSpec((2 * n1, a_rows), lambda j: (0, 0)),
        ],
        out_specs=pl.BlockSpec((2, n1, FFT_BT, ch), lambda j: (0, 0, j, 0)),
        out_shape=jax.ShapeDtypeStruct((2, n1, n2, ch), BF16),
        compiler_params=_cparams(("parallel",)),
        name="fft_stage_a",
    )(x, s1[:, :a_rows])


def _fft_c4_kernel(y_ref, s_ref, x_ref, g_ref, b_ref, o_ref, c_ref, *, half):
    s = s_ref[...]
    yr = _to_fine_major(y_ref[0].astype(BF16))
    yi = _to_fine_major(y_ref[1].astype(BF16))
    out = [_stacked_cdot(s, yr[b], yi[b], half) for b in range(FFT_BT)]
    c_ref[0] = pltpu.einshape("bac->abc", jnp.stack([o[0] for o in out]))
    c_ref[1] = pltpu.einshape("bac->abc", jnp.stack([o[1] for o in out]))
    o_ref[...] = g_ref[...] * (c_ref[...] + b_ref[...][None, None] * x_ref[...])


def _fft_c4(y, s3, x, gate, bias):
    _, n1, n2, ch = y.shape
    half = n1 // 2
    io = pl.BlockSpec((2, half, FFT_BT, ch), lambda j: (0, 0, j, 0))
    return pl.pallas_call(
        functools.partial(_fft_c4_kernel, half=half),
        grid=(n2 // FFT_BT,),
        in_specs=[
            pl.BlockSpec((2, n1, FFT_BT, ch), lambda j: (0, 0, j, 0)),
            pl.BlockSpec((n1, n1), lambda j: (0, 0)),
            io, io,
            pl.BlockSpec((1, ch), lambda j: (0, 0)),
        ],
        out_specs=io,
        out_shape=jax.ShapeDtypeStruct((2, half, n2, ch), F32),
        scratch_shapes=[pltpu.VMEM((2, half, FFT_BT, ch), F32)],
        compiler_params=_cparams(("parallel",)),
        name="fft_stage_c",
    )(y, s3, x, gate, bias)


def _hyena_mixer(v, gates, gfilt, bias):
    bsz, n, ch = v.shape
    assert bsz == 2, "the complex packing of the long convolution pairs exactly two batch rows"
    n_fft = 2 * n
    n1, n2 = _fft_split(n_fft)
    s1, s3, ft, it = _fft_tables(n_fft)
    if n2 == 1:
        y = v
        for o in range(HY_ORDER):
            gspec = _fft_a(gfilt[o][None], s1)
            y = _fft_c(_fft_a(y, s1, gspec), s3, y, gates[o], bias[o][None])
        return y
    half = n1 // 2
    y = v.reshape(2, half, n2, ch)
    for o in range(HY_ORDER):
        gspec = _fft_b(_fft_a4(gfilt[o].reshape(1, n1, n2, ch), s1), ft)
        spec = _fft_b(_fft_a4(y, s1), ft, it, gspec)
        y = _fft_c4(spec, s3, y, gates[o].reshape(2, half, n2, ch), bias[o][None])
    return y.reshape(2, n, ch)


def _gelu_tanh(x):
    return 0.5 * x * (1.0 + jnp.tanh(math.sqrt(2.0 / math.pi) * (x + 0.044715 * (x * x * x))))


def _merge_kernel(x_ref, g0_ref, g1_ref, g2_ref, a_ref, yf_ref, yb_ref, u_ref, e_ref, d_ref,
                  wglu_ref, wm_ref, ws_ref, wh_ref, wo_ref, gt_ref, o_ref):
    att = _dot(jnp.concatenate([a_ref[0, h] for h in range(MLA_HEADS)], axis=-1), wm_ref[...])
    y = yf_ref[0] + yb_ref[0] + d_ref[...] * u_ref[0].astype(F32)
    gy = _gelu_tanh(y)
    s5 = gy * jax.nn.sigmoid(_dot(gy.astype(BF16), wglu_ref[...]))
    s5 = _dot(s5.astype(BF16), ws_ref[...])
    hy = _dot(e_ref[0].astype(BF16), wh_ref[...])
    merged = jax.nn.sigmoid(g0_ref[0].astype(F32)) * att
    merged = merged + jax.nn.sigmoid(g1_ref[0].astype(F32)) * s5
    merged = merged + jax.nn.sigmoid(g2_ref[0].astype(F32)) * hy
    o_ref[0] = x_ref[0] + gt_ref[0] * _dot(merged.astype(BF16), wo_ref[...])


def _merge(x, z, att, y5, e, s5_d, w_glu, w_mla, w_s5, w_hy, w_out, gate):
    bsz, n, d = x.shape
    tm = min(n, 512)
    full = lambda shape: pl.BlockSpec(shape, lambda b, i: (0,) * len(shape))
    zs5 = Z_S5 // S5_WIDTH
    return pl.pallas_call(
        _merge_kernel,
        grid=(bsz, n // tm),
        in_specs=[
            pl.BlockSpec((1, tm, d), lambda b, i: (b, i, 0)),
            pl.BlockSpec((1, tm, d), lambda b, i: (b, i, 0)),
            pl.BlockSpec((1, tm, d), lambda b, i: (b, i, 1)),
            pl.BlockSpec((1, tm, d), lambda b, i: (b, i, 2)),
            pl.BlockSpec((1, MLA_HEADS, tm, MLA_V), lambda b, i: (b, 0, i, 0)),
            pl.BlockSpec((1, tm, S5_WIDTH), lambda b, i: (b, i, 0)),
            pl.BlockSpec((1, tm, S5_WIDTH), lambda b, i: (b, i, 0)),
            pl.BlockSpec((1, tm, S5_WIDTH), lambda b, i: (b, i, zs5)),
            pl.BlockSpec((1, tm, HY_WIDTH), lambda b, i: (b, i, 0)),
            full((1, S5_WIDTH)),
            full((S5_WIDTH, S5_WIDTH)),
            full((MLA_HEADS * MLA_V, d)),
            full((S5_WIDTH, d)),
            full((HY_WIDTH, d)),
            full((d, d)),
            pl.BlockSpec((1, 1, d), lambda b, i: (b, 0, 0)),
        ],
        out_specs=pl.BlockSpec((1, tm, d), lambda b, i: (b, i, 0)),
        out_shape=jax.ShapeDtypeStruct((bsz, n, d), F32),
        compiler_params=_cparams(("parallel", "parallel")),
        name="merge",
    )(x, z, z, z, att, y5[0], y5[1], z, e, s5_d, w_glu, w_mla, w_s5, w_hy, w_out, gate)


def _ffn_kernel(x_ref, g_ref, sh_ref, sc_ref, gt_ref, wg_ref, wu_ref, wd_ref, fg_ref, o_ref, h_ref, acc_ref,
                *, nk, final):
    k = pl.program_id(2)

    @pl.when(k == 0)
    def _():
        h_ref[...] = _norm_mod(x_ref[0], g_ref[...], sh_ref[0], sc_ref[0]).astype(BF16)
        acc_ref[...] = jnp.zeros_like(acc_ref)

    h = h_ref[...]
    act = jax.nn.silu(_dot(h, wg_ref[...])) * _dot(h, wu_ref[...])
    acc_ref[...] += _dot(act.astype(BF16), wd_ref[...])

    @pl.when(k == nk - 1)
    def _():
        r = x_ref[0] + gt_ref[0] * acc_ref[...]
        o_ref[0] = _rms(r, fg_ref[...]) if final else r


def _ffn(x, g, shift, scale, gate, w_g, w_u, w_d, final_g, final):
    bsz, n, d = x.shape
    dff = w_g.shape[1]
    tm = min(n, 1024)
    tf = 256
    nk = dff // tf
    vec = pl.BlockSpec((1, 1, d), lambda b, i, k: (b, 0, 0))
    row = pl.BlockSpec((1, d), lambda b, i, k: (0, 0))
    return pl.pallas_call(
        functools.partial(_ffn_kernel, nk=nk, final=final),
        grid=(bsz, n // tm, nk),
        in_specs=[
            pl.BlockSpec((1, tm, d), lambda b, i, k: (b, i, 0)),
            row, vec, vec, vec,
            pl.BlockSpec((d, tf), lambda b, i, k: (0, k)),
            pl.BlockSpec((d, tf), lambda b, i, k: (0, k)),
            pl.BlockSpec((tf, d), lambda b, i, k: (k, 0)),
            row,
        ],
        out_specs=pl.BlockSpec((1, tm, d), lambda b, i, k: (b, i, 0)),
        out_shape=jax.ShapeDtypeStruct((bsz, n, d), F32),
        scratch_shapes=[pltpu.VMEM((tm, d), BF16), pltpu.VMEM((tm, d), F32)],
        compiler_params=_cparams(("parallel", "parallel", "arbitrary")),
        name="ffn",
    )(x, g, shift, scale, gate, w_g, w_u, w_d, final_g)


def _rope_tables(n_tok):
    rows = n_tok // GRID_W
    row = jnp.broadcast_to(jnp.arange(rows, dtype=F32)[:, None], (rows, GRID_W)).reshape(-1)
    col = jnp.broadcast_to(jnp.arange(GRID_W, dtype=F32)[None, :], (rows, GRID_W)).reshape(-1)
    n_freq = MLA_ROPE // 4
    inv = ROPE_BASE ** (-jnp.arange(n_freq, dtype=F32) / n_freq)
    ang = jnp.concatenate([row[:, None] * inv, col[:, None] * inv], axis=-1)
    cos, sin = jnp.cos(ang), jnp.sin(ang)
    pad = jnp.zeros((n_tok, HEAD_W - MLA_ROPE), F32)
    return jnp.concatenate([cos, cos, pad], axis=-1), jnp.concatenate([-sin, sin, pad], axis=-1)


def _identity_rope_tables(n_tok):
    one = jnp.ones((n_tok, MLA_ROPE), F32)
    pad = jnp.zeros((n_tok, HEAD_W - MLA_ROPE), F32)
    return jnp.concatenate([one, pad], axis=-1), jnp.zeros((n_tok, HEAD_W), F32)


def _layout_w_in(w):
    kr = w[:, OFF_KR:OFF_S5]
    x1, x2 = kr[:, 0::2], kr[:, 1::2]
    pad = jnp.zeros((w.shape[0], HEAD_W - 3 * (MLA_ROPE // 2)), w.dtype)
    return jnp.concatenate([w[:, OFF_GATE:], w[:, OFF_HY:OFF_GATE], w[:, OFF_S5:OFF_HY],
                            w[:, OFF_CQ:OFF_CKV], w[:, OFF_CKV:OFF_KR], x1, x2, x1, pad], axis=1).astype(BF16)


def _layout_w_uq(w):
    w = w.reshape(MLA_Q_LORA, MLA_HEADS, MLA_NOPE + MLA_ROPE)
    nope, rope = w[..., :MLA_NOPE], w[..., MLA_NOPE:]
    x1, x2 = rope[..., 0::2], rope[..., 1::2]
    z32 = jnp.zeros_like(rope)
    wq = jnp.concatenate([x1, x2, z32, nope], axis=-1)
    wqs = jnp.concatenate([x2, x1, z32, jnp.zeros_like(nope)], axis=-1)
    shape = (MLA_Q_LORA, MLA_HEADS * HEAD_W)
    return wq.reshape(shape).astype(BF16), wqs.reshape(shape).astype(BF16)


def _layout_w_ukv(w):
    w = w.reshape(MLA_KV_LORA, MLA_HEADS, MLA_NOPE + MLA_V)
    nope, val = w[..., :MLA_NOPE], w[..., MLA_NOPE:]
    wk = jnp.concatenate([jnp.zeros_like(nope), nope], axis=-1).reshape(MLA_KV_LORA, MLA_HEADS * HEAD_W)
    wvt = jnp.concatenate([val, jnp.zeros_like(val[..., :VT_ROWS - MLA_V])], axis=-1).reshape(
        MLA_KV_LORA, MLA_HEADS * VT_ROWS).T
    return wk.astype(BF16), wvt.astype(BF16)


def kernel(x, c, ctx, c_ctx, ada_w, ada_b, norm_mix, w_in, mla_q_norm, mla_w_uq, mla_kv_norm, mla_w_ukv,
           s5_lam_re, s5_lam_im, s5_log_dt, s5_b_re, s5_b_im, s5_c_re, s5_c_im, s5_d, s5_w_glu,
           hy_conv_w, hy_conv_b, hy_f_w1, hy_f_b1, hy_f_w2, hy_f_b2, hy_f_w3, hy_f_freq, hy_bias,
           w_branch_mla, w_branch_s5, w_branch_hy, w_out, norm_ffn, ffn_w_gu, ffn_w_down, final_norm):
    bsz, n_tok, d = x.shape
    n_ctx = ctx.shape[1]
    depth = ada_w.shape[0]
    rope_x = _rope_tables(n_tok)
    rope_c = _identity_rope_tables(n_ctx)
    h_zero = jnp.zeros((bsz, 2, 4, S5_HSTATE), F32)

    for i in range(depth):
        last = i == depth - 1
        mx = (jax.nn.silu(c) @ ada_w[i] + ada_b[i]).reshape(bsz, N_MOD, 1, d)
        mc = jnp.broadcast_to((jax.nn.silu(c_ctx) @ ada_w[i] + ada_b[i]).reshape(1, N_MOD, 1, d),
                              (bsz, N_MOD, 1, d))
        w_z = _layout_w_in(w_in[i])
        wq, wqs = _layout_w_uq(mla_w_uq[i])
        wk, wvt = _layout_w_ukv(mla_w_ukv[i])
        qn, kvn = mla_q_norm[i][None], mla_kv_norm[i][None]
        s5_tabs = _s5_tables(s5_lam_re[i], s5_lam_im[i], s5_log_dt[i], s5_b_re[i], s5_b_im[i],
                             s5_c_re[i], s5_c_im[i])
        fparams = (hy_f_w1[i], hy_f_b1[i], hy_f_w2[i], hy_f_b2[i], hy_f_w3[i], hy_f_freq[i])
        merge_w = (s5_d[i][None], s5_w_glu[i].astype(BF16),
                   w_branch_mla[i].astype(BF16),
                   w_branch_s5[i].astype(BF16), w_branch_hy[i].astype(BF16), w_out[i].astype(BF16))
        w_g = ffn_w_gu[i][:, :D_FF].astype(BF16)
        w_u = ffn_w_gu[i][:, D_FF:].astype(BF16)
        w_d = ffn_w_down[i].astype(BF16)
        g_mix, g_ffn = norm_mix[i][None], norm_ffn[i][None]

        zx = _proj_in(x, g_mix, mx[:, 0], mx[:, 1], w_z)
        zc = _proj_in(ctx, g_mix, mc[:, 0], mc[:, 1], w_z)

        n_keys = n_tok + n_ctx
        q_x, k_all, vt_all = _mla_prep(zx, qn, kvn, wq, wqs, wk, wvt, *rope_x, n_keys, 0)
        q_c, k_all, vt_all = _mla_prep(zc, qn, kvn, wq, wqs, wk, wvt, *rope_c, n_keys, n_tok, (k_all, vt_all))
        a_x = _attention(q_x, k_all, vt_all, n_keys)

        *y5_c, finals = _s5_scan(zc, h_zero, s5_tabs)
        *y5_x, _ = _s5_scan(zx, finals, s5_tabs)

        v_x, g1_x, g2_x = _short_conv(zx, hy_conv_w[i], hy_conv_b[i][None])
        e_x = _hyena_mixer(v_x, (g1_x, g2_x), _hyena_filters(n_tok, *fparams), hy_bias[i])

        x = _merge(x, zx, a_x, y5_x, e_x, *merge_w, mx[:, 2])
        x = _ffn(x, g_ffn, mx[:, 3], mx[:, 4], mx[:, 5], w_g, w_u, w_d, final_norm[None], last)

        if not last:
            a_c = _attention(q_c, k_all, vt_all, n_ctx, n_tok // n_ctx)
            v_c, g1_c, g2_c = _short_conv(zc, hy_conv_w[i], hy_conv_b[i][None])
            e_c = _hyena_mixer(v_c, (g1_c, g2_c), _hyena_filters(n_ctx, *fparams), hy_bias[i])
            ctx = _merge(ctx, zc, a_c, y5_c, e_c, *merge_w, mc[:, 2])
            ctx = _ffn(ctx, g_ffn, mc[:, 3], mc[:, 4], mc[:, 5], w_g, w_u, w_d, final_norm[None], False)
    return x
```

```python
import functools
import math

import jax
import jax.numpy as jnp
import numpy as np
from jax import lax
from jax.experimental import pallas as pl
from jax.experimental.pallas import tpu as pltpu

F32 = jnp.float32
BF16 = jnp.bfloat16
HIGHEST = lax.Precision.HIGHEST

D_MODEL = 1024
GRID_W = 64
NORM_EPS = 1e-6
N_MOD = 6

MLA_HEADS = 8
MLA_NOPE = 64
MLA_ROPE = 32
MLA_V = 64
MLA_Q_LORA = 256
MLA_KV_LORA = 128
ROPE_BASE = 10000.0
HEAD_W = 128
MLA_PREP_ROWS = 512
VT_ROWS = 80
QK_SCALE_LOG2 = (MLA_NOPE + MLA_ROPE) ** -0.5 * math.log2(math.e)
ATTN_SUB = 128
ATTN_LAG = 2
ATTN_UNROLL = 12

S5_WIDTH = 512
S5_GROUP = 16
S5_GROUPS = 32
S5_STATE = 64
S5_HALF = 256
S5_HSTATE = 1024
S5_CHUNK = 128
S5_CHUNKS_PER_STEP = 4

HY_WIDTH = 512
HY_ORDER = 2
HY_BANDS = 16
HY_POS_DIM = 1 + 2 * HY_BANDS
HY_POS_PAD = 64
HY_FILTER_HIDDEN = 64
HY_FILTER_OUT = HY_ORDER * 2 * HY_WIDTH
HY_DECAY_TARGET = 1e-2
HY_DECAY_SHORT = 0.3
HY_DECAY_LONG = 1.5
HY_DECAY_SHIFT = 0.05

D_FF = 2816

OFF_CQ = 0
OFF_CKV = OFF_CQ + MLA_Q_LORA
OFF_KR = OFF_CKV + MLA_KV_LORA
OFF_S5 = OFF_KR + MLA_ROPE
OFF_HY = OFF_S5 + S5_WIDTH
OFF_GATE = OFF_HY + 3 * HY_WIDTH

Z_GATE = 0
Z_HY = 3072
Z_S5 = 4608
Z_MLA = 5120
Z_WIDTH = 5632

VMEM_LIMIT_V7X = 52 * 1024 * 1024


def _cparams(sem, flags=None):
    return pltpu.CompilerParams(dimension_semantics=sem, vmem_limit_bytes=VMEM_LIMIT_V7X, flags=flags)


def _dot(a, b):
    return jnp.dot(a, b, preferred_element_type=F32)


def _rms(x, g):
    return x * lax.rsqrt(jnp.mean(x * x, axis=-1, keepdims=True) + NORM_EPS) * g


def _norm_mod(x, g, shift, scale):
    return _rms(x, g) * (1.0 + scale) + shift


def _proj_kernel(x_ref, g_ref, sh_ref, sc_ref, w_ref, o_ref, h_ref):
    @pl.when(pl.program_id(2) == 0)
    def _():
        h_ref[...] = _norm_mod(x_ref[0], g_ref[...], sh_ref[0], sc_ref[0]).astype(BF16)

    o_ref[0] = _dot(h_ref[...], w_ref[...]).astype(BF16)


def _proj_in(x, g, shift, scale, w):
    bsz, n, d = x.shape
    nz = w.shape[1]
    tm = min(n, 2048)
    tn = 512
    return pl.pallas_call(
        _proj_kernel,
        grid=(bsz, n // tm, nz // tn),
        in_specs=[
            pl.BlockSpec((1, tm, d), lambda b, i, j: (b, i, 0)),
            pl.BlockSpec((1, d), lambda b, i, j: (0, 0)),
            pl.BlockSpec((1, 1, d), lambda b, i, j: (b, 0, 0)),
            pl.BlockSpec((1, 1, d), lambda b, i, j: (b, 0, 0)),
            pl.BlockSpec((d, tn), lambda b, i, j: (0, j)),
        ],
        out_specs=pl.BlockSpec((1, tm, tn), lambda b, i, j: (b, i, j)),
        out_shape=jax.ShapeDtypeStruct((bsz, n, nz), BF16),
        scratch_shapes=[pltpu.VMEM((tm, d), BF16)],
        compiler_params=_cparams(("parallel", "parallel", "arbitrary")),
        name="proj_in",
    )(x, g, shift, scale, w)


def _mla_prep_kernel(z_ref, qn_ref, kvn_ref, wq_ref, wqs_ref, wk_ref, wvt_ref, c_ref, s_ref, *rest):
    q_ref, k_ref, vt_ref = rest[-3:]
    z = z_ref[0].astype(F32)
    hq = _rms(z[:, :MLA_Q_LORA], qn_ref[...]).astype(BF16)
    hkv = _rms(z[:, MLA_Q_LORA:MLA_Q_LORA + MLA_KV_LORA], kvn_ref[...]).astype(BF16)
    krb = z[:, MLA_Q_LORA + MLA_KV_LORA:]
    qa = _dot(hq, wq_ref[...])
    qb = _dot(hq, wqs_ref[...])
    kn = _dot(hkv, wk_ref[...])
    vt = lax.dot_general(wvt_ref[...], hkv, (((1,), (1,)), ((), ())), preferred_element_type=F32)
    ck = c_ref[...]
    sn = s_ref[...]
    lane = lax.broadcasted_iota(jnp.int32, ck.shape, 1)
    cq = jnp.where(lane < MLA_ROPE, ck, 1.0)
    kr = krb * ck + pltpu.roll(krb, HEAD_W - MLA_ROPE // 2, 1) * sn
    row = lax.broadcasted_iota(jnp.int32, (VT_ROWS, z.shape[0]), 0)
    for h in range(MLA_HEADS):
        sl = slice(h * HEAD_W, (h + 1) * HEAD_W)
        q_ref[0, h] = ((qa[:, sl] * cq + qb[:, sl] * sn) * QK_SCALE_LOG2).astype(BF16)
        k_ref[0, h] = (kr + kn[:, sl]).astype(BF16)
        vt_ref[0, h] = jnp.where(row == MLA_V, 1.0, vt[h * VT_ROWS:(h + 1) * VT_ROWS, :]).astype(BF16)


def _mla_prep(z, qn, kvn, wq, wqs, wk, wvt, ctab, stab, n_keys, key_off, kv_bufs=None):
    bsz, n, _ = z.shape
    tm = min(n, MLA_PREP_ROWS)
    assert key_off % tm == 0, "the key offset must be a whole number of row tiles"
    ob = key_off // tm
    hw = MLA_HEADS * HEAD_W
    zb = Z_MLA // 512
    full = lambda shape: pl.BlockSpec(shape, lambda b, i: (0,) * len(shape))
    in_specs = [
        pl.BlockSpec((1, tm, 512), lambda b, i: (b, i, zb)),
        full((1, MLA_Q_LORA)),
        full((1, MLA_KV_LORA)),
        full((MLA_Q_LORA, hw)),
        full((MLA_Q_LORA, hw)),
        full((MLA_KV_LORA, hw)),
        full((MLA_HEADS * VT_ROWS, MLA_KV_LORA)),
        pl.BlockSpec((tm, HEAD_W), lambda b, i: (i, 0)),
        pl.BlockSpec((tm, HEAD_W), lambda b, i: (i, 0)),
    ]
    args = [z, qn, kvn, wq, wqs, wk, wvt, ctab, stab]
    aliases = {}
    if kv_bufs is not None:
        aliases = {len(args): 1, len(args) + 1: 2}
        in_specs += [pl.BlockSpec(memory_space=pl.ANY)] * 2
        args += list(kv_bufs)
    return pl.pallas_call(
        _mla_prep_kernel,
        grid=(bsz, n // tm),
        in_specs=in_specs,
        out_specs=[
            pl.BlockSpec((1, MLA_HEADS, tm, HEAD_W), lambda b, i: (b, 0, i, 0)),
            pl.BlockSpec((1, MLA_HEADS, tm, HEAD_W), lambda b, i: (b, 0, i + ob, 0)),
            pl.BlockSpec((1, MLA_HEADS, VT_ROWS, tm), lambda b, i: (b, 0, 0, i + ob)),
        ],
        out_shape=[
            jax.ShapeDtypeStruct((bsz, MLA_HEADS, n, HEAD_W), BF16),
            jax.ShapeDtypeStruct((bsz, MLA_HEADS, n_keys, HEAD_W), BF16),
            jax.ShapeDtypeStruct((bsz, MLA_HEADS, VT_ROWS, n_keys), BF16),
        ],
        input_output_aliases=aliases,
        compiler_params=_cparams(("parallel", "parallel")),
        name="mla_prep",
    )(*args)


def _attn_kernel(q_ref, k_ref, vt_ref, o_ref, s0_ref, s1_ref, s2_ref, p0_ref, p1_ref, p2_ref, acc_ref, *, tk, nkc):
    q = q_ref[0, 0]
    tq = q.shape[0]
    s_refs = (s0_ref, s1_ref, s2_ref)
    p_refs = (p0_ref, p1_ref, p2_ref)

    ts = min(tk, ATTN_SUB)
    subs = [slice(r, r + ts) for r in range(0, tk, ts)]

    def scores_sub(j, s_ref, sub, cmax):
        off = pl.multiple_of(j * tk + sub.start, ts)
        s = lax.dot_general(k_ref[0, 0, pl.ds(off, ts), :], q, (((1,), (1,)), ((), ())),
                            preferred_element_type=F32)
        s_ref[sub, :] = s
        cm = jnp.max(s, axis=0, keepdims=True)
        return cm if cmax is None else jnp.maximum(cmax, cm)

    def softmax_sub(slot, sub, mn):
        p_refs[slot][sub, :] = jnp.exp2(s_refs[slot][sub, :] - mn).astype(BF16)

    def step(i, slot, carry, do_softmax, do_scores):
        cm, m, alpha = carry
        mn = jnp.maximum(m, cm)
        off = pl.multiple_of(i * tk, tk)
        acc_ref[...] = alpha * acc_ref[...] + _dot(vt_ref[0, 0, :, pl.ds(off, tk)], p_refs[slot][...])
        cm_new, cms = None, []
        for r, sub in enumerate(subs):
            mn_sub = mn
            if do_scores:
                cm_new = scores_sub(i + 2, s_refs[(slot + 2) % 3], sub, cm_new)
                cms.append(cm_new)
                if r >= ATTN_LAG:
                    mn_sub = jnp.maximum(mn, cms[r - ATTN_LAG] - 1e30)
            if do_softmax:
                softmax_sub((slot + 1) % 3, sub, mn_sub)
        if do_softmax:
            alpha, m = jnp.exp2(m - mn), mn
        return (cm_new if do_scores else cm), m, alpha

    m = jnp.full((1, tq), -1e30, F32)
    acc_ref[...] = jnp.zeros_like(acc_ref)
    cm, cm_next = None, None
    for sub in subs:
        cm = scores_sub(0, s0_ref, sub, cm)
        if nkc > 1:
            cm_next = scores_sub(1, s1_ref, sub, cm_next)
    mn = jnp.maximum(m, cm)
    for sub in subs:
        softmax_sub(0, sub, mn)
    carry = (cm_next if nkc > 1 else cm, mn, jnp.exp2(m - mn))

    def body(t, carry):
        for r in range(ATTN_UNROLL):
            carry = step(ATTN_UNROLL * t + r, r % 3, carry, True, True)
        return carry

    nloop = max(nkc - 2, 0) // ATTN_UNROLL
    carry = lax.fori_loop(0, nloop, body, carry)
    for i in range(ATTN_UNROLL * nloop, nkc):
        carry = step(i, i % 3, carry, i + 1 < nkc, i + 2 < nkc)
    acc = acc_ref[...]
    out = acc * (1.0 / acc[MLA_V:MLA_V + 1, :])
    out = jnp.concatenate([out, jnp.zeros((HEAD_W - VT_ROWS, tq), F32)], axis=0).T
    o_ref[0, 0] = out[:, :MLA_V].astype(BF16)


def _kv_chunk(nk):
    for tk in (640, 512, 256, 128):
        if nk % tk == 0:
            return tk
    raise ValueError(f"unsupported key count {nk}")


def _attention(q, k, vt, nk, key_block=0):
    bsz, nh, nq, _ = q.shape
    tq = min(nq, 1024)
    tk = _kv_chunk(nk)
    return pl.pallas_call(
        functools.partial(_attn_kernel, tk=tk, nkc=nk // tk),
        grid=(bsz, nh, nq // tq),
        in_specs=[
            pl.BlockSpec((1, 1, tq, HEAD_W), lambda b, h, i: (b, h, i, 0)),
            pl.BlockSpec((1, 1, nk, HEAD_W), lambda b, h, i: (b, h, key_block, 0)),
            pl.BlockSpec((1, 1, VT_ROWS, nk), lambda b, h, i: (b, h, 0, key_block)),
        ],
        out_specs=pl.BlockSpec((1, 1, tq, MLA_V), lambda b, h, i: (b, h, i, 0)),
        out_shape=jax.ShapeDtypeStruct((bsz, nh, nq, MLA_V), BF16),
        scratch_shapes=[pltpu.VMEM((tk, tq), F32)] * 3 + [pltpu.VMEM((tk, tq), BF16)] * 3
        + [pltpu.VMEM((VT_ROWS, tq), F32)],
        compiler_params=_cparams(("parallel", "parallel", "arbitrary")),
        name="attention",
    )(q, k, vt)


def _cmul(ar, ai, br, bi):
    return ar * br - ai * bi, ar * bi + ai * br


def _s5_kernel(u_ref, h0_ref, bm_ref, cm_ref, wn_ref, wp_ref, l1_ref, tri_ref, y_ref, hf_ref,
               carry_ref, h0cat_ref, h1cat_ref, *, tc, nr, ns, reverse):
    i = pl.program_id(1)

    @pl.when(i == 0)
    def _():
        carry_ref[...] = h0_ref[0]

    u = u_ref[0]
    tri = tri_ref[...]
    order = range(nr - 1, -1, -1) if reverse else range(nr)
    edge = 0 if reverse else tc - 1
    hcat_refs = (h0cat_ref, h1cat_ref)
    bus = [_dot(u[:, hf * S5_HALF:(hf + 1) * S5_HALF], bm_ref[hf]) for hf in range(2)]
    carries = [(carry_ref[2 * hf:2 * hf + 1, :], carry_ref[2 * hf + 1:2 * hf + 2, :]) for hf in range(2)]
    for c in order:
        rows = slice(c * tc, (c + 1) * tc)
        for hf in range(2):
            bu = bus[hf]
            xr, xi = _cmul(wn_ref[hf, 0], wn_ref[hf, 1], bu[rows, :S5_HSTATE], bu[rows, S5_HSTATE:])
            s = _dot(tri, jnp.concatenate([xr, xi], axis=1).astype(BF16))
            cr, ci = _cmul(l1_ref[hf, 0], l1_ref[hf, 1], *carries[hf])
            hr, hi = _cmul(wp_ref[hf, 0], wp_ref[hf, 1], s[:, :S5_HSTATE] + cr, s[:, S5_HSTATE:] + ci)
            hcat_refs[hf][rows, :S5_HSTATE] = hr.astype(BF16)
            hcat_refs[hf][rows, S5_HSTATE:] = hi.astype(BF16)
            carries[hf] = (hr[edge:edge + 1], hi[edge:edge + 1])
    for hf in range(2):
        carry_ref[2 * hf:2 * hf + 1, :] = carries[hf][0]
        carry_ref[2 * hf + 1:2 * hf + 2, :] = carries[hf][1]
    y_ref[0] = jnp.concatenate([_dot(hcat_refs[hf][...], cm_ref[hf]) for hf in range(2)], axis=1)

    @pl.when(i == ns - 1)
    def _():
        hf_ref[0] = carry_ref[...]


def _s5_scan_dir(z, h0, tabs, reverse):
    bm, cm, wn, wp, l1, tri = tabs
    bsz, n, _ = z.shape
    tc = S5_CHUNK
    nr = min(S5_CHUNKS_PER_STEP, n // tc)
    rows = nr * tc
    ns = n // rows
    zb = Z_S5 // S5_WIDTH
    blk = (lambda i: ns - 1 - i) if reverse else (lambda i: i)
    full = lambda shape: pl.BlockSpec(shape, lambda b, i: (0,) * len(shape))
    return pl.pallas_call(
        functools.partial(_s5_kernel, tc=tc, nr=nr, ns=ns, reverse=reverse),
        grid=(bsz, ns),
        in_specs=[
            pl.BlockSpec((1, rows, S5_WIDTH), lambda b, i: (b, blk(i), zb)),
            pl.BlockSpec((1, 4, S5_HSTATE), lambda b, i: (b, 0, 0)),
            full((2, S5_HALF, 2 * S5_HSTATE)),
            full((2, 2 * S5_HSTATE, S5_HALF)),
            full((2, 2, tc, S5_HSTATE)),
            full((2, 2, tc, S5_HSTATE)),
            full((2, 2, 1, S5_HSTATE)),
            full((tc, tc)),
        ],
        out_specs=[
            pl.BlockSpec((1, rows, S5_WIDTH), lambda b, i: (b, blk(i), 0)),
            pl.BlockSpec((1, 4, S5_HSTATE), lambda b, i: (b, 0, 0)),
        ],
        out_shape=[
            jax.ShapeDtypeStruct((bsz, n, S5_WIDTH), F32),
            jax.ShapeDtypeStruct((bsz, 4, S5_HSTATE), F32),
        ],
        scratch_shapes=[pltpu.VMEM((4, S5_HSTATE), F32),
                        pltpu.VMEM((rows, 2 * S5_HSTATE), BF16), pltpu.VMEM((rows, 2 * S5_HSTATE), BF16)],
        compiler_params=_cparams(("parallel", "arbitrary")),
        name="s5_scan",
    )(z, h0, bm, cm, wn, wp, l1, tri)


def _s5_scan(z, h0, tabs):
    outs = [_s5_scan_dir(z, h0[:, d], [t[d] for t in tabs], reverse=bool(d)) for d in range(2)]
    return outs[0][0], outs[1][0], jnp.stack([outs[0][1], outs[1][1]], axis=1)


def _s5_tables(lam_re, lam_im, log_dt, b_re, b_im, c_re, c_im):
    tc = S5_CHUNK
    dt = jnp.exp(log_dt)[..., None]
    zr, zi = lam_re * dt, lam_im * dt
    mag = jnp.exp(zr)
    lbr, lbi = mag * jnp.cos(zi), mag * jnp.sin(zi)
    den = lam_re * lam_re + lam_im * lam_im
    nr, ni = lbr - 1.0, lbi
    cfr = (nr * lam_re + ni * lam_im) / den
    cfi = (ni * lam_re - nr * lam_im) / den
    bbr = cfr[..., None] * b_re - cfi[..., None] * b_im
    bbi = cfr[..., None] * b_im + cfi[..., None] * b_re
    eye = jnp.eye(S5_GROUP, dtype=F32)

    def blockdiag_in(b):
        b = b.reshape(2, 2, S5_GROUP, S5_STATE, S5_GROUP)
        return jnp.einsum('gk,dhgpn->dhgnkp', eye, b).reshape(2, 2, S5_HALF, S5_HSTATE)

    def blockdiag_out(c):
        c = c.reshape(2, 2, S5_GROUP, S5_GROUP, S5_STATE)
        return jnp.einsum('gk,dhgnp->dhgpkn', eye, c).reshape(2, 2, S5_HSTATE, S5_HALF)

    bm = jnp.concatenate([blockdiag_in(bbr), blockdiag_in(bbi)], axis=-1).astype(BF16)
    cm = jnp.concatenate([blockdiag_out(c_re), blockdiag_out(-c_im)], axis=-2).astype(BF16)

    def powers(k):
        zr_ = zr.reshape(2, 2, 1, S5_HSTATE)
        zi_ = zi.reshape(2, 2, 1, S5_HSTATE)
        kk = k[:, None, :, None]
        m = jnp.exp(kk * zr_)
        return jnp.stack([m * jnp.cos(kk * zi_), m * jnp.sin(kk * zi_)], axis=2)

    mid = tc // 2
    t = jnp.arange(tc, dtype=F32) - mid
    wn = powers(jnp.stack([-t, -t[::-1]]))
    wp = powers(jnp.stack([t, t[::-1]]))
    l1 = powers(jnp.full((2, 1), mid + 1, F32))
    r = jnp.arange(tc)
    tri = jnp.stack([r[:, None] >= r[None, :], r[:, None] <= r[None, :]]).astype(BF16)
    return bm, cm, wn, wp, l1, tri


HALO = 16


def _sconv_kernel(z_ref, zp_ref, zn_ref, w_ref, b_ref, v_ref, g1_ref, g2_ref, *, nt):
    i = pl.program_id(1)
    z = z_ref[0].astype(F32)
    tm = z.shape[0]
    row = lax.broadcasted_iota(jnp.int32, z.shape, 0)
    prev = jnp.where(i > 0, zp_ref[0, HALO - 1:HALO, :].astype(F32), 0.0)
    nxt = jnp.where(i < nt - 1, zn_ref[0, 0:1, :].astype(F32), 0.0)
    up = jnp.where(row == 0, prev, pltpu.roll(z, 1, 0))
    dn = jnp.where(row == tm - 1, nxt, pltpu.roll(z, tm - 1, 0))
    u = up * w_ref[0:1, :] + z * w_ref[1:2, :] + dn * w_ref[2:3, :] + b_ref[...]
    for k, o_ref in enumerate((v_ref, g1_ref, g2_ref)):
        o_ref[0] = u[:, k * HY_WIDTH:(k + 1) * HY_WIDTH]


def _short_conv(z, w, b):
    bsz, n, _ = z.shape
    cw = 3 * HY_WIDTH
    tm = min(n, 1024)
    nt = n // tm
    zb = Z_HY // cw
    rh = tm // HALO
    out = pl.BlockSpec((1, tm, HY_WIDTH), lambda b_, i: (b_, i, 0))
    return pl.pallas_call(
        functools.partial(_sconv_kernel, nt=nt),
        grid=(bsz, nt),
        in_specs=[
            pl.BlockSpec((1, tm, cw), lambda b_, i: (b_, i, zb)),
            pl.BlockSpec((1, HALO, cw), lambda b_, i: (b_, jnp.maximum(i * rh - 1, 0), zb)),
            pl.BlockSpec((1, HALO, cw), lambda b_, i: (b_, jnp.minimum((i + 1) * rh, n // HALO - 1), zb)),
            pl.BlockSpec((3, cw), lambda b_, i: (0, 0)),
            pl.BlockSpec((1, cw), lambda b_, i: (0, 0)),
        ],
        out_specs=[out] * 3,
        out_shape=[jax.ShapeDtypeStruct((bsz, n, HY_WIDTH), F32)] * 3,
        compiler_params=_cparams(("parallel", "parallel")),
        name="short_conv",
    )(z, z, z, w, b)


def _filt_kernel(f_ref, w1_ref, b1_ref, w2_ref, b2_ref, w3h_ref, w3l_ref, fq_ref, dl_ref, o_ref, *, n_tok):
    z = f_ref[...]
    tm = z.shape[0]
    fq = fq_ref[...]
    hid = jnp.sin(fq * (jnp.dot(z, w1_ref[...], precision=HIGHEST, preferred_element_type=F32) + b1_ref[...]))
    hid = jnp.sin(fq * (jnp.dot(hid, w2_ref[...], precision=HIGHEST, preferred_element_type=F32) + b2_ref[...]))
    hid_hi = hid.astype(BF16)
    hid_lo = (hid - hid_hi.astype(F32)).astype(BF16)

    def filters(rows, d):
        wh, wl = w3h_ref[d], w3l_ref[d]
        f = _dot(hid_hi[rows], wh) + (_dot(hid_hi[rows], wl) + _dot(hid_lo[rows], wh))
        return f * (jnp.exp(-z[rows, 0:1] * dl_ref[d]) + HY_DECAY_SHIFT)

    i = pl.program_id(0)
    filt = filters(slice(None), jnp.where(i * tm >= n_tok, 1, 0))
    m = i * tm + lax.broadcasted_iota(jnp.int32, filt.shape, 0)
    filt = jnp.where(m == n_tok, 0.0, filt)
    for o in range(HY_ORDER):
        o_ref[o] = filt[:, o * HY_WIDTH:(o + 1) * HY_WIDTH]

    @pl.when(i == 0)
    def _():
        head = slice(0, HALO)
        lag0 = filters(head, 1)
        lag0 = jnp.where(lax.broadcasted_iota(jnp.int32, lag0.shape, 0) == 0, lag0, 0.0)
        for o in range(HY_ORDER):
            o_ref[o, head, :] += lag0[:, o * HY_WIDTH:(o + 1) * HY_WIDTH]


def _hyena_filter_feats(n_tok):
    m = jnp.arange(2 * n_tok)
    lag = jnp.where(m < n_tok, m, jnp.where(m > n_tok, 2 * n_tok - m, 0))
    t = (lag.astype(F32) / (n_tok - 1))[:, None]
    w = (2.0 * math.pi * lag.astype(F32) / n_tok)[:, None]
    bands = jnp.linspace(1e-4, HY_BANDS - 1, HY_BANDS, dtype=F32)[None, :]
    feats = jnp.concatenate([t, jnp.cos(bands * w), -jnp.sin(bands * w)], axis=-1)
    return jnp.pad(feats, ((0, 0), (0, HY_POS_PAD - HY_POS_DIM)))


def _hyena_filters(n_tok, w1, b1, w2, b2, w3, freq):
    feats = _hyena_filter_feats(n_tok)
    deltas = jnp.abs(jnp.linspace(math.log(HY_DECAY_TARGET) / HY_DECAY_SHORT,
                                  math.log(HY_DECAY_TARGET) / HY_DECAY_LONG, HY_FILTER_OUT, dtype=F32))[None, :]
    w1p = jnp.pad(w1, ((0, HY_POS_PAD - HY_POS_DIM), (0, 0)))

    def by_direction(a):
        a = a.reshape(a.shape[:-1] + (HY_ORDER, 2, HY_WIDTH))
        return jnp.moveaxis(a, -2, 0).reshape((2,) + a.shape[:-3] + (HY_ORDER * HY_WIDTH,))

    w3, deltas = by_direction(w3), by_direction(deltas)
    w3_hi = w3.astype(BF16)
    w3_lo = (w3 - w3_hi.astype(F32)).astype(BF16)
    n2 = 2 * n_tok
    tm = min(n_tok, 1024)
    full = lambda shape: pl.BlockSpec(shape, lambda i: (0,) * len(shape))
    hh = HY_FILTER_HIDDEN
    wd = HY_ORDER * HY_WIDTH
    return pl.pallas_call(
        functools.partial(_filt_kernel, n_tok=n_tok),
        grid=(n2 // tm,),
        in_specs=[
            pl.BlockSpec((tm, HY_POS_PAD), lambda i: (i, 0)),
            full((HY_POS_PAD, hh)), full((1, hh)), full((hh, hh)), full((1, hh)),
            full((2, hh, wd)), full((2, hh, wd)), full((1, hh)), full((2, 1, wd)),
        ],
        out_specs=pl.BlockSpec((HY_ORDER, tm, HY_WIDTH), lambda i: (0, i, 0)),
        out_shape=jax.ShapeDtypeStruct((HY_ORDER, n2, HY_WIDTH), F32),
        compiler_params=_cparams(("parallel",)),
        name="hyena_filter",
    )(feats, w1p, b1[None], w2, b2[None], w3_hi, w3_lo, freq[None], deltas)


def _fft_split(n_fft):
    if n_fft <= 1024:
        return n_fft, 1
    n1 = 1 << (int(math.log2(n_fft)) // 2)
    return n1, n_fft // n1


def _cis(num, den, sign):
    ang = (2.0 * math.pi / den) * num.astype(F32)
    return jnp.cos(ang), sign * jnp.sin(ang)


def _fft_tables(n_fft):
    n1, n2 = _fft_split(n_fft)
    a = jnp.arange(n1)
    f1r, f1i = _cis((a[:, None] * a[None, :]) % n1, n1, -1.0)
    s1 = jnp.concatenate([f1r, f1i], axis=0).astype(BF16)
    half = n1 // 2
    s3 = (jnp.concatenate([f1r[:half], -f1i[:half]], axis=0) / n_fft).astype(BF16)
    if n2 == 1:
        return s1, s3, None, None
    b = jnp.arange(n2)
    f2r, f2i = _cis((b[:, None] * b[None, :]) % n2, n2, -1.0)
    twr, twi = _cis((jnp.arange(n1)[:, None] * b[None, :]) % n_fft, n_fft, -1.0)
    fr, fi = _cmul(f2r[None], f2i[None], twr[:, None, :], twi[:, None, :])
    ft = jnp.concatenate([fr, fi], axis=1).astype(BF16)
    frt, fit = _cmul(f2r[None], f2i[None], twr[:, :, None], twi[:, :, None])
    it = jnp.concatenate([frt, -fit], axis=1).astype(BF16)
    return s1, s3, ft, it


def _stacked_cdot(s, xr, xi, m):
    p = _dot(s, xr.astype(BF16))
    if xi is None:
        return p[:m], p[m:]
    q = _dot(s, xi.astype(BF16))
    return p[:m] - q[m:], q[:m] + p[m:]


def _fft_a_kernel(*refs, parts, n1, has_g):
    if has_g:
        x_ref, s_ref, g_ref, y_ref = refs
    else:
        x_ref, s_ref, y_ref = refs
    yr, yi = _stacked_cdot(s_ref[...], x_ref[0], x_ref[1] if parts == 2 else None, n1)
    if has_g:
        yr, yi = _cmul(yr, yi, g_ref[0], g_ref[1])
    y_ref[0] = yr
    y_ref[1] = yi


def _fft_a(x, s1, g=None):
    parts, a_rows, cols = x.shape
    n1 = s1.shape[0] // 2
    tc = min(cols, 2048)
    in_specs = [
        pl.BlockSpec((parts, a_rows, tc), lambda j: (0, 0, j)),
        pl.BlockSpec((2 * n1, a_rows), lambda j: (0, 0)),
    ]
    args = [x, s1[:, :a_rows]]
    if g is not None:
        in_specs.append(pl.BlockSpec((2, n1, tc), lambda j: (0, 0, j)))
        args.append(g)
    return pl.pallas_call(
        functools.partial(_fft_a_kernel, parts=parts, n1=n1, has_g=g is not None),
        grid=(cols // tc,),
        in_specs=in_specs,
        out_specs=pl.BlockSpec((2, n1, tc), lambda j: (0, 0, j)),
        out_shape=jax.ShapeDtypeStruct((2, n1, cols), F32),
        compiler_params=_cparams(("parallel",)),
        name="fft_stage_a",
    )(*args)


FFT_CB = 4


def _fft_b_kernel(*refs, n2, inverse):
    if inverse:
        y_ref, ft_ref, it_ref, g_ref, o_ref = refs
    else:
        y_ref, ft_ref, o_ref = refs
    for c in range(FFT_CB):
        xr, xi = _stacked_cdot(ft_ref[c], y_ref[0, c], y_ref[1, c], n2)
        if inverse:
            zr, zi = _cmul(xr, xi, g_ref[0, c], g_ref[1, c])
            xr, xi = _stacked_cdot(it_ref[c], zr, zi, n2)
        o_ref[0, c] = xr.astype(o_ref.dtype)
        o_ref[1, c] = xi.astype(o_ref.dtype)


def _fft_b(y, ft, it=None, g=None):
    _, n1, n2, ch = y.shape
    inverse = it is not None
    blk = pl.BlockSpec((2, FFT_CB, n2, ch), lambda c: (0, c, 0, 0))
    mat = pl.BlockSpec((FFT_CB, 2 * n2, n2), lambda c: (c, 0, 0))
    in_specs, args = [blk, mat], [y, ft]
    if inverse:
        in_specs += [mat, blk]
        args += [it, g]
    return pl.pallas_call(
        functools.partial(_fft_b_kernel, n2=n2, inverse=inverse),
        grid=(n1 // FFT_CB,),
        in_specs=in_specs,
        out_specs=blk,
        out_shape=jax.ShapeDtypeStruct(y.shape, BF16 if inverse else F32),
        compiler_params=_cparams(("parallel",)),
        name="fft_stage_b",
    )(*args)


def _fft_c_kernel(y_ref, s_ref, x_ref, g_ref, b_ref, o_ref, *, half):
    cr, ci = _stacked_cdot(s_ref[...], y_ref[0], y_ref[1], half)
    bias = b_ref[...]
    o_ref[0] = g_ref[0] * (cr + bias * x_ref[0])
    o_ref[1] = g_ref[1] * (ci + bias * x_ref[1])


def _fft_c(y, s3, x, gate, bias_cols):
    _, n1, cols = y.shape
    half = n1 // 2
    tc = min(cols, 2048)
    io = pl.BlockSpec((2, half, tc), lambda j: (0, 0, j))
    return pl.pallas_call(
        functools.partial(_fft_c_kernel, half=half),
        grid=(cols // tc,),
        in_specs=[
            pl.BlockSpec((2, n1, tc), lambda j: (0, 0, j)),
            pl.BlockSpec((n1, n1), lambda j: (0, 0)),
            io, io,
            pl.BlockSpec((1, tc), lambda j: (0, j)),
        ],
        out_specs=io,
        out_shape=jax.ShapeDtypeStruct((2, half, cols), F32),
        compiler_params=_cparams(("parallel",)),
        name="fft_stage_c",
    )(y, s3, x, gate, bias_cols)


FFT_BT = 16


def _to_fine_major(x):
    return pltpu.einshape("abc->bac", x)


def _fft_a4_kernel(x_ref, s_ref, y_ref, *, parts, n1):
    s = s_ref[...]
    xr = _to_fine_major(x_ref[0].astype(BF16))
    xi = _to_fine_major(x_ref[1].astype(BF16)) if parts == 2 else None
    out = [_stacked_cdot(s, xr[b], None if xi is None else xi[b], n1) for b in range(FFT_BT)]
    y_ref[0] = pltpu.einshape("bac->abc", jnp.stack([o[0] for o in out])).astype(BF16)
    y_ref[1] = pltpu.einshape("bac->abc", jnp.stack([o[1] for o in out])).astype(BF16)


def _fft_a4(x, s1):
    parts, a_rows, n2, ch = x.shape
    n1 = s1.shape[0] // 2
    return pl.pallas_call(
        functools.partial(_fft_a4_kernel, parts=parts, n1=n1),
        grid=(n2 // FFT_BT,),
        in_specs=[
            pl.BlockSpec((parts, a_rows, FFT_BT, ch), lambda j: (0, 0, j, 0)),
            pl.BlockSpec((2 * n1, a_rows), lambda j: (0, 0)),
        ],
        out_specs=pl.BlockSpec((2, n1, FFT_BT, ch), lambda j: (0, 0, j, 0)),
        out_shape=jax.ShapeDtypeStruct((2, n1, n2, ch), BF16),
        compiler_params=_cparams(("parallel",)),
        name="fft_stage_a",
    )(x, s1[:, :a_rows])


def _fft_c4_kernel(y_ref, s_ref, x_ref, g_ref, b_ref, o_ref, c_ref, *, half):
    s = s_ref[...]
    yr = _to_fine_major(y_ref[0].astype(BF16))
    yi = _to_fine_major(y_ref[1].astype(BF16))
    out = [_stacked_cdot(s, yr[b], yi[b], half) for b in range(FFT_BT)]
    c_ref[0] = pltpu.einshape("bac->abc", jnp.stack([o[0] for o in out]))
    c_ref[1] = pltpu.einshape("bac->abc", jnp.stack([o[1] for o in out]))
    o_ref[...] = g_ref[...] * (c_ref[...] + b_ref[...][None, None] * x_ref[...])


def _fft_c4(y, s3, x, gate, bias):
    _, n1, n2, ch = y.shape
    half = n1 // 2
    io = pl.BlockSpec((2, half, FFT_BT, ch), lambda j: (0, 0, j, 0))
    return pl.pallas_call(
        functools.partial(_fft_c4_kernel, half=half),
        grid=(n2 // FFT_BT,),
        in_specs=[
            pl.BlockSpec((2, n1, FFT_BT, ch), lambda j: (0, 0, j, 0)),
            pl.BlockSpec((n1, n1), lambda j: (0, 0)),
            io, io,
            pl.BlockSpec((1, ch), lambda j: (0, 0)),
        ],
        out_specs=io,
        out_shape=jax.ShapeDtypeStruct((2, half, n2, ch), F32),
        scratch_shapes=[pltpu.VMEM((2, half, FFT_BT, ch), F32)],
        compiler_params=_cparams(("parallel",)),
        name="fft_stage_c",
    )(y, s3, x, gate, bias)


def _hyena_mixer(v, gates, gfilt, bias):
    bsz, n, ch = v.shape
    assert bsz == 2, "the complex packing of the long convolution pairs exactly two batch rows"
    n_fft = 2 * n
    n1, n2 = _fft_split(n_fft)
    s1, s3, ft, it = _fft_tables(n_fft)
    if n2 == 1:
        y = v
        for o in range(HY_ORDER):
            gspec = _fft_a(gfilt[o][None], s1)
            y = _fft_c(_fft_a(y, s1, gspec), s3, y, gates[o], bias[o][None])
        return y
    half = n1 // 2
    y = v.reshape(2, half, n2, ch)
    for o in range(HY_ORDER):
        gspec = _fft_b(_fft_a4(gfilt[o].reshape(1, n1, n2, ch), s1), ft)
        spec = _fft_b(_fft_a4(y, s1), ft, it, gspec)
        y = _fft_c4(spec, s3, y, gates[o].reshape(2, half, n2, ch), bias[o][None])
    return y.reshape(2, n, ch)


def _gelu_tanh(x):
    return 0.5 * x * (1.0 + jnp.tanh(math.sqrt(2.0 / math.pi) * (x + 0.044715 * (x * x * x))))


def _merge_kernel(x_ref, g0_ref, g1_ref, g2_ref, a_ref, yf_ref, yb_ref, u_ref, e_ref, d_ref,
                  wglu_ref, wm_ref, ws_ref, wh_ref, wo_ref, gt_ref, o_ref):
    att = _dot(jnp.concatenate([a_ref[0, h] for h in range(MLA_HEADS)], axis=-1), wm_ref[...])
    y = yf_ref[0] + yb_ref[0] + d_ref[...] * u_ref[0].astype(F32)
    gy = _gelu_tanh(y)
    s5 = gy * jax.nn.sigmoid(_dot(gy.astype(BF16), wglu_ref[...]))
    s5 = _dot(s5.astype(BF16), ws_ref[...])
    hy = _dot(e_ref[0].astype(BF16), wh_ref[...])
    merged = jax.nn.sigmoid(g0_ref[0].astype(F32)) * att
    merged = merged + jax.nn.sigmoid(g1_ref[0].astype(F32)) * s5
    merged = merged + jax.nn.sigmoid(g2_ref[0].astype(F32)) * hy
    o_ref[0] = x_ref[0] + gt_ref[0] * _dot(merged.astype(BF16), wo_ref[...])


def _merge(x, z, att, y5, e, s5_d, w_glu, w_mla, w_s5, w_hy, w_out, gate):
    bsz, n, d = x.shape
    tm = min(n, 512)
    full = lambda shape: pl.BlockSpec(shape, lambda b, i: (0,) * len(shape))
    zs5 = Z_S5 // S5_WIDTH
    return pl.pallas_call(
        _merge_kernel,
        grid=(bsz, n // tm),
        in_specs=[
            pl.BlockSpec((1, tm, d), lambda b, i: (b, i, 0)),
            pl.BlockSpec((1, tm, d), lambda b, i: (b, i, 0)),
            pl.BlockSpec((1, tm, d), lambda b, i: (b, i, 1)),
            pl.BlockSpec((1, tm, d), lambda b, i: (b, i, 2)),
            pl.BlockSpec((1, MLA_HEADS, tm, MLA_V), lambda b, i: (b, 0, i, 0)),
            pl.BlockSpec((1, tm, S5_WIDTH), lambda b, i: (b, i, 0)),
            pl.BlockSpec((1, tm, S5_WIDTH), lambda b, i: (b, i, 0)),
            pl.BlockSpec((1, tm, S5_WIDTH), lambda b, i: (b, i, zs5)),
            pl.BlockSpec((1, tm, HY_WIDTH), lambda b, i: (b, i, 0)),
            full((1, S5_WIDTH)),
            full((S5_WIDTH, S5_WIDTH)),
            full((MLA_HEADS * MLA_V, d)),
            full((S5_WIDTH, d)),
            full((HY_WIDTH, d)),
            full((d, d)),
            pl.BlockSpec((1, 1, d), lambda b, i: (b, 0, 0)),
        ],
        out_specs=pl.BlockSpec((1, tm, d), lambda b, i: (b, i, 0)),
        out_shape=jax.ShapeDtypeStruct((bsz, n, d), F32),
        compiler_params=_cparams(("parallel", "parallel")),
        name="merge",
    )(x, z, z, z, att, y5[0], y5[1], z, e, s5_d, w_glu, w_mla, w_s5, w_hy, w_out, gate)


def _ffn_kernel(x_ref, g_ref, sh_ref, sc_ref, gt_ref, wg_ref, wu_ref, wd_ref, fg_ref, o_ref, h_ref, acc_ref,
                *, nk, final):
    k = pl.program_id(2)

    @pl.when(k == 0)
    def _():
        h_ref[...] = _norm_mod(x_ref[0], g_ref[...], sh_ref[0], sc_ref[0]).astype(BF16)
        acc_ref[...] = jnp.zeros_like(acc_ref)

    h = h_ref[...]
    act = jax.nn.silu(_dot(h, wg_ref[...])) * _dot(h, wu_ref[...])
    acc_ref[...] += _dot(act.astype(BF16), wd_ref[...])

    @pl.when(k == nk - 1)
    def _():
        r = x_ref[0] + gt_ref[0] * acc_ref[...]
        o_ref[0] = _rms(r, fg_ref[...]) if final else r


def _ffn(x, g, shift, scale, gate, w_g, w_u, w_d, final_g, final):
    bsz, n, d = x.shape
    dff = w_g.shape[1]
    tm = min(n, 1024)
    tf = 256
    nk = dff // tf
    vec = pl.BlockSpec((1, 1, d), lambda b, i, k: (b, 0, 0))
    row = pl.BlockSpec((1, d), lambda b, i, k: (0, 0))
    return pl.pallas_call(
        functools.partial(_ffn_kernel, nk=nk, final=final),
        grid=(bsz, n // tm, nk),
        in_specs=[
            pl.BlockSpec((1, tm, d), lambda b, i, k: (b, i, 0)),
            row, vec, vec, vec,
            pl.BlockSpec((d, tf), lambda b, i, k: (0, k)),
            pl.BlockSpec((d, tf), lambda b, i, k: (0, k)),
            pl.BlockSpec((tf, d), lambda b, i, k: (k, 0)),
            row,
        ],
        out_specs=pl.BlockSpec((1, tm, d), lambda b, i, k: (b, i, 0)),
        out_shape=jax.ShapeDtypeStruct((bsz, n, d), F32),
        scratch_shapes=[pltpu.VMEM((tm, d), BF16), pltpu.VMEM((tm, d), F32)],
        compiler_params=_cparams(("parallel", "parallel", "arbitrary")),
        name="ffn",
    )(x, g, shift, scale, gate, w_g, w_u, w_d, final_g)


def _rope_tables(n_tok):
    rows = n_tok // GRID_W
    row = jnp.broadcast_to(jnp.arange(rows, dtype=F32)[:, None], (rows, GRID_W)).reshape(-1)
    col = jnp.broadcast_to(jnp.arange(GRID_W, dtype=F32)[None, :], (rows, GRID_W)).reshape(-1)
    n_freq = MLA_ROPE // 4
    inv = ROPE_BASE ** (-jnp.arange(n_freq, dtype=F32) / n_freq)
    ang = jnp.concatenate([row[:, None] * inv, col[:, None] * inv], axis=-1)
    cos, sin = jnp.cos(ang), jnp.sin(ang)
    pad = jnp.zeros((n_tok, HEAD_W - MLA_ROPE), F32)
    return jnp.concatenate([cos, cos, pad], axis=-1), jnp.concatenate([-sin, sin, pad], axis=-1)


def _identity_rope_tables(n_tok):
    one = jnp.ones((n_tok, MLA_ROPE), F32)
    pad = jnp.zeros((n_tok, HEAD_W - MLA_ROPE), F32)
    return jnp.concatenate([one, pad], axis=-1), jnp.zeros((n_tok, HEAD_W), F32)


def _layout_w_in(w):
    kr = w[:, OFF_KR:OFF_S5]
    x1, x2 = kr[:, 0::2], kr[:, 1::2]
    pad = jnp.zeros((w.shape[0], HEAD_W - 3 * (MLA_ROPE // 2)), w.dtype)
    return jnp.concatenate([w[:, OFF_GATE:], w[:, OFF_HY:OFF_GATE], w[:, OFF_S5:OFF_HY],
                            w[:, OFF_CQ:OFF_CKV], w[:, OFF_CKV:OFF_KR], x1, x2, x1, pad], axis=1).astype(BF16)


def _layout_w_uq(w):
    w = w.reshape(MLA_Q_LORA, MLA_HEADS, MLA_NOPE + MLA_ROPE)
    nope, rope = w[..., :MLA_NOPE], w[..., MLA_NOPE:]
    x1, x2 = rope[..., 0::2], rope[..., 1::2]
    z32 = jnp.zeros_like(rope)
    wq = jnp.concatenate([x1, x2, z32, nope], axis=-1)
    wqs = jnp.concatenate([x2, x1, z32, jnp.zeros_like(nope)], axis=-1)
    shape = (MLA_Q_LORA, MLA_HEADS * HEAD_W)
    return wq.reshape(shape).astype(BF16), wqs.reshape(shape).astype(BF16)


def _layout_w_ukv(w):
    w = w.reshape(MLA_KV_LORA, MLA_HEADS, MLA_NOPE + MLA_V)
    nope, val = w[..., :MLA_NOPE], w[..., MLA_NOPE:]
    wk = jnp.concatenate([jnp.zeros_like(nope), nope], axis=-1).reshape(MLA_KV_LORA, MLA_HEADS * HEAD_W)
    wvt = jnp.concatenate([val, jnp.zeros_like(val[..., :VT_ROWS - MLA_V])], axis=-1).reshape(
        MLA_KV_LORA, MLA_HEADS * VT_ROWS).T
    return wk.astype(BF16), wvt.astype(BF16)


def kernel(x, c, ctx, c_ctx, ada_w, ada_b, norm_mix, w_in, mla_q_norm, mla_w_uq, mla_kv_norm, mla_w_ukv,
           s5_lam_re, s5_lam_im, s5_log_dt, s5_b_re, s5_b_im, s5_c_re, s5_c_im, s5_d, s5_w_glu,
           hy_conv_w, hy_conv_b, hy_f_w1, hy_f_b1, hy_f_w2, hy_f_b2, hy_f_w3, hy_f_freq, hy_bias,
           w_branch_mla, w_branch_s5, w_branch_hy, w_out, norm_ffn, ffn_w_gu, ffn_w_down, final_norm):
    bsz, n_tok, d = x.shape
    n_ctx = ctx.shape[1]
    depth = ada_w.shape[0]
    rope_x = _rope_tables(n_tok)
    rope_c = _identity_rope_tables(n_ctx)
    h_zero = jnp.zeros((bsz, 2, 4, S5_HSTATE), F32)

    for i in range(depth):
        last = i == depth - 1
        mx = (jax.nn.silu(c) @ ada_w[i] + ada_b[i]).reshape(bsz, N_MOD, 1, d)
        mc = jnp.broadcast_to((jax.nn.silu(c_ctx) @ ada_w[i] + ada_b[i]).reshape(1, N_MOD, 1, d),
                              (bsz, N_MOD, 1, d))
        w_z = _layout_w_in(w_in[i])
        wq, wqs = _layout_w_uq(mla_w_uq[i])
        wk, wvt = _layout_w_ukv(mla_w_ukv[i])
        qn, kvn = mla_q_norm[i][None], mla_kv_norm[i][None]
        s5_tabs = _s5_tables(s5_lam_re[i], s5_lam_im[i], s5_log_dt[i], s5_b_re[i], s5_b_im[i],
                             s5_c_re[i], s5_c_im[i])
        fparams = (hy_f_w1[i], hy_f_b1[i], hy_f_w2[i], hy_f_b2[i], hy_f_w3[i], hy_f_freq[i])
        merge_w = (s5_d[i][None], s5_w_glu[i].astype(BF16),
                   w_branch_mla[i].astype(BF16),
                   w_branch_s5[i].astype(BF16), w_branch_hy[i].astype(BF16), w_out[i].astype(BF16))
        w_g = ffn_w_gu[i][:, :D_FF].astype(BF16)
        w_u = ffn_w_gu[i][:, D_FF:].astype(BF16)
        w_d = ffn_w_down[i].astype(BF16)
        g_mix, g_ffn = norm_mix[i][None], norm_ffn[i][None]

        zx = _proj_in(x, g_mix, mx[:, 0], mx[:, 1], w_z)
        zc = _proj_in(ctx, g_mix, mc[:, 0], mc[:, 1], w_z)

        n_keys = n_tok + n_ctx
        q_x, k_all, vt_all = _mla_prep(zx, qn, kvn, wq, wqs, wk, wvt, *rope_x, n_keys, 0)
        q_c, k_all, vt_all = _mla_prep(zc, qn, kvn, wq, wqs, wk, wvt, *rope_c, n_keys, n_tok, (k_all, vt_all))
        a_x = _attention(q_x, k_all, vt_all, n_keys)

        *y5_c, finals = _s5_scan(zc, h_zero, s5_tabs)
        *y5_x, _ = _s5_scan(zx, finals, s5_tabs)

        v_x, g1_x, g2_x = _short_conv(zx, hy_conv_w[i], hy_conv_b[i][None])
        e_x = _hyena_mixer(v_x, (g1_x, g2_x), _hyena_filters(n_tok, *fparams), hy_bias[i])

        x = _merge(x, zx, a_x, y5_x, e_x, *merge_w, mx[:, 2])
        x = _ffn(x, g_ffn, mx[:, 3], mx[:, 4], mx[:, 5], w_g, w_u, w_d, final_norm[None], last)

        if not last:
            a_c = _attention(q_c, k_all, vt_all, n_ctx, n_tok // n_ctx)
            v_c, g1_c, g2_c = _short_conv(zc, hy_conv_w[i], hy_conv_b[i][None])
            e_c = _hyena_mixer(v_c, (g1_c, g2_c), _hyena_filters(n_ctx, *fparams), hy_bias[i])
            ctx = _merge(ctx, zc, a_c, y5_c, e_c, *merge_w, mc[:, 2])
            ctx = _ffn(ctx, g_ffn, mc[:, 3], mc[:, 4], mc[:, 5], w_g, w_u, w_d, final_norm[None], False)
    return x
```

```python
import functools
import math

import jax
import jax.numpy as jnp
import numpy as np
from jax import lax
from jax.experimental import pallas as pl
from jax.experimental.pallas import tpu as pltpu

F32 = jnp.float32
BF16 = jnp.bfloat16
HIGHEST = lax.Precision.HIGHEST

D_MODEL = 1024
GRID_W = 64
NORM_EPS = 1e-6
N_MOD = 6

MLA_HEADS = 8
MLA_NOPE = 64
MLA_ROPE = 32
MLA_V = 64
MLA_Q_LORA = 256
MLA_KV_LORA = 128
ROPE_BASE = 10000.0
HEAD_W = 128
MLA_PREP_ROWS = 512
VT_ROWS = 80
QK_SCALE_LOG2 = (MLA_NOPE + MLA_ROPE) ** -0.5 * math.log2(math.e)
QK_HALF_SCALE = QK_SCALE_LOG2 ** 0.5
QK_DTYPE = jnp.float8_e4m3fn
ATTN_SUB = 128
ATTN_LAG = 2
ATTN_UNROLL = 12

S5_WIDTH = 512
S5_GROUP = 16
S5_GROUPS = 32
S5_STATE = 64
S5_HALF = 256
S5_HSTATE = 1024
S5_CHUNK = 128
S5_CHUNKS_PER_STEP = 4

HY_WIDTH = 512
HY_ORDER = 2
HY_BANDS = 16
HY_POS_DIM = 1 + 2 * HY_BANDS
HY_POS_PAD = 64
HY_FILTER_HIDDEN = 64
HY_FILTER_OUT = HY_ORDER * 2 * HY_WIDTH
HY_DECAY_TARGET = 1e-2
HY_DECAY_SHORT = 0.3
HY_DECAY_LONG = 1.5
HY_DECAY_SHIFT = 0.05

D_FF = 2816

OFF_CQ = 0
OFF_CKV = OFF_CQ + MLA_Q_LORA
OFF_KR = OFF_CKV + MLA_KV_LORA
OFF_S5 = OFF_KR + MLA_ROPE
OFF_HY = OFF_S5 + S5_WIDTH
OFF_GATE = OFF_HY + 3 * HY_WIDTH

Z_GATE = 0
Z_HY = 3072
Z_S5 = 4608
Z_MLA = 5120
Z_WIDTH = 5632

VMEM_LIMIT_V7X = 52 * 1024 * 1024


def _cparams(sem, flags=None):
    return pltpu.CompilerParams(dimension_semantics=sem, vmem_limit_bytes=VMEM_LIMIT_V7X, flags=flags)


def _dot(a, b):
    return jnp.dot(a, b, preferred_element_type=F32)


def _rms(x, g):
    return x * lax.rsqrt(jnp.mean(x * x, axis=-1, keepdims=True) + NORM_EPS) * g


def _norm_mod(x, g, shift, scale):
    return _rms(x, g) * (1.0 + scale) + shift


def _proj_kernel(x_ref, g_ref, sh_ref, sc_ref, w_ref, o_ref, h_ref):
    @pl.when(pl.program_id(2) == 0)
    def _():
        h_ref[...] = _norm_mod(x_ref[0], g_ref[...], sh_ref[0], sc_ref[0]).astype(BF16)

    o_ref[0] = _dot(h_ref[...], w_ref[...]).astype(BF16)


def _proj_in(x, g, shift, scale, w):
    bsz, n, d = x.shape
    nz = w.shape[1]
    tm = min(n, 2048)
    tn = 512
    return pl.pallas_call(
        _proj_kernel,
        grid=(bsz, n // tm, nz // tn),
        in_specs=[
            pl.BlockSpec((1, tm, d), lambda b, i, j: (b, i, 0)),
            pl.BlockSpec((1, d), lambda b, i, j: (0, 0)),
            pl.BlockSpec((1, 1, d), lambda b, i, j: (b, 0, 0)),
            pl.BlockSpec((1, 1, d), lambda b, i, j: (b, 0, 0)),
            pl.BlockSpec((d, tn), lambda b, i, j: (0, j)),
        ],
        out_specs=pl.BlockSpec((1, tm, tn), lambda b, i, j: (b, i, j)),
        out_shape=jax.ShapeDtypeStruct((bsz, n, nz), BF16),
        scratch_shapes=[pltpu.VMEM((tm, d), BF16)],
        compiler_params=_cparams(("parallel", "parallel", "arbitrary")),
        name="proj_in",
    )(x, g, shift, scale, w)


def _mla_prep_kernel(z_ref, qn_ref, kvn_ref, wq_ref, wqs_ref, wk_ref, wvt_ref, c_ref, s_ref, *rest):
    q_ref, k_ref, vt_ref = rest[-3:]
    z = z_ref[0].astype(F32)
    hq = _rms(z[:, :MLA_Q_LORA], qn_ref[...]).astype(BF16)
    hkv = _rms(z[:, MLA_Q_LORA:MLA_Q_LORA + MLA_KV_LORA], kvn_ref[...]).astype(BF16)
    krb = z[:, MLA_Q_LORA + MLA_KV_LORA:]
    qa = _dot(hq, wq_ref[...])
    qb = _dot(hq, wqs_ref[...])
    kn = _dot(hkv, wk_ref[...])
    vt = lax.dot_general(wvt_ref[...], hkv, (((1,), (1,)), ((), ())), preferred_element_type=F32)
    ck = c_ref[...]
    sn = s_ref[...]
    lane = lax.broadcasted_iota(jnp.int32, ck.shape, 1)
    cq = jnp.where(lane < MLA_ROPE, ck, 1.0)
    kr = krb * ck + pltpu.roll(krb, HEAD_W - MLA_ROPE // 2, 1) * sn
    row = lax.broadcasted_iota(jnp.int32, (VT_ROWS, z.shape[0]), 0)
    for h in range(MLA_HEADS):
        sl = slice(h * HEAD_W, (h + 1) * HEAD_W)
        q_ref[0, h] = ((qa[:, sl] * cq + qb[:, sl] * sn) * QK_HALF_SCALE).astype(QK_DTYPE)
        k_ref[0, h] = ((kr + kn[:, sl]) * QK_HALF_SCALE).astype(QK_DTYPE)
        vt_ref[0, h] = jnp.where(row == MLA_V, 1.0, vt[h * VT_ROWS:(h + 1) * VT_ROWS, :]).astype(BF16)


def _mla_prep(z, qn, kvn, wq, wqs, wk, wvt, ctab, stab, n_keys, key_off, kv_bufs=None):
    bsz, n, _ = z.shape
    tm = min(n, MLA_PREP_ROWS)
    assert key_off % tm == 0, "the key offset must be a whole number of row tiles"
    ob = key_off // tm
    hw = MLA_HEADS * HEAD_W
    zb = Z_MLA // 512
    full = lambda shape: pl.BlockSpec(shape, lambda b, i: (0,) * len(shape))
    in_specs = [
        pl.BlockSpec((1, tm, 512), lambda b, i: (b, i, zb)),
        full((1, MLA_Q_LORA)),
        full((1, MLA_KV_LORA)),
        full((MLA_Q_LORA, hw)),
        full((MLA_Q_LORA, hw)),
        full((MLA_KV_LORA, hw)),
        full((MLA_HEADS * VT_ROWS, MLA_KV_LORA)),
        pl.BlockSpec((tm, HEAD_W), lambda b, i: (i, 0)),
        pl.BlockSpec((tm, HEAD_W), lambda b, i: (i, 0)),
    ]
    args = [z, qn, kvn, wq, wqs, wk, wvt, ctab, stab]
    aliases = {}
    if kv_bufs is not None:
        aliases = {len(args): 1, len(args) + 1: 2}
        in_specs += [pl.BlockSpec(memory_space=pl.ANY)] * 2
        args += list(kv_bufs)
    return pl.pallas_call(
        _mla_prep_kernel,
        grid=(bsz, n // tm),
        in_specs=in_specs,
        out_specs=[
            pl.BlockSpec((1, MLA_HEADS, tm, HEAD_W), lambda b, i: (b, 0, i, 0)),
            pl.BlockSpec((1, MLA_HEADS, tm, HEAD_W), lambda b, i: (b, 0, i + ob, 0)),
            pl.BlockSpec((1, MLA_HEADS, VT_ROWS, tm), lambda b, i: (b, 0, 0, i + ob)),
        ],
        out_shape=[
            jax.ShapeDtypeStruct((bsz, MLA_HEADS, n, HEAD_W), QK_DTYPE),
            jax.ShapeDtypeStruct((bsz, MLA_HEADS, n_keys, HEAD_W), QK_DTYPE),
            jax.ShapeDtypeStruct((bsz, MLA_HEADS, VT_ROWS, n_keys), BF16),
        ],
        input_output_aliases=aliases,
        compiler_params=_cparams(("parallel", "parallel")),
        name="mla_prep",
    )(*args)


def _attn_kernel(q_ref, k_ref, vt_ref, o_ref, s0_ref, s1_ref, s2_ref, p0_ref, p1_ref, p2_ref, acc_ref, *, tk, nkc):
    q = q_ref[0, 0]
    tq = q.shape[0]
    s_refs = (s0_ref, s1_ref, s2_ref)
    p_refs = (p0_ref, p1_ref, p2_ref)

    ts = min(tk, ATTN_SUB)
    subs = [slice(r, r + ts) for r in range(0, tk, ts)]

    def scores_sub(j, s_ref, sub, cmax):
        off = pl.multiple_of(j * tk + sub.start, ts)
        s = lax.dot_general(k_ref[0, 0, pl.ds(off, ts), :], q, (((1,), (1,)), ((), ())),
                            preferred_element_type=F32)
        s_ref[sub, :] = s
        cm = jnp.max(s, axis=0, keepdims=True)
        return cm if cmax is None else jnp.maximum(cmax, cm)

    def softmax_sub(slot, sub, mn):
        p_refs[slot][sub, :] = jnp.exp2(s_refs[slot][sub, :] - mn).astype(BF16)

    def step(i, slot, carry, do_softmax, do_scores):
        cm, m, alpha = carry
        mn = jnp.maximum(m, cm)
        off = pl.multiple_of(i * tk, tk)
        acc_ref[...] = alpha * acc_ref[...] + _dot(vt_ref[0, 0, :, pl.ds(off, tk)], p_refs[slot][...])
        cm_new, cms = None, []
        for r, sub in enumerate(subs):
            mn_sub = mn
            if do_scores:
                cm_new = scores_sub(i + 2, s_refs[(slot + 2) % 3], sub, cm_new)
                cms.append(cm_new)
                if r >= ATTN_LAG:
                    mn_sub = jnp.maximum(mn, cms[r - ATTN_LAG] - 1e30)
            if do_softmax:
                softmax_sub((slot + 1) % 3, sub, mn_sub)
        if do_softmax:
            alpha, m = jnp.exp2(m - mn), mn
        return (cm_new if do_scores else cm), m, alpha

    m = jnp.full((1, tq), -1e30, F32)
    acc_ref[...] = jnp.zeros_like(acc_ref)
    cm, cm_next = None, None
    for sub in subs:
        cm = scores_sub(0, s0_ref, sub, cm)
        if nkc > 1:
            cm_next = scores_sub(1, s1_ref, sub, cm_next)
    mn = jnp.maximum(m, cm)
    for sub in subs:
        softmax_sub(0, sub, mn)
    carry = (cm_next if nkc > 1 else cm, mn, jnp.exp2(m - mn))

    def body(t, carry):
        for r in range(ATTN_UNROLL):
            carry = step(ATTN_UNROLL * t + r, r % 3, carry, True, True)
        return carry

    nloop = max(nkc - 2, 0) // ATTN_UNROLL
    carry = lax.fori_loop(0, nloop, body, carry)
    for i in range(ATTN_UNROLL * nloop, nkc):
        carry = step(i, i % 3, carry, i + 1 < nkc, i + 2 < nkc)
    acc = acc_ref[...]
    out = acc * (1.0 / acc[MLA_V:MLA_V + 1, :])
    out = jnp.concatenate([out, jnp.zeros((HEAD_W - VT_ROWS, tq), F32)], axis=0).T
    o_ref[0, 0] = out[:, :MLA_V].astype(BF16)


def _kv_chunk(nk):
    for tk in (640, 512, 256, 128):
        if nk % tk == 0:
            return tk
    raise ValueError(f"unsupported key count {nk}")


def _attention(q, k, vt, nk, key_block=0):
    bsz, nh, nq, _ = q.shape
    tq = min(nq, 1024)
    tk = _kv_chunk(nk)
    return pl.pallas_call(
        functools.partial(_attn_kernel, tk=tk, nkc=nk // tk),
        grid=(bsz, nh, nq // tq),
        in_specs=[
            pl.BlockSpec((1, 1, tq, HEAD_W), lambda b, h, i: (b, h, i, 0)),
            pl.BlockSpec((1, 1, nk, HEAD_W), lambda b, h, i: (b, h, key_block, 0)),
            pl.BlockSpec((1, 1, VT_ROWS, nk), lambda b, h, i: (b, h, 0, key_block)),
        ],
        out_specs=pl.BlockSpec((1, 1, tq, MLA_V), lambda b, h, i: (b, h, i, 0)),
        out_shape=jax.ShapeDtypeStruct((bsz, nh, nq, MLA_V), BF16),
        scratch_shapes=[pltpu.VMEM((tk, tq), F32)] * 3 + [pltpu.VMEM((tk, tq), BF16)] * 3
        + [pltpu.VMEM((VT_ROWS, tq), F32)],
        compiler_params=_cparams(("parallel", "parallel", "arbitrary")),
        name="attention",
    )(q, k, vt)


def _cmul(ar, ai, br, bi):
    return ar * br - ai * bi, ar * bi + ai * br


def _s5_kernel(u_ref, h0_ref, bm_ref, cm_ref, wn_ref, wp_ref, l1_ref, tri_ref, y_ref, hf_ref,
               carry_ref, h0cat_ref, h1cat_ref, *, tc, nr, ns, reverse):
    i = pl.program_id(1)

    @pl.when(i == 0)
    def _():
        carry_ref[...] = h0_ref[0]

    u = u_ref[0]
    tri = tri_ref[...]
    order = range(nr - 1, -1, -1) if reverse else range(nr)
    edge = 0 if reverse else tc - 1
    hcat_refs = (h0cat_ref, h1cat_ref)
    bus = [_dot(u[:, hf * S5_HALF:(hf + 1) * S5_HALF], bm_ref[hf]) for hf in range(2)]
    carries = [(carry_ref[2 * hf:2 * hf + 1, :], carry_ref[2 * hf + 1:2 * hf + 2, :]) for hf in range(2)]
    for c in order:
        rows = slice(c * tc, (c + 1) * tc)
        for hf in range(2):
            bu = bus[hf]
            xr, xi = _cmul(wn_ref[hf, 0], wn_ref[hf, 1], bu[rows, :S5_HSTATE], bu[rows, S5_HSTATE:])
            s = _dot(tri, jnp.concatenate([xr, xi], axis=1).astype(BF16))
            cr, ci = _cmul(l1_ref[hf, 0], l1_ref[hf, 1], *carries[hf])
            hr, hi = _cmul(wp_ref[hf, 0], wp_ref[hf, 1], s[:, :S5_HSTATE] + cr, s[:, S5_HSTATE:] + ci)
            hcat_refs[hf][rows, :S5_HSTATE] = hr.astype(BF16)
            hcat_refs[hf][rows, S5_HSTATE:] = hi.astype(BF16)
            carries[hf] = (hr[edge:edge + 1], hi[edge:edge + 1])
    for hf in range(2):
        carry_ref[2 * hf:2 * hf + 1, :] = carries[hf][0]
        carry_ref[2 * hf + 1:2 * hf + 2, :] = carries[hf][1]
    y_ref[0] = jnp.concatenate([_dot(hcat_refs[hf][...], cm_ref[hf]) for hf in range(2)], axis=1)

    @pl.when(i == ns - 1)
    def _():
        hf_ref[0] = carry_ref[...]


def _s5_scan_dir(z, h0, tabs, reverse):
    bm, cm, wn, wp, l1, tri = tabs
    bsz, n, _ = z.shape
    tc = S5_CHUNK
    nr = min(S5_CHUNKS_PER_STEP, n // tc)
    rows = nr * tc
    ns = n // rows
    zb = Z_S5 // S5_WIDTH
    blk = (lambda i: ns - 1 - i) if reverse else (lambda i: i)
    full = lambda shape: pl.BlockSpec(shape, lambda b, i: (0,) * len(shape))
    return pl.pallas_call(
        functools.partial(_s5_kernel, tc=tc, nr=nr, ns=ns, reverse=reverse),
        grid=(bsz, ns),
        in_specs=[
            pl.BlockSpec((1, rows, S5_WIDTH), lambda b, i: (b, blk(i), zb)),
            pl.BlockSpec((1, 4, S5_HSTATE), lambda b, i: (b, 0, 0)),
            full((2, S5_HALF, 2 * S5_HSTATE)),
            full((2, 2 * S5_HSTATE, S5_HALF)),
            full((2, 2, tc, S5_HSTATE)),
            full((2, 2, tc, S5_HSTATE)),
            full((2, 2, 1, S5_HSTATE)),
            full((tc, tc)),
        ],
        out_specs=[
            pl.BlockSpec((1, rows, S5_WIDTH), lambda b, i: (b, blk(i), 0)),
            pl.BlockSpec((1, 4, S5_HSTATE), lambda b, i: (b, 0, 0)),
        ],
        out_shape=[
            jax.ShapeDtypeStruct((bsz, n, S5_WIDTH), F32),
            jax.ShapeDtypeStruct((bsz, 4, S5_HSTATE), F32),
        ],
        scratch_shapes=[pltpu.VMEM((4, S5_HSTATE), F32),
                        pltpu.VMEM((rows, 2 * S5_HSTATE), BF16), pltpu.VMEM((rows, 2 * S5_HSTATE), BF16)],
        compiler_params=_cparams(("parallel", "arbitrary")),
        name="s5_scan",
    )(z, h0, bm, cm, wn, wp, l1, tri)


def _s5_scan(z, h0, tabs):
    outs = [_s5_scan_dir(z, h0[:, d], [t[d] for t in tabs], reverse=bool(d)) for d in range(2)]
    return outs[0][0], outs[1][0], jnp.stack([outs[0][1], outs[1][1]], axis=1)


def _s5_tables(lam_re, lam_im, log_dt, b_re, b_im, c_re, c_im):
    tc = S5_CHUNK
    dt = jnp.exp(log_dt)[..., None]
    zr, zi = lam_re * dt, lam_im * dt
    mag = jnp.exp(zr)
    lbr, lbi = mag * jnp.cos(zi), mag * jnp.sin(zi)
    den = lam_re * lam_re + lam_im * lam_im
    nr, ni = lbr - 1.0, lbi
    cfr = (nr * lam_re + ni * lam_im) / den
    cfi = (ni * lam_re - nr * lam_im) / den
    bbr = cfr[..., None] * b_re - cfi[..., None] * b_im
    bbi = cfr[..., None] * b_im + cfi[..., None] * b_re
    eye = jnp.eye(S5_GROUP, dtype=F32)

    def blockdiag_in(b):
        b = b.reshape(2, 2, S5_GROUP, S5_STATE, S5_GROUP)
        return jnp.einsum('gk,dhgpn->dhgnkp', eye, b).reshape(2, 2, S5_HALF, S5_HSTATE)

    def blockdiag_out(c):
        c = c.reshape(2, 2, S5_GROUP, S5_GROUP, S5_STATE)
        return jnp.einsum('gk,dhgnp->dhgpkn', eye, c).reshape(2, 2, S5_HSTATE, S5_HALF)

    bm = jnp.concatenate([blockdiag_in(bbr), blockdiag_in(bbi)], axis=-1).astype(BF16)
    cm = jnp.concatenate([blockdiag_out(c_re), blockdiag_out(-c_im)], axis=-2).astype(BF16)

    def powers(k):
        zr_ = zr.reshape(2, 2, 1, S5_HSTATE)
        zi_ = zi.reshape(2, 2, 1, S5_HSTATE)
        kk = k[:, None, :, None]
        m = jnp.exp(kk * zr_)
        return jnp.stack([m * jnp.cos(kk * zi_), m * jnp.sin(kk * zi_)], axis=2)

    mid = tc // 2
    t = jnp.arange(tc, dtype=F32) - mid
    wn = powers(jnp.stack([-t, -t[::-1]]))
    wp = powers(jnp.stack([t, t[::-1]]))
    l1 = powers(jnp.full((2, 1), mid + 1, F32))
    r = jnp.arange(tc)
    tri = jnp.stack([r[:, None] >= r[None, :], r[:, None] <= r[None, :]]).astype(BF16)
    return bm, cm, wn, wp, l1, tri


HALO = 16


def _sconv_kernel(z_ref, zp_ref, zn_ref, w_ref, b_ref, v_ref, g1_ref, g2_ref, *, nt):
    i = pl.program_id(1)
    z = z_ref[0].astype(F32)
    tm = z.shape[0]
    row = lax.broadcasted_iota(jnp.int32, z.shape, 0)
    prev = jnp.where(i > 0, zp_ref[0, HALO - 1:HALO, :].astype(F32), 0.0)
    nxt = jnp.where(i < nt - 1, zn_ref[0, 0:1, :].astype(F32), 0.0)
    up = jnp.where(row == 0, prev, pltpu.roll(z, 1, 0))
    dn = jnp.where(row == tm - 1, nxt, pltpu.roll(z, tm - 1, 0))
    u = up * w_ref[0:1, :] + z * w_ref[1:2, :] + dn * w_ref[2:3, :] + b_ref[...]
    for k, o_ref in enumerate((v_ref, g1_ref, g2_ref)):
        o_ref[0] = u[:, k * HY_WIDTH:(k + 1) * HY_WIDTH]


def _short_conv(z, w, b):
    bsz, n, _ = z.shape
    cw = 3 * HY_WIDTH
    tm = min(n, 1024)
    nt = n // tm
    zb = Z_HY // cw
    rh = tm // HALO
    out = pl.BlockSpec((1, tm, HY_WIDTH), lambda b_, i: (b_, i, 0))
    return pl.pallas_call(
        functools.partial(_sconv_kernel, nt=nt),
        grid=(bsz, nt),
        in_specs=[
            pl.BlockSpec((1, tm, cw), lambda b_, i: (b_, i, zb)),
            pl.BlockSpec((1, HALO, cw), lambda b_, i: (b_, jnp.maximum(i * rh - 1, 0), zb)),
            pl.BlockSpec((1, HALO, cw), lambda b_, i: (b_, jnp.minimum((i + 1) * rh, n // HALO - 1), zb)),
            pl.BlockSpec((3, cw), lambda b_, i: (0, 0)),
            pl.BlockSpec((1, cw), lambda b_, i: (0, 0)),
        ],
        out_specs=[out] * 3,
        out_shape=[jax.ShapeDtypeStruct((bsz, n, HY_WIDTH), F32)] * 3,
        compiler_params=_cparams(("parallel", "parallel")),
        name="short_conv",
    )(z, z, z, w, b)


def _filt_kernel(f_ref, w1_ref, b1_ref, w2_ref, b2_ref, w3h_ref, w3l_ref, fq_ref, dl_ref, o_ref, *, n_tok):
    z = f_ref[...]
    tm = z.shape[0]
    fq = fq_ref[...]
    hid = jnp.sin(fq * (jnp.dot(z, w1_ref[...], precision=HIGHEST, preferred_element_type=F32) + b1_ref[...]))
    hid = jnp.sin(fq * (jnp.dot(hid, w2_ref[...], precision=HIGHEST, preferred_element_type=F32) + b2_ref[...]))
    hid_hi = hid.astype(BF16)
    hid_lo = (hid - hid_hi.astype(F32)).astype(BF16)

    def filters(rows, d):
        wh, wl = w3h_ref[d], w3l_ref[d]
        f = _dot(hid_hi[rows], wh) + (_dot(hid_hi[rows], wl) + _dot(hid_lo[rows], wh))
        return f * (jnp.exp(-z[rows, 0:1] * dl_ref[d]) + HY_DECAY_SHIFT)

    i = pl.program_id(0)
    filt = filters(slice(None), jnp.where(i * tm >= n_tok, 1, 0))
    m = i * tm + lax.broadcasted_iota(jnp.int32, filt.shape, 0)
    filt = jnp.where(m == n_tok, 0.0, filt)
    for o in range(HY_ORDER):
        o_ref[o] = filt[:, o * HY_WIDTH:(o + 1) * HY_WIDTH]

    @pl.when(i == 0)
    def _():
        head = slice(0, HALO)
        lag0 = filters(head, 1)
        lag0 = jnp.where(lax.broadcasted_iota(jnp.int32, lag0.shape, 0) == 0, lag0, 0.0)
        for o in range(HY_ORDER):
            o_ref[o, head, :] += lag0[:, o * HY_WIDTH:(o + 1) * HY_WIDTH]


def _hyena_filter_feats(n_tok):
    m = jnp.arange(2 * n_tok)
    lag = jnp.where(m < n_tok, m, jnp.where(m > n_tok, 2 * n_tok - m, 0))
    t = (lag.astype(F32) / (n_tok - 1))[:, None]
    w = (2.0 * math.pi * lag.astype(F32) / n_tok)[:, None]
    bands = jnp.linspace(1e-4, HY_BANDS - 1, HY_BANDS, dtype=F32)[None, :]
    feats = jnp.concatenate([t, jnp.cos(bands * w), -jnp.sin(bands * w)], axis=-1)
    return jnp.pad(feats, ((0, 0), (0, HY_POS_PAD - HY_POS_DIM)))


def _hyena_filters(n_tok, w1, b1, w2, b2, w3, freq):
    feats = _hyena_filter_feats(n_tok)
    deltas = jnp.abs(jnp.linspace(math.log(HY_DECAY_TARGET) / HY_DECAY_SHORT,
                                  math.log(HY_DECAY_TARGET) / HY_DECAY_LONG, HY_FILTER_OUT, dtype=F32))[None, :]
    w1p = jnp.pad(w1, ((0, HY_POS_PAD - HY_POS_DIM), (0, 0)))

    def by_direction(a):
        a = a.reshape(a.shape[:-1] + (HY_ORDER, 2, HY_WIDTH))
        return jnp.moveaxis(a, -2, 0).reshape((2,) + a.shape[:-3] + (HY_ORDER * HY_WIDTH,))

    w3, deltas = by_direction(w3), by_direction(deltas)
    w3_hi = w3.astype(BF16)
    w3_lo = (w3 - w3_hi.astype(F32)).astype(BF16)
    n2 = 2 * n_tok
    tm = min(n_tok, 1024)
    full = lambda shape: pl.BlockSpec(shape, lambda i: (0,) * len(shape))
    hh = HY_FILTER_HIDDEN
    wd = HY_ORDER * HY_WIDTH
    return pl.pallas_call(
        functools.partial(_filt_kernel, n_tok=n_tok),
        grid=(n2 // tm,),
        in_specs=[
            pl.BlockSpec((tm, HY_POS_PAD), lambda i: (i, 0)),
            full((HY_POS_PAD, hh)), full((1, hh)), full((hh, hh)), full((1, hh)),
            full((2, hh, wd)), full((2, hh, wd)), full((1, hh)), full((2, 1, wd)),
        ],
        out_specs=pl.BlockSpec((HY_ORDER, tm, HY_WIDTH), lambda i: (0, i, 0)),
        out_shape=jax.ShapeDtypeStruct((HY_ORDER, n2, HY_WIDTH), F32),
        compiler_params=_cparams(("parallel",)),
        name="hyena_filter",
    )(feats, w1p, b1[None], w2, b2[None], w3_hi, w3_lo, freq[None], deltas)


def _fft_split(n_fft):
    if n_fft <= 1024:
        return n_fft, 1
    n1 = 1 << (int(math.log2(n_fft)) // 2)
    return n1, n_fft // n1


def _cis(num, den, sign):
    ang = (2.0 * math.pi / den) * num.astype(F32)
    return jnp.cos(ang), sign * jnp.sin(ang)


def _fft_tables(n_fft):
    n1, n2 = _fft_split(n_fft)
    a = jnp.arange(n1)
    f1r, f1i = _cis((a[:, None] * a[None, :]) % n1, n1, -1.0)
    s1 = jnp.concatenate([f1r, f1i], axis=0).astype(BF16)
    half = n1 // 2
    s3 = (jnp.concatenate([f1r[:half], -f1i[:half]], axis=0) / n_fft).astype(BF16)
    if n2 == 1:
        return s1, s3, None, None
    b = jnp.arange(n2)
    f2r, f2i = _cis((b[:, None] * b[None, :]) % n2, n2, -1.0)
    twr, twi = _cis((jnp.arange(n1)[:, None] * b[None, :]) % n_fft, n_fft, -1.0)
    fr, fi = _cmul(f2r[None], f2i[None], twr[:, None, :], twi[:, None, :])
    ft = jnp.concatenate([fr, fi], axis=1).astype(BF16)
    frt, fit = _cmul(f2r[None], f2i[None], twr[:, :, None], twi[:, :, None])
    it = jnp.concatenate([frt, -fit], axis=1).astype(BF16)
    return s1, s3, ft, it


def _stacked_cdot(s, xr, xi, m):
    p = _dot(s, xr.astype(BF16))
    if xi is None:
        return p[:m], p[m:]
    q = _dot(s, xi.astype(BF16))
    return p[:m] - q[m:], q[:m] + p[m:]


def _fft_a_kernel(*refs, parts, n1, has_g):
    if has_g:
        x_ref, s_ref, g_ref, y_ref = refs
    else:
        x_ref, s_ref, y_ref = refs
    yr, yi = _stacked_cdot(s_ref[...], x_ref[0], x_ref[1] if parts == 2 else None, n1)
    if has_g:
        yr, yi = _cmul(yr, yi, g_ref[0], g_ref[1])
    y_ref[0] = yr
    y_ref[1] = yi


def _fft_a(x, s1, g=None):
    parts, a_rows, cols = x.shape
    n1 = s1.shape[0] // 2
    tc = min(cols, 2048)
    in_specs = [
        pl.BlockSpec((parts, a_rows, tc), lambda j: (0, 0, j)),
        pl.BlockSpec((2 * n1, a_rows), lambda j: (0, 0)),
    ]
    args = [x, s1[:, :a_rows]]
    if g is not None:
        in_specs.append(pl.BlockSpec((2, n1, tc), lambda j: (0, 0, j)))
        args.append(g)
    return pl.pallas_call(
        functools.partial(_fft_a_kernel, parts=parts, n1=n1, has_g=g is not None),
        grid=(cols // tc,),
        in_specs=in_specs,
        out_specs=pl.BlockSpec((2, n1, tc), lambda j: (0, 0, j)),
        out_shape=jax.ShapeDtypeStruct((2, n1, cols), F32),
        compiler_params=_cparams(("parallel",)),
        name="fft_stage_a",
    )(*args)


FFT_CB = 4


def _fft_b_kernel(*refs, n2, inverse):
    if inverse:
        y_ref, ft_ref, it_ref, g_ref, o_ref = refs
    else:
        y_ref, ft_ref, o_ref = refs
    for c in range(FFT_CB):
        xr, xi = _stacked_cdot(ft_ref[c], y_ref[0, c], y_ref[1, c], n2)
        if inverse:
            zr, zi = _cmul(xr, xi, g_ref[0, c], g_ref[1, c])
            xr, xi = _stacked_cdot(it_ref[c], zr, zi, n2)
        o_ref[0, c] = xr.astype(o_ref.dtype)
        o_ref[1, c] = xi.astype(o_ref.dtype)


def _fft_b(y, ft, it=None, g=None):
    _, n1, n2, ch = y.shape
    inverse = it is not None
    blk = pl.BlockSpec((2, FFT_CB, n2, ch), lambda c: (0, c, 0, 0))
    mat = pl.BlockSpec((FFT_CB, 2 * n2, n2), lambda c: (c, 0, 0))
    in_specs, args = [blk, mat], [y, ft]
    if inverse:
        in_specs += [mat, blk]
        args += [it, g]
    return pl.pallas_call(
        functools.partial(_fft_b_kernel, n2=n2, inverse=inverse),
        grid=(n1 // FFT_CB,),
        in_specs=in_specs,
        out_specs=blk,
        out_shape=jax.ShapeDtypeStruct(y.shape, BF16 if inverse else F32),
        compiler_params=_cparams(("parallel",)),
        name="fft_stage_b",
    )(*args)


def _fft_c_kernel(y_ref, s_ref, x_ref, g_ref, b_ref, o_ref, *, half):
    cr, ci = _stacked_cdot(s_ref[...], y_ref[0], y_ref[1], half)
    bias = b_ref[...]
    o_ref[0] = g_ref[0] * (cr + bias * x_ref[0])
    o_ref[1] = g_ref[1] * (ci + bias * x_ref[1])


def _fft_c(y, s3, x, gate, bias_cols):
    _, n1, cols = y.shape
    half = n1 // 2
    tc = min(cols, 2048)
    io = pl.BlockSpec((2, half, tc), lambda j: (0, 0, j))
    return pl.pallas_call(
        functools.partial(_fft_c_kernel, half=half),
        grid=(cols // tc,),
        in_specs=[
            pl.BlockSpec((2, n1, tc), lambda j: (0, 0, j)),
            pl.BlockSpec((n1, n1), lambda j: (0, 0)),
            io, io,
            pl.BlockSpec((1, tc), lambda j: (0, j)),
        ],
        out_specs=io,
        out_shape=jax.ShapeDtypeStruct((2, half, cols), F32),
        compiler_params=_cparams(("parallel",)),
        name="fft_stage_c",
    )(y, s3, x, gate, bias_cols)


FFT_BT = 16


def _to_fine_major(x):
    return pltpu.einshape("abc->bac", x)


def _fft_a4_kernel(x_ref, s_ref, y_ref, *, parts, n1):
    s = s_ref[...]
    xr = _to_fine_major(x_ref[0].astype(BF16))
    xi = _to_fine_major(x_ref[1].astype(BF16)) if parts == 2 else None
    out = [_stacked_cdot(s, xr[b], None if xi is None else xi[b], n1) for b in range(FFT_BT)]
    y_ref[0] = pltpu.einshape("bac->abc", jnp.stack([o[0] for o in out])).astype(BF16)
    y_ref[1] = pltpu.einshape("bac->abc", jnp.stack([o[1] for o in out])).astype(BF16)


def _fft_a4(x, s1):
    parts, a_rows, n2, ch = x.shape
    n1 = s1.shape[0] // 2
    return pl.pallas_call(
        functools.partial(_fft_a4_kernel, parts=parts, n1=n1),
        grid=(n2 // FFT_BT,),
        in_specs=[
            pl.BlockSpec((parts, a_rows, FFT_BT, ch), lambda j: (0, 0, j, 0)),
            pl.BlockSpec((2 * n1, a_rows), lambda j: (0, 0)),
        ],
        out_specs=pl.BlockSpec((2, n1, FFT_BT, ch), lambda j: (0, 0, j, 0)),
        out_shape=jax.ShapeDtypeStruct((2, n1, n2, ch), BF16),
        compiler_params=_cparams(("parallel",)),
        name="fft_stage_a",
    )(x, s1[:, :a_rows])


def _fft_c4_kernel(y_ref, s_ref, x_ref, g_ref, b_ref, o_ref, c_ref, *, half):
    s = s_ref[...]
    yr = _to_fine_major(y_ref[0].astype(BF16))
    yi = _to_fine_major(y_ref[1].astype(BF16))
    out = [_stacked_cdot(s, yr[b], yi[b], half) for b in range(FFT_BT)]
    c_ref[0] = pltpu.einshape("bac->abc", jnp.stack([o[0] for o in out]))
    c_ref[1] = pltpu.einshape("bac->abc", jnp.stack([o[1] for o in out]))
    o_ref[...] = g_ref[...] * (c_ref[...] + b_ref[...][None, None] * x_ref[...])


def _fft_c4(y, s3, x, gate, bias):
    _, n1, n2, ch = y.shape
    half = n1 // 2
    io = pl.BlockSpec((2, half, FFT_BT, ch), lambda j: (0, 0, j, 0))
    return pl.pallas_call(
        functools.partial(_fft_c4_kernel, half=half),
        grid=(n2 // FFT_BT,),
        in_specs=[
            pl.BlockSpec((2, n1, FFT_BT, ch), lambda j: (0, 0, j, 0)),
            pl.BlockSpec((n1, n1), lambda j: (0, 0)),
            io, io,
            pl.BlockSpec((1, ch), lambda j: (0, 0)),
        ],
        out_specs=io,
        out_shape=jax.ShapeDtypeStruct((2, half, n2, ch), F32),
        scratch_shapes=[pltpu.VMEM((2, half, FFT_BT, ch), F32)],
        compiler_params=_cparams(("parallel",)),
        name="fft_stage_c",
    )(y, s3, x, gate, bias)


def _hyena_mixer(v, gates, gfilt, bias):
    bsz, n, ch = v.shape
    assert bsz == 2, "the complex packing of the long convolution pairs exactly two batch rows"
    n_fft = 2 * n
    n1, n2 = _fft_split(n_fft)
    s1, s3, ft, it = _fft_tables(n_fft)
    if n2 == 1:
        y = v
        for o in range(HY_ORDER):
            gspec = _fft_a(gfilt[o][None], s1)
            y = _fft_c(_fft_a(y, s1, gspec), s3, y, gates[o], bias[o][None])
        return y
    half = n1 // 2
    y = v.reshape(2, half, n2, ch)
    for o in range(HY_ORDER):
        gspec = _fft_b(_fft_a4(gfilt[o].reshape(1, n1, n2, ch), s1), ft)
        spec = _fft_b(_fft_a4(y, s1), ft, it, gspec)
        y = _fft_c4(spec, s3, y, gates[o].reshape(2, half, n2, ch), bias[o][None])
    return y.reshape(2, n, ch)


def _gelu_tanh(x):
    return 0.5 * x * (1.0 + jnp.tanh(math.sqrt(2.0 / math.pi) * (x + 0.044715 * (x * x * x))))


def _merge_kernel(x_ref, g0_ref, g1_ref, g2_ref, a_ref, yf_ref, yb_ref, u_ref, e_ref, d_ref,
                  wglu_ref, wm_ref, ws_ref, wh_ref, wo_ref, gt_ref, o_ref):
    att = _dot(jnp.concatenate([a_ref[0, h] for h in range(MLA_HEADS)], axis=-1), wm_ref[...])
    y = yf_ref[0] + yb_ref[0] + d_ref[...] * u_ref[0].astype(F32)
    gy = _gelu_tanh(y)
    s5 = gy * jax.nn.sigmoid(_dot(gy.astype(BF16), wglu_ref[...]))
    s5 = _dot(s5.astype(BF16), ws_ref[...])
    hy = _dot(e_ref[0].astype(BF16), wh_ref[...])
    merged = jax.nn.sigmoid(g0_ref[0].astype(F32)) * att
    merged = merged + jax.nn.sigmoid(g1_ref[0].astype(F32)) * s5
    merged = merged + jax.nn.sigmoid(g2_ref[0].astype(F32)) * hy
    o_ref[0] = x_ref[0] + gt_ref[0] * _dot(merged.astype(BF16), wo_ref[...])


def _merge(x, z, att, y5, e, s5_d, w_glu, w_mla, w_s5, w_hy, w_out, gate):
    bsz, n, d = x.shape
    tm = min(n, 512)
    full = lambda shape: pl.BlockSpec(shape, lambda b, i: (0,) * len(shape))
    zs5 = Z_S5 // S5_WIDTH
    return pl.pallas_call(
        _merge_kernel,
        grid=(bsz, n // tm),
        in_specs=[
            pl.BlockSpec((1, tm, d), lambda b, i: (b, i, 0)),
            pl.BlockSpec((1, tm, d), lambda b, i: (b, i, 0)),
            pl.BlockSpec((1, tm, d), lambda b, i: (b, i, 1)),
            pl.BlockSpec((1, tm, d), lambda b, i: (b, i, 2)),
            pl.BlockSpec((1, MLA_HEADS, tm, MLA_V), lambda b, i: (b, 0, i, 0)),
            pl.BlockSpec((1, tm, S5_WIDTH), lambda b, i: (b, i, 0)),
            pl.BlockSpec((1, tm, S5_WIDTH), lambda b, i: (b, i, 0)),
            pl.BlockSpec((1, tm, S5_WIDTH), lambda b, i: (b, i, zs5)),
            pl.BlockSpec((1, tm, HY_WIDTH), lambda b, i: (b, i, 0)),
            full((1, S5_WIDTH)),
            full((S5_WIDTH, S5_WIDTH)),
            full((MLA_HEADS * MLA_V, d)),
            full((S5_WIDTH, d)),
            full((HY_WIDTH, d)),
            full((d, d)),
            pl.BlockSpec((1, 1, d), lambda b, i: (b, 0, 0)),
        ],
        out_specs=pl.BlockSpec((1, tm, d), lambda b, i: (b, i, 0)),
        out_shape=jax.ShapeDtypeStruct((bsz, n, d), F32),
        compiler_params=_cparams(("parallel", "parallel")),
        name="merge",
    )(x, z, z, z, att, y5[0], y5[1], z, e, s5_d, w_glu, w_mla, w_s5, w_hy, w_out, gate)


def _ffn_kernel(x_ref, g_ref, sh_ref, sc_ref, gt_ref, wg_ref, wu_ref, wd_ref, fg_ref, o_ref, h_ref, acc_ref,
                *, nk, final):
    k = pl.program_id(2)

    @pl.when(k == 0)
    def _():
        h_ref[...] = _norm_mod(x_ref[0], g_ref[...], sh_ref[0], sc_ref[0]).astype(BF16)
        acc_ref[...] = jnp.zeros_like(acc_ref)

    h = h_ref[...]
    act = jax.nn.silu(_dot(h, wg_ref[...])) * _dot(h, wu_ref[...])
    acc_ref[...] += _dot(act.astype(BF16), wd_ref[...])

    @pl.when(k == nk - 1)
    def _():
        r = x_ref[0] + gt_ref[0] * acc_ref[...]
        o_ref[0] = _rms(r, fg_ref[...]) if final else r


def _ffn(x, g, shift, scale, gate, w_g, w_u, w_d, final_g, final):
    bsz, n, d = x.shape
    dff = w_g.shape[1]
    tm = min(n, 1024)
    tf = 256
    nk = dff // tf
    vec = pl.BlockSpec((1, 1, d), lambda b, i, k: (b, 0, 0))
    row = pl.BlockSpec((1, d), lambda b, i, k: (0, 0))
    return pl.pallas_call(
        functools.partial(_ffn_kernel, nk=nk, final=final),
        grid=(bsz, n // tm, nk),
        in_specs=[
            pl.BlockSpec((1, tm, d), lambda b, i, k: (b, i, 0)),
            row, vec, vec, vec,
            pl.BlockSpec((d, tf), lambda b, i, k: (0, k)),
            pl.BlockSpec((d, tf), lambda b, i, k: (0, k)),
            pl.BlockSpec((tf, d), lambda b, i, k: (k, 0)),
            row,
        ],
        out_specs=pl.BlockSpec((1, tm, d), lambda b, i, k: (b, i, 0)),
        out_shape=jax.ShapeDtypeStruct((bsz, n, d), F32),
        scratch_shapes=[pltpu.VMEM((tm, d), BF16), pltpu.VMEM((tm, d), F32)],
        compiler_params=_cparams(("parallel", "parallel", "arbitrary")),
        name="ffn",
    )(x, g, shift, scale, gate, w_g, w_u, w_d, final_g)


def _rope_tables(n_tok):
    rows = n_tok // GRID_W
    row = jnp.broadcast_to(jnp.arange(rows, dtype=F32)[:, None], (rows, GRID_W)).reshape(-1)
    col = jnp.broadcast_to(jnp.arange(GRID_W, dtype=F32)[None, :], (rows, GRID_W)).reshape(-1)
    n_freq = MLA_ROPE // 4
    inv = ROPE_BASE ** (-jnp.arange(n_freq, dtype=F32) / n_freq)
    ang = jnp.concatenate([row[:, None] * inv, col[:, None] * inv], axis=-1)
    cos, sin = jnp.cos(ang), jnp.sin(ang)
    pad = jnp.zeros((n_tok, HEAD_W - MLA_ROPE), F32)
    return jnp.concatenate([cos, cos, pad], axis=-1), jnp.concatenate([-sin, sin, pad], axis=-1)


def _identity_rope_tables(n_tok):
    one = jnp.ones((n_tok, MLA_ROPE), F32)
    pad = jnp.zeros((n_tok, HEAD_W - MLA_ROPE), F32)
    return jnp.concatenate([one, pad], axis=-1), jnp.zeros((n_tok, HEAD_W), F32)


def _layout_w_in(w):
    kr = w[:, OFF_KR:OFF_S5]
    x1, x2 = kr[:, 0::2], kr[:, 1::2]
    pad = jnp.zeros((w.shape[0], HEAD_W - 3 * (MLA_ROPE // 2)), w.dtype)
    return jnp.concatenate([w[:, OFF_GATE:], w[:, OFF_HY:OFF_GATE], w[:, OFF_S5:OFF_HY],
                            w[:, OFF_CQ:OFF_CKV], w[:, OFF_CKV:OFF_KR], x1, x2, x1, pad], axis=1).astype(BF16)


def _layout_w_uq(w):
    w = w.reshape(MLA_Q_LORA, MLA_HEADS, MLA_NOPE + MLA_ROPE)
    nope, rope = w[..., :MLA_NOPE], w[..., MLA_NOPE:]
    x1, x2 = rope[..., 0::2], rope[..., 1::2]
    z32 = jnp.zeros_like(rope)
    wq = jnp.concatenate([x1, x2, z32, nope], axis=-1)
    wqs = jnp.concatenate([x2, x1, z32, jnp.zeros_like(nope)], axis=-1)
    shape = (MLA_Q_LORA, MLA_HEADS * HEAD_W)
    return wq.reshape(shape).astype(BF16), wqs.reshape(shape).astype(BF16)


def _layout_w_ukv(w):
    w = w.reshape(MLA_KV_LORA, MLA_HEADS, MLA_NOPE + MLA_V)
    nope, val = w[..., :MLA_NOPE], w[..., MLA_NOPE:]
    wk = jnp.concatenate([jnp.zeros_like(nope), nope], axis=-1).reshape(MLA_KV_LORA, MLA_HEADS * HEAD_W)
    wvt = jnp.concatenate([val, jnp.zeros_like(val[..., :VT_ROWS - MLA_V])], axis=-1).reshape(
        MLA_KV_LORA, MLA_HEADS * VT_ROWS).T
    return wk.astype(BF16), wvt.astype(BF16)


def kernel(x, c, ctx, c_ctx, ada_w, ada_b, norm_mix, w_in, mla_q_norm, mla_w_uq, mla_kv_norm, mla_w_ukv,
           s5_lam_re, s5_lam_im, s5_log_dt, s5_b_re, s5_b_im, s5_c_re, s5_c_im, s5_d, s5_w_glu,
           hy_conv_w, hy_conv_b, hy_f_w1, hy_f_b1, hy_f_w2, hy_f_b2, hy_f_w3, hy_f_freq, hy_bias,
           w_branch_mla, w_branch_s5, w_branch_hy, w_out, norm_ffn, ffn_w_gu, ffn_w_down, final_norm):
    bsz, n_tok, d = x.shape
    n_ctx = ctx.shape[1]
    depth = ada_w.shape[0]
    rope_x = _rope_tables(n_tok)
    rope_c = _identity_rope_tables(n_ctx)
    h_zero = jnp.zeros((bsz, 2, 4, S5_HSTATE), F32)

    for i in range(depth):
        last = i == depth - 1
        mx = (jax.nn.silu(c) @ ada_w[i] + ada_b[i]).reshape(bsz, N_MOD, 1, d)
        mc = jnp.broadcast_to((jax.nn.silu(c_ctx) @ ada_w[i] + ada_b[i]).reshape(1, N_MOD, 1, d),
                              (bsz, N_MOD, 1, d))
        w_z = _layout_w_in(w_in[i])
        wq, wqs = _layout_w_uq(mla_w_uq[i])
        wk, wvt = _layout_w_ukv(mla_w_ukv[i])
        qn, kvn = mla_q_norm[i][None], mla_kv_norm[i][None]
        s5_tabs = _s5_tables(s5_lam_re[i], s5_lam_im[i], s5_log_dt[i], s5_b_re[i], s5_b_im[i],
                             s5_c_re[i], s5_c_im[i])
        fparams = (hy_f_w1[i], hy_f_b1[i], hy_f_w2[i], hy_f_b2[i], hy_f_w3[i], hy_f_freq[i])
        merge_w = (s5_d[i][None], s5_w_glu[i].astype(BF16),
                   w_branch_mla[i].astype(BF16),
                   w_branch_s5[i].astype(BF16), w_branch_hy[i].astype(BF16), w_out[i].astype(BF16))
        w_g = ffn_w_gu[i][:, :D_FF].astype(BF16)
        w_u = ffn_w_gu[i][:, D_FF:].astype(BF16)
        w_d = ffn_w_down[i].astype(BF16)
        g_mix, g_ffn = norm_mix[i][None], norm_ffn[i][None]

        zx = _proj_in(x, g_mix, mx[:, 0], mx[:, 1], w_z)
        zc = _proj_in(ctx, g_mix, mc[:, 0], mc[:, 1], w_z)

        n_keys = n_tok + n_ctx
        q_x, k_all, vt_all = _mla_prep(zx, qn, kvn, wq, wqs, wk, wvt, *rope_x, n_keys, 0)
        q_c, k_all, vt_all = _mla_prep(zc, qn, kvn, wq, wqs, wk, wvt, *rope_c, n_keys, n_tok, (k_all, vt_all))
        a_x = _attention(q_x, k_all, vt_all, n_keys)

        *y5_c, finals = _s5_scan(zc, h_zero, s5_tabs)
        *y5_x, _ = _s5_scan(zx, finals, s5_tabs)

        v_x, g1_x, g2_x = _short_conv(zx, hy_conv_w[i], hy_conv_b[i][None])
        e_x = _hyena_mixer(v_x, (g1_x, g2_x), _hyena_filters(n_tok, *fparams), hy_bias[i])

        x = _merge(x, zx, a_x, y5_x, e_x, *merge_w, mx[:, 2])
        x = _ffn(x, g_ffn, mx[:, 3], mx[:, 4], mx[:, 5], w_g, w_u, w_d, final_norm[None], last)

        if not last:
            a_c = _attention(q_c, k_all, vt_all, n_ctx, n_tok // n_ctx)
            v_c, g1_c, g2_c = _short_conv(zc, hy_conv_w[i], hy_conv_b[i][None])
            e_c = _hyena_mixer(v_c, (g1_c, g2_c), _hyena_filters(n_ctx, *fparams), hy_bias[i])
            ctx = _merge(ctx, zc, a_c, y5_c, e_c, *merge_w, mc[:, 2])
            ctx = _ffn(ctx, g_ffn, mc[:, 3], mc[:, 4], mc[:, 5], w_g, w_u, w_d, final_norm[None], False)
    return x
```

```python
import functools
import math

import jax
import jax.numpy as jnp
from jax import lax
from jax.experimental import pallas as pl
from jax.experimental.pallas import tpu as pltpu

F32 = jnp.float32
BF16 = jnp.bfloat16
HIGHEST = lax.Precision.HIGHEST

D_MODEL = 1024
GRID_W = 64
NORM_EPS = 1e-6
N_MOD = 6

MLA_HEADS = 8
MLA_NOPE = 64
MLA_ROPE = 32
MLA_V = 64
MLA_Q_LORA = 256
MLA_KV_LORA = 128
ROPE_BASE = 10000.0
HEAD_W = 128
MLA_PREP_ROWS = 1024
VT_ROWS = 80
QK_SCALE_LOG2 = (MLA_NOPE + MLA_ROPE) ** -0.5 * math.log2(math.e)
QK_HALF_SCALE = QK_SCALE_LOG2 ** 0.5
QK_DTYPE = jnp.float8_e4m3fn
ATTN_SUB = 128
ATTN_LAG = 2
ATTN_UNROLL = 12

S5_WIDTH = 512
S5_GROUP = 16
S5_GROUPS = 32
S5_STATE = 64
S5_HALF = 256
S5_HSTATE = 1024
S5_CHUNK = 128
S5_CHUNKS_PER_STEP = 4

HY_WIDTH = 512
HY_ORDER = 2
HY_BANDS = 16
HY_POS_DIM = 1 + 2 * HY_BANDS
HY_POS_PAD = 64
HY_FILTER_HIDDEN = 64
HY_FILTER_OUT = HY_ORDER * 2 * HY_WIDTH
HY_DECAY_TARGET = 1e-2
HY_DECAY_SHORT = 0.3
HY_DECAY_LONG = 1.5
HY_DECAY_SHIFT = 0.05

D_FF = 2816

OFF_CQ = 0
OFF_CKV = OFF_CQ + MLA_Q_LORA
OFF_KR = OFF_CKV + MLA_KV_LORA
OFF_S5 = OFF_KR + MLA_ROPE
OFF_HY = OFF_S5 + S5_WIDTH
OFF_GATE = OFF_HY + 3 * HY_WIDTH

Z_GATE = 0
Z_HY = 3072
Z_S5 = 4608
Z_MLA = 5120
Z_WIDTH = 5632

VMEM_LIMIT_V7X = 52 * 1024 * 1024


def _cparams(sem, flags=None):
    return pltpu.CompilerParams(dimension_semantics=sem, vmem_limit_bytes=VMEM_LIMIT_V7X, flags=flags)


def _dot(a, b):
    return jnp.dot(a, b, preferred_element_type=F32)


def _rms(x, g):
    return x * lax.rsqrt(jnp.mean(x * x, axis=-1, keepdims=True) + NORM_EPS) * g


def _norm_mod(x, g, shift, scale):
    return _rms(x, g) * (1.0 + scale) + shift


ADA_ROWS = 8


def _ada_kernel(c_ref, w_ref, b_ref, o_ref):
    c = c_ref[...]
    h = c * jax.nn.sigmoid(c)
    o_ref[...] = jnp.dot(h, w_ref[...], precision=HIGHEST, preferred_element_type=F32) + b_ref[...]


def _ada_modulation(cond, w, b):
    rows, d = cond.shape
    nout = w.shape[1]
    tn = nout // 4
    return pl.pallas_call(
        _ada_kernel,
        grid=(nout // tn,),
        in_specs=[
            pl.BlockSpec((rows, d), lambda j: (0, 0)),
            pl.BlockSpec((d, tn), lambda j: (0, j)),
            pl.BlockSpec((1, tn), lambda j: (0, j)),
        ],
        out_specs=pl.BlockSpec((rows, tn), lambda j: (0, j)),
        out_shape=jax.ShapeDtypeStruct((rows, nout), F32),
        compiler_params=_cparams(("parallel",)),
        name="ada_modulation",
    )(cond, w, b)


def _proj_kernel(x_ref, g_ref, sh_ref, sc_ref, w_ref, o_ref, h_ref):
    @pl.when(pl.program_id(2) == 0)
    def _():
        h_ref[...] = _norm_mod(x_ref[0], g_ref[...], sh_ref[0], sc_ref[0]).astype(BF16)

    o_ref[0] = _dot(h_ref[...], w_ref[...]).astype(BF16)


def _proj_in(x, g, shift, scale, w):
    bsz, n, d = x.shape
    nz = w.shape[1]
    tm = min(n, 2048)
    tn = 512
    return pl.pallas_call(
        _proj_kernel,
        grid=(bsz, n // tm, nz // tn),
        in_specs=[
            pl.BlockSpec((1, tm, d), lambda b, i, j: (b, i, 0)),
            pl.BlockSpec((1, d), lambda b, i, j: (0, 0)),
            pl.BlockSpec((1, 1, d), lambda b, i, j: (b, 0, 0)),
            pl.BlockSpec((1, 1, d), lambda b, i, j: (b, 0, 0)),
            pl.BlockSpec((d, tn), lambda b, i, j: (0, j)),
        ],
        out_specs=pl.BlockSpec((1, tm, tn), lambda b, i, j: (b, i, j)),
        out_shape=jax.ShapeDtypeStruct((bsz, n, nz), BF16),
        scratch_shapes=[pltpu.VMEM((tm, d), BF16)],
        compiler_params=_cparams(("parallel", "parallel", "arbitrary")),
        name="proj_in",
    )(x, g, shift, scale, w)


def _mla_prep_kernel(z_ref, qn_ref, kvn_ref, wq_ref, wqs_ref, wk_ref, wvt_ref, c_ref, s_ref, *rest):
    q_ref, k_ref, vt_ref = rest[-3:]
    z = z_ref[0].astype(F32)
    hq = _rms(z[:, :MLA_Q_LORA], qn_ref[...]).astype(BF16)
    hkv = _rms(z[:, MLA_Q_LORA:MLA_Q_LORA + MLA_KV_LORA], kvn_ref[...]).astype(BF16)
    krb = z[:, MLA_Q_LORA + MLA_KV_LORA:]
    qa = _dot(hq, wq_ref[...])
    qb = _dot(hq, wqs_ref[...])
    kn = _dot(hkv, wk_ref[...])
    vt = lax.dot_general(wvt_ref[...], hkv, (((1,), (1,)), ((), ())), preferred_element_type=F32)
    ck = c_ref[...]
    sn = s_ref[...]
    lane = lax.broadcasted_iota(jnp.int32, ck.shape, 1)
    cq = jnp.where(lane < MLA_ROPE, ck, 1.0)
    kr = krb * ck + pltpu.roll(krb, HEAD_W - MLA_ROPE // 2, 1) * sn
    row = lax.broadcasted_iota(jnp.int32, (VT_ROWS, z.shape[0]), 0)
    for h in range(MLA_HEADS):
        sl = slice(h * HEAD_W, (h + 1) * HEAD_W)
        q_ref[0, h] = ((qa[:, sl] * cq + qb[:, sl] * sn) * QK_HALF_SCALE).astype(QK_DTYPE)
        k_ref[0, h] = ((kr + kn[:, sl]) * QK_HALF_SCALE).astype(QK_DTYPE)
        vt_ref[0, h] = jnp.where(row == MLA_V, 1.0, vt[h * VT_ROWS:(h + 1) * VT_ROWS, :]).astype(BF16)


def _mla_prep(z, qn, kvn, wq, wqs, wk, wvt, ctab, stab, n_keys, key_off, kv_bufs=None):
    bsz, n, _ = z.shape
    tm = min(n, MLA_PREP_ROWS)
    assert key_off % tm == 0, "the key offset must be a whole number of row tiles"
    ob = key_off // tm
    hw = MLA_HEADS * HEAD_W
    zb = Z_MLA // 512
    full = lambda shape: pl.BlockSpec(shape, lambda b, i: (0,) * len(shape))
    in_specs = [
        pl.BlockSpec((1, tm, 512), lambda b, i: (b, i, zb)),
        full((1, MLA_Q_LORA)),
        full((1, MLA_KV_LORA)),
        full((MLA_Q_LORA, hw)),
        full((MLA_Q_LORA, hw)),
        full((MLA_KV_LORA, hw)),
        full((MLA_HEADS * VT_ROWS, MLA_KV_LORA)),
        pl.BlockSpec((tm, HEAD_W), lambda b, i: (i, 0)),
        pl.BlockSpec((tm, HEAD_W), lambda b, i: (i, 0)),
    ]
    args = [z, qn, kvn, wq, wqs, wk, wvt, ctab, stab]
    aliases = {}
    if kv_bufs is not None:
        aliases = {len(args): 1, len(args) + 1: 2}
        in_specs += [pl.BlockSpec(memory_space=pl.ANY)] * 2
        args += list(kv_bufs)
    return pl.pallas_call(
        _mla_prep_kernel,
        grid=(bsz, n // tm),
        in_specs=in_specs,
        out_specs=[
            pl.BlockSpec((1, MLA_HEADS, tm, HEAD_W), lambda b, i: (b, 0, i, 0)),
            pl.BlockSpec((1, MLA_HEADS, tm, HEAD_W), lambda b, i: (b, 0, i + ob, 0)),
            pl.BlockSpec((1, MLA_HEADS, VT_ROWS, tm), lambda b, i: (b, 0, 0, i + ob)),
        ],
        out_shape=[
            jax.ShapeDtypeStruct((bsz, MLA_HEADS, n, HEAD_W), QK_DTYPE),
            jax.ShapeDtypeStruct((bsz, MLA_HEADS, n_keys, HEAD_W), QK_DTYPE),
            jax.ShapeDtypeStruct((bsz, MLA_HEADS, VT_ROWS, n_keys), BF16),
        ],
        input_output_aliases=aliases,
        compiler_params=_cparams(("parallel", "parallel")),
        name="mla_prep",
    )(*args)


def _attn_kernel(q_ref, k_ref, vt_ref, o_ref, s0_ref, s1_ref, s2_ref, p0_ref, p1_ref, p2_ref, acc_ref, *, tk, nkc):
    q = q_ref[0, 0]
    tq = q.shape[0]
    s_refs = (s0_ref, s1_ref, s2_ref)
    p_refs = (p0_ref, p1_ref, p2_ref)

    ts = min(tk, ATTN_SUB)
    subs = [slice(r, r + ts) for r in range(0, tk, ts)]

    def scores_sub(j, s_ref, sub, cmax):
        off = pl.multiple_of(j * tk + sub.start, ts)
        s = lax.dot_general(k_ref[0, 0, pl.ds(off, ts), :], q, (((1,), (1,)), ((), ())),
                            preferred_element_type=F32)
        s_ref[sub, :] = s
        cm = jnp.max(s, axis=0, keepdims=True)
        return cm if cmax is None else jnp.maximum(cmax, cm)

    def softmax_sub(slot, sub, mn):
        p_refs[slot][sub, :] = jnp.exp2(s_refs[slot][sub, :] - mn).astype(BF16)

    def step(i, slot, carry, do_softmax, do_scores):
        cm, m, alpha = carry
        mn = jnp.maximum(m, cm)
        off = pl.multiple_of(i * tk, tk)
        acc_ref[...] = alpha * acc_ref[...] + _dot(vt_ref[0, 0, :, pl.ds(off, tk)], p_refs[slot][...])
        cm_new, cms = None, []
        for r, sub in enumerate(subs):
            mn_sub = mn
            if do_scores:
                cm_new = scores_sub(i + 2, s_refs[(slot + 2) % 3], sub, cm_new)
                cms.append(cm_new)
                if r >= ATTN_LAG:
                    mn_sub = jnp.maximum(mn, cms[r - ATTN_LAG] - 1e30)
            if do_softmax:
                softmax_sub((slot + 1) % 3, sub, mn_sub)
        if do_softmax:
            alpha, m = jnp.exp2(m - mn), mn
        return (cm_new if do_scores else cm), m, alpha

    m = jnp.full((1, tq), -1e30, F32)
    acc_ref[...] = jnp.zeros_like(acc_ref)
    cm, cm_next = None, None
    for sub in subs:
        cm = scores_sub(0, s0_ref, sub, cm)
        if nkc > 1:
            cm_next = scores_sub(1, s1_ref, sub, cm_next)
    mn = jnp.maximum(m, cm)
    for sub in subs:
        softmax_sub(0, sub, mn)
    carry = (cm_next if nkc > 1 else cm, mn, jnp.exp2(m - mn))

    def body(t, carry):
        for r in range(ATTN_UNROLL):
            carry = step(ATTN_UNROLL * t + r, r % 3, carry, True, True)
        return carry

    nloop = max(nkc - 2, 0) // ATTN_UNROLL
    carry = lax.fori_loop(0, nloop, body, carry)
    for i in range(ATTN_UNROLL * nloop, nkc):
        carry = step(i, i % 3, carry, i + 1 < nkc, i + 2 < nkc)
    acc = acc_ref[...]
    out = acc * (1.0 / acc[MLA_V:MLA_V + 1, :])
    out = jnp.concatenate([out, jnp.zeros((HEAD_W - VT_ROWS, tq), F32)], axis=0).T
    o_ref[0, 0] = out[:, :MLA_V].astype(BF16)


def _kv_chunk(nk):
    for tk in (640, 512, 256, 128):
        if nk % tk == 0:
            return tk
    raise ValueError(f"unsupported key count {nk}")


def _attention(q, k, vt, nk, key_block=0):
    bsz, nh, nq, _ = q.shape
    tq = min(nq, 1024)
    tk = _kv_chunk(nk)
    return pl.pallas_call(
        functools.partial(_attn_kernel, tk=tk, nkc=nk // tk),
        grid=(bsz, nh, nq // tq),
        in_specs=[
            pl.BlockSpec((1, 1, tq, HEAD_W), lambda b, h, i: (b, h, i, 0)),
            pl.BlockSpec((1, 1, nk, HEAD_W), lambda b, h, i: (b, h, key_block, 0)),
            pl.BlockSpec((1, 1, VT_ROWS, nk), lambda b, h, i: (b, h, 0, key_block)),
        ],
        out_specs=pl.BlockSpec((1, 1, tq, MLA_V), lambda b, h, i: (b, h, i, 0)),
        out_shape=jax.ShapeDtypeStruct((bsz, nh, nq, MLA_V), BF16),
        scratch_shapes=[pltpu.VMEM((tk, tq), F32)] * 3 + [pltpu.VMEM((tk, tq), BF16)] * 3
        + [pltpu.VMEM((VT_ROWS, tq), F32)],
        compiler_params=_cparams(("parallel", "parallel", "arbitrary")),
        name="attention",
    )(q, k, vt)


def _cmul(ar, ai, br, bi):
    return ar * br - ai * bi, ar * bi + ai * br


def _s5_kernel(u_ref, h0_ref, bm_ref, cm_ref, wn_ref, wp_ref, l1_ref, tri_ref, y_ref, hf_ref,
               carry_ref, h0cat_ref, h1cat_ref, *, tc, nr, ns, reverse):
    i = pl.program_id(1)

    @pl.when(i == 0)
    def _():
        carry_ref[...] = h0_ref[0]

    u = u_ref[0]
    tri = tri_ref[...]
    order = range(nr - 1, -1, -1) if reverse else range(nr)
    edge = 0 if reverse else tc - 1
    hcat_refs = (h0cat_ref, h1cat_ref)
    bus = [_dot(u[:, hf * S5_HALF:(hf + 1) * S5_HALF], bm_ref[hf]) for hf in range(2)]
    carries = [(carry_ref[2 * hf:2 * hf + 1, :], carry_ref[2 * hf + 1:2 * hf + 2, :]) for hf in range(2)]
    for c in order:
        rows = slice(c * tc, (c + 1) * tc)
        for hf in range(2):
            bu = bus[hf]
            xr, xi = _cmul(wn_ref[hf, 0], wn_ref[hf, 1], bu[rows, :S5_HSTATE], bu[rows, S5_HSTATE:])
            s = _dot(tri, jnp.concatenate([xr, xi], axis=1).astype(BF16))
            cr, ci = _cmul(l1_ref[hf, 0], l1_ref[hf, 1], *carries[hf])
            hr, hi = _cmul(wp_ref[hf, 0], wp_ref[hf, 1], s[:, :S5_HSTATE] + cr, s[:, S5_HSTATE:] + ci)
            hcat_refs[hf][rows, :S5_HSTATE] = hr.astype(BF16)
            hcat_refs[hf][rows, S5_HSTATE:] = hi.astype(BF16)
            carries[hf] = (hr[edge:edge + 1], hi[edge:edge + 1])
    for hf in range(2):
        carry_ref[2 * hf:2 * hf + 1, :] = carries[hf][0]
        carry_ref[2 * hf + 1:2 * hf + 2, :] = carries[hf][1]
    y_ref[0] = jnp.concatenate([_dot(hcat_refs[hf][...], cm_ref[hf]) for hf in range(2)], axis=1)

    @pl.when(i == ns - 1)
    def _():
        hf_ref[0] = carry_ref[...]


def _s5_scan_dir(z, h0, tabs, reverse):
    bm, cm, wn, wp, l1, tri = tabs
    bsz, n, _ = z.shape
    tc = S5_CHUNK
    nr = min(S5_CHUNKS_PER_STEP, n // tc)
    rows = nr * tc
    ns = n // rows
    zb = Z_S5 // S5_WIDTH
    blk = (lambda i: ns - 1 - i) if reverse else (lambda i: i)
    full = lambda shape: pl.BlockSpec(shape, lambda b, i: (0,) * len(shape))
    return pl.pallas_call(
        functools.partial(_s5_kernel, tc=tc, nr=nr, ns=ns, reverse=reverse),
        grid=(bsz, ns),
        in_specs=[
            pl.BlockSpec((1, rows, S5_WIDTH), lambda b, i: (b, blk(i), zb)),
            pl.BlockSpec((1, 4, S5_HSTATE), lambda b, i: (b, 0, 0)),
            full((2, S5_HALF, 2 * S5_HSTATE)),
            full((2, 2 * S5_HSTATE, S5_HALF)),
            full((2, 2, tc, S5_HSTATE)),
            full((2, 2, tc, S5_HSTATE)),
            full((2, 2, 1, S5_HSTATE)),
            full((tc, tc)),
        ],
        out_specs=[
            pl.BlockSpec((1, rows, S5_WIDTH), lambda b, i: (b, blk(i), 0)),
            pl.BlockSpec((1, 4, S5_HSTATE), lambda b, i: (b, 0, 0)),
        ],
        out_shape=[
            jax.ShapeDtypeStruct((bsz, n, S5_WIDTH), F32),
            jax.ShapeDtypeStruct((bsz, 4, S5_HSTATE), F32),
        ],
        scratch_shapes=[pltpu.VMEM((4, S5_HSTATE), F32),
                        pltpu.VMEM((rows, 2 * S5_HSTATE), BF16), pltpu.VMEM((rows, 2 * S5_HSTATE), BF16)],
        compiler_params=_cparams(("parallel", "arbitrary")),
        name="s5_scan",
    )(z, h0, bm, cm, wn, wp, l1, tri)


def _s5_scan(z, h0, tabs):
    outs = [_s5_scan_dir(z, h0[:, d], [t[d] for t in tabs], reverse=bool(d)) for d in range(2)]
    return outs[0][0], outs[1][0], jnp.stack([outs[0][1], outs[1][1]], axis=1)


def _s5_tables(lam_re, lam_im, log_dt, b_re, b_im, c_re, c_im):
    tc = S5_CHUNK
    dt = jnp.exp(log_dt)[..., None]
    zr, zi = lam_re * dt, lam_im * dt
    mag = jnp.exp(zr)
    lbr, lbi = mag * jnp.cos(zi), mag * jnp.sin(zi)
    den = lam_re * lam_re + lam_im * lam_im
    nr, ni = lbr - 1.0, lbi
    cfr = (nr * lam_re + ni * lam_im) / den
    cfi = (ni * lam_re - nr * lam_im) / den
    bbr = cfr[..., None] * b_re - cfi[..., None] * b_im
    bbi = cfr[..., None] * b_im + cfi[..., None] * b_re
    eye = jnp.eye(S5_GROUP, dtype=F32)

    def blockdiag_in(b):
        b = b.reshape(2, 2, S5_GROUP, S5_STATE, S5_GROUP)
        return jnp.einsum('gk,dhgpn->dhgnkp', eye, b).reshape(2, 2, S5_HALF, S5_HSTATE)

    def blockdiag_out(c):
        c = c.reshape(2, 2, S5_GROUP, S5_GROUP, S5_STATE)
        return jnp.einsum('gk,dhgnp->dhgpkn', eye, c).reshape(2, 2, S5_HSTATE, S5_HALF)

    bm = jnp.concatenate([blockdiag_in(bbr), blockdiag_in(bbi)], axis=-1).astype(BF16)
    cm = jnp.concatenate([blockdiag_out(c_re), blockdiag_out(-c_im)], axis=-2).astype(BF16)

    def powers(k):
        zr_ = zr.reshape(2, 2, 1, S5_HSTATE)
        zi_ = zi.reshape(2, 2, 1, S5_HSTATE)
        kk = k[:, None, :, None]
        m = jnp.exp(kk * zr_)
        return jnp.stack([m * jnp.cos(kk * zi_), m * jnp.sin(kk * zi_)], axis=2)

    mid = tc // 2
    t = jnp.arange(tc, dtype=F32) - mid
    wn = powers(jnp.stack([-t, -t[::-1]]))
    wp = powers(jnp.stack([t, t[::-1]]))
    l1 = powers(jnp.full((2, 1), mid + 1, F32))
    r = jnp.arange(tc)
    tri = jnp.stack([r[:, None] >= r[None, :], r[:, None] <= r[None, :]]).astype(BF16)
    return bm, cm, wn, wp, l1, tri


HALO = 16


def _sconv_kernel(z_ref, zp_ref, zn_ref, w_ref, b_ref, v_ref, g1_ref, g2_ref, *, nt):
    i = pl.program_id(1)
    z = z_ref[0].astype(F32)
    tm = z.shape[0]
    row = lax.broadcasted_iota(jnp.int32, z.shape, 0)
    prev = jnp.where(i > 0, zp_ref[0, HALO - 1:HALO, :].astype(F32), 0.0)
    nxt = jnp.where(i < nt - 1, zn_ref[0, 0:1, :].astype(F32), 0.0)
    up = jnp.where(row == 0, prev, pltpu.roll(z, 1, 0))
    dn = jnp.where(row == tm - 1, nxt, pltpu.roll(z, tm - 1, 0))
    u = up * w_ref[0:1, :] + z * w_ref[1:2, :] + dn * w_ref[2:3, :] + b_ref[...]
    for k, o_ref in enumerate((v_ref, g1_ref, g2_ref)):
        o_ref[0] = u[:, k * HY_WIDTH:(k + 1) * HY_WIDTH]


def _short_conv(z, w, b):
    bsz, n, _ = z.shape
    cw = 3 * HY_WIDTH
    tm = min(n, 1024)
    nt = n // tm
    zb = Z_HY // cw
    rh = tm // HALO
    out = pl.BlockSpec((1, tm, HY_WIDTH), lambda b_, i: (b_, i, 0))
    return pl.pallas_call(
        functools.partial(_sconv_kernel, nt=nt),
        grid=(bsz, nt),
        in_specs=[
            pl.BlockSpec((1, tm, cw), lambda b_, i: (b_, i, zb)),
            pl.BlockSpec((1, HALO, cw), lambda b_, i: (b_, jnp.maximum(i * rh - 1, 0), zb)),
            pl.BlockSpec((1, HALO, cw), lambda b_, i: (b_, jnp.minimum((i + 1) * rh, n // HALO - 1), zb)),
            pl.BlockSpec((3, cw), lambda b_, i: (0, 0)),
            pl.BlockSpec((1, cw), lambda b_, i: (0, 0)),
        ],
        out_specs=[out] * 3,
        out_shape=[jax.ShapeDtypeStruct((bsz, n, HY_WIDTH), F32)] * 3,
        compiler_params=_cparams(("parallel", "parallel")),
        name="short_conv",
    )(z, z, z, w, b)


def _filt_kernel(f_ref, w1_ref, b1_ref, w2_ref, b2_ref, w3h_ref, w3l_ref, fq_ref, dl_ref, o_ref, *, n_tok):
    z = f_ref[...]
    tm = z.shape[0]
    fq = fq_ref[...]
    hid = jnp.sin(fq * (jnp.dot(z, w1_ref[...], precision=HIGHEST, preferred_element_type=F32) + b1_ref[...]))
    hid = jnp.sin(fq * (jnp.dot(hid, w2_ref[...], precision=HIGHEST, preferred_element_type=F32) + b2_ref[...]))
    hid_hi = hid.astype(BF16)
    hid_lo = (hid - hid_hi.astype(F32)).astype(BF16)

    def filters(rows, d):
        wh, wl = w3h_ref[d], w3l_ref[d]
        f = _dot(hid_hi[rows], wh) + (_dot(hid_hi[rows], wl) + _dot(hid_lo[rows], wh))
        return f * (jnp.exp(-z[rows, 0:1] * dl_ref[d]) + HY_DECAY_SHIFT)

    i = pl.program_id(0)
    filt = filters(slice(None), jnp.where(i * tm >= n_tok, 1, 0))
    m = i * tm + lax.broadcasted_iota(jnp.int32, filt.shape, 0)
    filt = jnp.where(m == n_tok, 0.0, filt)
    for o in range(HY_ORDER):
        o_ref[o] = filt[:, o * HY_WIDTH:(o + 1) * HY_WIDTH]

    @pl.when(i == 0)
    def _():
        head = slice(0, HALO)
        lag0 = filters(head, 1)
        lag0 = jnp.where(lax.broadcasted_iota(jnp.int32, lag0.shape, 0) == 0, lag0, 0.0)
        for o in range(HY_ORDER):
            o_ref[o, head, :] += lag0[:, o * HY_WIDTH:(o + 1) * HY_WIDTH]


def _hyena_filter_feats(n_tok):
    m = jnp.arange(2 * n_tok)
    lag = jnp.where(m < n_tok, m, jnp.where(m > n_tok, 2 * n_tok - m, 0))
    t = (lag.astype(F32) / (n_tok - 1))[:, None]
    w = (2.0 * math.pi * lag.astype(F32) / n_tok)[:, None]
    bands = jnp.linspace(1e-4, HY_BANDS - 1, HY_BANDS, dtype=F32)[None, :]
    feats = jnp.concatenate([t, jnp.cos(bands * w), -jnp.sin(bands * w)], axis=-1)
    return jnp.pad(feats, ((0, 0), (0, HY_POS_PAD - HY_POS_DIM)))


def _hyena_filters(n_tok, w1, b1, w2, b2, w3, freq):
    feats = _hyena_filter_feats(n_tok)
    deltas = jnp.abs(jnp.linspace(math.log(HY_DECAY_TARGET) / HY_DECAY_SHORT,
                                  math.log(HY_DECAY_TARGET) / HY_DECAY_LONG, HY_FILTER_OUT, dtype=F32))[None, :]
    w1p = jnp.pad(w1, ((0, HY_POS_PAD - HY_POS_DIM), (0, 0)))

    def by_direction(a):
        a = a.reshape(a.shape[:-1] + (HY_ORDER, 2, HY_WIDTH))
        return jnp.moveaxis(a, -2, 0).reshape((2,) + a.shape[:-3] + (HY_ORDER * HY_WIDTH,))

    w3, deltas = by_direction(w3), by_direction(deltas)
    w3_hi = w3.astype(BF16)
    w3_lo = (w3 - w3_hi.astype(F32)).astype(BF16)
    n2 = 2 * n_tok
    tm = min(n_tok, 1024)
    full = lambda shape: pl.BlockSpec(shape, lambda i: (0,) * len(shape))
    hh = HY_FILTER_HIDDEN
    wd = HY_ORDER * HY_WIDTH
    return pl.pallas_call(
        functools.partial(_filt_kernel, n_tok=n_tok),
        grid=(n2 // tm,),
        in_specs=[
            pl.BlockSpec((tm, HY_POS_PAD), lambda i: (i, 0)),
            full((HY_POS_PAD, hh)), full((1, hh)), full((hh, hh)), full((1, hh)),
            full((2, hh, wd)), full((2, hh, wd)), full((1, hh)), full((2, 1, wd)),
        ],
        out_specs=pl.BlockSpec((HY_ORDER, tm, HY_WIDTH), lambda i: (0, i, 0)),
        out_shape=jax.ShapeDtypeStruct((HY_ORDER, n2, HY_WIDTH), F32),
        compiler_params=_cparams(("parallel",)),
        name="hyena_filter",
    )(feats, w1p, b1[None], w2, b2[None], w3_hi, w3_lo, freq[None], deltas)


def _fft_split(n_fft):
    if n_fft <= 1024:
        return n_fft, 1
    n1 = 1 << (int(math.log2(n_fft)) // 2)
    return n1, n_fft // n1


def _cis(num, den, sign):
    ang = (2.0 * math.pi / den) * num.astype(F32)
    return jnp.cos(ang), sign * jnp.sin(ang)


def _fft_tables(n_fft):
    n1, n2 = _fft_split(n_fft)
    a = jnp.arange(n1)
    f1r, f1i = _cis((a[:, None] * a[None, :]) % n1, n1, -1.0)
    s1 = jnp.concatenate([f1r, f1i], axis=0).astype(BF16)
    half = n1 // 2
    s3 = (jnp.concatenate([f1r[:half], -f1i[:half]], axis=0) / n_fft).astype(BF16)
    if n2 == 1:
        return s1, s3, None, None
    b = jnp.arange(n2)
    f2r, f2i = _cis((b[:, None] * b[None, :]) % n2, n2, -1.0)
    twr, twi = _cis((jnp.arange(n1)[:, None] * b[None, :]) % n_fft, n_fft, -1.0)
    fr, fi = _cmul(f2r[None], f2i[None], twr[:, None, :], twi[:, None, :])
    ft = jnp.concatenate([fr, fi], axis=1).astype(BF16)
    frt, fit = _cmul(f2r[None], f2i[None], twr[:, :, None], twi[:, :, None])
    it = jnp.concatenate([frt, -fit], axis=1).astype(BF16)
    return s1, s3, ft, it


def _stacked_cdot(s, xr, xi, m):
    p = _dot(s, xr.astype(BF16))
    if xi is None:
        return p[:m], p[m:]
    q = _dot(s, xi.astype(BF16))
    return p[:m] - q[m:], q[:m] + p[m:]


def _fft_a_kernel(*refs, parts, n1, has_g):
    if has_g:
        x_ref, s_ref, g_ref, y_ref = refs
    else:
        x_ref, s_ref, y_ref = refs
    yr, yi = _stacked_cdot(s_ref[...], x_ref[0], x_ref[1] if parts == 2 else None, n1)
    if has_g:
        yr, yi = _cmul(yr, yi, g_ref[0], g_ref[1])
    y_ref[0] = yr
    y_ref[1] = yi


def _fft_a(x, s1, g=None):
    parts, a_rows, cols = x.shape
    n1 = s1.shape[0] // 2
    tc = min(cols, 2048)
    in_specs = [
        pl.BlockSpec((parts, a_rows, tc), lambda j: (0, 0, j)),
        pl.BlockSpec((2 * n1, a_rows), lambda j: (0, 0)),
    ]
    args = [x, s1[:, :a_rows]]
    if g is not None:
        in_specs.append(pl.BlockSpec((2, n1, tc), lambda j: (0, 0, j)))
        args.append(g)
    return pl.pallas_call(
        functools.partial(_fft_a_kernel, parts=parts, n1=n1, has_g=g is not None),
        grid=(cols // tc,),
        in_specs=in_specs,
        out_specs=pl.BlockSpec((2, n1, tc), lambda j: (0, 0, j)),
        out_shape=jax.ShapeDtypeStruct((2, n1, cols), F32),
        compiler_params=_cparams(("parallel",)),
        name="fft_stage_a",
    )(*args)


FFT_CB = 4


def _fft_b_kernel(*refs, n2, inverse):
    if inverse:
        y_ref, ft_ref, it_ref, g_ref, o_ref = refs
    else:
        y_ref, ft_ref, o_ref = refs
    for c in range(FFT_CB):
        xr, xi = _stacked_cdot(ft_ref[c], y_ref[0, c], y_ref[1, c], n2)
        if inverse:
            zr, zi = _cmul(xr, xi, g_ref[0, c], g_ref[1, c])
            xr, xi = _stacked_cdot(it_ref[c], zr, zi, n2)
        o_ref[0, c] = xr.astype(o_ref.dtype)
        o_ref[1, c] = xi.astype(o_ref.dtype)


def _fft_b(y, ft, it=None, g=None):
    _, n1, n2, ch = y.shape
    inverse = it is not None
    blk = pl.BlockSpec((2, FFT_CB, n2, ch), lambda c: (0, c, 0, 0))
    mat = pl.BlockSpec((FFT_CB, 2 * n2, n2), lambda c: (c, 0, 0))
    in_specs, args = [blk, mat], [y, ft]
    if inverse:
        in_specs += [mat, blk]
        args += [it, g]
    return pl.pallas_call(
        functools.partial(_fft_b_kernel, n2=n2, inverse=inverse),
        grid=(n1 // FFT_CB,),
        in_specs=in_specs,
        out_specs=blk,
        out_shape=jax.ShapeDtypeStruct(y.shape, BF16 if inverse else F32),
        compiler_params=_cparams(("parallel",)),
        name="fft_stage_b",
    )(*args)


def _fft_c_kernel(y_ref, s_ref, x_ref, g_ref, b_ref, o_ref, *, half):
    cr, ci = _stacked_cdot(s_ref[...], y_ref[0], y_ref[1], half)
    bias = b_ref[...]
    o_ref[0] = g_ref[0] * (cr + bias * x_ref[0])
    o_ref[1] = g_ref[1] * (ci + bias * x_ref[1])


def _fft_c(y, s3, x, gate, bias_cols):
    _, n1, cols = y.shape
    half = n1 // 2
    tc = min(cols, 2048)
    io = pl.BlockSpec((2, half, tc), lambda j: (0, 0, j))
    return pl.pallas_call(
        functools.partial(_fft_c_kernel, half=half),
        grid=(cols // tc,),
        in_specs=[
            pl.BlockSpec((2, n1, tc), lambda j: (0, 0, j)),
            pl.BlockSpec((n1, n1), lambda j: (0, 0)),
            io, io,
            pl.BlockSpec((1, tc), lambda j: (0, j)),
        ],
        out_specs=io,
        out_shape=jax.ShapeDtypeStruct((2, half, cols), F32),
        compiler_params=_cparams(("parallel",)),
        name="fft_stage_c",
    )(y, s3, x, gate, bias_cols)


FFT_BT = 16


def _to_fine_major(x):
    return pltpu.einshape("abc->bac", x)


def _fft_a4_kernel(x_ref, s_ref, y_ref, *, parts, n1):
    s = s_ref[...]
    xr = _to_fine_major(x_ref[0].astype(BF16))
    xi = _to_fine_major(x_ref[1].astype(BF16)) if parts == 2 else None
    out = [_stacked_cdot(s, xr[b], None if xi is None else xi[b], n1) for b in range(FFT_BT)]
    y_ref[0] = pltpu.einshape("bac->abc", jnp.stack([o[0] for o in out])).astype(BF16)
    y_ref[1] = pltpu.einshape("bac->abc", jnp.stack([o[1] for o in out])).astype(BF16)


def _fft_a4(x, s1):
    parts, a_rows, n2, ch = x.shape
    n1 = s1.shape[0] // 2
    return pl.pallas_call(
        functools.partial(_fft_a4_kernel, parts=parts, n1=n1),
        grid=(n2 // FFT_BT,),
        in_specs=[
            pl.BlockSpec((parts, a_rows, FFT_BT, ch), lambda j: (0, 0, j, 0)),
            pl.BlockSpec((2 * n1, a_rows), lambda j: (0, 0)),
        ],
        out_specs=pl.BlockSpec((2, n1, FFT_BT, ch), lambda j: (0, 0, j, 0)),
        out_shape=jax.ShapeDtypeStruct((2, n1, n2, ch), BF16),
        compiler_params=_cparams(("parallel",)),
        name="fft_stage_a",
    )(x, s1[:, :a_rows])


def _fft_c4_kernel(y_ref, s_ref, x_ref, g_ref, b_ref, o_ref, c_ref, *, half):
    s = s_ref[...]
    yr = _to_fine_major(y_ref[0].astype(BF16))
    yi = _to_fine_major(y_ref[1].astype(BF16))
    out = [_stacked_cdot(s, yr[b], yi[b], half) for b in range(FFT_BT)]
    c_ref[0] = pltpu.einshape("bac->abc", jnp.stack([o[0] for o in out]))
    c_ref[1] = pltpu.einshape("bac->abc", jnp.stack([o[1] for o in out]))
    o_ref[...] = g_ref[...] * (c_ref[...] + b_ref[...][None, None] * x_ref[...])


def _fft_c4(y, s3, x, gate, bias):
    _, n1, n2, ch = y.shape
    half = n1 // 2
    io = pl.BlockSpec((2, half, FFT_BT, ch), lambda j: (0, 0, j, 0))
    return pl.pallas_call(
        functools.partial(_fft_c4_kernel, half=half),
        grid=(n2 // FFT_BT,),
        in_specs=[
            pl.BlockSpec((2, n1, FFT_BT, ch), lambda j: (0, 0, j, 0)),
            pl.BlockSpec((n1, n1), lambda j: (0, 0)),
            io, io,
            pl.BlockSpec((1, ch), lambda j: (0, 0)),
        ],
        out_specs=io,
        out_shape=jax.ShapeDtypeStruct((2, half, n2, ch), F32),
        scratch_shapes=[pltpu.VMEM((2, half, FFT_BT, ch), F32)],
        compiler_params=_cparams(("parallel",)),
        name="fft_stage_c",
    )(y, s3, x, gate, bias)


def _hyena_mixer(v, gates, gfilt, bias):
    bsz, n, ch = v.shape
    assert bsz == 2, "the complex packing of the long convolution pairs exactly two batch rows"
    n_fft = 2 * n
    n1, n2 = _fft_split(n_fft)
    s1, s3, ft, it = _fft_tables(n_fft)
    if n2 == 1:
        y = v
        for o in range(HY_ORDER):
            gspec = _fft_a(gfilt[o][None], s1)
            y = _fft_c(_fft_a(y, s1, gspec), s3, y, gates[o], bias[o][None])
        return y
    half = n1 // 2
    y = v.reshape(2, half, n2, ch)
    for o in range(HY_ORDER):
        gspec = _fft_b(_fft_a4(gfilt[o].reshape(1, n1, n2, ch), s1), ft)
        spec = _fft_b(_fft_a4(y, s1), ft, it, gspec)
        y = _fft_c4(spec, s3, y, gates[o].reshape(2, half, n2, ch), bias[o][None])
    return y.reshape(2, n, ch)


def _gelu_tanh(x):
    return 0.5 * x * (1.0 + jnp.tanh(math.sqrt(2.0 / math.pi) * (x + 0.044715 * (x * x * x))))


def _merge_kernel(x_ref, g0_ref, g1_ref, g2_ref, a_ref, yf_ref, yb_ref, u_ref, e_ref, d_ref,
                  wglu_ref, wm_ref, ws_ref, wh_ref, wo_ref, gt_ref, o_ref):
    att = _dot(jnp.concatenate([a_ref[0, h] for h in range(MLA_HEADS)], axis=-1), wm_ref[...])
    y = yf_ref[0] + yb_ref[0] + d_ref[...] * u_ref[0].astype(F32)
    gy = _gelu_tanh(y)
    s5 = gy * jax.nn.sigmoid(_dot(gy.astype(BF16), wglu_ref[...]))
    s5 = _dot(s5.astype(BF16), ws_ref[...])
    hy = _dot(e_ref[0].astype(BF16), wh_ref[...])
    merged = jax.nn.sigmoid(g0_ref[0].astype(F32)) * att
    merged = merged + jax.nn.sigmoid(g1_ref[0].astype(F32)) * s5
    merged = merged + jax.nn.sigmoid(g2_ref[0].astype(F32)) * hy
    o_ref[0] = x_ref[0] + gt_ref[0] * _dot(merged.astype(BF16), wo_ref[...])


def _merge(x, z, att, y5, e, s5_d, w_glu, w_mla, w_s5, w_hy, w_out, gate):
    bsz, n, d = x.shape
    tm = min(n, 512)
    full = lambda shape: pl.BlockSpec(shape, lambda b, i: (0,) * len(shape))
    zs5 = Z_S5 // S5_WIDTH
    return pl.pallas_call(
        _merge_kernel,
        grid=(bsz, n // tm),
        in_specs=[
            pl.BlockSpec((1, tm, d), lambda b, i: (b, i, 0)),
            pl.BlockSpec((1, tm, d), lambda b, i: (b, i, 0)),
            pl.BlockSpec((1, tm, d), lambda b, i: (b, i, 1)),
            pl.BlockSpec((1, tm, d), lambda b, i: (b, i, 2)),
            pl.BlockSpec((1, MLA_HEADS, tm, MLA_V), lambda b, i: (b, 0, i, 0)),
            pl.BlockSpec((1, tm, S5_WIDTH), lambda b, i: (b, i, 0)),
            pl.BlockSpec((1, tm, S5_WIDTH), lambda b, i: (b, i, 0)),
            pl.BlockSpec((1, tm, S5_WIDTH), lambda b, i: (b, i, zs5)),
            pl.BlockSpec((1, tm, HY_WIDTH), lambda b, i: (b, i, 0)),
            full((1, S5_WIDTH)),
            full((S5_WIDTH, S5_WIDTH)),
            full((MLA_HEADS * MLA_V, d)),
            full((S5_WIDTH, d)),
            full((HY_WIDTH, d)),
            full((d, d)),
            pl.BlockSpec((1, 1, d), lambda b, i: (b, 0, 0)),
        ],
        out_specs=pl.BlockSpec((1, tm, d), lambda b, i: (b, i, 0)),
        out_shape=jax.ShapeDtypeStruct((bsz, n, d), F32),
        compiler_params=_cparams(("parallel", "parallel")),
        name="merge",
    )(x, z, z, z, att, y5[0], y5[1], z, e, s5_d, w_glu, w_mla, w_s5, w_hy, w_out, gate)


def _ffn_kernel(x_ref, g_ref, sh_ref, sc_ref, gt_ref, wg_ref, wu_ref, wd_ref, fg_ref, o_ref, h_ref, acc_ref,
                *, nk, final):
    k = pl.program_id(2)

    @pl.when(k == 0)
    def _():
        h_ref[...] = _norm_mod(x_ref[0], g_ref[...], sh_ref[0], sc_ref[0]).astype(BF16)
        acc_ref[...] = jnp.zeros_like(acc_ref)

    h = h_ref[...]
    act = jax.nn.silu(_dot(h, wg_ref[...])) * _dot(h, wu_ref[...])
    acc_ref[...] += _dot(act.astype(BF16), wd_ref[...])

    @pl.when(k == nk - 1)
    def _():
        r = x_ref[0] + gt_ref[0] * acc_ref[...]
        o_ref[0] = _rms(r, fg_ref[...]) if final else r


def _ffn(x, g, shift, scale, gate, w_g, w_u, w_d, final_g, final):
    bsz, n, d = x.shape
    dff = w_g.shape[1]
    tm = min(n, 1024)
    tf = 256
    nk = dff // tf
    vec = pl.BlockSpec((1, 1, d), lambda b, i, k: (b, 0, 0))
    row = pl.BlockSpec((1, d), lambda b, i, k: (0, 0))
    return pl.pallas_call(
        functools.partial(_ffn_kernel, nk=nk, final=final),
        grid=(bsz, n // tm, nk),
        in_specs=[
            pl.BlockSpec((1, tm, d), lambda b, i, k: (b, i, 0)),
            row, vec, vec, vec,
            pl.BlockSpec((d, tf), lambda b, i, k: (0, k)),
            pl.BlockSpec((d, tf), lambda b, i, k: (0, k)),
            pl.BlockSpec((tf, d), lambda b, i, k: (k, 0)),
            row,
        ],
        out_specs=pl.BlockSpec((1, tm, d), lambda b, i, k: (b, i, 0)),
        out_shape=jax.ShapeDtypeStruct((bsz, n, d), F32),
        scratch_shapes=[pltpu.VMEM((tm, d), BF16), pltpu.VMEM((tm, d), F32)],
        compiler_params=_cparams(("parallel", "parallel", "arbitrary")),
        name="ffn",
    )(x, g, shift, scale, gate, w_g, w_u, w_d, final_g)


def _rope_tables(n_tok):
    rows = n_tok // GRID_W
    row = jnp.broadcast_to(jnp.arange(rows, dtype=F32)[:, None], (rows, GRID_W)).reshape(-1)
    col = jnp.broadcast_to(jnp.arange(GRID_W, dtype=F32)[None, :], (rows, GRID_W)).reshape(-1)
    n_freq = MLA_ROPE // 4
    inv = ROPE_BASE ** (-jnp.arange(n_freq, dtype=F32) / n_freq)
    ang = jnp.concatenate([row[:, None] * inv, col[:, None] * inv], axis=-1)
    cos, sin = jnp.cos(ang), jnp.sin(ang)
    pad = jnp.zeros((n_tok, HEAD_W - MLA_ROPE), F32)
    return jnp.concatenate([cos, cos, pad], axis=-1), jnp.concatenate([-sin, sin, pad], axis=-1)


def _identity_rope_tables(n_tok):
    one = jnp.ones((n_tok, MLA_ROPE), F32)
    pad = jnp.zeros((n_tok, HEAD_W - MLA_ROPE), F32)
    return jnp.concatenate([one, pad], axis=-1), jnp.zeros((n_tok, HEAD_W), F32)


def _layout_w_in(w):
    kr = w[:, OFF_KR:OFF_S5]
    x1, x2 = kr[:, 0::2], kr[:, 1::2]
    pad = jnp.zeros((w.shape[0], HEAD_W - 3 * (MLA_ROPE // 2)), w.dtype)
    return jnp.concatenate([w[:, OFF_GATE:], w[:, OFF_HY:OFF_GATE], w[:, OFF_S5:OFF_HY],
                            w[:, OFF_CQ:OFF_CKV], w[:, OFF_CKV:OFF_KR], x1, x2, x1, pad], axis=1).astype(BF16)


def _layout_w_uq(w):
    w = w.reshape(MLA_Q_LORA, MLA_HEADS, MLA_NOPE + MLA_ROPE)
    nope, rope = w[..., :MLA_NOPE], w[..., MLA_NOPE:]
    x1, x2 = rope[..., 0::2], rope[..., 1::2]
    z32 = jnp.zeros_like(rope)
    wq = jnp.concatenate([x1, x2, z32, nope], axis=-1)
    wqs = jnp.concatenate([x2, x1, z32, jnp.zeros_like(nope)], axis=-1)
    shape = (MLA_Q_LORA, MLA_HEADS * HEAD_W)
    return wq.reshape(shape).astype(BF16), wqs.reshape(shape).astype(BF16)


def _layout_w_ukv(w):
    w = w.reshape(MLA_KV_LORA, MLA_HEADS, MLA_NOPE + MLA_V)
    nope, val = w[..., :MLA_NOPE], w[..., MLA_NOPE:]
    wk = jnp.concatenate([jnp.zeros_like(nope), nope], axis=-1).reshape(MLA_KV_LORA, MLA_HEADS * HEAD_W)
    wvt = jnp.concatenate([val, jnp.zeros_like(val[..., :VT_ROWS - MLA_V])], axis=-1).reshape(
        MLA_KV_LORA, MLA_HEADS * VT_ROWS).T
    return wk.astype(BF16), wvt.astype(BF16)


def kernel(x, c, ctx, c_ctx, ada_w, ada_b, norm_mix, w_in, mla_q_norm, mla_w_uq, mla_kv_norm, mla_w_ukv,
           s5_lam_re, s5_lam_im, s5_log_dt, s5_b_re, s5_b_im, s5_c_re, s5_c_im, s5_d, s5_w_glu,
           hy_conv_w, hy_conv_b, hy_f_w1, hy_f_b1, hy_f_w2, hy_f_b2, hy_f_w3, hy_f_freq, hy_bias,
           w_branch_mla, w_branch_s5, w_branch_hy, w_out, norm_ffn, ffn_w_gu, ffn_w_down, final_norm):
    bsz, n_tok, d = x.shape
    n_ctx = ctx.shape[1]
    depth = ada_w.shape[0]
    rope_x = _rope_tables(n_tok)
    rope_c = _identity_rope_tables(n_ctx)
    h_zero = jnp.zeros((bsz, 2, 4, S5_HSTATE), F32)

    assert bsz < ADA_ROWS, "the conditioning rows of the batch and the context must fit one row tile"
    cond = jnp.zeros((ADA_ROWS, d), F32).at[:bsz].set(c).at[bsz].set(c_ctx)
    for i in range(depth):
        last = i == depth - 1
        mods = _ada_modulation(cond, ada_w[i], ada_b[i][None])
        mx = mods[:bsz].reshape(bsz, N_MOD, 1, d)
        mc = jnp.broadcast_to(mods[bsz].reshape(1, N_MOD, 1, d), (bsz, N_MOD, 1, d))
        w_z = _layout_w_in(w_in[i])
        wq, wqs = _layout_w_uq(mla_w_uq[i])
        wk, wvt = _layout_w_ukv(mla_w_ukv[i])
        qn, kvn = mla_q_norm[i][None], mla_kv_norm[i][None]
        s5_tabs = _s5_tables(s5_lam_re[i], s5_lam_im[i], s5_log_dt[i], s5_b_re[i], s5_b_im[i],
                             s5_c_re[i], s5_c_im[i])
        fparams = (hy_f_w1[i], hy_f_b1[i], hy_f_w2[i], hy_f_b2[i], hy_f_w3[i], hy_f_freq[i])
        merge_w = (s5_d[i][None], s5_w_glu[i].astype(BF16),
                   w_branch_mla[i].astype(BF16),
                   w_branch_s5[i].astype(BF16), w_branch_hy[i].astype(BF16), w_out[i].astype(BF16))
        w_g = ffn_w_gu[i][:, :D_FF].astype(BF16)
        w_u = ffn_w_gu[i][:, D_FF:].astype(BF16)
        w_d = ffn_w_down[i].astype(BF16)
        g_mix, g_ffn = norm_mix[i][None], norm_ffn[i][None]

        zx = _proj_in(x, g_mix, mx[:, 0], mx[:, 1], w_z)
        zc = _proj_in(ctx, g_mix, mc[:, 0], mc[:, 1], w_z)

        n_keys = n_tok + n_ctx
        q_x, k_all, vt_all = _mla_prep(zx, qn, kvn, wq, wqs, wk, wvt, *rope_x, n_keys, 0)
        q_c, k_all, vt_all = _mla_prep(zc, qn, kvn, wq, wqs, wk, wvt, *rope_c, n_keys, n_tok, (k_all, vt_all))
        a_x = _attention(q_x, k_all, vt_all, n_keys)

        *y5_c, finals = _s5_scan(zc, h_zero, s5_tabs)
        *y5_x, _ = _s5_scan(zx, finals, s5_tabs)

        v_x, g1_x, g2_x = _short_conv(zx, hy_conv_w[i], hy_conv_b[i][None])
        e_x = _hyena_mixer(v_x, (g1_x, g2_x), _hyena_filters(n_tok, *fparams), hy_bias[i])

        x = _merge(x, zx, a_x, y5_x, e_x, *merge_w, mx[:, 2])
        x = _ffn(x, g_ffn, mx[:, 3], mx[:, 4], mx[:, 5], w_g, w_u, w_d, final_norm[None], last)

        if not last:
            a_c = _attention(q_c, k_all, vt_all, n_ctx, n_tok // n_ctx)
            v_c, g1_c, g2_c = _short_conv(zc, hy_conv_w[i], hy_conv_b[i][None])
            e_c = _hyena_mixer(v_c, (g1_c, g2_c), _hyena_filters(n_ctx, *fparams), hy_bias[i])
            ctx = _merge(ctx, zc, a_c, y5_c, e_c, *merge_w, mc[:, 2])
            ctx = _ffn(ctx, g_ffn, mc[:, 3], mc[:, 4], mc[:, 5], w_g, w_u, w_d, final_norm[None], False)
    return x
```

```python
import functools
import math

import jax
import jax.numpy as jnp
from jax import lax
from jax.experimental import pallas as pl
from jax.experimental.pallas import tpu as pltpu

F32 = jnp.float32
BF16 = jnp.bfloat16
HIGHEST = lax.Precision.HIGHEST

D_MODEL = 1024
GRID_W = 64
NORM_EPS = 1e-6
N_MOD = 6

MLA_HEADS = 8
MLA_NOPE = 64
MLA_ROPE = 32
MLA_V = 64
MLA_Q_LORA = 256
MLA_KV_LORA = 128
ROPE_BASE = 10000.0
HEAD_W = 128
MLA_PREP_ROWS = 1024
VT_ROWS = 80
QK_SCALE_LOG2 = (MLA_NOPE + MLA_ROPE) ** -0.5 * math.log2(math.e)
QK_HALF_SCALE = QK_SCALE_LOG2 ** 0.5
QK_DTYPE = jnp.float8_e4m3fn
ATTN_SUB = 128
ATTN_LAG = 2
ATTN_UNROLL = 12

S5_WIDTH = 512
S5_GROUP = 16
S5_GROUPS = 32
S5_STATE = 64
S5_HALF = 256
S5_HSTATE = 1024
S5_CHUNK = 128
S5_CHUNKS_PER_STEP = 4

HY_WIDTH = 512
HY_ORDER = 2
HY_BANDS = 16
HY_POS_DIM = 1 + 2 * HY_BANDS
HY_POS_PAD = 64
HY_FILTER_HIDDEN = 64
HY_FILTER_OUT = HY_ORDER * 2 * HY_WIDTH
HY_DECAY_TARGET = 1e-2
HY_DECAY_SHORT = 0.3
HY_DECAY_LONG = 1.5
HY_DECAY_SHIFT = 0.05

D_FF = 2816

OFF_CQ = 0
OFF_CKV = OFF_CQ + MLA_Q_LORA
OFF_KR = OFF_CKV + MLA_KV_LORA
OFF_S5 = OFF_KR + MLA_ROPE
OFF_HY = OFF_S5 + S5_WIDTH
OFF_GATE = OFF_HY + 3 * HY_WIDTH

Z_GATE = 0
Z_HY = 3072
Z_S5 = 4608
Z_MLA = 5120
Z_WIDTH = 5632

VMEM_LIMIT_V7X = 52 * 1024 * 1024


def _cparams(sem, flags=None):
    return pltpu.CompilerParams(dimension_semantics=sem, vmem_limit_bytes=VMEM_LIMIT_V7X, flags=flags)


def _dot(a, b):
    return jnp.dot(a, b, preferred_element_type=F32)


def _rms(x, g):
    return x * lax.rsqrt(jnp.mean(x * x, axis=-1, keepdims=True) + NORM_EPS) * g


def _norm_mod(x, g, shift, scale):
    return _rms(x, g) * (1.0 + scale) + shift


ADA_ROWS = 8


def _ada_kernel(c_ref, w_ref, b_ref, o_ref):
    c = c_ref[...]
    h = c * jax.nn.sigmoid(c)
    o_ref[...] = jnp.dot(h, w_ref[...], precision=HIGHEST, preferred_element_type=F32) + b_ref[...]


def _ada_modulation(cond, w, b):
    rows, d = cond.shape
    nout = w.shape[1]
    tn = nout // 4
    return pl.pallas_call(
        _ada_kernel,
        grid=(nout // tn,),
        in_specs=[
            pl.BlockSpec((rows, d), lambda j: (0, 0)),
            pl.BlockSpec((d, tn), lambda j: (0, j)),
            pl.BlockSpec((1, tn), lambda j: (0, j)),
        ],
        out_specs=pl.BlockSpec((rows, tn), lambda j: (0, j)),
        out_shape=jax.ShapeDtypeStruct((rows, nout), F32),
        compiler_params=_cparams(("parallel",)),
        name="ada_modulation",
    )(cond, w, b)


def _proj_kernel(x_ref, g_ref, sh_ref, sc_ref, w_ref, o_ref, h_ref):
    @pl.when(pl.program_id(2) == 0)
    def _():
        h_ref[...] = _norm_mod(x_ref[0], g_ref[...], sh_ref[0], sc_ref[0]).astype(BF16)

    o_ref[0] = _dot(h_ref[...], w_ref[...]).astype(BF16)


def _proj_in(x, g, shift, scale, w):
    bsz, n, d = x.shape
    nz = w.shape[1]
    tm = min(n, 2048)
    tn = 512
    return pl.pallas_call(
        _proj_kernel,
        grid=(bsz, n // tm, nz // tn),
        in_specs=[
            pl.BlockSpec((1, tm, d), lambda b, i, j: (b, i, 0)),
            pl.BlockSpec((1, d), lambda b, i, j: (0, 0)),
            pl.BlockSpec((1, 1, d), lambda b, i, j: (b, 0, 0)),
            pl.BlockSpec((1, 1, d), lambda b, i, j: (b, 0, 0)),
            pl.BlockSpec((d, tn), lambda b, i, j: (0, j)),
        ],
        out_specs=pl.BlockSpec((1, tm, tn), lambda b, i, j: (b, i, j)),
        out_shape=jax.ShapeDtypeStruct((bsz, n, nz), BF16),
        scratch_shapes=[pltpu.VMEM((tm, d), BF16)],
        compiler_params=_cparams(("parallel", "parallel", "arbitrary")),
        name="proj_in",
    )(x, g, shift, scale, w)


def _mla_prep_kernel(z_ref, qn_ref, kvn_ref, wq_ref, wqs_ref, wk_ref, wvt_ref, c_ref, s_ref, *rest):
    q_ref, k_ref, vt_ref = rest[-3:]
    z = z_ref[0].astype(F32)
    hq = _rms(z[:, :MLA_Q_LORA], qn_ref[...]).astype(BF16)
    hkv = _rms(z[:, MLA_Q_LORA:MLA_Q_LORA + MLA_KV_LORA], kvn_ref[...]).astype(BF16)
    krb = z[:, MLA_Q_LORA + MLA_KV_LORA:]
    qa = _dot(hq, wq_ref[...])
    qb = _dot(hq, wqs_ref[...])
    kn = _dot(hkv, wk_ref[...])
    vt = lax.dot_general(wvt_ref[...], hkv, (((1,), (1,)), ((), ())), preferred_element_type=F32)
    ck = c_ref[...]
    sn = s_ref[...]
    lane = lax.broadcasted_iota(jnp.int32, ck.shape, 1)
    cq = jnp.where(lane < MLA_ROPE, ck, 1.0)
    kr = krb * ck + pltpu.roll(krb, HEAD_W - MLA_ROPE // 2, 1) * sn
    row = lax.broadcasted_iota(jnp.int32, (VT_ROWS, z.shape[0]), 0)
    for h in range(MLA_HEADS):
        sl = slice(h * HEAD_W, (h + 1) * HEAD_W)
        q_ref[0, h] = ((qa[:, sl] * cq + qb[:, sl] * sn) * QK_HALF_SCALE).astype(QK_DTYPE)
        k_ref[0, h] = ((kr + kn[:, sl]) * QK_HALF_SCALE).astype(QK_DTYPE)
        vt_ref[0, h] = jnp.where(row == MLA_V, 1.0, vt[h * VT_ROWS:(h + 1) * VT_ROWS, :]).astype(BF16)


def _mla_prep(z, qn, kvn, wq, wqs, wk, wvt, ctab, stab, n_keys, key_off, kv_bufs=None):
    bsz, n, _ = z.shape
    tm = min(n, MLA_PREP_ROWS)
    assert key_off % tm == 0, "the key offset must be a whole number of row tiles"
    ob = key_off // tm
    hw = MLA_HEADS * HEAD_W
    zb = Z_MLA // 512
    full = lambda shape: pl.BlockSpec(shape, lambda b, i: (0,) * len(shape))
    in_specs = [
        pl.BlockSpec((1, tm, 512), lambda b, i: (b, i, zb)),
        full((1, MLA_Q_LORA)),
        full((1, MLA_KV_LORA)),
        full((MLA_Q_LORA, hw)),
        full((MLA_Q_LORA, hw)),
        full((MLA_KV_LORA, hw)),
        full((MLA_HEADS * VT_ROWS, MLA_KV_LORA)),
        pl.BlockSpec((tm, HEAD_W), lambda b, i: (i, 0)),
        pl.BlockSpec((tm, HEAD_W), lambda b, i: (i, 0)),
    ]
    args = [z, qn, kvn, wq, wqs, wk, wvt, ctab, stab]
    aliases = {}
    if kv_bufs is not None:
        aliases = {len(args): 1, len(args) + 1: 2}
        in_specs += [pl.BlockSpec(memory_space=pl.ANY)] * 2
        args += list(kv_bufs)
    return pl.pallas_call(
        _mla_prep_kernel,
        grid=(bsz, n // tm),
        in_specs=in_specs,
        out_specs=[
            pl.BlockSpec((1, MLA_HEADS, tm, HEAD_W), lambda b, i: (b, 0, i, 0)),
            pl.BlockSpec((1, MLA_HEADS, tm, HEAD_W), lambda b, i: (b, 0, i + ob, 0)),
            pl.BlockSpec((1, MLA_HEADS, VT_ROWS, tm), lambda b, i: (b, 0, 0, i + ob)),
        ],
        out_shape=[
            jax.ShapeDtypeStruct((bsz, MLA_HEADS, n, HEAD_W), QK_DTYPE),
            jax.ShapeDtypeStruct((bsz, MLA_HEADS, n_keys, HEAD_W), QK_DTYPE),
            jax.ShapeDtypeStruct((bsz, MLA_HEADS, VT_ROWS, n_keys), BF16),
        ],
        input_output_aliases=aliases,
        compiler_params=_cparams(("parallel", "parallel")),
        name="mla_prep",
    )(*args)


def _attn_kernel(q_ref, k_ref, vt_ref, o_ref, s0_ref, s1_ref, s2_ref, p0_ref, p1_ref, p2_ref, acc_ref, *, tk, nkc):
    q = q_ref[0, 0]
    tq = q.shape[0]
    s_refs = (s0_ref, s1_ref, s2_ref)
    p_refs = (p0_ref, p1_ref, p2_ref)

    ts = min(tk, ATTN_SUB)
    subs = [slice(r, r + ts) for r in range(0, tk, ts)]

    def scores_sub(j, s_ref, sub, cmax):
        off = pl.multiple_of(j * tk + sub.start, ts)
        s = lax.dot_general(k_ref[0, 0, pl.ds(off, ts), :], q, (((1,), (1,)), ((), ())),
                            preferred_element_type=F32)
        s_ref[sub, :] = s
        cm = jnp.max(s, axis=0, keepdims=True)
        return cm if cmax is None else jnp.maximum(cmax, cm)

    def softmax_sub(slot, sub, mn):
        p_refs[slot][sub, :] = jnp.exp2(s_refs[slot][sub, :] - mn).astype(BF16)

    def step(i, slot, carry, do_softmax, do_scores):
        cm, m, alpha = carry
        mn = jnp.maximum(m, cm)
        off = pl.multiple_of(i * tk, tk)
        acc_ref[...] = alpha * acc_ref[...] + _dot(vt_ref[0, 0, :, pl.ds(off, tk)], p_refs[slot][...])
        cm_new, cms = None, []
        for r, sub in enumerate(subs):
            mn_sub = mn
            if do_scores:
                cm_new = scores_sub(i + 2, s_refs[(slot + 2) % 3], sub, cm_new)
                cms.append(cm_new)
                if r >= ATTN_LAG:
                    mn_sub = jnp.maximum(mn, cms[r - ATTN_LAG] - 1e30)
            if do_softmax:
                softmax_sub((slot + 1) % 3, sub, mn_sub)
        if do_softmax:
            alpha, m = jnp.exp2(m - mn), mn
        return (cm_new if do_scores else cm), m, alpha

    m = jnp.full((1, tq), -1e30, F32)
    acc_ref[...] = jnp.zeros_like(acc_ref)
    cm, cm_next = None, None
    for sub in subs:
        cm = scores_sub(0, s0_ref, sub, cm)
        if nkc > 1:
            cm_next = scores_sub(1, s1_ref, sub, cm_next)
    mn = jnp.maximum(m, cm)
    for sub in subs:
        softmax_sub(0, sub, mn)
    carry = (cm_next if nkc > 1 else cm, mn, jnp.exp2(m - mn))

    def body(t, carry):
        for r in range(ATTN_UNROLL):
            carry = step(ATTN_UNROLL * t + r, r % 3, carry, True, True)
        return carry

    nloop = max(nkc - 2, 0) // ATTN_UNROLL
    carry = lax.fori_loop(0, nloop, body, carry)
    for i in range(ATTN_UNROLL * nloop, nkc):
        carry = step(i, i % 3, carry, i + 1 < nkc, i + 2 < nkc)
    acc = acc_ref[...]
    out = acc * (1.0 / acc[MLA_V:MLA_V + 1, :])
    out = jnp.concatenate([out, jnp.zeros((HEAD_W - VT_ROWS, tq), F32)], axis=0).T
    o_ref[0, 0] = out[:, :MLA_V].astype(BF16)


def _kv_chunk(nk):
    for tk in (640, 512, 256, 128):
        if nk % tk == 0:
            return tk
    raise ValueError(f"unsupported key count {nk}")


def _attention(q, k, vt, nk, key_block=0):
    bsz, nh, nq, _ = q.shape
    tq = min(nq, 1024)
    tk = _kv_chunk(nk)
    return pl.pallas_call(
        functools.partial(_attn_kernel, tk=tk, nkc=nk // tk),
        grid=(bsz, nh, nq // tq),
        in_specs=[
            pl.BlockSpec((1, 1, tq, HEAD_W), lambda b, h, i: (b, h, i, 0)),
            pl.BlockSpec((1, 1, nk, HEAD_W), lambda b, h, i: (b, h, key_block, 0)),
            pl.BlockSpec((1, 1, VT_ROWS, nk), lambda b, h, i: (b, h, 0, key_block)),
        ],
        out_specs=pl.BlockSpec((1, 1, tq, MLA_V), lambda b, h, i: (b, h, i, 0)),
        out_shape=jax.ShapeDtypeStruct((bsz, nh, nq, MLA_V), BF16),
        scratch_shapes=[pltpu.VMEM((tk, tq), F32)] * 3 + [pltpu.VMEM((tk, tq), BF16)] * 3
        + [pltpu.VMEM((VT_ROWS, tq), F32)],
        compiler_params=_cparams(("parallel", "parallel", "arbitrary")),
        name="attention",
    )(q, k, vt)


def _cmul(ar, ai, br, bi):
    return ar * br - ai * bi, ar * bi + ai * br


def _s5_kernel(u_ref, h0_ref, bm_ref, cm_ref, wn_ref, wp_ref, l1_ref, tri_ref, y_ref, hf_ref,
               carry_ref, h0cat_ref, h1cat_ref, *, tc, nr, ns, reverse):
    i = pl.program_id(1)

    @pl.when(i == 0)
    def _():
        carry_ref[...] = h0_ref[0]

    u = u_ref[0]
    tri = tri_ref[...]
    order = range(nr - 1, -1, -1) if reverse else range(nr)
    edge = 0 if reverse else tc - 1
    hcat_refs = (h0cat_ref, h1cat_ref)
    bus = [_dot(u[:, hf * S5_HALF:(hf + 1) * S5_HALF], bm_ref[hf]) for hf in range(2)]
    carries = [(carry_ref[2 * hf:2 * hf + 1, :], carry_ref[2 * hf + 1:2 * hf + 2, :]) for hf in range(2)]
    for c in order:
        rows = slice(c * tc, (c + 1) * tc)
        for hf in range(2):
            bu = bus[hf]
            xr, xi = _cmul(wn_ref[hf, 0], wn_ref[hf, 1], bu[rows, :S5_HSTATE], bu[rows, S5_HSTATE:])
            s = _dot(tri, jnp.concatenate([xr, xi], axis=1).astype(BF16))
            cr, ci = _cmul(l1_ref[hf, 0], l1_ref[hf, 1], *carries[hf])
            hr, hi = _cmul(wp_ref[hf, 0], wp_ref[hf, 1], s[:, :S5_HSTATE] + cr, s[:, S5_HSTATE:] + ci)
            hcat_refs[hf][rows, :S5_HSTATE] = hr.astype(BF16)
            hcat_refs[hf][rows, S5_HSTATE:] = hi.astype(BF16)
            carries[hf] = (hr[edge:edge + 1], hi[edge:edge + 1])
    for hf in range(2):
        carry_ref[2 * hf:2 * hf + 1, :] = carries[hf][0]
        carry_ref[2 * hf + 1:2 * hf + 2, :] = carries[hf][1]
    y_ref[0] = jnp.concatenate([_dot(hcat_refs[hf][...], cm_ref[hf]) for hf in range(2)], axis=1)

    @pl.when(i == ns - 1)
    def _():
        hf_ref[0] = carry_ref[...]


def _s5_scan_dir(z, h0, tabs, reverse):
    bm, cm, wn, wp, l1, tri = tabs
    bsz, n, _ = z.shape
    tc = S5_CHUNK
    nr = min(S5_CHUNKS_PER_STEP, n // tc)
    rows = nr * tc
    ns = n // rows
    zb = Z_S5 // S5_WIDTH
    blk = (lambda i: ns - 1 - i) if reverse else (lambda i: i)
    full = lambda shape: pl.BlockSpec(shape, lambda b, i: (0,) * len(shape))
    return pl.pallas_call(
        functools.partial(_s5_kernel, tc=tc, nr=nr, ns=ns, reverse=reverse),
        grid=(bsz, ns),
        in_specs=[
            pl.BlockSpec((1, rows, S5_WIDTH), lambda b, i: (b, blk(i), zb)),
            pl.BlockSpec((1, 4, S5_HSTATE), lambda b, i: (b, 0, 0)),
            full((2, S5_HALF, 2 * S5_HSTATE)),
            full((2, 2 * S5_HSTATE, S5_HALF)),
            full((2, 2, tc, S5_HSTATE)),
            full((2, 2, tc, S5_HSTATE)),
            full((2, 2, 1, S5_HSTATE)),
            full((tc, tc)),
        ],
        out_specs=[
            pl.BlockSpec((1, rows, S5_WIDTH), lambda b, i: (b, blk(i), 0)),
            pl.BlockSpec((1, 4, S5_HSTATE), lambda b, i: (b, 0, 0)),
        ],
        out_shape=[
            jax.ShapeDtypeStruct((bsz, n, S5_WIDTH), F32),
            jax.ShapeDtypeStruct((bsz, 4, S5_HSTATE), F32),
        ],
        scratch_shapes=[pltpu.VMEM((4, S5_HSTATE), F32),
                        pltpu.VMEM((rows, 2 * S5_HSTATE), BF16), pltpu.VMEM((rows, 2 * S5_HSTATE), BF16)],
        compiler_params=_cparams(("parallel", "arbitrary")),
        name="s5_scan",
    )(z, h0, bm, cm, wn, wp, l1, tri)


def _s5_scan(z, h0, tabs):
    outs = [_s5_scan_dir(z, h0[:, d], [t[d] for t in tabs], reverse=bool(d)) for d in range(2)]
    return outs[0][0], outs[1][0], jnp.stack([outs[0][1], outs[1][1]], axis=1)


def _s5_tables(lam_re, lam_im, log_dt, b_re, b_im, c_re, c_im):
    tc = S5_CHUNK
    dt = jnp.exp(log_dt)[..., None]
    zr, zi = lam_re * dt, lam_im * dt
    mag = jnp.exp(zr)
    lbr, lbi = mag * jnp.cos(zi), mag * jnp.sin(zi)
    den = lam_re * lam_re + lam_im * lam_im
    nr, ni = lbr - 1.0, lbi
    cfr = (nr * lam_re + ni * lam_im) / den
    cfi = (ni * lam_re - nr * lam_im) / den
    bbr = cfr[..., None] * b_re - cfi[..., None] * b_im
    bbi = cfr[..., None] * b_im + cfi[..., None] * b_re
    eye = jnp.eye(S5_GROUP, dtype=F32)

    def blockdiag_in(b):
        b = b.reshape(2, 2, S5_GROUP, S5_STATE, S5_GROUP)
        return jnp.einsum('gk,dhgpn->dhgnkp', eye, b).reshape(2, 2, S5_HALF, S5_HSTATE)

    def blockdiag_out(c):
        c = c.reshape(2, 2, S5_GROUP, S5_GROUP, S5_STATE)
        return jnp.einsum('gk,dhgnp->dhgpkn', eye, c).reshape(2, 2, S5_HSTATE, S5_HALF)

    bm = jnp.concatenate([blockdiag_in(bbr), blockdiag_in(bbi)], axis=-1).astype(BF16)
    cm = jnp.concatenate([blockdiag_out(c_re), blockdiag_out(-c_im)], axis=-2).astype(BF16)

    def powers(k):
        zr_ = zr.reshape(2, 2, 1, S5_HSTATE)
        zi_ = zi.reshape(2, 2, 1, S5_HSTATE)
        kk = k[:, None, :, None]
        m = jnp.exp(kk * zr_)
        return jnp.stack([m * jnp.cos(kk * zi_), m * jnp.sin(kk * zi_)], axis=2)

    mid = tc // 2
    t = jnp.arange(tc, dtype=F32) - mid
    wn = powers(jnp.stack([-t, -t[::-1]]))
    wp = powers(jnp.stack([t, t[::-1]]))
    l1 = powers(jnp.full((2, 1), mid + 1, F32))
    r = jnp.arange(tc)
    tri = jnp.stack([r[:, None] >= r[None, :], r[:, None] <= r[None, :]]).astype(BF16)
    return bm, cm, wn, wp, l1, tri


HALO = 16


def _sconv_kernel(z_ref, zp_ref, zn_ref, w_ref, b_ref, v_ref, g1_ref, g2_ref, *, nt):
    i = pl.program_id(1)
    z = z_ref[0].astype(F32)
    tm = z.shape[0]
    row = lax.broadcasted_iota(jnp.int32, z.shape, 0)
    prev = jnp.where(i > 0, zp_ref[0, HALO - 1:HALO, :].astype(F32), 0.0)
    nxt = jnp.where(i < nt - 1, zn_ref[0, 0:1, :].astype(F32), 0.0)
    up = jnp.where(row == 0, prev, pltpu.roll(z, 1, 0))
    dn = jnp.where(row == tm - 1, nxt, pltpu.roll(z, tm - 1, 0))
    u = up * w_ref[0:1, :] + z * w_ref[1:2, :] + dn * w_ref[2:3, :] + b_ref[...]
    for k, o_ref in enumerate((v_ref, g1_ref, g2_ref)):
        o_ref[0] = u[:, k * HY_WIDTH:(k + 1) * HY_WIDTH].astype(BF16)


def _short_conv(z, w, b):
    bsz, n, _ = z.shape
    cw = 3 * HY_WIDTH
    tm = min(n, 1024)
    nt = n // tm
    zb = Z_HY // cw
    rh = tm // HALO
    out = pl.BlockSpec((1, tm, HY_WIDTH), lambda b_, i: (b_, i, 0))
    return pl.pallas_call(
        functools.partial(_sconv_kernel, nt=nt),
        grid=(bsz, nt),
        in_specs=[
            pl.BlockSpec((1, tm, cw), lambda b_, i: (b_, i, zb)),
            pl.BlockSpec((1, HALO, cw), lambda b_, i: (b_, jnp.maximum(i * rh - 1, 0), zb)),
            pl.BlockSpec((1, HALO, cw), lambda b_, i: (b_, jnp.minimum((i + 1) * rh, n // HALO - 1), zb)),
            pl.BlockSpec((3, cw), lambda b_, i: (0, 0)),
            pl.BlockSpec((1, cw), lambda b_, i: (0, 0)),
        ],
        out_specs=[out] * 3,
        out_shape=[jax.ShapeDtypeStruct((bsz, n, HY_WIDTH), BF16)] * 3,
        compiler_params=_cparams(("parallel", "parallel")),
        name="short_conv",
    )(z, z, z, w, b)


def _filt_kernel(f_ref, w1_ref, b1_ref, w2_ref, b2_ref, w3h_ref, w3l_ref, fq_ref, dl_ref, o_ref, *, n_tok):
    z = f_ref[...]
    tm = z.shape[0]
    fq = fq_ref[...]
    hid = jnp.sin(fq * (jnp.dot(z, w1_ref[...], precision=HIGHEST, preferred_element_type=F32) + b1_ref[...]))
    hid = jnp.sin(fq * (jnp.dot(hid, w2_ref[...], precision=HIGHEST, preferred_element_type=F32) + b2_ref[...]))
    hid_hi = hid.astype(BF16)
    hid_lo = (hid - hid_hi.astype(F32)).astype(BF16)

    def filters(rows, d):
        wh, wl = w3h_ref[d], w3l_ref[d]
        f = _dot(hid_hi[rows], wh) + (_dot(hid_hi[rows], wl) + _dot(hid_lo[rows], wh))
        return f * (jnp.exp(-z[rows, 0:1] * dl_ref[d]) + HY_DECAY_SHIFT)

    i = pl.program_id(0)
    filt = filters(slice(None), jnp.where(i * tm >= n_tok, 1, 0))
    m = i * tm + lax.broadcasted_iota(jnp.int32, filt.shape, 0)
    filt = jnp.where(m == n_tok, 0.0, filt)
    for o in range(HY_ORDER):
        o_ref[o] = filt[:, o * HY_WIDTH:(o + 1) * HY_WIDTH]

    @pl.when(i == 0)
    def _():
        head = slice(0, HALO)
        lag0 = filters(head, 1)
        lag0 = jnp.where(lax.broadcasted_iota(jnp.int32, lag0.shape, 0) == 0, lag0, 0.0)
        for o in range(HY_ORDER):
            o_ref[o, head, :] += lag0[:, o * HY_WIDTH:(o + 1) * HY_WIDTH]


def _hyena_filter_feats(n_tok):
    m = jnp.arange(2 * n_tok)
    lag = jnp.where(m < n_tok, m, jnp.where(m > n_tok, 2 * n_tok - m, 0))
    t = (lag.astype(F32) / (n_tok - 1))[:, None]
    w = (2.0 * math.pi * lag.astype(F32) / n_tok)[:, None]
    bands = jnp.linspace(1e-4, HY_BANDS - 1, HY_BANDS, dtype=F32)[None, :]
    feats = jnp.concatenate([t, jnp.cos(bands * w), -jnp.sin(bands * w)], axis=-1)
    return jnp.pad(feats, ((0, 0), (0, HY_POS_PAD - HY_POS_DIM)))


def _hyena_filters(n_tok, w1, b1, w2, b2, w3, freq):
    feats = _hyena_filter_feats(n_tok)
    deltas = jnp.abs(jnp.linspace(math.log(HY_DECAY_TARGET) / HY_DECAY_SHORT,
                                  math.log(HY_DECAY_TARGET) / HY_DECAY_LONG, HY_FILTER_OUT, dtype=F32))[None, :]
    w1p = jnp.pad(w1, ((0, HY_POS_PAD - HY_POS_DIM), (0, 0)))

    def by_direction(a):
        a = a.reshape(a.shape[:-1] + (HY_ORDER, 2, HY_WIDTH))
        return jnp.moveaxis(a, -2, 0).reshape((2,) + a.shape[:-3] + (HY_ORDER * HY_WIDTH,))

    w3, deltas = by_direction(w3), by_direction(deltas)
    w3_hi = w3.astype(BF16)
    w3_lo = (w3 - w3_hi.astype(F32)).astype(BF16)
    n2 = 2 * n_tok
    tm = min(n_tok, 1024)
    full = lambda shape: pl.BlockSpec(shape, lambda i: (0,) * len(shape))
    hh = HY_FILTER_HIDDEN
    wd = HY_ORDER * HY_WIDTH
    return pl.pallas_call(
        functools.partial(_filt_kernel, n_tok=n_tok),
        grid=(n2 // tm,),
        in_specs=[
            pl.BlockSpec((tm, HY_POS_PAD), lambda i: (i, 0)),
            full((HY_POS_PAD, hh)), full((1, hh)), full((hh, hh)), full((1, hh)),
            full((2, hh, wd)), full((2, hh, wd)), full((1, hh)), full((2, 1, wd)),
        ],
        out_specs=pl.BlockSpec((HY_ORDER, tm, HY_WIDTH), lambda i: (0, i, 0)),
        out_shape=jax.ShapeDtypeStruct((HY_ORDER, n2, HY_WIDTH), F32),
        compiler_params=_cparams(("parallel",)),
        name="hyena_filter",
    )(feats, w1p, b1[None], w2, b2[None], w3_hi, w3_lo, freq[None], deltas)


def _fft_split(n_fft):
    if n_fft <= 1024:
        return n_fft, 1
    n1 = 1 << (int(math.log2(n_fft)) // 2)
    return n1, n_fft // n1


def _cis(num, den, sign):
    ang = (2.0 * math.pi / den) * num.astype(F32)
    return jnp.cos(ang), sign * jnp.sin(ang)


def _fft_tables(n_fft):
    n1, n2 = _fft_split(n_fft)
    a = jnp.arange(n1)
    f1r, f1i = _cis((a[:, None] * a[None, :]) % n1, n1, -1.0)
    s1 = jnp.concatenate([f1r, f1i], axis=0).astype(BF16)
    half = n1 // 2
    s3 = (jnp.concatenate([f1r[:half], -f1i[:half]], axis=0) / n_fft).astype(BF16)
    if n2 == 1:
        return s1, s3, None, None
    b = jnp.arange(n2)
    f2r, f2i = _cis((b[:, None] * b[None, :]) % n2, n2, -1.0)
    twr, twi = _cis((jnp.arange(n1)[:, None] * b[None, :]) % n_fft, n_fft, -1.0)
    fr, fi = _cmul(f2r[None], f2i[None], twr[:, None, :], twi[:, None, :])
    ft = jnp.concatenate([fr, fi], axis=1).astype(BF16)
    frt, fit = _cmul(f2r[None], f2i[None], twr[:, :, None], twi[:, :, None])
    it = jnp.concatenate([frt, -fit], axis=1).astype(BF16)
    return s1, s3, ft, it


def _stacked_cdot(s, xr, xi, m):
    p = _dot(s, xr.astype(BF16))
    if xi is None:
        return p[:m], p[m:]
    q = _dot(s, xi.astype(BF16))
    return p[:m] - q[m:], q[:m] + p[m:]


def _fft_a_kernel(*refs, parts, n1, has_g):
    if has_g:
        x_ref, s_ref, g_ref, y_ref = refs
    else:
        x_ref, s_ref, y_ref = refs
    yr, yi = _stacked_cdot(s_ref[...], x_ref[0], x_ref[1] if parts == 2 else None, n1)
    if has_g:
        yr, yi = _cmul(yr, yi, g_ref[0], g_ref[1])
    y_ref[0] = yr
    y_ref[1] = yi


def _fft_a(x, s1, g=None):
    parts, a_rows, cols = x.shape
    n1 = s1.shape[0] // 2
    tc = min(cols, 2048)
    in_specs = [
        pl.BlockSpec((parts, a_rows, tc), lambda j: (0, 0, j)),
        pl.BlockSpec((2 * n1, a_rows), lambda j: (0, 0)),
    ]
    args = [x, s1[:, :a_rows]]
    if g is not None:
        in_specs.append(pl.BlockSpec((2, n1, tc), lambda j: (0, 0, j)))
        args.append(g)
    return pl.pallas_call(
        functools.partial(_fft_a_kernel, parts=parts, n1=n1, has_g=g is not None),
        grid=(cols // tc,),
        in_specs=in_specs,
        out_specs=pl.BlockSpec((2, n1, tc), lambda j: (0, 0, j)),
        out_shape=jax.ShapeDtypeStruct((2, n1, cols), F32),
        compiler_params=_cparams(("parallel",)),
        name="fft_stage_a",
    )(*args)


FFT_CB = 4


def _fft_b_kernel(*refs, n2, inverse):
    if inverse:
        y_ref, ft_ref, it_ref, g_ref, o_ref = refs
    else:
        y_ref, ft_ref, o_ref = refs
    for c in range(FFT_CB):
        xr, xi = _stacked_cdot(ft_ref[c], y_ref[0, c], y_ref[1, c], n2)
        if inverse:
            zr, zi = _cmul(xr, xi, g_ref[0, c], g_ref[1, c])
            xr, xi = _stacked_cdot(it_ref[c], zr, zi, n2)
        o_ref[0, c] = xr.astype(o_ref.dtype)
        o_ref[1, c] = xi.astype(o_ref.dtype)


def _fft_b(y, ft, it=None, g=None):
    _, n1, n2, ch = y.shape
    inverse = it is not None
    blk = pl.BlockSpec((2, FFT_CB, n2, ch), lambda c: (0, c, 0, 0))
    mat = pl.BlockSpec((FFT_CB, 2 * n2, n2), lambda c: (c, 0, 0))
    in_specs, args = [blk, mat], [y, ft]
    if inverse:
        in_specs += [mat, blk]
        args += [it, g]
    return pl.pallas_call(
        functools.partial(_fft_b_kernel, n2=n2, inverse=inverse),
        grid=(n1 // FFT_CB,),
        in_specs=in_specs,
        out_specs=blk,
        out_shape=jax.ShapeDtypeStruct(y.shape, BF16 if inverse else F32),
        compiler_params=_cparams(("parallel",)),
        name="fft_stage_b",
    )(*args)


def _fft_c_kernel(y_ref, s_ref, x_ref, g_ref, b_ref, o_ref, *, half):
    cr, ci = _stacked_cdot(s_ref[...], y_ref[0], y_ref[1], half)
    bias = b_ref[...]
    o_ref[0] = g_ref[0].astype(F32) * (cr + bias * x_ref[0].astype(F32))
    o_ref[1] = g_ref[1].astype(F32) * (ci + bias * x_ref[1].astype(F32))


def _fft_c(y, s3, x, gate, bias_cols):
    _, n1, cols = y.shape
    half = n1 // 2
    tc = min(cols, 2048)
    io = pl.BlockSpec((2, half, tc), lambda j: (0, 0, j))
    return pl.pallas_call(
        functools.partial(_fft_c_kernel, half=half),
        grid=(cols // tc,),
        in_specs=[
            pl.BlockSpec((2, n1, tc), lambda j: (0, 0, j)),
            pl.BlockSpec((n1, n1), lambda j: (0, 0)),
            io, io,
            pl.BlockSpec((1, tc), lambda j: (0, j)),
        ],
        out_specs=io,
        out_shape=jax.ShapeDtypeStruct((2, half, cols), F32),
        compiler_params=_cparams(("parallel",)),
        name="fft_stage_c",
    )(y, s3, x, gate, bias_cols)


FFT_BT = 16


def _to_fine_major(x):
    return pltpu.einshape("abc->bac", x)


def _fft_a4_kernel(x_ref, s_ref, y_ref, *, parts, n1):
    s = s_ref[...]
    xr = _to_fine_major(x_ref[0].astype(BF16))
    xi = _to_fine_major(x_ref[1].astype(BF16)) if parts == 2 else None
    out = [_stacked_cdot(s, xr[b], None if xi is None else xi[b], n1) for b in range(FFT_BT)]
    y_ref[0] = pltpu.einshape("bac->abc", jnp.stack([o[0] for o in out])).astype(BF16)
    y_ref[1] = pltpu.einshape("bac->abc", jnp.stack([o[1] for o in out])).astype(BF16)


def _fft_a4(x, s1):
    parts, a_rows, n2, ch = x.shape
    n1 = s1.shape[0] // 2
    return pl.pallas_call(
        functools.partial(_fft_a4_kernel, parts=parts, n1=n1),
        grid=(n2 // FFT_BT,),
        in_specs=[
            pl.BlockSpec((parts, a_rows, FFT_BT, ch), lambda j: (0, 0, j, 0)),
            pl.BlockSpec((2 * n1, a_rows), lambda j: (0, 0)),
        ],
        out_specs=pl.BlockSpec((2, n1, FFT_BT, ch), lambda j: (0, 0, j, 0)),
        out_shape=jax.ShapeDtypeStruct((2, n1, n2, ch), BF16),
        compiler_params=_cparams(("parallel",)),
        name="fft_stage_a",
    )(x, s1[:, :a_rows])


def _fft_c4_kernel(y_ref, s_ref, x_ref, g_ref, b_ref, o_ref, c_ref, *, half):
    s = s_ref[...]
    yr = _to_fine_major(y_ref[0].astype(BF16))
    yi = _to_fine_major(y_ref[1].astype(BF16))
    out = [_stacked_cdot(s, yr[b], yi[b], half) for b in range(FFT_BT)]
    c_ref[0] = pltpu.einshape("bac->abc", jnp.stack([o[0] for o in out]))
    c_ref[1] = pltpu.einshape("bac->abc", jnp.stack([o[1] for o in out]))
    o_ref[...] = g_ref[...].astype(F32) * (c_ref[...] + b_ref[...][None, None] * x_ref[...].astype(F32))


def _fft_c4(y, s3, x, gate, bias):
    _, n1, n2, ch = y.shape
    half = n1 // 2
    io = pl.BlockSpec((2, half, FFT_BT, ch), lambda j: (0, 0, j, 0))
    return pl.pallas_call(
        functools.partial(_fft_c4_kernel, half=half),
        grid=(n2 // FFT_BT,),
        in_specs=[
            pl.BlockSpec((2, n1, FFT_BT, ch), lambda j: (0, 0, j, 0)),
            pl.BlockSpec((n1, n1), lambda j: (0, 0)),
            io, io,
            pl.BlockSpec((1, ch), lambda j: (0, 0)),
        ],
        out_specs=io,
        out_shape=jax.ShapeDtypeStruct((2, half, n2, ch), F32),
        scratch_shapes=[pltpu.VMEM((2, half, FFT_BT, ch), F32)],
        compiler_params=_cparams(("parallel",)),
        name="fft_stage_c",
    )(y, s3, x, gate, bias)


def _hyena_mixer(v, gates, gfilt, bias):
    bsz, n, ch = v.shape
    assert bsz == 2, "the complex packing of the long convolution pairs exactly two batch rows"
    n_fft = 2 * n
    n1, n2 = _fft_split(n_fft)
    s1, s3, ft, it = _fft_tables(n_fft)
    if n2 == 1:
        y = v
        for o in range(HY_ORDER):
            gspec = _fft_a(gfilt[o][None], s1)
            y = _fft_c(_fft_a(y, s1, gspec), s3, y, gates[o], bias[o][None])
        return y
    half = n1 // 2
    y = v.reshape(2, half, n2, ch)
    for o in range(HY_ORDER):
        gspec = _fft_b(_fft_a4(gfilt[o].reshape(1, n1, n2, ch), s1), ft)
        spec = _fft_b(_fft_a4(y, s1), ft, it, gspec)
        y = _fft_c4(spec, s3, y, gates[o].reshape(2, half, n2, ch), bias[o][None])
    return y.reshape(2, n, ch)


def _gelu_tanh(x):
    return 0.5 * x * (1.0 + jnp.tanh(math.sqrt(2.0 / math.pi) * (x + 0.044715 * (x * x * x))))


def _merge_kernel(x_ref, g0_ref, g1_ref, g2_ref, a_ref, yf_ref, yb_ref, u_ref, e_ref, d_ref,
                  wglu_ref, wm_ref, ws_ref, wh_ref, wo_ref, gt_ref, o_ref):
    att = _dot(jnp.concatenate([a_ref[0, h] for h in range(MLA_HEADS)], axis=-1), wm_ref[...])
    y = yf_ref[0] + yb_ref[0] + d_ref[...] * u_ref[0].astype(F32)
    gy = _gelu_tanh(y)
    s5 = gy * jax.nn.sigmoid(_dot(gy.astype(BF16), wglu_ref[...]))
    s5 = _dot(s5.astype(BF16), ws_ref[...])
    hy = _dot(e_ref[0].astype(BF16), wh_ref[...])
    merged = jax.nn.sigmoid(g0_ref[0].astype(F32)) * att
    merged = merged + jax.nn.sigmoid(g1_ref[0].astype(F32)) * s5
    merged = merged + jax.nn.sigmoid(g2_ref[0].astype(F32)) * hy
    o_ref[0] = x_ref[0] + gt_ref[0] * _dot(merged.astype(BF16), wo_ref[...])


def _merge(x, z, att, y5, e, s5_d, w_glu, w_mla, w_s5, w_hy, w_out, gate):
    bsz, n, d = x.shape
    tm = min(n, 512)
    full = lambda shape: pl.BlockSpec(shape, lambda b, i: (0,) * len(shape))
    zs5 = Z_S5 // S5_WIDTH
    return pl.pallas_call(
        _merge_kernel,
        grid=(bsz, n // tm),
        in_specs=[
            pl.BlockSpec((1, tm, d), lambda b, i: (b, i, 0)),
            pl.BlockSpec((1, tm, d), lambda b, i: (b, i, 0)),
            pl.BlockSpec((1, tm, d), lambda b, i: (b, i, 1)),
            pl.BlockSpec((1, tm, d), lambda b, i: (b, i, 2)),
            pl.BlockSpec((1, MLA_HEADS, tm, MLA_V), lambda b, i: (b, 0, i, 0)),
            pl.BlockSpec((1, tm, S5_WIDTH), lambda b, i: (b, i, 0)),
            pl.BlockSpec((1, tm, S5_WIDTH), lambda b, i: (b, i, 0)),
            pl.BlockSpec((1, tm, S5_WIDTH), lambda b, i: (b, i, zs5)),
            pl.BlockSpec((1, tm, HY_WIDTH), lambda b, i: (b, i, 0)),
            full((1, S5_WIDTH)),
            full((S5_WIDTH, S5_WIDTH)),
            full((MLA_HEADS * MLA_V, d)),
            full((S5_WIDTH, d)),
            full((HY_WIDTH, d)),
            full((d, d)),
            pl.BlockSpec((1, 1, d), lambda b, i: (b, 0, 0)),
        ],
        out_specs=pl.BlockSpec((1, tm, d), lambda b, i: (b, i, 0)),
        out_shape=jax.ShapeDtypeStruct((bsz, n, d), F32),
        compiler_params=_cparams(("parallel", "parallel")),
        name="merge",
    )(x, z, z, z, att, y5[0], y5[1], z, e, s5_d, w_glu, w_mla, w_s5, w_hy, w_out, gate)


def _ffn_kernel(x_ref, g_ref, sh_ref, sc_ref, gt_ref, wg_ref, wu_ref, wd_ref, fg_ref, o_ref, h_ref, acc_ref,
                *, nk, final):
    k = pl.program_id(2)

    @pl.when(k == 0)
    def _():
        h_ref[...] = _norm_mod(x_ref[0], g_ref[...], sh_ref[0], sc_ref[0]).astype(BF16)
        acc_ref[...] = jnp.zeros_like(acc_ref)

    h = h_ref[...]
    act = jax.nn.silu(_dot(h, wg_ref[...])) * _dot(h, wu_ref[...])
    acc_ref[...] += _dot(act.astype(BF16), wd_ref[...])

    @pl.when(k == nk - 1)
    def _():
        r = x_ref[0] + gt_ref[0] * acc_ref[...]
        o_ref[0] = _rms(r, fg_ref[...]) if final else r


def _ffn(x, g, shift, scale, gate, w_g, w_u, w_d, final_g, final):
    bsz, n, d = x.shape
    dff = w_g.shape[1]
    tm = min(n, 1024)
    tf = 256
    nk = dff // tf
    vec = pl.BlockSpec((1, 1, d), lambda b, i, k: (b, 0, 0))
    row = pl.BlockSpec((1, d), lambda b, i, k: (0, 0))
    return pl.pallas_call(
        functools.partial(_ffn_kernel, nk=nk, final=final),
        grid=(bsz, n // tm, nk),
        in_specs=[
            pl.BlockSpec((1, tm, d), lambda b, i, k: (b, i, 0)),
            row, vec, vec, vec,
            pl.BlockSpec((d, tf), lambda b, i, k: (0, k)),
            pl.BlockSpec((d, tf), lambda b, i, k: (0, k)),
            pl.BlockSpec((tf, d), lambda b, i, k: (k, 0)),
            row,
        ],
        out_specs=pl.BlockSpec((1, tm, d), lambda b, i, k: (b, i, 0)),
        out_shape=jax.ShapeDtypeStruct((bsz, n, d), F32),
        scratch_shapes=[pltpu.VMEM((tm, d), BF16), pltpu.VMEM((tm, d), F32)],
        compiler_params=_cparams(("parallel", "parallel", "arbitrary")),
        name="ffn",
    )(x, g, shift, scale, gate, w_g, w_u, w_d, final_g)


def _rope_tables(n_tok):
    rows = n_tok // GRID_W
    row = jnp.broadcast_to(jnp.arange(rows, dtype=F32)[:, None], (rows, GRID_W)).reshape(-1)
    col = jnp.broadcast_to(jnp.arange(GRID_W, dtype=F32)[None, :], (rows, GRID_W)).reshape(-1)
    n_freq = MLA_ROPE // 4
    inv = ROPE_BASE ** (-jnp.arange(n_freq, dtype=F32) / n_freq)
    ang = jnp.concatenate([row[:, None] * inv, col[:, None] * inv], axis=-1)
    cos, sin = jnp.cos(ang), jnp.sin(ang)
    pad = jnp.zeros((n_tok, HEAD_W - MLA_ROPE), F32)
    return jnp.concatenate([cos, cos, pad], axis=-1), jnp.concatenate([-sin, sin, pad], axis=-1)


def _identity_rope_tables(n_tok):
    one = jnp.ones((n_tok, MLA_ROPE), F32)
    pad = jnp.zeros((n_tok, HEAD_W - MLA_ROPE), F32)
    return jnp.concatenate([one, pad], axis=-1), jnp.zeros((n_tok, HEAD_W), F32)


def _layout_w_in(w):
    kr = w[:, OFF_KR:OFF_S5]
    x1, x2 = kr[:, 0::2], kr[:, 1::2]
    pad = jnp.zeros((w.shape[0], HEAD_W - 3 * (MLA_ROPE // 2)), w.dtype)
    return jnp.concatenate([w[:, OFF_GATE:], w[:, OFF_HY:OFF_GATE], w[:, OFF_S5:OFF_HY],
                            w[:, OFF_CQ:OFF_CKV], w[:, OFF_CKV:OFF_KR], x1, x2, x1, pad], axis=1).astype(BF16)


def _layout_w_uq(w):
    w = w.reshape(MLA_Q_LORA, MLA_HEADS, MLA_NOPE + MLA_ROPE)
    nope, rope = w[..., :MLA_NOPE], w[..., MLA_NOPE:]
    x1, x2 = rope[..., 0::2], rope[..., 1::2]
    z32 = jnp.zeros_like(rope)
    wq = jnp.concatenate([x1, x2, z32, nope], axis=-1)
    wqs = jnp.concatenate([x2, x1, z32, jnp.zeros_like(nope)], axis=-1)
    shape = (MLA_Q_LORA, MLA_HEADS * HEAD_W)
    return wq.reshape(shape).astype(BF16), wqs.reshape(shape).astype(BF16)


def _layout_w_ukv(w):
    w = w.reshape(MLA_KV_LORA, MLA_HEADS, MLA_NOPE + MLA_V)
    nope, val = w[..., :MLA_NOPE], w[..., MLA_NOPE:]
    wk = jnp.concatenate([jnp.zeros_like(nope), nope], axis=-1).reshape(MLA_KV_LORA, MLA_HEADS * HEAD_W)
    wvt = jnp.concatenate([val, jnp.zeros_like(val[..., :VT_ROWS - MLA_V])], axis=-1).reshape(
        MLA_KV_LORA, MLA_HEADS * VT_ROWS).T
    return wk.astype(BF16), wvt.astype(BF16)


def kernel(x, c, ctx, c_ctx, ada_w, ada_b, norm_mix, w_in, mla_q_norm, mla_w_uq, mla_kv_norm, mla_w_ukv,
           s5_lam_re, s5_lam_im, s5_log_dt, s5_b_re, s5_b_im, s5_c_re, s5_c_im, s5_d, s5_w_glu,
           hy_conv_w, hy_conv_b, hy_f_w1, hy_f_b1, hy_f_w2, hy_f_b2, hy_f_w3, hy_f_freq, hy_bias,
           w_branch_mla, w_branch_s5, w_branch_hy, w_out, norm_ffn, ffn_w_gu, ffn_w_down, final_norm):
    bsz, n_tok, d = x.shape
    n_ctx = ctx.shape[1]
    depth = ada_w.shape[0]
    rope_x = _rope_tables(n_tok)
    rope_c = _identity_rope_tables(n_ctx)
    h_zero = jnp.zeros((bsz, 2, 4, S5_HSTATE), F32)

    assert bsz < ADA_ROWS, "the conditioning rows of the batch and the context must fit one row tile"
    cond = jnp.zeros((ADA_ROWS, d), F32).at[:bsz].set(c).at[bsz].set(c_ctx)
    for i in range(depth):
        last = i == depth - 1
        mods = _ada_modulation(cond, ada_w[i], ada_b[i][None])
        mx = mods[:bsz].reshape(bsz, N_MOD, 1, d)
        mc = jnp.broadcast_to(mods[bsz].reshape(1, N_MOD, 1, d), (bsz, N_MOD, 1, d))
        w_z = _layout_w_in(w_in[i])
        wq, wqs = _layout_w_uq(mla_w_uq[i])
        wk, wvt = _layout_w_ukv(mla_w_ukv[i])
        qn, kvn = mla_q_norm[i][None], mla_kv_norm[i][None]
        s5_tabs = _s5_tables(s5_lam_re[i], s5_lam_im[i], s5_log_dt[i], s5_b_re[i], s5_b_im[i],
                             s5_c_re[i], s5_c_im[i])
        fparams = (hy_f_w1[i], hy_f_b1[i], hy_f_w2[i], hy_f_b2[i], hy_f_w3[i], hy_f_freq[i])
        merge_w = (s5_d[i][None], s5_w_glu[i].astype(BF16),
                   w_branch_mla[i].astype(BF16),
                   w_branch_s5[i].astype(BF16), w_branch_hy[i].astype(BF16), w_out[i].astype(BF16))
        w_g = ffn_w_gu[i][:, :D_FF].astype(BF16)
        w_u = ffn_w_gu[i][:, D_FF:].astype(BF16)
        w_d = ffn_w_down[i].astype(BF16)
        g_mix, g_ffn = norm_mix[i][None], norm_ffn[i][None]

        zx = _proj_in(x, g_mix, mx[:, 0], mx[:, 1], w_z)
        zc = _proj_in(ctx, g_mix, mc[:, 0], mc[:, 1], w_z)

        n_keys = n_tok + n_ctx
        q_x, k_all, vt_all = _mla_prep(zx, qn, kvn, wq, wqs, wk, wvt, *rope_x, n_keys, 0)
        q_c, k_all, vt_all = _mla_prep(zc, qn, kvn, wq, wqs, wk, wvt, *rope_c, n_keys, n_tok, (k_all, vt_all))
        a_x = _attention(q_x, k_all, vt_all, n_keys)

        *y5_c, finals = _s5_scan(zc, h_zero, s5_tabs)
        *y5_x, _ = _s5_scan(zx, finals, s5_tabs)

        v_x, g1_x, g2_x = _short_conv(zx, hy_conv_w[i], hy_conv_b[i][None])
        e_x = _hyena_mixer(v_x, (g1_x, g2_x), _hyena_filters(n_tok, *fparams), hy_bias[i])

        x = _merge(x, zx, a_x, y5_x, e_x, *merge_w, mx[:, 2])
        x = _ffn(x, g_ffn, mx[:, 3], mx[:, 4], mx[:, 5], w_g, w_u, w_d, final_norm[None], last)

        if not last:
            a_c = _attention(q_c, k_all, vt_all, n_ctx, n_tok // n_ctx)
            v_c, g1_c, g2_c = _short_conv(zc, hy_conv_w[i], hy_conv_b[i][None])
            e_c = _hyena_mixer(v_c, (g1_c, g2_c), _hyena_filters(n_ctx, *fparams), hy_bias[i])
            ctx = _merge(ctx, zc, a_c, y5_c, e_c, *merge_w, mc[:, 2])
            ctx = _ffn(ctx, g_ffn, mc[:, 3], mc[:, 4], mc[:, 5], w_g, w_u, w_d, final_norm[None], False)
    return x
```

```python
import functools
import math

import jax
import jax.numpy as jnp
from jax import lax
from jax.experimental import pallas as pl
from jax.experimental.pallas import tpu as pltpu

F32 = jnp.float32
BF16 = jnp.bfloat16
HIGHEST = lax.Precision.HIGHEST

D_MODEL = 1024
GRID_W = 64
NORM_EPS = 1e-6
N_MOD = 6

MLA_HEADS = 8
MLA_NOPE = 64
MLA_ROPE = 32
MLA_V = 64
MLA_Q_LORA = 256
MLA_KV_LORA = 128
ROPE_BASE = 10000.0
HEAD_W = 128
MLA_PREP_ROWS = 1024
VT_ROWS = 80
QK_SCALE_LOG2 = (MLA_NOPE + MLA_ROPE) ** -0.5 * math.log2(math.e)
QK_HALF_SCALE = QK_SCALE_LOG2 ** 0.5
QK_DTYPE = jnp.float8_e4m3fn
ATTN_SUB = 128
ATTN_LAG = 3
ATTN_UNROLL = 12

S5_WIDTH = 512
S5_GROUP = 16
S5_GROUPS = 32
S5_STATE = 64
S5_HALF = 256
S5_HSTATE = 1024
S5_CHUNK = 128
S5_CHUNKS_PER_STEP = 4

HY_WIDTH = 512
HY_ORDER = 2
HY_BANDS = 16
HY_POS_DIM = 1 + 2 * HY_BANDS
HY_POS_PAD = 64
HY_FILTER_HIDDEN = 64
HY_FILTER_OUT = HY_ORDER * 2 * HY_WIDTH
HY_DECAY_TARGET = 1e-2
HY_DECAY_SHORT = 0.3
HY_DECAY_LONG = 1.5
HY_DECAY_SHIFT = 0.05

D_FF = 2816

OFF_CQ = 0
OFF_CKV = OFF_CQ + MLA_Q_LORA
OFF_KR = OFF_CKV + MLA_KV_LORA
OFF_S5 = OFF_KR + MLA_ROPE
OFF_HY = OFF_S5 + S5_WIDTH
OFF_GATE = OFF_HY + 3 * HY_WIDTH

Z_GATE = 0
Z_HY = 3072
Z_S5 = 4608
Z_MLA = 5120
Z_WIDTH = 5632

VMEM_LIMIT_V7X = 52 * 1024 * 1024


def _cparams(sem, flags=None):
    return pltpu.CompilerParams(dimension_semantics=sem, vmem_limit_bytes=VMEM_LIMIT_V7X, flags=flags)


def _dot(a, b):
    return jnp.dot(a, b, preferred_element_type=F32)


def _rms(x, g):
    return x * lax.rsqrt(jnp.mean(x * x, axis=-1, keepdims=True) + NORM_EPS) * g


def _norm_mod(x, g, shift, scale):
    return _rms(x, g) * (1.0 + scale) + shift


ADA_ROWS = 8


def _ada_kernel(c_ref, w_ref, b_ref, o_ref):
    c = c_ref[...]
    h = c * jax.nn.sigmoid(c)
    o_ref[...] = jnp.dot(h, w_ref[...], precision=HIGHEST, preferred_element_type=F32) + b_ref[...]


def _ada_modulation(cond, w, b):
    rows, d = cond.shape
    nout = w.shape[1]
    tn = nout // 4
    return pl.pallas_call(
        _ada_kernel,
        grid=(nout // tn,),
        in_specs=[
            pl.BlockSpec((rows, d), lambda j: (0, 0)),
            pl.BlockSpec((d, tn), lambda j: (0, j)),
            pl.BlockSpec((1, tn), lambda j: (0, j)),
        ],
        out_specs=pl.BlockSpec((rows, tn), lambda j: (0, j)),
        out_shape=jax.ShapeDtypeStruct((rows, nout), F32),
        compiler_params=_cparams(("parallel",)),
        name="ada_modulation",
    )(cond, w, b)


def _proj_kernel(x_ref, g_ref, sh_ref, sc_ref, w_ref, o_ref, h_ref):
    @pl.when(pl.program_id(2) == 0)
    def _():
        h_ref[...] = _norm_mod(x_ref[0], g_ref[...], sh_ref[0], sc_ref[0]).astype(BF16)

    o_ref[0] = _dot(h_ref[...], w_ref[...]).astype(BF16)


def _proj_in(x, g, shift, scale, w):
    bsz, n, d = x.shape
    nz = w.shape[1]
    tm = min(n, 2048)
    tn = 512
    return pl.pallas_call(
        _proj_kernel,
        grid=(bsz, n // tm, nz // tn),
        in_specs=[
            pl.BlockSpec((1, tm, d), lambda b, i, j: (b, i, 0)),
            pl.BlockSpec((1, d), lambda b, i, j: (0, 0)),
            pl.BlockSpec((1, 1, d), lambda b, i, j: (b, 0, 0)),
            pl.BlockSpec((1, 1, d), lambda b, i, j: (b, 0, 0)),
            pl.BlockSpec((d, tn), lambda b, i, j: (0, j)),
        ],
        out_specs=pl.BlockSpec((1, tm, tn), lambda b, i, j: (b, i, j)),
        out_shape=jax.ShapeDtypeStruct((bsz, n, nz), BF16),
        scratch_shapes=[pltpu.VMEM((tm, d), BF16)],
        compiler_params=_cparams(("parallel", "parallel", "arbitrary")),
        name="proj_in",
    )(x, g, shift, scale, w)


def _mla_prep_kernel(z_ref, qn_ref, kvn_ref, wq_ref, wqs_ref, wk_ref, wvt_ref, c_ref, s_ref, *rest):
    q_ref, k_ref, vt_ref = rest[-3:]
    z = z_ref[0].astype(F32)
    hq = _rms(z[:, :MLA_Q_LORA], qn_ref[...]).astype(BF16)
    hkv = _rms(z[:, MLA_Q_LORA:MLA_Q_LORA + MLA_KV_LORA], kvn_ref[...]).astype(BF16)
    krb = z[:, MLA_Q_LORA + MLA_KV_LORA:]
    qa = _dot(hq, wq_ref[...])
    qb = _dot(hq, wqs_ref[...])
    kn = _dot(hkv, wk_ref[...])
    vt = lax.dot_general(wvt_ref[...], hkv, (((1,), (1,)), ((), ())), preferred_element_type=F32)
    ck = c_ref[...]
    sn = s_ref[...]
    lane = lax.broadcasted_iota(jnp.int32, ck.shape, 1)
    cq = jnp.where(lane < MLA_ROPE, ck, 1.0)
    kr = krb * ck + pltpu.roll(krb, HEAD_W - MLA_ROPE // 2, 1) * sn
    row = lax.broadcasted_iota(jnp.int32, (VT_ROWS, z.shape[0]), 0)
    for h in range(MLA_HEADS):
        sl = slice(h * HEAD_W, (h + 1) * HEAD_W)
        q_ref[0, h] = ((qa[:, sl] * cq + qb[:, sl] * sn) * QK_HALF_SCALE).astype(QK_DTYPE)
        k_ref[0, h] = ((kr + kn[:, sl]) * QK_HALF_SCALE).astype(QK_DTYPE)
        vt_ref[0, h] = jnp.where(row == MLA_V, 1.0, vt[h * VT_ROWS:(h + 1) * VT_ROWS, :]).astype(BF16)


def _mla_prep(z, qn, kvn, wq, wqs, wk, wvt, ctab, stab, n_keys, key_off, kv_bufs=None):
    bsz, n, _ = z.shape
    tm = min(n, MLA_PREP_ROWS)
    assert key_off % tm == 0, "the key offset must be a whole number of row tiles"
    ob = key_off // tm
    hw = MLA_HEADS * HEAD_W
    zb = Z_MLA // 512
    full = lambda shape: pl.BlockSpec(shape, lambda b, i: (0,) * len(shape))
    in_specs = [
        pl.BlockSpec((1, tm, 512), lambda b, i: (b, i, zb)),
        full((1, MLA_Q_LORA)),
        full((1, MLA_KV_LORA)),
        full((MLA_Q_LORA, hw)),
        full((MLA_Q_LORA, hw)),
        full((MLA_KV_LORA, hw)),
        full((MLA_HEADS * VT_ROWS, MLA_KV_LORA)),
        pl.BlockSpec((tm, HEAD_W), lambda b, i: (i, 0)),
        pl.BlockSpec((tm, HEAD_W), lambda b, i: (i, 0)),
    ]
    args = [z, qn, kvn, wq, wqs, wk, wvt, ctab, stab]
    aliases = {}
    if kv_bufs is not None:
        aliases = {len(args): 1, len(args) + 1: 2}
        in_specs += [pl.BlockSpec(memory_space=pl.ANY)] * 2
        args += list(kv_bufs)
    return pl.pallas_call(
        _mla_prep_kernel,
        grid=(bsz, n // tm),
        in_specs=in_specs,
        out_specs=[
            pl.BlockSpec((1, MLA_HEADS, tm, HEAD_W), lambda b, i: (b, 0, i, 0)),
            pl.BlockSpec((1, MLA_HEADS, tm, HEAD_W), lambda b, i: (b, 0, i + ob, 0)),
            pl.BlockSpec((1, MLA_HEADS, VT_ROWS, tm), lambda b, i: (b, 0, 0, i + ob)),
        ],
        out_shape=[
            jax.ShapeDtypeStruct((bsz, MLA_HEADS, n, HEAD_W), QK_DTYPE),
            jax.ShapeDtypeStruct((bsz, MLA_HEADS, n_keys, HEAD_W), QK_DTYPE),
            jax.ShapeDtypeStruct((bsz, MLA_HEADS, VT_ROWS, n_keys), BF16),
        ],
        input_output_aliases=aliases,
        compiler_params=_cparams(("parallel", "parallel")),
        name="mla_prep",
    )(*args)


def _attn_kernel(q_ref, k_ref, vt_ref, o_ref, s0_ref, s1_ref, s2_ref, p0_ref, p1_ref, p2_ref, acc_ref, *, tk, nkc):
    q = q_ref[0, 0]
    tq = q.shape[0]
    s_refs = (s0_ref, s1_ref, s2_ref)
    p_refs = (p0_ref, p1_ref, p2_ref)

    ts = min(tk, ATTN_SUB)
    subs = [slice(r, r + ts) for r in range(0, tk, ts)]

    def scores_sub(j, s_ref, sub, cmax):
        off = pl.multiple_of(j * tk + sub.start, ts)
        s = lax.dot_general(k_ref[0, 0, pl.ds(off, ts), :], q, (((1,), (1,)), ((), ())),
                            preferred_element_type=F32)
        s_ref[sub, :] = s
        cm = jnp.max(s, axis=0, keepdims=True)
        return cm if cmax is None else jnp.maximum(cmax, cm)

    def softmax_sub(slot, sub, mn):
        p_refs[slot][sub, :] = jnp.exp2(s_refs[slot][sub, :] - mn).astype(BF16)

    def step(i, slot, carry, do_softmax, do_scores):
        cm, m, alpha = carry
        mn = jnp.maximum(m, cm)
        off = pl.multiple_of(i * tk, tk)
        acc_ref[...] = alpha * acc_ref[...] + _dot(vt_ref[0, 0, :, pl.ds(off, tk)], p_refs[slot][...])
        cm_new, cms = None, []
        for r, sub in enumerate(subs):
            mn_sub = mn
            if do_scores:
                cm_new = scores_sub(i + 2, s_refs[(slot + 2) % 3], sub, cm_new)
                cms.append(cm_new)
                if r >= ATTN_LAG:
                    mn_sub = jnp.maximum(mn, cms[r - ATTN_LAG] - 1e30)
            if do_softmax:
                softmax_sub((slot + 1) % 3, sub, mn_sub)
        if do_softmax:
            alpha, m = jnp.exp2(m - mn), mn
        return (cm_new if do_scores else cm), m, alpha

    m = jnp.full((1, tq), -1e30, F32)
    acc_ref[...] = jnp.zeros_like(acc_ref)
    cm, cm_next = None, None
    for sub in subs:
        cm = scores_sub(0, s0_ref, sub, cm)
        if nkc > 1:
            cm_next = scores_sub(1, s1_ref, sub, cm_next)
    mn = jnp.maximum(m, cm)
    for sub in subs:
        softmax_sub(0, sub, mn)
    carry = (cm_next if nkc > 1 else cm, mn, jnp.exp2(m - mn))

    def body(t, carry):
        for r in range(ATTN_UNROLL):
            carry = step(ATTN_UNROLL * t + r, r % 3, carry, True, True)
        return carry

    nloop = max(nkc - 2, 0) // ATTN_UNROLL
    carry = lax.fori_loop(0, nloop, body, carry)
    for i in range(ATTN_UNROLL * nloop, nkc):
        carry = step(i, i % 3, carry, i + 1 < nkc, i + 2 < nkc)
    acc = acc_ref[...]
    out = acc * (1.0 / acc[MLA_V:MLA_V + 1, :])
    out = jnp.concatenate([out, jnp.zeros((HEAD_W - VT_ROWS, tq), F32)], axis=0).T
    o_ref[0, 0] = out[:, :MLA_V].astype(BF16)


def _kv_chunk(nk):
    for tk in (640, 512, 256, 128):
        if nk % tk == 0:
            return tk
    raise ValueError(f"unsupported key count {nk}")


def _attention(q, k, vt, nk, key_block=0):
    bsz, nh, nq, _ = q.shape
    tq = min(nq, 1024)
    tk = _kv_chunk(nk)
    return pl.pallas_call(
        functools.partial(_attn_kernel, tk=tk, nkc=nk // tk),
        grid=(bsz, nh, nq // tq),
        in_specs=[
            pl.BlockSpec((1, 1, tq, HEAD_W), lambda b, h, i: (b, h, i, 0)),
            pl.BlockSpec((1, 1, nk, HEAD_W), lambda b, h, i: (b, h, key_block, 0)),
            pl.BlockSpec((1, 1, VT_ROWS, nk), lambda b, h, i: (b, h, 0, key_block)),
        ],
        out_specs=pl.BlockSpec((1, 1, tq, MLA_V), lambda b, h, i: (b, h, i, 0)),
        out_shape=jax.ShapeDtypeStruct((bsz, nh, nq, MLA_V), BF16),
        scratch_shapes=[pltpu.VMEM((tk, tq), F32)] * 3 + [pltpu.VMEM((tk, tq), BF16)] * 3
        + [pltpu.VMEM((VT_ROWS, tq), F32)],
        compiler_params=_cparams(("parallel", "parallel", "arbitrary")),
        name="attention",
    )(q, k, vt)


def _cmul(ar, ai, br, bi):
    return ar * br - ai * bi, ar * bi + ai * br


def _s5_kernel(u_ref, h0_ref, bm_ref, cm_ref, wn_ref, wp_ref, l1_ref, tri_ref, y_ref, hf_ref,
               carry_ref, h0cat_ref, h1cat_ref, *, tc, nr, ns, reverse):
    i = pl.program_id(1)

    @pl.when(i == 0)
    def _():
        carry_ref[...] = h0_ref[0]

    u = u_ref[0]
    tri = tri_ref[...]
    order = range(nr - 1, -1, -1) if reverse else range(nr)
    edge = 0 if reverse else tc - 1
    hcat_refs = (h0cat_ref, h1cat_ref)
    bus = [_dot(u[:, hf * S5_HALF:(hf + 1) * S5_HALF], bm_ref[hf]) for hf in range(2)]
    carries = [(carry_ref[2 * hf:2 * hf + 1, :], carry_ref[2 * hf + 1:2 * hf + 2, :]) for hf in range(2)]
    for c in order:
        rows = slice(c * tc, (c + 1) * tc)
        for hf in range(2):
            bu = bus[hf]
            xr, xi = _cmul(wn_ref[hf, 0], wn_ref[hf, 1], bu[rows, :S5_HSTATE], bu[rows, S5_HSTATE:])
            s = _dot(tri, jnp.concatenate([xr, xi], axis=1).astype(BF16))
            cr, ci = _cmul(l1_ref[hf, 0], l1_ref[hf, 1], *carries[hf])
            hr, hi = _cmul(wp_ref[hf, 0], wp_ref[hf, 1], s[:, :S5_HSTATE] + cr, s[:, S5_HSTATE:] + ci)
            hcat_refs[hf][rows, :S5_HSTATE] = hr.astype(BF16)
            hcat_refs[hf][rows, S5_HSTATE:] = hi.astype(BF16)
            carries[hf] = (hr[edge:edge + 1], hi[edge:edge + 1])
    for hf in range(2):
        carry_ref[2 * hf:2 * hf + 1, :] = carries[hf][0]
        carry_ref[2 * hf + 1:2 * hf + 2, :] = carries[hf][1]
    y_ref[0] = jnp.concatenate([_dot(hcat_refs[hf][...], cm_ref[hf]) for hf in range(2)], axis=1)

    @pl.when(i == ns - 1)
    def _():
        hf_ref[0] = carry_ref[...]


def _s5_scan_dir(z, h0, tabs, reverse):
    bm, cm, wn, wp, l1, tri = tabs
    bsz, n, _ = z.shape
    tc = S5_CHUNK
    nr = min(S5_CHUNKS_PER_STEP, n // tc)
    rows = nr * tc
    ns = n // rows
    zb = Z_S5 // S5_WIDTH
    blk = (lambda i: ns - 1 - i) if reverse else (lambda i: i)
    full = lambda shape: pl.BlockSpec(shape, lambda b, i: (0,) * len(shape))
    return pl.pallas_call(
        functools.partial(_s5_kernel, tc=tc, nr=nr, ns=ns, reverse=reverse),
        grid=(bsz, ns),
        in_specs=[
            pl.BlockSpec((1, rows, S5_WIDTH), lambda b, i: (b, blk(i), zb)),
            pl.BlockSpec((1, 4, S5_HSTATE), lambda b, i: (b, 0, 0)),
            full((2, S5_HALF, 2 * S5_HSTATE)),
            full((2, 2 * S5_HSTATE, S5_HALF)),
            full((2, 2, tc, S5_HSTATE)),
            full((2, 2, tc, S5_HSTATE)),
            full((2, 2, 1, S5_HSTATE)),
            full((tc, tc)),
        ],
        out_specs=[
            pl.BlockSpec((1, rows, S5_WIDTH), lambda b, i: (b, blk(i), 0)),
            pl.BlockSpec((1, 4, S5_HSTATE), lambda b, i: (b, 0, 0)),
        ],
        out_shape=[
            jax.ShapeDtypeStruct((bsz, n, S5_WIDTH), F32),
            jax.ShapeDtypeStruct((bsz, 4, S5_HSTATE), F32),
        ],
        scratch_shapes=[pltpu.VMEM((4, S5_HSTATE), F32),
                        pltpu.VMEM((rows, 2 * S5_HSTATE), BF16), pltpu.VMEM((rows, 2 * S5_HSTATE), BF16)],
        compiler_params=_cparams(("parallel", "arbitrary")),
        name="s5_scan",
    )(z, h0, bm, cm, wn, wp, l1, tri)


def _s5_scan(z, h0, tabs):
    outs = [_s5_scan_dir(z, h0[:, d], [t[d] for t in tabs], reverse=bool(d)) for d in range(2)]
    return outs[0][0], outs[1][0], jnp.stack([outs[0][1], outs[1][1]], axis=1)


def _s5_tables(lam_re, lam_im, log_dt, b_re, b_im, c_re, c_im):
    tc = S5_CHUNK
    dt = jnp.exp(log_dt)[..., None]
    zr, zi = lam_re * dt, lam_im * dt
    mag = jnp.exp(zr)
    lbr, lbi = mag * jnp.cos(zi), mag * jnp.sin(zi)
    den = lam_re * lam_re + lam_im * lam_im
    nr, ni = lbr - 1.0, lbi
    cfr = (nr * lam_re + ni * lam_im) / den
    cfi = (ni * lam_re - nr * lam_im) / den
    bbr = cfr[..., None] * b_re - cfi[..., None] * b_im
    bbi = cfr[..., None] * b_im + cfi[..., None] * b_re
    eye = jnp.eye(S5_GROUP, dtype=F32)

    def blockdiag_in(b):
        b = b.reshape(2, 2, S5_GROUP, S5_STATE, S5_GROUP)
        return jnp.einsum('gk,dhgpn->dhgnkp', eye, b).reshape(2, 2, S5_HALF, S5_HSTATE)

    def blockdiag_out(c):
        c = c.reshape(2, 2, S5_GROUP, S5_GROUP, S5_STATE)
        return jnp.einsum('gk,dhgnp->dhgpkn', eye, c).reshape(2, 2, S5_HSTATE, S5_HALF)

    bm = jnp.concatenate([blockdiag_in(bbr), blockdiag_in(bbi)], axis=-1).astype(BF16)
    cm = jnp.concatenate([blockdiag_out(c_re), blockdiag_out(-c_im)], axis=-2).astype(BF16)

    def powers(k):
        zr_ = zr.reshape(2, 2, 1, S5_HSTATE)
        zi_ = zi.reshape(2, 2, 1, S5_HSTATE)
        kk = k[:, None, :, None]
        m = jnp.exp(kk * zr_)
        return jnp.stack([m * jnp.cos(kk * zi_), m * jnp.sin(kk * zi_)], axis=2)

    mid = tc // 2
    t = jnp.arange(tc, dtype=F32) - mid
    wn = powers(jnp.stack([-t, -t[::-1]]))
    wp = powers(jnp.stack([t, t[::-1]]))
    l1 = powers(jnp.full((2, 1), mid + 1, F32))
    r = jnp.arange(tc)
    tri = jnp.stack([r[:, None] >= r[None, :], r[:, None] <= r[None, :]]).astype(BF16)
    return bm, cm, wn, wp, l1, tri


HALO = 16


def _sconv_kernel(z_ref, zp_ref, zn_ref, w_ref, b_ref, v_ref, g1_ref, g2_ref, *, nt):
    i = pl.program_id(1)
    z = z_ref[0].astype(F32)
    tm = z.shape[0]
    row = lax.broadcasted_iota(jnp.int32, z.shape, 0)
    prev = jnp.where(i > 0, zp_ref[0, HALO - 1:HALO, :].astype(F32), 0.0)
    nxt = jnp.where(i < nt - 1, zn_ref[0, 0:1, :].astype(F32), 0.0)
    up = jnp.where(row == 0, prev, pltpu.roll(z, 1, 0))
    dn = jnp.where(row == tm - 1, nxt, pltpu.roll(z, tm - 1, 0))
    u = up * w_ref[0:1, :] + z * w_ref[1:2, :] + dn * w_ref[2:3, :] + b_ref[...]
    for k, o_ref in enumerate((v_ref, g1_ref, g2_ref)):
        o_ref[0] = u[:, k * HY_WIDTH:(k + 1) * HY_WIDTH]


def _short_conv(z, w, b):
    bsz, n, _ = z.shape
    cw = 3 * HY_WIDTH
    tm = min(n, 1024)
    nt = n // tm
    zb = Z_HY // cw
    rh = tm // HALO
    out = pl.BlockSpec((1, tm, HY_WIDTH), lambda b_, i: (b_, i, 0))
    return pl.pallas_call(
        functools.partial(_sconv_kernel, nt=nt),
        grid=(bsz, nt),
        in_specs=[
            pl.BlockSpec((1, tm, cw), lambda b_, i: (b_, i, zb)),
            pl.BlockSpec((1, HALO, cw), lambda b_, i: (b_, jnp.maximum(i * rh - 1, 0), zb)),
            pl.BlockSpec((1, HALO, cw), lambda b_, i: (b_, jnp.minimum((i + 1) * rh, n // HALO - 1), zb)),
            pl.BlockSpec((3, cw), lambda b_, i: (0, 0)),
            pl.BlockSpec((1, cw), lambda b_, i: (0, 0)),
        ],
        out_specs=[out] * 3,
        out_shape=[jax.ShapeDtypeStruct((bsz, n, HY_WIDTH), F32)] * 3,
        compiler_params=_cparams(("parallel", "parallel")),
        name="short_conv",
    )(z, z, z, w, b)


def _filt_kernel(f_ref, w1_ref, b1_ref, w2_ref, b2_ref, w3h_ref, w3l_ref, fq_ref, dl_ref, o_ref, *, n_tok):
    z = f_ref[...]
    tm = z.shape[0]
    fq = fq_ref[...]
    hid = jnp.sin(fq * (jnp.dot(z, w1_ref[...], precision=HIGHEST, preferred_element_type=F32) + b1_ref[...]))
    hid = jnp.sin(fq * (jnp.dot(hid, w2_ref[...], precision=HIGHEST, preferred_element_type=F32) + b2_ref[...]))
    hid_hi = hid.astype(BF16)
    hid_lo = (hid - hid_hi.astype(F32)).astype(BF16)

    def filters(rows, d):
        wh, wl = w3h_ref[d], w3l_ref[d]
        f = _dot(hid_hi[rows], wh) + (_dot(hid_hi[rows], wl) + _dot(hid_lo[rows], wh))
        return f * (jnp.exp(-z[rows, 0:1] * dl_ref[d]) + HY_DECAY_SHIFT)

    i = pl.program_id(0)
    filt = filters(slice(None), jnp.where(i * tm >= n_tok, 1, 0))
    m = i * tm + lax.broadcasted_iota(jnp.int32, filt.shape, 0)
    filt = jnp.where(m == n_tok, 0.0, filt)
    for o in range(HY_ORDER):
        o_ref[o] = filt[:, o * HY_WIDTH:(o + 1) * HY_WIDTH]

    @pl.when(i == 0)
    def _():
        head = slice(0, HALO)
        lag0 = filters(head, 1)
        lag0 = jnp.where(lax.broadcasted_iota(jnp.int32, lag0.shape, 0) == 0, lag0, 0.0)
        for o in range(HY_ORDER):
            o_ref[o, head, :] += lag0[:, o * HY_WIDTH:(o + 1) * HY_WIDTH]


def _hyena_filter_feats(n_tok):
    m = jnp.arange(2 * n_tok)
    lag = jnp.where(m < n_tok, m, jnp.where(m > n_tok, 2 * n_tok - m, 0))
    t = (lag.astype(F32) / (n_tok - 1))[:, None]
    w = (2.0 * math.pi * lag.astype(F32) / n_tok)[:, None]
    bands = jnp.linspace(1e-4, HY_BANDS - 1, HY_BANDS, dtype=F32)[None, :]
    feats = jnp.concatenate([t, jnp.cos(bands * w), -jnp.sin(bands * w)], axis=-1)
    return jnp.pad(feats, ((0, 0), (0, HY_POS_PAD - HY_POS_DIM)))


def _hyena_filters(n_tok, w1, b1, w2, b2, w3, freq):
    feats = _hyena_filter_feats(n_tok)
    deltas = jnp.abs(jnp.linspace(math.log(HY_DECAY_TARGET) / HY_DECAY_SHORT,
                                  math.log(HY_DECAY_TARGET) / HY_DECAY_LONG, HY_FILTER_OUT, dtype=F32))[None, :]
    w1p = jnp.pad(w1, ((0, HY_POS_PAD - HY_POS_DIM), (0, 0)))

    def by_direction(a):
        a = a.reshape(a.shape[:-1] + (HY_ORDER, 2, HY_WIDTH))
        return jnp.moveaxis(a, -2, 0).reshape((2,) + a.shape[:-3] + (HY_ORDER * HY_WIDTH,))

    w3, deltas = by_direction(w3), by_direction(deltas)
    w3_hi = w3.astype(BF16)
    w3_lo = (w3 - w3_hi.astype(F32)).astype(BF16)
    n2 = 2 * n_tok
    tm = min(n_tok, 1024)
    full = lambda shape: pl.BlockSpec(shape, lambda i: (0,) * len(shape))
    hh = HY_FILTER_HIDDEN
    wd = HY_ORDER * HY_WIDTH
    return pl.pallas_call(
        functools.partial(_filt_kernel, n_tok=n_tok),
        grid=(n2 // tm,),
        in_specs=[
            pl.BlockSpec((tm, HY_POS_PAD), lambda i: (i, 0)),
            full((HY_POS_PAD, hh)), full((1, hh)), full((hh, hh)), full((1, hh)),
            full((2, hh, wd)), full((2, hh, wd)), full((1, hh)), full((2, 1, wd)),
        ],
        out_specs=pl.BlockSpec((HY_ORDER, tm, HY_WIDTH), lambda i: (0, i, 0)),
        out_shape=jax.ShapeDtypeStruct((HY_ORDER, n2, HY_WIDTH), F32),
        compiler_params=_cparams(("parallel",)),
        name="hyena_filter",
    )(feats, w1p, b1[None], w2, b2[None], w3_hi, w3_lo, freq[None], deltas)


def _fft_split(n_fft):
    if n_fft <= 1024:
        return n_fft, 1
    n1 = 1 << (int(math.log2(n_fft)) // 2)
    return n1, n_fft // n1


def _cis(num, den, sign):
    ang = (2.0 * math.pi / den) * num.astype(F32)
    return jnp.cos(ang), sign * jnp.sin(ang)


def _fft_tables(n_fft):
    n1, n2 = _fft_split(n_fft)
    a = jnp.arange(n1)
    f1r, f1i = _cis((a[:, None] * a[None, :]) % n1, n1, -1.0)
    s1 = jnp.concatenate([f1r, f1i], axis=0).astype(BF16)
    half = n1 // 2
    s3 = (jnp.concatenate([f1r[:half], -f1i[:half]], axis=0) / n_fft).astype(BF16)
    if n2 == 1:
        return s1, s3, None, None
    b = jnp.arange(n2)
    f2 = jnp.stack(_cis((b[:, None] * b[None, :]) % n2, n2, -1.0))
    tw = jnp.stack(_cis((jnp.arange(n1)[:, None] * b[None, :]) % n_fft, n_fft, -1.0))
    cb = 8
    ft, it = pl.pallas_call(
        functools.partial(_twiddle_kernel, cb=cb),
        grid=(n1 // cb,),
        in_specs=[pl.BlockSpec((2, n2, n2), lambda c: (0, 0, 0)), pl.BlockSpec((2, cb, n2), lambda c: (0, c, 0))],
        out_specs=[pl.BlockSpec((cb, 2 * n2, n2), lambda c: (c, 0, 0))] * 2,
        out_shape=[jax.ShapeDtypeStruct((n1, 2 * n2, n2), BF16)] * 2,
        compiler_params=_cparams(("parallel",)),
        name="fft_twiddles",
    )(f2, tw)
    return s1, s3, ft, it


def _twiddle_kernel(f2_ref, tw_ref, ft_ref, it_ref, *, cb):
    for c in range(cb):
        fr, fi = _cmul(f2_ref[0], f2_ref[1], tw_ref[0, c:c + 1, :], tw_ref[1, c:c + 1, :])
        ft_ref[c] = jnp.concatenate([fr, fi], axis=0).astype(BF16)
        it_ref[c] = jnp.concatenate([fr.T, -fi.T], axis=0).astype(BF16)


def _stacked_cdot(s, xr, xi, m):
    p = _dot(s, xr.astype(BF16))
    if xi is None:
        return p[:m], p[m:]
    q = _dot(s, xi.astype(BF16))
    return p[:m] - q[m:], q[:m] + p[m:]


def _fft_a_kernel(*refs, parts, n1, has_g):
    if has_g:
        x_ref, s_ref, g_ref, y_ref = refs
    else:
        x_ref, s_ref, y_ref = refs
    yr, yi = _stacked_cdot(s_ref[...], x_ref[0], x_ref[1] if parts == 2 else None, n1)
    if has_g:
        yr, yi = _cmul(yr, yi, g_ref[0], g_ref[1])
    y_ref[0] = yr
    y_ref[1] = yi


def _fft_a(x, s1, g=None):
    parts, a_rows, cols = x.shape
    n1 = s1.shape[0] // 2
    tc = min(cols, 2048)
    in_specs = [
        pl.BlockSpec((parts, a_rows, tc), lambda j: (0, 0, j)),
        pl.BlockSpec((2 * n1, a_rows), lambda j: (0, 0)),
    ]
    args = [x, s1[:, :a_rows]]
    if g is not None:
        in_specs.append(pl.BlockSpec((2, n1, tc), lambda j: (0, 0, j)))
        args.append(g)
    return pl.pallas_call(
        functools.partial(_fft_a_kernel, parts=parts, n1=n1, has_g=g is not None),
        grid=(cols // tc,),
        in_specs=in_specs,
        out_specs=pl.BlockSpec((2, n1, tc), lambda j: (0, 0, j)),
        out_shape=jax.ShapeDtypeStruct((2, n1, cols), F32),
        compiler_params=_cparams(("parallel",)),
        name="fft_stage_a",
    )(*args)


FFT_CB = 4


def _fft_b_kernel(*refs, n2, inverse):
    if inverse:
        y_ref, ft_ref, it_ref, g_ref, o_ref = refs
    else:
        y_ref, ft_ref, o_ref = refs
    for c in range(FFT_CB):
        xr, xi = _stacked_cdot(ft_ref[c], y_ref[0, c], y_ref[1, c], n2)
        if inverse:
            zr, zi = _cmul(xr, xi, g_ref[0, c], g_ref[1, c])
            xr, xi = _stacked_cdot(it_ref[c], zr, zi, n2)
        o_ref[0, c] = xr.astype(o_ref.dtype)
        o_ref[1, c] = xi.astype(o_ref.dtype)


def _fft_b(y, ft, it=None, g=None):
    _, n1, n2, ch = y.shape
    inverse = it is not None
    blk = pl.BlockSpec((2, FFT_CB, n2, ch), lambda c: (0, c, 0, 0))
    mat = pl.BlockSpec((FFT_CB, 2 * n2, n2), lambda c: (c, 0, 0))
    in_specs, args = [blk, mat], [y, ft]
    if inverse:
        in_specs += [mat, blk]
        args += [it, g]
    return pl.pallas_call(
        functools.partial(_fft_b_kernel, n2=n2, inverse=inverse),
        grid=(n1 // FFT_CB,),
        in_specs=in_specs,
        out_specs=blk,
        out_shape=jax.ShapeDtypeStruct(y.shape, BF16 if inverse else F32),
        compiler_params=_cparams(("parallel",)),
        name="fft_stage_b",
    )(*args)


def _fft_c_kernel(y_ref, s_ref, x_ref, g_ref, b_ref, o_ref, *, half):
    cr, ci = _stacked_cdot(s_ref[...], y_ref[0], y_ref[1], half)
    bias = b_ref[...]
    o_ref[0] = g_ref[0] * (cr + bias * x_ref[0])
    o_ref[1] = g_ref[1] * (ci + bias * x_ref[1])


def _fft_c(y, s3, x, gate, bias_cols):
    _, n1, cols = y.shape
    half = n1 // 2
    tc = min(cols, 2048)
    io = pl.BlockSpec((2, half, tc), lambda j: (0, 0, j))
    return pl.pallas_call(
        functools.partial(_fft_c_kernel, half=half),
        grid=(cols // tc,),
        in_specs=[
            pl.BlockSpec((2, n1, tc), lambda j: (0, 0, j)),
            pl.BlockSpec((n1, n1), lambda j: (0, 0)),
            io, io,
            pl.BlockSpec((1, tc), lambda j: (0, j)),
        ],
        out_specs=io,
        out_shape=jax.ShapeDtypeStruct((2, half, cols), F32),
        compiler_params=_cparams(("parallel",)),
        name="fft_stage_c",
    )(y, s3, x, gate, bias_cols)


FFT_BT = 16


def _to_fine_major(x):
    return pltpu.einshape("abc->bac", x)


def _fft_a4_kernel(x_ref, s_ref, y_ref, *, parts, n1):
    s = s_ref[...]
    xr = _to_fine_major(x_ref[0].astype(BF16))
    xi = _to_fine_major(x_ref[1].astype(BF16)) if parts == 2 else None
    out = [_stacked_cdot(s, xr[b], None if xi is None else xi[b], n1) for b in range(FFT_BT)]
    y_ref[0] = pltpu.einshape("bac->abc", jnp.stack([o[0] for o in out])).astype(BF16)
    y_ref[1] = pltpu.einshape("bac->abc", jnp.stack([o[1] for o in out])).astype(BF16)


def _fft_a4(x, s1):
    parts, a_rows, n2, ch = x.shape
    n1 = s1.shape[0] // 2
    return pl.pallas_call(
        functools.partial(_fft_a4_kernel, parts=parts, n1=n1),
        grid=(n2 // FFT_BT,),
        in_specs=[
            pl.BlockSpec((parts, a_rows, FFT_BT, ch), lambda j: (0, 0, j, 0)),
            pl.BlockSpec((2 * n1, a_rows), lambda j: (0, 0)),
        ],
        out_specs=pl.BlockSpec((2, n1, FFT_BT, ch), lambda j: (0, 0, j, 0)),
        out_shape=jax.ShapeDtypeStruct((2, n1, n2, ch), BF16),
        compiler_params=_cparams(("parallel",)),
        name="fft_stage_a",
    )(x, s1[:, :a_rows])


def _fft_c4_kernel(y_ref, s_ref, x_ref, g_ref, b_ref, o_ref, c_ref, *, half):
    s = s_ref[...]
    yr = _to_fine_major(y_ref[0].astype(BF16))
    yi = _to_fine_major(y_ref[1].astype(BF16))
    out = [_stacked_cdot(s, yr[b], yi[b], half) for b in range(FFT_BT)]
    c_ref[0] = pltpu.einshape("bac->abc", jnp.stack([o[0] for o in out]))
    c_ref[1] = pltpu.einshape("bac->abc", jnp.stack([o[1] for o in out]))
    o_ref[...] = g_ref[...] * (c_ref[...] + b_ref[...][None, None] * x_ref[...])


def _fft_c4(y, s3, x, gate, bias):
    _, n1, n2, ch = y.shape
    half = n1 // 2
    io = pl.BlockSpec((2, half, FFT_BT, ch), lambda j: (0, 0, j, 0))
    return pl.pallas_call(
        functools.partial(_fft_c4_kernel, half=half),
        grid=(n2 // FFT_BT,),
        in_specs=[
            pl.BlockSpec((2, n1, FFT_BT, ch), lambda j: (0, 0, j, 0)),
            pl.BlockSpec((n1, n1), lambda j: (0, 0)),
            io, io,
            pl.BlockSpec((1, ch), lambda j: (0, 0)),
        ],
        out_specs=io,
        out_shape=jax.ShapeDtypeStruct((2, half, n2, ch), F32),
        scratch_shapes=[pltpu.VMEM((2, half, FFT_BT, ch), F32)],
        compiler_params=_cparams(("parallel",)),
        name="fft_stage_c",
    )(y, s3, x, gate, bias)


def _hyena_mixer(v, gates, gfilt, bias):
    bsz, n, ch = v.shape
    assert bsz == 2, "the complex packing of the long convolution pairs exactly two batch rows"
    n_fft = 2 * n
    n1, n2 = _fft_split(n_fft)
    s1, s3, ft, it = _fft_tables(n_fft)
    if n2 == 1:
        y = v
        for o in range(HY_ORDER):
            gspec = _fft_a(gfilt[o][None], s1)
            y = _fft_c(_fft_a(y, s1, gspec), s3, y, gates[o], bias[o][None])
        return y
    half = n1 // 2
    y = v.reshape(2, half, n2, ch)
    for o in range(HY_ORDER):
        gspec = _fft_b(_fft_a4(gfilt[o].reshape(1, n1, n2, ch), s1), ft)
        spec = _fft_b(_fft_a4(y, s1), ft, it, gspec)
        y = _fft_c4(spec, s3, y, gates[o].reshape(2, half, n2, ch), bias[o][None])
    return y.reshape(2, n, ch)


def _gelu_tanh(x):
    return 0.5 * x * (1.0 + jnp.tanh(math.sqrt(2.0 / math.pi) * (x + 0.044715 * (x * x * x))))


def _merge_kernel(x_ref, g0_ref, g1_ref, g2_ref, a_ref, yf_ref, yb_ref, u_ref, e_ref, d_ref,
                  wglu_ref, wm_ref, ws_ref, wh_ref, wo_ref, gt_ref, o_ref):
    att = _dot(jnp.concatenate([a_ref[0, h] for h in range(MLA_HEADS)], axis=-1), wm_ref[...])
    y = yf_ref[0] + yb_ref[0] + d_ref[...] * u_ref[0].astype(F32)
    gy = _gelu_tanh(y)
    s5 = gy * jax.nn.sigmoid(_dot(gy.astype(BF16), wglu_ref[...]))
    s5 = _dot(s5.astype(BF16), ws_ref[...])
    hy = _dot(e_ref[0].astype(BF16), wh_ref[...])
    merged = jax.nn.sigmoid(g0_ref[0].astype(F32)) * att
    merged = merged + jax.nn.sigmoid(g1_ref[0].astype(F32)) * s5
    merged = merged + jax.nn.sigmoid(g2_ref[0].astype(F32)) * hy
    o_ref[0] = x_ref[0] + gt_ref[0] * _dot(merged.astype(BF16), wo_ref[...])


def _merge(x, z, att, y5, e, s5_d, w_glu, w_mla, w_s5, w_hy, w_out, gate):
    bsz, n, d = x.shape
    tm = min(n, 512)
    full = lambda shape: pl.BlockSpec(shape, lambda b, i: (0,) * len(shape))
    zs5 = Z_S5 // S5_WIDTH
    return pl.pallas_call(
        _merge_kernel,
        grid=(bsz, n // tm),
        in_specs=[
            pl.BlockSpec((1, tm, d), lambda b, i: (b, i, 0)),
            pl.BlockSpec((1, tm, d), lambda b, i: (b, i, 0)),
            pl.BlockSpec((1, tm, d), lambda b, i: (b, i, 1)),
            pl.BlockSpec((1, tm, d), lambda b, i: (b, i, 2)),
            pl.BlockSpec((1, MLA_HEADS, tm, MLA_V), lambda b, i: (b, 0, i, 0)),
            pl.BlockSpec((1, tm, S5_WIDTH), lambda b, i: (b, i, 0)),
            pl.BlockSpec((1, tm, S5_WIDTH), lambda b, i: (b, i, 0)),
            pl.BlockSpec((1, tm, S5_WIDTH), lambda b, i: (b, i, zs5)),
            pl.BlockSpec((1, tm, HY_WIDTH), lambda b, i: (b, i, 0)),
            full((1, S5_WIDTH)),
            full((S5_WIDTH, S5_WIDTH)),
            full((MLA_HEADS * MLA_V, d)),
            full((S5_WIDTH, d)),
            full((HY_WIDTH, d)),
            full((d, d)),
            pl.BlockSpec((1, 1, d), lambda b, i: (b, 0, 0)),
        ],
        out_specs=pl.BlockSpec((1, tm, d), lambda b, i: (b, i, 0)),
        out_shape=jax.ShapeDtypeStruct((bsz, n, d), F32),
        compiler_params=_cparams(("parallel", "parallel")),
        name="merge",
    )(x, z, z, z, att, y5[0], y5[1], z, e, s5_d, w_glu, w_mla, w_s5, w_hy, w_out, gate)


def _ffn_kernel(x_ref, g_ref, sh_ref, sc_ref, gt_ref, wg_ref, wu_ref, wd_ref, fg_ref, o_ref, h_ref, acc_ref,
                *, nk, final):
    k = pl.program_id(2)

    @pl.when(k == 0)
    def _():
        h_ref[...] = _norm_mod(x_ref[0], g_ref[...], sh_ref[0], sc_ref[0]).astype(BF16)
        acc_ref[...] = jnp.zeros_like(acc_ref)

    h = h_ref[...]
    act = jax.nn.silu(_dot(h, wg_ref[...])) * _dot(h, wu_ref[...])
    acc_ref[...] += _dot(act.astype(BF16), wd_ref[...])

    @pl.when(k == nk - 1)
    def _():
        r = x_ref[0] + gt_ref[0] * acc_ref[...]
        o_ref[0] = _rms(r, fg_ref[...]) if final else r


def _ffn(x, g, shift, scale, gate, w_g, w_u, w_d, final_g, final):
    bsz, n, d = x.shape
    dff = w_g.shape[1]
    tm = min(n, 1024)
    tf = 256
    nk = dff // tf
    vec = pl.BlockSpec((1, 1, d), lambda b, i, k: (b, 0, 0))
    row = pl.BlockSpec((1, d), lambda b, i, k: (0, 0))
    return pl.pallas_call(
        functools.partial(_ffn_kernel, nk=nk, final=final),
        grid=(bsz, n // tm, nk),
        in_specs=[
            pl.BlockSpec((1, tm, d), lambda b, i, k: (b, i, 0)),
            row, vec, vec, vec,
            pl.BlockSpec((d, tf), lambda b, i, k: (0, k)),
            pl.BlockSpec((d, tf), lambda b, i, k: (0, k)),
            pl.BlockSpec((tf, d), lambda b, i, k: (k, 0)),
            row,
        ],
        out_specs=pl.BlockSpec((1, tm, d), lambda b, i, k: (b, i, 0)),
        out_shape=jax.ShapeDtypeStruct((bsz, n, d), F32),
        scratch_shapes=[pltpu.VMEM((tm, d), BF16), pltpu.VMEM((tm, d), F32)],
        compiler_params=_cparams(("parallel", "parallel", "arbitrary")),
        name="ffn",
    )(x, g, shift, scale, gate, w_g, w_u, w_d, final_g)


def _rope_tables(n_tok):
    rows = n_tok // GRID_W
    row = jnp.broadcast_to(jnp.arange(rows, dtype=F32)[:, None], (rows, GRID_W)).reshape(-1)
    col = jnp.broadcast_to(jnp.arange(GRID_W, dtype=F32)[None, :], (rows, GRID_W)).reshape(-1)
    n_freq = MLA_ROPE // 4
    inv = ROPE_BASE ** (-jnp.arange(n_freq, dtype=F32) / n_freq)
    ang = jnp.concatenate([row[:, None] * inv, col[:, None] * inv], axis=-1)
    cos, sin = jnp.cos(ang), jnp.sin(ang)
    pad = jnp.zeros((n_tok, HEAD_W - MLA_ROPE), F32)
    return jnp.concatenate([cos, cos, pad], axis=-1), jnp.concatenate([-sin, sin, pad], axis=-1)


def _identity_rope_tables(n_tok):
    one = jnp.ones((n_tok, MLA_ROPE), F32)
    pad = jnp.zeros((n_tok, HEAD_W - MLA_ROPE), F32)
    return jnp.concatenate([one, pad], axis=-1), jnp.zeros((n_tok, HEAD_W), F32)


def _layout_w_in(w):
    kr = w[:, OFF_KR:OFF_S5]
    x1, x2 = kr[:, 0::2], kr[:, 1::2]
    pad = jnp.zeros((w.shape[0], HEAD_W - 3 * (MLA_ROPE // 2)), w.dtype)
    return jnp.concatenate([w[:, OFF_GATE:], w[:, OFF_HY:OFF_GATE], w[:, OFF_S5:OFF_HY],
                            w[:, OFF_CQ:OFF_CKV], w[:, OFF_CKV:OFF_KR], x1, x2, x1, pad], axis=1).astype(BF16)


def _layout_w_uq(w):
    w = w.reshape(MLA_Q_LORA, MLA_HEADS, MLA_NOPE + MLA_ROPE)
    nope, rope = w[..., :MLA_NOPE], w[..., MLA_NOPE:]
    x1, x2 = rope[..., 0::2], rope[..., 1::2]
    z32 = jnp.zeros_like(rope)
    wq = jnp.concatenate([x1, x2, z32, nope], axis=-1)
    wqs = jnp.concatenate([x2, x1, z32, jnp.zeros_like(nope)], axis=-1)
    shape = (MLA_Q_LORA, MLA_HEADS * HEAD_W)
    return wq.reshape(shape).astype(BF16), wqs.reshape(shape).astype(BF16)


def _layout_w_ukv(w):
    w = w.reshape(MLA_KV_LORA, MLA_HEADS, MLA_NOPE + MLA_V)
    nope, val = w[..., :MLA_NOPE], w[..., MLA_NOPE:]
    wk = jnp.concatenate([jnp.zeros_like(nope), nope], axis=-1).reshape(MLA_KV_LORA, MLA_HEADS * HEAD_W)
    wvt = jnp.concatenate([val, jnp.zeros_like(val[..., :VT_ROWS - MLA_V])], axis=-1).reshape(
        MLA_KV_LORA, MLA_HEADS * VT_ROWS).T
    return wk.astype(BF16), wvt.astype(BF16)


def kernel(x, c, ctx, c_ctx, ada_w, ada_b, norm_mix, w_in, mla_q_norm, mla_w_uq, mla_kv_norm, mla_w_ukv,
           s5_lam_re, s5_lam_im, s5_log_dt, s5_b_re, s5_b_im, s5_c_re, s5_c_im, s5_d, s5_w_glu,
           hy_conv_w, hy_conv_b, hy_f_w1, hy_f_b1, hy_f_w2, hy_f_b2, hy_f_w3, hy_f_freq, hy_bias,
           w_branch_mla, w_branch_s5, w_branch_hy, w_out, norm_ffn, ffn_w_gu, ffn_w_down, final_norm):
    bsz, n_tok, d = x.shape
    n_ctx = ctx.shape[1]
    depth = ada_w.shape[0]
    rope_x = _rope_tables(n_tok)
    rope_c = _identity_rope_tables(n_ctx)
    h_zero = jnp.zeros((bsz, 2, 4, S5_HSTATE), F32)

    assert bsz < ADA_ROWS, "the conditioning rows of the batch and the context must fit one row tile"
    cond = jnp.zeros((ADA_ROWS, d), F32).at[:bsz].set(c).at[bsz].set(c_ctx)
    for i in range(depth):
        last = i == depth - 1
        mods = _ada_modulation(cond, ada_w[i], ada_b[i][None])
        mx = mods[:bsz].reshape(bsz, N_MOD, 1, d)
        mc = jnp.broadcast_to(mods[bsz].reshape(1, N_MOD, 1, d), (bsz, N_MOD, 1, d))
        w_z = _layout_w_in(w_in[i])
        wq, wqs = _layout_w_uq(mla_w_uq[i])
        wk, wvt = _layout_w_ukv(mla_w_ukv[i])
        qn, kvn = mla_q_norm[i][None], mla_kv_norm[i][None]
        s5_tabs = _s5_tables(s5_lam_re[i], s5_lam_im[i], s5_log_dt[i], s5_b_re[i], s5_b_im[i],
                             s5_c_re[i], s5_c_im[i])
        fparams = (hy_f_w1[i], hy_f_b1[i], hy_f_w2[i], hy_f_b2[i], hy_f_w3[i], hy_f_freq[i])
        merge_w = (s5_d[i][None], s5_w_glu[i].astype(BF16),
                   w_branch_mla[i].astype(BF16),
                   w_branch_s5[i].astype(BF16), w_branch_hy[i].astype(BF16), w_out[i].astype(BF16))
        w_g = ffn_w_gu[i][:, :D_FF].astype(BF16)
        w_u = ffn_w_gu[i][:, D_FF:].astype(BF16)
        w_d = ffn_w_down[i].astype(BF16)
        g_mix, g_ffn = norm_mix[i][None], norm_ffn[i][None]

        zx = _proj_in(x, g_mix, mx[:, 0], mx[:, 1], w_z)
        zc = _proj_in(ctx, g_mix, mc[:, 0], mc[:, 1], w_z)

        n_keys = n_tok + n_ctx
        q_x, k_all, vt_all = _mla_prep(zx, qn, kvn, wq, wqs, wk, wvt, *rope_x, n_keys, 0)
        q_c, k_all, vt_all = _mla_prep(zc, qn, kvn, wq, wqs, wk, wvt, *rope_c, n_keys, n_tok, (k_all, vt_all))
        a_x = _attention(q_x, k_all, vt_all, n_keys)

        *y5_c, finals = _s5_scan(zc, h_zero, s5_tabs)
        *y5_x, _ = _s5_scan(zx, finals, s5_tabs)

        v_x, g1_x, g2_x = _short_conv(zx, hy_conv_w[i], hy_conv_b[i][None])
        e_x = _hyena_mixer(v_x, (g1_x, g2_x), _hyena_filters(n_tok, *fparams), hy_bias[i])

        x = _merge(x, zx, a_x, y5_x, e_x, *merge_w, mx[:, 2])
        x = _ffn(x, g_ffn, mx[:, 3], mx[:, 4], mx[:, 5], w_g, w_u, w_d, final_norm[None], last)

        if not last:
            a_c = _attention(q_c, k_all, vt_all, n_ctx, n_tok // n_ctx)
            v_c, g1_c, g2_c = _short_conv(zc, hy_conv_w[i], hy_conv_b[i][None])
            e_c = _hyena_mixer(v_c, (g1_c, g2_c), _hyena_filters(n_ctx, *fparams), hy_bias[i])
            ctx = _merge(ctx, zc, a_c, y5_c, e_c, *merge_w, mc[:, 2])
            ctx = _ffn(ctx, g_ffn, mc[:, 3], mc[:, 4], mc[:, 5], w_g, w_u, w_d, final_norm[None], False)
    return x
```

```python
import functools
import math

import jax
import jax.numpy as jnp
from jax import lax
from jax.experimental import pallas as pl
from jax.experimental.pallas import tpu as pltpu

F32 = jnp.float32
BF16 = jnp.bfloat16
HIGHEST = lax.Precision.HIGHEST

D_MODEL = 1024
GRID_W = 64
NORM_EPS = 1e-6
N_MOD = 6

MLA_HEADS = 8
MLA_NOPE = 64
MLA_ROPE = 32
MLA_V = 64
MLA_Q_LORA = 256
MLA_KV_LORA = 128
ROPE_BASE = 10000.0
HEAD_W = 128
MLA_PREP_ROWS = 1024
VT_ROWS = 80
QK_SCALE_LOG2 = (MLA_NOPE + MLA_ROPE) ** -0.5 * math.log2(math.e)
QK_HALF_SCALE = QK_SCALE_LOG2 ** 0.5
QK_DTYPE = jnp.float8_e4m3fn
ATTN_SUB = 128
ATTN_LAG = 3
ATTN_UNROLL = 12

S5_WIDTH = 512
S5_GROUP = 16
S5_GROUPS = 32
S5_STATE = 64
S5_HALF = 256
S5_HSTATE = 1024
S5_CHUNK = 128
S5_CHUNKS_PER_STEP = 4

HY_WIDTH = 512
HY_ORDER = 2
HY_BANDS = 16
HY_POS_DIM = 1 + 2 * HY_BANDS
HY_POS_PAD = 64
HY_FILTER_HIDDEN = 64
HY_FILTER_OUT = HY_ORDER * 2 * HY_WIDTH
HY_DECAY_TARGET = 1e-2
HY_DECAY_SHORT = 0.3
HY_DECAY_LONG = 1.5
HY_DECAY_SHIFT = 0.05

D_FF = 2816

OFF_CQ = 0
OFF_CKV = OFF_CQ + MLA_Q_LORA
OFF_KR = OFF_CKV + MLA_KV_LORA
OFF_S5 = OFF_KR + MLA_ROPE
OFF_HY = OFF_S5 + S5_WIDTH
OFF_GATE = OFF_HY + 3 * HY_WIDTH

Z_GATE = 0
Z_HY = 3072
Z_S5 = 4608
Z_MLA = 5120
Z_WIDTH = 5632

VMEM_LIMIT_V7X = 52 * 1024 * 1024


def _cparams(sem, flags=None):
    return pltpu.CompilerParams(dimension_semantics=sem, vmem_limit_bytes=VMEM_LIMIT_V7X, flags=flags)


def _dot(a, b):
    return jnp.dot(a, b, preferred_element_type=F32)


def _rms(x, g):
    return x * lax.rsqrt(jnp.mean(x * x, axis=-1, keepdims=True) + NORM_EPS) * g


def _norm_mod(x, g, shift, scale):
    return _rms(x, g) * (1.0 + scale) + shift


ADA_ROWS = 8


def _ada_kernel(c_ref, w_ref, b_ref, o_ref):
    c = c_ref[...]
    h = c * jax.nn.sigmoid(c)
    o_ref[...] = jnp.dot(h, w_ref[...], precision=HIGHEST, preferred_element_type=F32) + b_ref[...]


def _ada_modulation(cond, w, b):
    rows, d = cond.shape
    nout = w.shape[1]
    tn = nout // 4
    return pl.pallas_call(
        _ada_kernel,
        grid=(nout // tn,),
        in_specs=[
            pl.BlockSpec((rows, d), lambda j: (0, 0)),
            pl.BlockSpec((d, tn), lambda j: (0, j)),
            pl.BlockSpec((1, tn), lambda j: (0, j)),
        ],
        out_specs=pl.BlockSpec((rows, tn), lambda j: (0, j)),
        out_shape=jax.ShapeDtypeStruct((rows, nout), F32),
        compiler_params=_cparams(("parallel",)),
        name="ada_modulation",
    )(cond, w, b)


def _proj_kernel(x_ref, g_ref, sh_ref, sc_ref, w_ref, o_ref, h_ref):
    @pl.when(pl.program_id(2) == 0)
    def _():
        h_ref[...] = _norm_mod(x_ref[0], g_ref[...], sh_ref[0], sc_ref[0]).astype(BF16)

    o_ref[0] = _dot(h_ref[...], w_ref[...]).astype(BF16)


def _proj_in(x, g, shift, scale, w):
    bsz, n, d = x.shape
    nz = w.shape[1]
    tm = min(n, 2048)
    tn = 512
    return pl.pallas_call(
        _proj_kernel,
        grid=(bsz, n // tm, nz // tn),
        in_specs=[
            pl.BlockSpec((1, tm, d), lambda b, i, j: (b, i, 0)),
            pl.BlockSpec((1, d), lambda b, i, j: (0, 0)),
            pl.BlockSpec((1, 1, d), lambda b, i, j: (b, 0, 0)),
            pl.BlockSpec((1, 1, d), lambda b, i, j: (b, 0, 0)),
            pl.BlockSpec((d, tn), lambda b, i, j: (0, j)),
        ],
        out_specs=pl.BlockSpec((1, tm, tn), lambda b, i, j: (b, i, j)),
        out_shape=jax.ShapeDtypeStruct((bsz, n, nz), BF16),
        scratch_shapes=[pltpu.VMEM((tm, d), BF16)],
        compiler_params=_cparams(("parallel", "parallel", "arbitrary")),
        name="proj_in",
    )(x, g, shift, scale, w)


def _to_qk_dtype(x):
    lim = float(jnp.finfo(QK_DTYPE).max)
    return jnp.clip(x, -lim, lim).astype(QK_DTYPE)


def _mla_prep_kernel(z_ref, qn_ref, kvn_ref, wq_ref, wqs_ref, wk_ref, wvt_ref, c_ref, s_ref, *rest):
    q_ref, k_ref, vt_ref = rest[-3:]
    z = z_ref[0].astype(F32)
    hq = _rms(z[:, :MLA_Q_LORA], qn_ref[...]).astype(BF16)
    hkv = _rms(z[:, MLA_Q_LORA:MLA_Q_LORA + MLA_KV_LORA], kvn_ref[...]).astype(BF16)
    krb = z[:, MLA_Q_LORA + MLA_KV_LORA:]
    qa = _dot(hq, wq_ref[...])
    qb = _dot(hq, wqs_ref[...])
    kn = _dot(hkv, wk_ref[...])
    vt = lax.dot_general(wvt_ref[...], hkv, (((1,), (1,)), ((), ())), preferred_element_type=F32)
    ck = c_ref[...]
    sn = s_ref[...]
    lane = lax.broadcasted_iota(jnp.int32, ck.shape, 1)
    cq = jnp.where(lane < MLA_ROPE, ck, 1.0)
    kr = krb * ck + pltpu.roll(krb, HEAD_W - MLA_ROPE // 2, 1) * sn
    row = lax.broadcasted_iota(jnp.int32, (VT_ROWS, z.shape[0]), 0)
    for h in range(MLA_HEADS):
        sl = slice(h * HEAD_W, (h + 1) * HEAD_W)
        q_ref[0, h] = _to_qk_dtype((qa[:, sl] * cq + qb[:, sl] * sn) * QK_HALF_SCALE)
        k_ref[0, h] = _to_qk_dtype((kr + kn[:, sl]) * QK_HALF_SCALE)
        vt_ref[0, h] = jnp.where(row == MLA_V, 1.0, vt[h * VT_ROWS:(h + 1) * VT_ROWS, :]).astype(BF16)


def _mla_prep(z, qn, kvn, wq, wqs, wk, wvt, ctab, stab, n_keys, key_off, kv_bufs=None):
    bsz, n, _ = z.shape
    tm = min(n, MLA_PREP_ROWS)
    assert key_off % tm == 0, "the key offset must be a whole number of row tiles"
    ob = key_off // tm
    hw = MLA_HEADS * HEAD_W
    zb = Z_MLA // 512
    full = lambda shape: pl.BlockSpec(shape, lambda b, i: (0,) * len(shape))
    in_specs = [
        pl.BlockSpec((1, tm, 512), lambda b, i: (b, i, zb)),
        full((1, MLA_Q_LORA)),
        full((1, MLA_KV_LORA)),
        full((MLA_Q_LORA, hw)),
        full((MLA_Q_LORA, hw)),
        full((MLA_KV_LORA, hw)),
        full((MLA_HEADS * VT_ROWS, MLA_KV_LORA)),
        pl.BlockSpec((tm, HEAD_W), lambda b, i: (i, 0)),
        pl.BlockSpec((tm, HEAD_W), lambda b, i: (i, 0)),
    ]
    args = [z, qn, kvn, wq, wqs, wk, wvt, ctab, stab]
    aliases = {}
    if kv_bufs is not None:
        aliases = {len(args): 1, len(args) + 1: 2}
        in_specs += [pl.BlockSpec(memory_space=pl.ANY)] * 2
        args += list(kv_bufs)
    return pl.pallas_call(
        _mla_prep_kernel,
        grid=(bsz, n // tm),
        in_specs=in_specs,
        out_specs=[
            pl.BlockSpec((1, MLA_HEADS, tm, HEAD_W), lambda b, i: (b, 0, i, 0)),
            pl.BlockSpec((1, MLA_HEADS, tm, HEAD_W), lambda b, i: (b, 0, i + ob, 0)),
            pl.BlockSpec((1, MLA_HEADS, VT_ROWS, tm), lambda b, i: (b, 0, 0, i + ob)),
        ],
        out_shape=[
            jax.ShapeDtypeStruct((bsz, MLA_HEADS, n, HEAD_W), QK_DTYPE),
            jax.ShapeDtypeStruct((bsz, MLA_HEADS, n_keys, HEAD_W), QK_DTYPE),
            jax.ShapeDtypeStruct((bsz, MLA_HEADS, VT_ROWS, n_keys), BF16),
        ],
        input_output_aliases=aliases,
        compiler_params=_cparams(("parallel", "parallel")),
        name="mla_prep",
    )(*args)


def _attn_kernel(q_ref, k_ref, vt_ref, o_ref, s0_ref, s1_ref, s2_ref, p0_ref, p1_ref, p2_ref, acc_ref, *, tk, nkc):
    q = q_ref[0, 0]
    tq = q.shape[0]
    s_refs = (s0_ref, s1_ref, s2_ref)
    p_refs = (p0_ref, p1_ref, p2_ref)

    ts = min(tk, ATTN_SUB)
    subs = [slice(r, r + ts) for r in range(0, tk, ts)]

    def scores_sub(j, s_ref, sub, cmax):
        off = pl.multiple_of(j * tk + sub.start, ts)
        s = lax.dot_general(k_ref[0, 0, pl.ds(off, ts), :], q, (((1,), (1,)), ((), ())),
                            preferred_element_type=F32)
        s_ref[sub, :] = s
        cm = jnp.max(s, axis=0, keepdims=True)
        return cm if cmax is None else jnp.maximum(cmax, cm)

    def softmax_sub(slot, sub, mn):
        p_refs[slot][sub, :] = jnp.exp2(s_refs[slot][sub, :] - mn).astype(BF16)

    def step(i, slot, carry, do_softmax, do_scores):
        cm, m, alpha = carry
        mn = jnp.maximum(m, cm)
        off = pl.multiple_of(i * tk, tk)
        acc_ref[...] = alpha * acc_ref[...] + _dot(vt_ref[0, 0, :, pl.ds(off, tk)], p_refs[slot][...])
        cm_new, cms = None, []
        for r, sub in enumerate(subs):
            mn_sub = mn
            if do_scores:
                cm_new = scores_sub(i + 2, s_refs[(slot + 2) % 3], sub, cm_new)
                cms.append(cm_new)
                if r >= ATTN_LAG:
                    mn_sub = jnp.maximum(mn, cms[r - ATTN_LAG] - 1e30)
            if do_softmax:
                softmax_sub((slot + 1) % 3, sub, mn_sub)
        if do_softmax:
            alpha, m = jnp.exp2(m - mn), mn
        return (cm_new if do_scores else cm), m, alpha

    m = jnp.full((1, tq), -1e30, F32)
    acc_ref[...] = jnp.zeros_like(acc_ref)
    cm, cm_next = None, None
    for sub in subs:
        cm = scores_sub(0, s0_ref, sub, cm)
        if nkc > 1:
            cm_next = scores_sub(1, s1_ref, sub, cm_next)
    mn = jnp.maximum(m, cm)
    for sub in subs:
        softmax_sub(0, sub, mn)
    carry = (cm_next if nkc > 1 else cm, mn, jnp.exp2(m - mn))

    def body(t, carry):
        for r in range(ATTN_UNROLL):
            carry = step(ATTN_UNROLL * t + r, r % 3, carry, True, True)
        return carry

    nloop = max(nkc - 2, 0) // ATTN_UNROLL
    carry = lax.fori_loop(0, nloop, body, carry)
    for i in range(ATTN_UNROLL * nloop, nkc):
        carry = step(i, i % 3, carry, i + 1 < nkc, i + 2 < nkc)
    acc = acc_ref[...]
    out = acc * (1.0 / acc[MLA_V:MLA_V + 1, :])
    out = jnp.concatenate([out, jnp.zeros((HEAD_W - VT_ROWS, tq), F32)], axis=0).T
    o_ref[0, 0] = out[:, :MLA_V].astype(BF16)


def _kv_chunk(nk):
    for tk in (640, 512, 256, 128):
        if nk % tk == 0:
            return tk
    raise ValueError(f"unsupported key count {nk}")


def _attention(q, k, vt, nk, key_block=0):
    bsz, nh, nq, _ = q.shape
    tq = min(nq, 1024)
    tk = _kv_chunk(nk)
    return pl.pallas_call(
        functools.partial(_attn_kernel, tk=tk, nkc=nk // tk),
        grid=(bsz, nh, nq // tq),
        in_specs=[
            pl.BlockSpec((1, 1, tq, HEAD_W), lambda b, h, i: (b, h, i, 0)),
            pl.BlockSpec((1, 1, nk, HEAD_W), lambda b, h, i: (b, h, key_block, 0)),
            pl.BlockSpec((1, 1, VT_ROWS, nk), lambda b, h, i: (b, h, 0, key_block)),
        ],
        out_specs=pl.BlockSpec((1, 1, tq, MLA_V), lambda b, h, i: (b, h, i, 0)),
        out_shape=jax.ShapeDtypeStruct((bsz, nh, nq, MLA_V), BF16),
        scratch_shapes=[pltpu.VMEM((tk, tq), F32)] * 3 + [pltpu.VMEM((tk, tq), BF16)] * 3
        + [pltpu.VMEM((VT_ROWS, tq), F32)],
        compiler_params=_cparams(("parallel", "parallel", "arbitrary")),
        name="attention",
    )(q, k, vt)


def _cmul(ar, ai, br, bi):
    return ar * br - ai * bi, ar * bi + ai * br


def _s5_kernel(u_ref, h0_ref, bm_ref, cm_ref, wn_ref, wp_ref, l1_ref, tri_ref, y_ref, hf_ref,
               carry_ref, h0cat_ref, h1cat_ref, *, tc, nr, ns, reverse):
    i = pl.program_id(1)

    @pl.when(i == 0)
    def _():
        carry_ref[...] = h0_ref[0]

    u = u_ref[0]
    tri = tri_ref[...]
    order = range(nr - 1, -1, -1) if reverse else range(nr)
    edge = 0 if reverse else tc - 1
    hcat_refs = (h0cat_ref, h1cat_ref)
    bus = [_dot(u[:, hf * S5_HALF:(hf + 1) * S5_HALF], bm_ref[hf]) for hf in range(2)]
    carries = [(carry_ref[2 * hf:2 * hf + 1, :], carry_ref[2 * hf + 1:2 * hf + 2, :]) for hf in range(2)]
    for c in order:
        rows = slice(c * tc, (c + 1) * tc)
        for hf in range(2):
            bu = bus[hf]
            xr, xi = _cmul(wn_ref[hf, 0], wn_ref[hf, 1], bu[rows, :S5_HSTATE], bu[rows, S5_HSTATE:])
            s = _dot(tri, jnp.concatenate([xr, xi], axis=1).astype(BF16))
            cr, ci = _cmul(l1_ref[hf, 0], l1_ref[hf, 1], *carries[hf])
            hr, hi = _cmul(wp_ref[hf, 0], wp_ref[hf, 1], s[:, :S5_HSTATE] + cr, s[:, S5_HSTATE:] + ci)
            hcat_refs[hf][rows, :S5_HSTATE] = hr.astype(BF16)
            hcat_refs[hf][rows, S5_HSTATE:] = hi.astype(BF16)
            carries[hf] = (hr[edge:edge + 1], hi[edge:edge + 1])
    for hf in range(2):
        carry_ref[2 * hf:2 * hf + 1, :] = carries[hf][0]
        carry_ref[2 * hf + 1:2 * hf + 2, :] = carries[hf][1]
    y_ref[0] = jnp.concatenate([_dot(hcat_refs[hf][...], cm_ref[hf]) for hf in range(2)], axis=1)

    @pl.when(i == ns - 1)
    def _():
        hf_ref[0] = carry_ref[...]


def _s5_scan_dir(z, h0, tabs, reverse):
    bm, cm, wn, wp, l1, tri = tabs
    bsz, n, _ = z.shape
    tc = S5_CHUNK
    nr = min(S5_CHUNKS_PER_STEP, n // tc)
    rows = nr * tc
    ns = n // rows
    zb = Z_S5 // S5_WIDTH
    blk = (lambda i: ns - 1 - i) if reverse else (lambda i: i)
    full = lambda shape: pl.BlockSpec(shape, lambda b, i: (0,) * len(shape))
    return pl.pallas_call(
        functools.partial(_s5_kernel, tc=tc, nr=nr, ns=ns, reverse=reverse),
        grid=(bsz, ns),
        in_specs=[
            pl.BlockSpec((1, rows, S5_WIDTH), lambda b, i: (b, blk(i), zb)),
            pl.BlockSpec((1, 4, S5_HSTATE), lambda b, i: (b, 0, 0)),
            full((2, S5_HALF, 2 * S5_HSTATE)),
            full((2, 2 * S5_HSTATE, S5_HALF)),
            full((2, 2, tc, S5_HSTATE)),
            full((2, 2, tc, S5_HSTATE)),
            full((2, 2, 1, S5_HSTATE)),
            full((tc, tc)),
        ],
        out_specs=[
            pl.BlockSpec((1, rows, S5_WIDTH), lambda b, i: (b, blk(i), 0)),
            pl.BlockSpec((1, 4, S5_HSTATE), lambda b, i: (b, 0, 0)),
        ],
        out_shape=[
            jax.ShapeDtypeStruct((bsz, n, S5_WIDTH), F32),
            jax.ShapeDtypeStruct((bsz, 4, S5_HSTATE), F32),
        ],
        scratch_shapes=[pltpu.VMEM((4, S5_HSTATE), F32),
                        pltpu.VMEM((rows, 2 * S5_HSTATE), BF16), pltpu.VMEM((rows, 2 * S5_HSTATE), BF16)],
        compiler_params=_cparams(("parallel", "arbitrary")),
        name="s5_scan",
    )(z, h0, bm, cm, wn, wp, l1, tri)


def _s5_scan(z, h0, tabs):
    outs = [_s5_scan_dir(z, h0[:, d], [t[d] for t in tabs], reverse=bool(d)) for d in range(2)]
    return outs[0][0], outs[1][0], jnp.stack([outs[0][1], outs[1][1]], axis=1)


def _s5_tables(lam_re, lam_im, log_dt, b_re, b_im, c_re, c_im):
    tc = S5_CHUNK
    dt = jnp.exp(log_dt)[..., None]
    zr, zi = lam_re * dt, lam_im * dt
    mag = jnp.exp(zr)
    lbr, lbi = mag * jnp.cos(zi), mag * jnp.sin(zi)
    den = lam_re * lam_re + lam_im * lam_im
    nr, ni = lbr - 1.0, lbi
    cfr = (nr * lam_re + ni * lam_im) / den
    cfi = (ni * lam_re - nr * lam_im) / den
    bbr = cfr[..., None] * b_re - cfi[..., None] * b_im
    bbi = cfr[..., None] * b_im + cfi[..., None] * b_re
    eye = jnp.eye(S5_GROUP, dtype=F32)

    def blockdiag_in(b):
        b = b.reshape(2, 2, S5_GROUP, S5_STATE, S5_GROUP)
        return jnp.einsum('gk,dhgpn->dhgnkp', eye, b).reshape(2, 2, S5_HALF, S5_HSTATE)

    def blockdiag_out(c):
        c = c.reshape(2, 2, S5_GROUP, S5_GROUP, S5_STATE)
        return jnp.einsum('gk,dhgnp->dhgpkn', eye, c).reshape(2, 2, S5_HSTATE, S5_HALF)

    bm = jnp.concatenate([blockdiag_in(bbr), blockdiag_in(bbi)], axis=-1).astype(BF16)
    cm = jnp.concatenate([blockdiag_out(c_re), blockdiag_out(-c_im)], axis=-2).astype(BF16)

    def powers(k):
        zr_ = zr.reshape(2, 2, 1, S5_HSTATE)
        zi_ = zi.reshape(2, 2, 1, S5_HSTATE)
        kk = k[:, None, :, None]
        m = jnp.exp(kk * zr_)
        return jnp.stack([m * jnp.cos(kk * zi_), m * jnp.sin(kk * zi_)], axis=2)

    mid = tc // 2
    t = jnp.arange(tc, dtype=F32) - mid
    wn = powers(jnp.stack([-t, -t[::-1]]))
    wp = powers(jnp.stack([t, t[::-1]]))
    l1 = powers(jnp.full((2, 1), mid + 1, F32))
    r = jnp.arange(tc)
    tri = jnp.stack([r[:, None] >= r[None, :], r[:, None] <= r[None, :]]).astype(BF16)
    return bm, cm, wn, wp, l1, tri


HALO = 16


def _sconv_kernel(z_ref, zp_ref, zn_ref, w_ref, b_ref, v_ref, g1_ref, g2_ref, *, nt):
    i = pl.program_id(1)
    z = z_ref[0].astype(F32)
    tm = z.shape[0]
    row = lax.broadcasted_iota(jnp.int32, z.shape, 0)
    prev = jnp.where(i > 0, zp_ref[0, HALO - 1:HALO, :].astype(F32), 0.0)
    nxt = jnp.where(i < nt - 1, zn_ref[0, 0:1, :].astype(F32), 0.0)
    up = jnp.where(row == 0, prev, pltpu.roll(z, 1, 0))
    dn = jnp.where(row == tm - 1, nxt, pltpu.roll(z, tm - 1, 0))
    u = up * w_ref[0:1, :] + z * w_ref[1:2, :] + dn * w_ref[2:3, :] + b_ref[...]
    for k, o_ref in enumerate((v_ref, g1_ref, g2_ref)):
        o_ref[0] = u[:, k * HY_WIDTH:(k + 1) * HY_WIDTH]


def _short_conv(z, w, b):
    bsz, n, _ = z.shape
    cw = 3 * HY_WIDTH
    tm = min(n, 1024)
    nt = n // tm
    zb = Z_HY // cw
    rh = tm // HALO
    out = pl.BlockSpec((1, tm, HY_WIDTH), lambda b_, i: (b_, i, 0))
    return pl.pallas_call(
        functools.partial(_sconv_kernel, nt=nt),
        grid=(bsz, nt),
        in_specs=[
            pl.BlockSpec((1, tm, cw), lambda b_, i: (b_, i, zb)),
            pl.BlockSpec((1, HALO, cw), lambda b_, i: (b_, jnp.maximum(i * rh - 1, 0), zb)),
            pl.BlockSpec((1, HALO, cw), lambda b_, i: (b_, jnp.minimum((i + 1) * rh, n // HALO - 1), zb)),
            pl.BlockSpec((3, cw), lambda b_, i: (0, 0)),
            pl.BlockSpec((1, cw), lambda b_, i: (0, 0)),
        ],
        out_specs=[out] * 3,
        out_shape=[jax.ShapeDtypeStruct((bsz, n, HY_WIDTH), F32)] * 3,
        compiler_params=_cparams(("parallel", "parallel")),
        name="short_conv",
    )(z, z, z, w, b)


def _filt_kernel(f_ref, w1_ref, b1_ref, w2_ref, b2_ref, w3h_ref, w3l_ref, fq_ref, dl_ref, o_ref, *, n_tok):
    z = f_ref[...]
    tm = z.shape[0]
    fq = fq_ref[...]
    hid = jnp.sin(fq * (jnp.dot(z, w1_ref[...], precision=HIGHEST, preferred_element_type=F32) + b1_ref[...]))
    hid = jnp.sin(fq * (jnp.dot(hid, w2_ref[...], precision=HIGHEST, preferred_element_type=F32) + b2_ref[...]))
    hid_hi = hid.astype(BF16)
    hid_lo = (hid - hid_hi.astype(F32)).astype(BF16)

    def filters(rows, d):
        wh, wl = w3h_ref[d], w3l_ref[d]
        f = _dot(hid_hi[rows], wh) + (_dot(hid_hi[rows], wl) + _dot(hid_lo[rows], wh))
        return f * (jnp.exp(-z[rows, 0:1] * dl_ref[d]) + HY_DECAY_SHIFT)

    i = pl.program_id(0)
    filt = filters(slice(None), jnp.where(i * tm >= n_tok, 1, 0))
    m = i * tm + lax.broadcasted_iota(jnp.int32, filt.shape, 0)
    filt = jnp.where(m == n_tok, 0.0, filt)
    for o in range(HY_ORDER):
        o_ref[o] = filt[:, o * HY_WIDTH:(o + 1) * HY_WIDTH]

    @pl.when(i == 0)
    def _():
        head = slice(0, HALO)
        lag0 = filters(head, 1)
        lag0 = jnp.where(lax.broadcasted_iota(jnp.int32, lag0.shape, 0) == 0, lag0, 0.0)
        for o in range(HY_ORDER):
            o_ref[o, head, :] += lag0[:, o * HY_WIDTH:(o + 1) * HY_WIDTH]


def _hyena_filter_feats(n_tok):
    m = jnp.arange(2 * n_tok)
    lag = jnp.where(m < n_tok, m, jnp.where(m > n_tok, 2 * n_tok - m, 0))
    t = (lag.astype(F32) / (n_tok - 1))[:, None]
    w = (2.0 * math.pi * lag.astype(F32) / n_tok)[:, None]
    bands = jnp.linspace(1e-4, HY_BANDS - 1, HY_BANDS, dtype=F32)[None, :]
    feats = jnp.concatenate([t, jnp.cos(bands * w), -jnp.sin(bands * w)], axis=-1)
    return jnp.pad(feats, ((0, 0), (0, HY_POS_PAD - HY_POS_DIM)))


def _hyena_filters(n_tok, w1, b1, w2, b2, w3, freq):
    feats = _hyena_filter_feats(n_tok)
    deltas = jnp.abs(jnp.linspace(math.log(HY_DECAY_TARGET) / HY_DECAY_SHORT,
                                  math.log(HY_DECAY_TARGET) / HY_DECAY_LONG, HY_FILTER_OUT, dtype=F32))[None, :]
    w1p = jnp.pad(w1, ((0, HY_POS_PAD - HY_POS_DIM), (0, 0)))

    def by_direction(a):
        a = a.reshape(a.shape[:-1] + (HY_ORDER, 2, HY_WIDTH))
        return jnp.moveaxis(a, -2, 0).reshape((2,) + a.shape[:-3] + (HY_ORDER * HY_WIDTH,))

    w3, deltas = by_direction(w3), by_direction(deltas)
    w3_hi = w3.astype(BF16)
    w3_lo = (w3 - w3_hi.astype(F32)).astype(BF16)
    n2 = 2 * n_tok
    tm = min(n_tok, 1024)
    full = lambda shape: pl.BlockSpec(shape, lambda i: (0,) * len(shape))
    hh = HY_FILTER_HIDDEN
    wd = HY_ORDER * HY_WIDTH
    return pl.pallas_call(
        functools.partial(_filt_kernel, n_tok=n_tok),
        grid=(n2 // tm,),
        in_specs=[
            pl.BlockSpec((tm, HY_POS_PAD), lambda i: (i, 0)),
            full((HY_POS_PAD, hh)), full((1, hh)), full((hh, hh)), full((1, hh)),
            full((2, hh, wd)), full((2, hh, wd)), full((1, hh)), full((2, 1, wd)),
        ],
        out_specs=pl.BlockSpec((HY_ORDER, tm, HY_WIDTH), lambda i: (0, i, 0)),
        out_shape=jax.ShapeDtypeStruct((HY_ORDER, n2, HY_WIDTH), F32),
        compiler_params=_cparams(("parallel",)),
        name="hyena_filter",
    )(feats, w1p, b1[None], w2, b2[None], w3_hi, w3_lo, freq[None], deltas)


def _fft_split(n_fft):
    if n_fft <= 1024:
        return n_fft, 1
    n1 = 1 << (int(math.log2(n_fft)) // 2)
    return n1, n_fft // n1


def _cis(num, den, sign):
    ang = (2.0 * math.pi / den) * num.astype(F32)
    return jnp.cos(ang), sign * jnp.sin(ang)


def _fft_tables(n_fft):
    n1, n2 = _fft_split(n_fft)
    a = jnp.arange(n1)
    f1r, f1i = _cis((a[:, None] * a[None, :]) % n1, n1, -1.0)
    s1 = jnp.concatenate([f1r, f1i], axis=0).astype(BF16)
    half = n1 // 2
    s3 = (jnp.concatenate([f1r[:half], -f1i[:half]], axis=0) / n_fft).astype(BF16)
    if n2 == 1:
        return s1, s3, None, None
    b = jnp.arange(n2)
    f2 = jnp.stack(_cis((b[:, None] * b[None, :]) % n2, n2, -1.0))
    tw = jnp.stack(_cis((jnp.arange(n1)[:, None] * b[None, :]) % n_fft, n_fft, -1.0))
    cb = 8
    ft, it = pl.pallas_call(
        functools.partial(_twiddle_kernel, cb=cb),
        grid=(n1 // cb,),
        in_specs=[pl.BlockSpec((2, n2, n2), lambda c: (0, 0, 0)), pl.BlockSpec((2, cb, n2), lambda c: (0, c, 0))],
        out_specs=[pl.BlockSpec((cb, 2 * n2, n2), lambda c: (c, 0, 0))] * 2,
        out_shape=[jax.ShapeDtypeStruct((n1, 2 * n2, n2), BF16)] * 2,
        compiler_params=_cparams(("parallel",)),
        name="fft_twiddles",
    )(f2, tw)
    return s1, s3, ft, it


def _twiddle_kernel(f2_ref, tw_ref, ft_ref, it_ref, *, cb):
    for c in range(cb):
        fr, fi = _cmul(f2_ref[0], f2_ref[1], tw_ref[0, c:c + 1, :], tw_ref[1, c:c + 1, :])
        ft_ref[c] = jnp.concatenate([fr, fi], axis=0).astype(BF16)
        it_ref[c] = jnp.concatenate([fr.T, -fi.T], axis=0).astype(BF16)


def _stacked_cdot(s, xr, xi, m):
    p = _dot(s, xr.astype(BF16))
    if xi is None:
        return p[:m], p[m:]
    q = _dot(s, xi.astype(BF16))
    return p[:m] - q[m:], q[:m] + p[m:]


def _fft_a_kernel(*refs, parts, n1, has_g):
    if has_g:
        x_ref, s_ref, g_ref, y_ref = refs
    else:
        x_ref, s_ref, y_ref = refs
    yr, yi = _stacked_cdot(s_ref[...], x_ref[0], x_ref[1] if parts == 2 else None, n1)
    if has_g:
        yr, yi = _cmul(yr, yi, g_ref[0], g_ref[1])
    y_ref[0] = yr
    y_ref[1] = yi


def _fft_a(x, s1, g=None):
    parts, a_rows, cols = x.shape
    n1 = s1.shape[0] // 2
    tc = min(cols, 2048)
    in_specs = [
        pl.BlockSpec((parts, a_rows, tc), lambda j: (0, 0, j)),
        pl.BlockSpec((2 * n1, a_rows), lambda j: (0, 0)),
    ]
    args = [x, s1[:, :a_rows]]
    if g is not None:
        in_specs.append(pl.BlockSpec((2, n1, tc), lambda j: (0, 0, j)))
        args.append(g)
    return pl.pallas_call(
        functools.partial(_fft_a_kernel, parts=parts, n1=n1, has_g=g is not None),
        grid=(cols // tc,),
        in_specs=in_specs,
        out_specs=pl.BlockSpec((2, n1, tc), lambda j: (0, 0, j)),
        out_shape=jax.ShapeDtypeStruct((2, n1, cols), F32),
        compiler_params=_cparams(("parallel",)),
        name="fft_stage_a",
    )(*args)


FFT_CB = 4


def _fft_b_kernel(*refs, n2, inverse):
    if inverse:
        y_ref, ft_ref, it_ref, g_ref, o_ref = refs
    else:
        y_ref, ft_ref, o_ref = refs
    for c in range(FFT_CB):
        xr, xi = _stacked_cdot(ft_ref[c], y_ref[0, c], y_ref[1, c], n2)
        if inverse:
            zr, zi = _cmul(xr, xi, g_ref[0, c], g_ref[1, c])
            xr, xi = _stacked_cdot(it_ref[c], zr, zi, n2)
        o_ref[0, c] = xr.astype(o_ref.dtype)
        o_ref[1, c] = xi.astype(o_ref.dtype)


def _fft_b(y, ft, it=None, g=None):
    _, n1, n2, ch = y.shape
    inverse = it is not None
    blk = pl.BlockSpec((2, FFT_CB, n2, ch), lambda c: (0, c, 0, 0))
    mat = pl.BlockSpec((FFT_CB, 2 * n2, n2), lambda c: (c, 0, 0))
    in_specs, args = [blk, mat], [y, ft]
    if inverse:
        in_specs += [mat, blk]
        args += [it, g]
    return pl.pallas_call(
        functools.partial(_fft_b_kernel, n2=n2, inverse=inverse),
        grid=(n1 // FFT_CB,),
        in_specs=in_specs,
        out_specs=blk,
        out_shape=jax.ShapeDtypeStruct(y.shape, BF16 if inverse else F32),
        compiler_params=_cparams(("parallel",)),
        name="fft_stage_b",
    )(*args)


def _fft_c_kernel(y_ref, s_ref, x_ref, g_ref, b_ref, o_ref, *, half):
    cr, ci = _stacked_cdot(s_ref[...], y_ref[0], y_ref[1], half)
    bias = b_ref[...]
    o_ref[0] = g_ref[0] * (cr + bias * x_ref[0])
    o_ref[1] = g_ref[1] * (ci + bias * x_ref[1])


def _fft_c(y, s3, x, gate, bias_cols):
    _, n1, cols = y.shape
    half = n1 // 2
    tc = min(cols, 2048)
    io = pl.BlockSpec((2, half, tc), lambda j: (0, 0, j))
    return pl.pallas_call(
        functools.partial(_fft_c_kernel, half=half),
        grid=(cols // tc,),
        in_specs=[
            pl.BlockSpec((2, n1, tc), lambda j: (0, 0, j)),
            pl.BlockSpec((n1, n1), lambda j: (0, 0)),
            io, io,
            pl.BlockSpec((1, tc), lambda j: (0, j)),
        ],
        out_specs=io,
        out_shape=jax.ShapeDtypeStruct((2, half, cols), F32),
        compiler_params=_cparams(("parallel",)),
        name="fft_stage_c",
    )(y, s3, x, gate, bias_cols)


FFT_BT = 16


def _to_fine_major(x):
    return pltpu.einshape("abc->bac", x)


def _fft_a4_kernel(x_ref, s_ref, y_ref, *, parts, n1):
    s = s_ref[...]
    xr = _to_fine_major(x_ref[0].astype(BF16))
    xi = _to_fine_major(x_ref[1].astype(BF16)) if parts == 2 else None
    out = [_stacked_cdot(s, xr[b], None if xi is None else xi[b], n1) for b in range(FFT_BT)]
    y_ref[0] = pltpu.einshape("bac->abc", jnp.stack([o[0] for o in out])).astype(BF16)
    y_ref[1] = pltpu.einshape("bac->abc", jnp.stack([o[1] for o in out])).astype(BF16)


def _fft_a4(x, s1):
    parts, a_rows, n2, ch = x.shape
    n1 = s1.shape[0] // 2
    return pl.pallas_call(
        functools.partial(_fft_a4_kernel, parts=parts, n1=n1),
        grid=(n2 // FFT_BT,),
        in_specs=[
            pl.BlockSpec((parts, a_rows, FFT_BT, ch), lambda j: (0, 0, j, 0)),
            pl.BlockSpec((2 * n1, a_rows), lambda j: (0, 0)),
        ],
        out_specs=pl.BlockSpec((2, n1, FFT_BT, ch), lambda j: (0, 0, j, 0)),
        out_shape=jax.ShapeDtypeStruct((2, n1, n2, ch), BF16),
        compiler_params=_cparams(("parallel",)),
        name="fft_stage_a",
    )(x, s1[:, :a_rows])


def _fft_c4_kernel(y_ref, s_ref, x_ref, g_ref, b_ref, o_ref, c_ref, *, half):
    s = s_ref[...]
    yr = _to_fine_major(y_ref[0].astype(BF16))
    yi = _to_fine_major(y_ref[1].astype(BF16))
    out = [_stacked_cdot(s, yr[b], yi[b], half) for b in range(FFT_BT)]
    c_ref[0] = pltpu.einshape("bac->abc", jnp.stack([o[0] for o in out]))
    c_ref[1] = pltpu.einshape("bac->abc", jnp.stack([o[1] for o in out]))
    o_ref[...] = g_ref[...] * (c_ref[...] + b_ref[...][None, None] * x_ref[...])


def _fft_c4(y, s3, x, gate, bias):
    _, n1, n2, ch = y.shape
    half = n1 // 2
    io = pl.BlockSpec((2, half, FFT_BT, ch), lambda j: (0, 0, j, 0))
    return pl.pallas_call(
        functools.partial(_fft_c4_kernel, half=half),
        grid=(n2 // FFT_BT,),
        in_specs=[
            pl.BlockSpec((2, n1, FFT_BT, ch), lambda j: (0, 0, j, 0)),
            pl.BlockSpec((n1, n1), lambda j: (0, 0)),
            io, io,
            pl.BlockSpec((1, ch), lambda j: (0, 0)),
        ],
        out_specs=io,
        out_shape=jax.ShapeDtypeStruct((2, half, n2, ch), F32),
        scratch_shapes=[pltpu.VMEM((2, half, FFT_BT, ch), F32)],
        compiler_params=_cparams(("parallel",)),
        name="fft_stage_c",
    )(y, s3, x, gate, bias)


def _hyena_mixer(v, gates, gfilt, bias):
    bsz, n, ch = v.shape
    assert bsz == 2, "the complex packing of the long convolution pairs exactly two batch rows"
    n_fft = 2 * n
    n1, n2 = _fft_split(n_fft)
    s1, s3, ft, it = _fft_tables(n_fft)
    if n2 == 1:
        y = v
        for o in range(HY_ORDER):
            gspec = _fft_a(gfilt[o][None], s1)
            y = _fft_c(_fft_a(y, s1, gspec), s3, y, gates[o], bias[o][None])
        return y
    half = n1 // 2
    y = v.reshape(2, half, n2, ch)
    for o in range(HY_ORDER):
        gspec = _fft_b(_fft_a4(gfilt[o].reshape(1, n1, n2, ch), s1), ft)
        spec = _fft_b(_fft_a4(y, s1), ft, it, gspec)
        y = _fft_c4(spec, s3, y, gates[o].reshape(2, half, n2, ch), bias[o][None])
    return y.reshape(2, n, ch)


def _gelu_tanh(x):
    return 0.5 * x * (1.0 + jnp.tanh(math.sqrt(2.0 / math.pi) * (x + 0.044715 * (x * x * x))))


def _merge_kernel(x_ref, g0_ref, g1_ref, g2_ref, a_ref, yf_ref, yb_ref, u_ref, e_ref, d_ref,
                  wglu_ref, wm_ref, ws_ref, wh_ref, wo_ref, gt_ref, o_ref):
    att = _dot(jnp.concatenate([a_ref[0, h] for h in range(MLA_HEADS)], axis=-1), wm_ref[...])
    y = yf_ref[0] + yb_ref[0] + d_ref[...] * u_ref[0].astype(F32)
    gy = _gelu_tanh(y)
    s5 = gy * jax.nn.sigmoid(_dot(gy.astype(BF16), wglu_ref[...]))
    s5 = _dot(s5.astype(BF16), ws_ref[...])
    hy = _dot(e_ref[0].astype(BF16), wh_ref[...])
    merged = jax.nn.sigmoid(g0_ref[0].astype(F32)) * att
    merged = merged + jax.nn.sigmoid(g1_ref[0].astype(F32)) * s5
    merged = merged + jax.nn.sigmoid(g2_ref[0].astype(F32)) * hy
    o_ref[0] = x_ref[0] + gt_ref[0] * _dot(merged.astype(BF16), wo_ref[...])


def _merge(x, z, att, y5, e, s5_d, w_glu, w_mla, w_s5, w_hy, w_out, gate):
    bsz, n, d = x.shape
    tm = min(n, 512)
    full = lambda shape: pl.BlockSpec(shape, lambda b, i: (0,) * len(shape))
    zs5 = Z_S5 // S5_WIDTH
    return pl.pallas_call(
        _merge_kernel,
        grid=(bsz, n // tm),
        in_specs=[
            pl.BlockSpec((1, tm, d), lambda b, i: (b, i, 0)),
            pl.BlockSpec((1, tm, d), lambda b, i: (b, i, 0)),
            pl.BlockSpec((1, tm, d), lambda b, i: (b, i, 1)),
            pl.BlockSpec((1, tm, d), lambda b, i: (b, i, 2)),
            pl.BlockSpec((1, MLA_HEADS, tm, MLA_V), lambda b, i: (b, 0, i, 0)),
            pl.BlockSpec((1, tm, S5_WIDTH), lambda b, i: (b, i, 0)),
            pl.BlockSpec((1, tm, S5_WIDTH), lambda b, i: (b, i, 0)),
            pl.BlockSpec((1, tm, S5_WIDTH), lambda b, i: (b, i, zs5)),
            pl.BlockSpec((1, tm, HY_WIDTH), lambda b, i: (b, i, 0)),
            full((1, S5_WIDTH)),
            full((S5_WIDTH, S5_WIDTH)),
            full((MLA_HEADS * MLA_V, d)),
            full((S5_WIDTH, d)),
            full((HY_WIDTH, d)),
            full((d, d)),
            pl.BlockSpec((1, 1, d), lambda b, i: (b, 0, 0)),
        ],
        out_specs=pl.BlockSpec((1, tm, d), lambda b, i: (b, i, 0)),
        out_shape=jax.ShapeDtypeStruct((bsz, n, d), F32),
        compiler_params=_cparams(("parallel", "parallel")),
        name="merge",
    )(x, z, z, z, att, y5[0], y5[1], z, e, s5_d, w_glu, w_mla, w_s5, w_hy, w_out, gate)


def _ffn_kernel(x_ref, g_ref, sh_ref, sc_ref, gt_ref, wg_ref, wu_ref, wd_ref, fg_ref, o_ref, h_ref, acc_ref,
                *, nk, final):
    k = pl.program_id(2)

    @pl.when(k == 0)
    def _():
        h_ref[...] = _norm_mod(x_ref[0], g_ref[...], sh_ref[0], sc_ref[0]).astype(BF16)
        acc_ref[...] = jnp.zeros_like(acc_ref)

    h = h_ref[...]
    act = jax.nn.silu(_dot(h, wg_ref[...])) * _dot(h, wu_ref[...])
    acc_ref[...] += _dot(act.astype(BF16), wd_ref[...])

    @pl.when(k == nk - 1)
    def _():
        r = x_ref[0] + gt_ref[0] * acc_ref[...]
        o_ref[0] = _rms(r, fg_ref[...]) if final else r


def _ffn(x, g, shift, scale, gate, w_g, w_u, w_d, final_g, final):
    bsz, n, d = x.shape
    dff = w_g.shape[1]
    tm = min(n, 1024)
    tf = 256
    nk = dff // tf
    vec = pl.BlockSpec((1, 1, d), lambda b, i, k: (b, 0, 0))
    row = pl.BlockSpec((1, d), lambda b, i, k: (0, 0))
    return pl.pallas_call(
        functools.partial(_ffn_kernel, nk=nk, final=final),
        grid=(bsz, n // tm, nk),
        in_specs=[
            pl.BlockSpec((1, tm, d), lambda b, i, k: (b, i, 0)),
            row, vec, vec, vec,
            pl.BlockSpec((d, tf), lambda b, i, k: (0, k)),
            pl.BlockSpec((d, tf), lambda b, i, k: (0, k)),
            pl.BlockSpec((tf, d), lambda b, i, k: (k, 0)),
            row,
        ],
        out_specs=pl.BlockSpec((1, tm, d), lambda b, i, k: (b, i, 0)),
        out_shape=jax.ShapeDtypeStruct((bsz, n, d), F32),
        scratch_shapes=[pltpu.VMEM((tm, d), BF16), pltpu.VMEM((tm, d), F32)],
        compiler_params=_cparams(("parallel", "parallel", "arbitrary")),
        name="ffn",
    )(x, g, shift, scale, gate, w_g, w_u, w_d, final_g)


def _rope_tables(n_tok):
    rows = n_tok // GRID_W
    row = jnp.broadcast_to(jnp.arange(rows, dtype=F32)[:, None], (rows, GRID_W)).reshape(-1)
    col = jnp.broadcast_to(jnp.arange(GRID_W, dtype=F32)[None, :], (rows, GRID_W)).reshape(-1)
    n_freq = MLA_ROPE // 4
    inv = ROPE_BASE ** (-jnp.arange(n_freq, dtype=F32) / n_freq)
    ang = jnp.concatenate([row[:, None] * inv, col[:, None] * inv], axis=-1)
    cos, sin = jnp.cos(ang), jnp.sin(ang)
    pad = jnp.zeros((n_tok, HEAD_W - MLA_ROPE), F32)
    return jnp.concatenate([cos, cos, pad], axis=-1), jnp.concatenate([-sin, sin, pad], axis=-1)


def _identity_rope_tables(n_tok):
    one = jnp.ones((n_tok, MLA_ROPE), F32)
    pad = jnp.zeros((n_tok, HEAD_W - MLA_ROPE), F32)
    return jnp.concatenate([one, pad], axis=-1), jnp.zeros((n_tok, HEAD_W), F32)


def _layout_w_in(w):
    kr = w[:, OFF_KR:OFF_S5]
    x1, x2 = kr[:, 0::2], kr[:, 1::2]
    pad = jnp.zeros((w.shape[0], HEAD_W - 3 * (MLA_ROPE // 2)), w.dtype)
    return jnp.concatenate([w[:, OFF_GATE:], w[:, OFF_HY:OFF_GATE], w[:, OFF_S5:OFF_HY],
                            w[:, OFF_CQ:OFF_CKV], w[:, OFF_CKV:OFF_KR], x1, x2, x1, pad], axis=1).astype(BF16)


def _layout_w_uq(w):
    w = w.reshape(MLA_Q_LORA, MLA_HEADS, MLA_NOPE + MLA_ROPE)
    nope, rope = w[..., :MLA_NOPE], w[..., MLA_NOPE:]
    x1, x2 = rope[..., 0::2], rope[..., 1::2]
    z32 = jnp.zeros_like(rope)
    wq = jnp.concatenate([x1, x2, z32, nope], axis=-1)
    wqs = jnp.concatenate([x2, x1, z32, jnp.zeros_like(nope)], axis=-1)
    shape = (MLA_Q_LORA, MLA_HEADS * HEAD_W)
    return wq.reshape(shape).astype(BF16), wqs.reshape(shape).astype(BF16)


def _layout_w_ukv(w):
    w = w.reshape(MLA_KV_LORA, MLA_HEADS, MLA_NOPE + MLA_V)
    nope, val = w[..., :MLA_NOPE], w[..., MLA_NOPE:]
    wk = jnp.concatenate([jnp.zeros_like(nope), nope], axis=-1).reshape(MLA_KV_LORA, MLA_HEADS * HEAD_W)
    wvt = jnp.concatenate([val, jnp.zeros_like(val[..., :VT_ROWS - MLA_V])], axis=-1).reshape(
        MLA_KV_LORA, MLA_HEADS * VT_ROWS).T
    return wk.astype(BF16), wvt.astype(BF16)


def kernel(x, c, ctx, c_ctx, ada_w, ada_b, norm_mix, w_in, mla_q_norm, mla_w_uq, mla_kv_norm, mla_w_ukv,
           s5_lam_re, s5_lam_im, s5_log_dt, s5_b_re, s5_b_im, s5_c_re, s5_c_im, s5_d, s5_w_glu,
           hy_conv_w, hy_conv_b, hy_f_w1, hy_f_b1, hy_f_w2, hy_f_b2, hy_f_w3, hy_f_freq, hy_bias,
           w_branch_mla, w_branch_s5, w_branch_hy, w_out, norm_ffn, ffn_w_gu, ffn_w_down, final_norm):
    bsz, n_tok, d = x.shape
    n_ctx = ctx.shape[1]
    depth = ada_w.shape[0]
    rope_x = _rope_tables(n_tok)
    rope_c = _identity_rope_tables(n_ctx)
    h_zero = jnp.zeros((bsz, 2, 4, S5_HSTATE), F32)

    assert bsz < ADA_ROWS, "the conditioning rows of the batch and the context must fit one row tile"
    cond = jnp.zeros((ADA_ROWS, d), F32).at[:bsz].set(c).at[bsz].set(c_ctx)
    for i in range(depth):
        last = i == depth - 1
        mods = _ada_modulation(cond, ada_w[i], ada_b[i][None])
        mx = mods[:bsz].reshape(bsz, N_MOD, 1, d)
        mc = jnp.broadcast_to(mods[bsz].reshape(1, N_MOD, 1, d), (bsz, N_MOD, 1, d))
        w_z = _layout_w_in(w_in[i])
        wq, wqs = _layout_w_uq(mla_w_uq[i])
        wk, wvt = _layout_w_ukv(mla_w_ukv[i])
        qn, kvn = mla_q_norm[i][None], mla_kv_norm[i][None]
        s5_tabs = _s5_tables(s5_lam_re[i], s5_lam_im[i], s5_log_dt[i], s5_b_re[i], s5_b_im[i],
                             s5_c_re[i], s5_c_im[i])
        fparams = (hy_f_w1[i], hy_f_b1[i], hy_f_w2[i], hy_f_b2[i], hy_f_w3[i], hy_f_freq[i])
        merge_w = (s5_d[i][None], s5_w_glu[i].astype(BF16),
                   w_branch_mla[i].astype(BF16),
                   w_branch_s5[i].astype(BF16), w_branch_hy[i].astype(BF16), w_out[i].astype(BF16))
        w_g = ffn_w_gu[i][:, :D_FF].astype(BF16)
        w_u = ffn_w_gu[i][:, D_FF:].astype(BF16)
        w_d = ffn_w_down[i].astype(BF16)
        g_mix, g_ffn = norm_mix[i][None], norm_ffn[i][None]

        zx = _proj_in(x, g_mix, mx[:, 0], mx[:, 1], w_z)
        zc = _proj_in(ctx, g_mix, mc[:, 0], mc[:, 1], w_z)

        n_keys = n_tok + n_ctx
        q_x, k_all, vt_all = _mla_prep(zx, qn, kvn, wq, wqs, wk, wvt, *rope_x, n_keys, 0)
        q_c, k_all, vt_all = _mla_prep(zc, qn, kvn, wq, wqs, wk, wvt, *rope_c, n_keys, n_tok, (k_all, vt_all))
        a_x = _attention(q_x, k_all, vt_all, n_keys)

        *y5_c, finals = _s5_scan(zc, h_zero, s5_tabs)
        *y5_x, _ = _s5_scan(zx, finals, s5_tabs)

        v_x, g1_x, g2_x = _short_conv(zx, hy_conv_w[i], hy_conv_b[i][None])
        e_x = _hyena_mixer(v_x, (g1_x, g2_x), _hyena_filters(n_tok, *fparams), hy_bias[i])

        x = _merge(x, zx, a_x, y5_x, e_x, *merge_w, mx[:, 2])
        x = _ffn(x, g_ffn, mx[:, 3], mx[:, 4], mx[:, 5], w_g, w_u, w_d, final_norm[None], last)

        if not last:
            a_c = _attention(q_c, k_all, vt_all, n_ctx, n_tok // n_ctx)
            v_c, g1_c, g2_c = _short_conv(zc, hy_conv_w[i], hy_conv_b[i][None])
            e_c = _hyena_mixer(v_c, (g1_c, g2_c), _hyena_filters(n_ctx, *fparams), hy_bias[i])
            ctx = _merge(ctx, zc, a_c, y5_c, e_c, *merge_w, mc[:, 2])
            ctx = _ffn(ctx, g_ffn, mc[:, 3], mc[:, 4], mc[:, 5], w_g, w_u, w_d, final_norm[None], False)
    return x
```
